```python
import math
import jax, jax.numpy as jnp
from jax import lax
import numpy as np

D_MODEL = 2048
BATCH = 2
SEQ = 4096
DEPTH = 2
DEC_BATCH = 16
DEC_SEQ = 16
PAST_LEN = 2048

CHUNK = 64
HEAD_DIM = 64
SWA_WINDOW = 128
SWA_BACK = SWA_WINDOW // CHUNK
SWA_HEADS = 16
SWA_KV_HEADS = 2
T5_BUCKETS = 32
T5_MAX_DIST = 128
RET_HEADS = 8
RET_DK = 64
RET_DV = 128
ROPE_BASE = 10000.0
RET_NORM_EPS = 1e-5
BAND_BACK = 8
BAND_HEADS = 16
BAND_MAX_REL = 256
MEM_LEN = 256
MEM_HEADS = 4
MEM_HD = D_MODEL // MEM_HEADS
MEM_W = MEM_HEADS * MEM_HD
D_FF = 5632
CONV_W = 3
DN_ALPHA = (2 * DEPTH) ** 0.25
DN_BETA = (8 * DEPTH) ** -0.25
LN_EPS = 1e-5

SWA_Q_W = SWA_HEADS * HEAD_DIM
SWA_KV_W = SWA_KV_HEADS * HEAD_DIM
RET_QK_W = RET_HEADS * RET_DK
RET_V_W = RET_HEADS * RET_DV
BAND_W = BAND_HEADS * HEAD_DIM
IN_WIDTHS = (SWA_Q_W, SWA_KV_W, SWA_KV_W, RET_QK_W, RET_QK_W, RET_V_W, RET_V_W,
             BAND_W, BAND_W, BAND_W, D_MODEL, D_MODEL, D_MODEL)
IN_COLS = sum(IN_WIDTHS)

kernel_name = "hybrid_streaming_encoder_step"

F32 = jnp.float32


def layer_norm(x, g, b):
    xf = x.astype(F32)
    mu = xf.mean(-1, keepdims=True)
    var = jnp.mean(jnp.square(xf - mu), -1, keepdims=True)
    return ((xf - mu) * lax.rsqrt(var + LN_EPS) * g.astype(F32) + b.astype(F32)).astype(x.dtype)


def rope(x, pos):
    half = x.shape[-1] // 2
    inv = ROPE_BASE ** (-jnp.arange(half, dtype=F32) / half)
    ang = pos.astype(F32)[:, None] * inv[None, :]
    cos = jnp.cos(ang)[None, :, None, :]
    sin = jnp.sin(ang)[None, :, None, :]
    x1 = x[..., :half].astype(F32)
    x2 = x[..., half:].astype(F32)
    return jnp.concatenate([x1 * cos - x2 * sin, x2 * cos + x1 * sin], -1).astype(x.dtype)


def t5_bucket(rel):
    nb = T5_BUCKETS // 2
    max_exact = nb // 2
    n = jnp.abs(rel)
    nf = jnp.maximum(n, 1).astype(F32)
    large = max_exact + (jnp.log(nf / max_exact) / math.log(T5_MAX_DIST / max_exact)
                         * (nb - max_exact)).astype(jnp.int32)
    large = jnp.minimum(large, nb - 1)
    return jnp.where(rel > 0, nb, 0) + jnp.where(n < max_exact, n, large)


def t5_bias(table, rel):
    return jnp.transpose(table[t5_bucket(rel)], (2, 0, 1)).astype(F32)


def clipped_rel_bias(table, rel):
    return table[:, jnp.clip(rel, -BAND_MAX_REL, BAND_MAX_REL) + BAND_MAX_REL].astype(F32)


def sink_softmax(s, sink):
    m = s.max(-1, keepdims=True)
    if sink is not None:
        m = jnp.maximum(m, sink)
    p = jnp.exp(s - m)
    den = p.sum(-1, keepdims=True)
    if sink is not None:
        den = den + jnp.exp(sink - m)
    return p / den


def band_attention(q, k, v, n_back, bias, sink):
    B, S, H, dh = q.shape
    G = k.shape[2]
    R = H // G
    nC = S // CHUNK
    qc = q.reshape(B, nC, CHUNK, G, R, dh)
    pad = ((0, 0), (n_back * CHUNK, 0), (0, 0), (0, 0))
    kc = jnp.pad(k, pad).reshape(B, nC + n_back, CHUNK, G, dh)
    vc = jnp.pad(v, pad).reshape(B, nC + n_back, CHUNK, G, dh)
    s = jnp.concatenate([jnp.einsum('bnqgrd,bnkgd->bngrqk', qc, kc[:, j:j + nC])
                         for j in range(n_back + 1)], axis=-1).astype(F32) * (dh ** -0.5)
    s = s + bias.reshape(G, R, CHUNK, -1)
    chunk_ok = (jnp.arange(nC)[:, None] + jnp.arange(n_back + 1)[None, :]) >= n_back
    key_ok = jnp.repeat(chunk_ok, CHUNK, axis=1)
    s = jnp.where(key_ok[None, :, None, None, None, :], s, -jnp.inf)
    sk = None if sink is None else sink.astype(F32).reshape(G, R, 1, 1)
    p = sink_softmax(s, sk).astype(v.dtype)
    o = sum(jnp.einsum('bngrqk,bnkgd->bnqgrd', p[..., j * CHUNK:(j + 1) * CHUNK], vc[:, j:j + nC])
            for j in range(n_back + 1))
    return o.reshape(B, S, H * dh)


def cached_attention(q, k, v, bias, sink):
    B, n, H, dh = q.shape
    G = k.shape[2]
    R = H // G
    s = jnp.einsum('bqgrd,bkgd->bgrqk', q.reshape(B, n, G, R, dh), k).astype(F32) * (dh ** -0.5)
    s = s + bias.reshape(G, R, n, -1)
    sk = None if sink is None else sink.astype(F32).reshape(G, R, 1, 1)
    p = sink_softmax(s, sk).astype(v.dtype)
    return jnp.einsum('bgrqk,bkgd->bqgrd', p, v).reshape(B, n, H * dh)


def ret_log_decay():
    return jnp.log(1.0 - 2.0 ** (-5.0 - jnp.arange(RET_HEADS, dtype=F32)))


def ret_block_output(q, k, v, s0, log_g):
    L = q.shape[2]
    i = jnp.arange(L, dtype=F32)
    diff = i[:, None] - i[None, :]
    decay = jnp.where(diff >= 0, jnp.exp(log_g[:, None, None] * jnp.maximum(diff, 0.0)), 0.0)
    qk = jnp.einsum('bnihd,bnjhd->bnhij', q, k).astype(F32) * decay
    o = jnp.einsum('bnhij,bnjhe->bnihe', qk, v.astype(F32))
    cross = jnp.exp(log_g[None, :] * (i[:, None] + 1.0))
    return o + jnp.einsum('bnihd,bnhde->bnihe', q.astype(F32), s0) * cross[:, :, None]


def ret_block_update(k, v, log_g):
    L = k.shape[2]
    i = jnp.arange(L, dtype=F32)
    w = jnp.exp(log_g[None, :] * (L - 1.0 - i)[:, None])
    return jnp.einsum('bnjhd,bnjhe->bnhde', k.astype(F32) * w[:, :, None], v.astype(F32))


def ret_finish(o, gate):
    mu = o.mean(-1, keepdims=True)
    var = jnp.mean(jnp.square(o - mu), -1, keepdims=True)
    o = (o - mu) * lax.rsqrt(var + RET_NORM_EPS)
    B, T = o.shape[:2]
    return (o.reshape(B, T, -1) * jax.nn.silu(gate.astype(F32))).astype(gate.dtype)


def project_in(x, w_in):
    B, T, _ = x.shape
    points = [int(p) for p in np.cumsum(IN_WIDTHS)[:-1]]
    qa, ka, va, qb, kb, vb, gr, qc, kc, vc, ga, gb, gc = jnp.split(x @ w_in, points, axis=-1)
    hd = lambda t, h: t.reshape(B, T, h, -1)
    return (hd(qa, SWA_HEADS), hd(ka, SWA_KV_HEADS), hd(va, SWA_KV_HEADS),
            hd(qb, RET_HEADS), hd(kb, RET_HEADS), hd(vb, RET_HEADS), gr,
            hd(qc, BAND_HEADS), hd(kc, BAND_HEADS), hd(vc, BAND_HEADS), ga, gb, gc)


def merge_branches(oa, ob, oc, ga, gb, gc, w_br_a, w_br_b, w_br_c, w_mix_o):
    mix = (jax.nn.sigmoid(ga) * (oa @ w_br_a) + jax.nn.sigmoid(gb) * (ob @ w_br_b)
           + jax.nn.sigmoid(gc) * (oc @ w_br_c))
    return mix @ w_mix_o


def mixer_prompt(x, w_in, t5_table, sink, band_table, w_br_a, w_br_b, w_br_c, w_mix_o):
    B, S, _ = x.shape
    qa, ka, va, qb, kb, vb, gr, qc, kc, vc, ga, gb, gc = project_in(x, w_in)
    nka = (SWA_BACK + 1) * CHUNK
    rel_a = jnp.arange(nka)[None, :] - SWA_BACK * CHUNK - jnp.arange(CHUNK)[:, None]
    oa = band_attention(qa, ka, va, SWA_BACK, t5_bias(t5_table, rel_a), sink)
    pos = jnp.arange(S)
    qb = rope(qb, pos)
    kb = rope(kb, pos) * (RET_DK ** -0.5)
    nC = S // CHUNK
    ch = lambda t: t.reshape(B, nC, CHUNK, RET_HEADS, -1)
    qr, kr, vr = ch(qb), ch(kb), ch(vb)
    log_g = ret_log_decay()
    upd = ret_block_update(kr, vr, log_g)
    g_chunk = jnp.exp(log_g * CHUNK)[:, None, None]

    def step(s, u):
        return g_chunk * s + u, s

    s_final, s_start = lax.scan(step, jnp.zeros((B, RET_HEADS, RET_DK, RET_DV), F32),
                                jnp.moveaxis(upd, 1, 0))
    o_r = ret_block_output(qr, kr, vr, jnp.moveaxis(s_start, 0, 1), log_g)
    ob = ret_finish(o_r.reshape(B, S, RET_HEADS, RET_DV), gr)
    nkc = (BAND_BACK + 1) * CHUNK
    rel_c = jnp.arange(nkc)[None, :] - BAND_BACK * CHUNK - jnp.arange(CHUNK)[:, None]
    oc = band_attention(qc, kc, vc, BAND_BACK, clipped_rel_bias(band_table, rel_c), None)
    out = merge_branches(oa, ob, oc, ga, gb, gc, w_br_a, w_br_b, w_br_c, w_mix_o)
    la = min(SWA_WINDOW, S)
    lc = min(BAND_BACK * CHUNK, S)
    return out, (ka[:, S - la:], va[:, S - la:], s_final, kc[:, S - lc:], vc[:, S - lc:])


def mixer_sample(x, swa_k, swa_v, ret_s, band_k, band_v, w_in, t5_table, sink, band_table,
                 w_br_a, w_br_b, w_br_c, w_mix_o):
    B, n, _ = x.shape
    qa, ka, va, qb, kb, vb, gr, qc, kc, vc, ga, gb, gc = project_in(x, w_in)
    qpos = PAST_LEN + jnp.arange(n)
    la = swa_k.shape[1]
    kpos_a = jnp.concatenate([PAST_LEN - la + jnp.arange(la), qpos])
    ka_all = jnp.concatenate([swa_k.astype(ka.dtype), ka], 1)
    va_all = jnp.concatenate([swa_v.astype(va.dtype), va], 1)
    oa = cached_attention(qa, ka_all, va_all, t5_bias(t5_table, kpos_a[None, :] - qpos[:, None]), sink)
    qb = rope(qb, qpos)
    kb = rope(kb, qpos) * (RET_DK ** -0.5)
    log_g = ret_log_decay()
    s0 = ret_s.astype(F32)
    o_r = ret_block_output(qb[:, None], kb[:, None], vb[:, None], s0[:, None], log_g)[:, 0]
    s_new = jnp.exp(log_g * n)[:, None, None] * s0 + ret_block_update(kb[:, None], vb[:, None], log_g)[:, 0]
    ob = ret_finish(o_r, gr)
    lc = band_k.shape[1]
    kpos_c = jnp.concatenate([PAST_LEN - lc + jnp.arange(lc), qpos])
    kc_all = jnp.concatenate([band_k.astype(kc.dtype), kc], 1)
    vc_all = jnp.concatenate([band_v.astype(vc.dtype), vc], 1)
    oc = cached_attention(qc, kc_all, vc_all,
                          clipped_rel_bias(band_table, kpos_c[None, :] - qpos[:, None]), None)
    out = merge_branches(oa, ob, oc, ga, gb, gc, w_br_a, w_br_b, w_br_c, w_mix_o)
    return out, (ka, va, s_new, kc, vc)


def mem_kv(mem, w_mk, w_mv):
    B, M, _ = mem.shape
    return ((mem @ w_mk).reshape(B, M, MEM_HEADS, MEM_HD), (mem @ w_mv).reshape(B, M, MEM_HEADS, MEM_HD))


def mem_attend(x, mk, mv, w_mq, w_mo):
    B, T, _ = x.shape
    q = (x @ w_mq).reshape(B, T, MEM_HEADS, MEM_HD)
    s = jnp.einsum('bqhd,bkhd->bhqk', q, mk.astype(q.dtype)).astype(F32) * (MEM_HD ** -0.5)
    p = jax.nn.softmax(s, axis=-1).astype(q.dtype)
    o = jnp.einsum('bhqk,bkhd->bqhd', p, mv.astype(q.dtype)).reshape(B, T, MEM_W)
    return o @ w_mo


def conv_ffn(x, conv_state, w_ffn_in, conv_w, conv_b, w_ffn_out):
    u = x @ w_ffn_in
    T = u.shape[1]
    up = jnp.concatenate([conv_state.astype(u.dtype), u], 1)
    c = sum(conv_w[j] * up[:, j:j + T] for j in range(CONV_W)) + conv_b
    g, val = jnp.split(c, 2, axis=-1)
    h = jax.nn.gelu(g, approximate=False) * val
    return h @ w_ffn_out, up[:, T:]


def setup_inputs(seed: int = 0) -> dict:
    key = jax.random.key(seed)
    ks = iter(jax.random.split(key, 40))
    nrm = lambda shape, scale: jax.random.normal(next(ks), shape, F32) * scale
    swa_cache = min(SWA_WINDOW, PAST_LEN)
    band_cache = min(BAND_BACK * CHUNK, PAST_LEN)
    F2 = 2 * D_FF
    D = D_MODEL
    return {
        "x_prompt": nrm((BATCH, SEQ, D), 1.0),
        "x_sample": nrm((DEC_BATCH, DEC_SEQ, D), 1.0),
        "mem_prompt": nrm((BATCH, MEM_LEN, D), 1.0),
        "cache_swa_k": nrm((DEPTH, DEC_BATCH, swa_cache, SWA_KV_HEADS, HEAD_DIM), 1.0),
        "cache_swa_v": nrm((DEPTH, DEC_BATCH, swa_cache, SWA_KV_HEADS, HEAD_DIM), 1.0),
        "state_ret": nrm((DEPTH, DEC_BATCH, RET_HEADS, RET_DK, RET_DV), 0.5),
        "cache_band_k": nrm((DEPTH, DEC_BATCH, band_cache, BAND_HEADS, HEAD_DIM), 1.0),
        "cache_band_v": nrm((DEPTH, DEC_BATCH, band_cache, BAND_HEADS, HEAD_DIM), 1.0),
        "state_ffn_conv": nrm((DEPTH, DEC_BATCH, CONV_W - 1, F2), 1.0),
        "cache_mem_k": nrm((DEPTH, DEC_BATCH, MEM_LEN, MEM_HEADS, MEM_HD), 1.0),
        "cache_mem_v": nrm((DEPTH, DEC_BATCH, MEM_LEN, MEM_HEADS, MEM_HD), 1.0),
        "w_in": nrm((DEPTH, D, IN_COLS), D ** -0.5),
        "t5_table": nrm((T5_BUCKETS, SWA_HEADS), 0.2),
        "swa_sink": nrm((DEPTH, SWA_HEADS), 0.5),
        "band_rel_table": nrm((DEPTH, BAND_HEADS, 2 * BAND_MAX_REL + 1), 0.2),
        "w_br_a": nrm((DEPTH, SWA_Q_W, D), SWA_Q_W ** -0.5),
        "w_br_b": nrm((DEPTH, RET_V_W, D), RET_V_W ** -0.5),
        "w_br_c": nrm((DEPTH, BAND_W, D), BAND_W ** -0.5),
        "w_mix_o": nrm((DEPTH, D, D), D ** -0.5 * DN_BETA),
        "ln1_g": 1.0 + nrm((DEPTH, D), 0.01),
        "ln1_b": nrm((DEPTH, D), 0.01),
        "w_mq": nrm((DEPTH, D, MEM_W), D ** -0.5),
        "w_mk": nrm((DEPTH, D, MEM_W), D ** -0.5),
        "w_mv": nrm((DEPTH, D, MEM_W), D ** -0.5),
        "w_mo": nrm((DEPTH, MEM_W, D), MEM_W ** -0.5 * DN_BETA),
        "ln2_g": 1.0 + nrm((DEPTH, D), 0.01),
        "ln2_b": nrm((DEPTH, D), 0.01),
        "w_ffn_in": nrm((DEPTH, D, F2), D ** -0.5),
        "ffn_conv_w": nrm((DEPTH, CONV_W, F2), CONV_W ** -0.5),
        "ffn_conv_b": nrm((DEPTH, F2), 0.01),
        "w_ffn_out": nrm((DEPTH, D_FF, D), D_FF ** -0.5 * DN_BETA),
        "ln3_g": 1.0 + nrm((DEPTH, D), 0.01),
        "ln3_b": nrm((DEPTH, D), 0.01),
    }


def reference(x_prompt, x_sample, mem_prompt, cache_swa_k, cache_swa_v, state_ret, cache_band_k,
              cache_band_v, state_ffn_conv, cache_mem_k, cache_mem_v, w_in, t5_table, swa_sink,
              band_rel_table, w_br_a, w_br_b, w_br_c, w_mix_o, ln1_g, ln1_b, w_mq, w_mk, w_mv, w_mo,
              ln2_g, ln2_b, w_ffn_in, ffn_conv_w, ffn_conv_b, w_ffn_out, ln3_g, ln3_b):
    xp = x_prompt
    xs = x_sample
    Bp = xp.shape[0]
    p_ak, p_av, p_rs, p_bk, p_bv, p_fc, p_mk, p_mv = [], [], [], [], [], [], [], []
    s_ak, s_av, s_rs, s_bk, s_bv, s_fc = [], [], [], [], [], []
    for l in range(DEPTH):
        mix, (ak, av, rs, bk, bv) = mixer_prompt(xp, w_in[l], t5_table, swa_sink[l], band_rel_table[l],
                                                 w_br_a[l], w_br_b[l], w_br_c[l], w_mix_o[l])
        xp = layer_norm(DN_ALPHA * xp + mix, ln1_g[l], ln1_b[l])
        mk, mv = mem_kv(mem_prompt, w_mk[l], w_mv[l])
        xp = layer_norm(DN_ALPHA * xp + mem_attend(xp, mk, mv, w_mq[l], w_mo[l]), ln2_g[l], ln2_b[l])
        f, fc = conv_ffn(xp, jnp.zeros((Bp, CONV_W - 1, 2 * D_FF), xp.dtype), w_ffn_in[l],
                         ffn_conv_w[l], ffn_conv_b[l], w_ffn_out[l])
        xp = layer_norm(DN_ALPHA * xp + f, ln3_g[l], ln3_b[l])
        p_ak.append(ak); p_av.append(av); p_rs.append(rs); p_bk.append(bk); p_bv.append(bv)
        p_fc.append(fc); p_mk.append(mk); p_mv.append(mv)
        mix, (ak, av, rs, bk, bv) = mixer_sample(xs, cache_swa_k[l], cache_swa_v[l], state_ret[l],
                                                 cache_band_k[l], cache_band_v[l], w_in[l], t5_table,
                                                 swa_sink[l], band_rel_table[l], w_br_a[l], w_br_b[l],
                                                 w_br_c[l], w_mix_o[l])
        xs = layer_norm(DN_ALPHA * xs + mix, ln1_g[l], ln1_b[l])
        xs = layer_norm(DN_ALPHA * xs + mem_attend(xs, cache_mem_k[l], cache_mem_v[l], w_mq[l], w_mo[l]),
                        ln2_g[l], ln2_b[l])
        f, fc = conv_ffn(xs, state_ffn_conv[l], w_ffn_in[l], ffn_conv_w[l], ffn_conv_b[l], w_ffn_out[l])
        xs = layer_norm(DN_ALPHA * xs + f, ln3_g[l], ln3_b[l])
        s_ak.append(ak); s_av.append(av); s_rs.append(rs); s_bk.append(bk); s_bv.append(bv)
        s_fc.append(fc)
    st = lambda a: jnp.stack(a, 0)
    return (xp, xs,
            st(p_ak), st(p_av), st(p_rs), st(p_bk), st(p_bv), st(p_fc), st(p_mk), st(p_mv),
            st(s_ak), st(s_av), st(s_rs), st(s_bk), st(s_bv), st(s_fc))
```

```python
import functools
import math

import numpy as np
import jax
import jax.numpy as jnp
from jax import lax
from jax.experimental import pallas as pl
from jax.experimental.pallas import tpu as pltpu

F32 = jnp.float32
BF16 = jnp.bfloat16

D_MODEL = 2048
BATCH = 2
SEQ = 4096
DEPTH = 2
DEC_BATCH = 16
DEC_SEQ = 16
PAST_LEN = 2048
CHUNK = 64
HEAD_DIM = 64
SWA_BACK = 2
SWA_HEADS = 16
SWA_KV_HEADS = 2
T5_BUCKETS = 32
T5_MAX_DIST = 128
RET_HEADS = 8
RET_DK = 64
RET_DV = 128
ROPE_BASE = 10000.0
RET_NORM_EPS = 1e-5
BAND_BACK = 8
BAND_HEADS = 16
BAND_MAX_REL = 256
MEM_LEN = 256
MEM_HEADS = 4
MEM_HD = D_MODEL // MEM_HEADS
D_FF = 5632
CONV_W = 3
DN_ALPHA = (2 * DEPTH) ** 0.25
LN_EPS = 1e-5

P_ROWS = BATCH * SEQ
S_ROWS = DEC_BATCH * DEC_SEQ
ROWS = P_ROWS + S_ROWS

COL_QA = 0
COL_QC = 1024
COL_KC = 2048
COL_VC = 3072
COL_VB = 4096
COL_GR = 5120
COL_GA = 6144
COL_GB = 8192
COL_GC = 10240
COL_QB = 12288
COL_KB = 12800
COL_KA2 = 13312
COL_VA2 = 13568
PROJ_COLS = 13824

LANES = 128
QB = 256
RET_L = 256
NEG = -1e30
VMEM_LIMIT = 48 * 1024 * 1024

NT = (((1,), (1,)), ((), ()))
TN = (((0,), (0,)), ((), ()))


def _params(sem, vmem=VMEM_LIMIT):
    return pltpu.CompilerParams(dimension_semantics=sem, vmem_limit_bytes=vmem)


def _mm_kernel(a_ref, b_ref, o_ref):
    o_ref[...] = jnp.dot(a_ref[...], b_ref[...], preferred_element_type=F32).astype(o_ref.dtype)


def matmul(a, b, *, tm, tn, out_dtype):
    m, k = a.shape
    n = b.shape[1]
    return pl.pallas_call(
        _mm_kernel,
        grid=(m // tm, n // tn),
        in_specs=[pl.BlockSpec((tm, k), lambda i, j: (i, 0)),
                  pl.BlockSpec((k, tn), lambda i, j: (0, j))],
        out_specs=pl.BlockSpec((tm, tn), lambda i, j: (i, j)),
        out_shape=jax.ShapeDtypeStruct((m, n), out_dtype),
        compiler_params=_params(("parallel", "parallel")),
        name="matmul",
    )(a, b)


def _gate_mm_kernel(oa_ref, ob_ref, oc_ref, wa_ref, wb_ref, wc_ref, ga_ref, gb_ref, gc_ref, o_ref):
    acc = None
    for o, w, g in ((oa_ref, wa_ref, ga_ref), (ob_ref, wb_ref, gb_ref), (oc_ref, wc_ref, gc_ref)):
        t = jnp.dot(o[...], w[...], preferred_element_type=F32) * jax.nn.sigmoid(g[...].astype(F32))
        acc = t if acc is None else acc + t
    o_ref[...] = acc.astype(o_ref.dtype)


def gated_branch_sum(oa, ob, oc, wa, wb, wc, proj, *, tm=528, tn=1024):
    m, k = oa.shape
    n = wa.shape[1]
    o_spec = pl.BlockSpec((tm, k), lambda i, j: (i, 0))
    w_spec = pl.BlockSpec((k, tn), lambda i, j: (0, j))

    def g_spec(col):
        return pl.BlockSpec((tm, tn), lambda i, j: (i, col // tn + j))

    return pl.pallas_call(
        _gate_mm_kernel,
        grid=(m // tm, n // tn),
        in_specs=[o_spec, o_spec, o_spec, w_spec, w_spec, w_spec,
                  g_spec(COL_GA), g_spec(COL_GB), g_spec(COL_GC)],
        out_specs=pl.BlockSpec((tm, tn), lambda i, j: (i, j)),
        out_shape=jax.ShapeDtypeStruct((m, n), BF16),
        compiler_params=_params(("parallel", "parallel")),
        name="gated_branch_sum",
    )(oa, ob, oc, wa, wb, wc, proj, proj, proj)


def _mm_ln_kernel(a_ref, b_ref, r_ref, g_ref, bt_ref, of_ref, ob_ref, acc_ref):
    k = pl.program_id(1)

    @pl.when(k == 0)
    def _():
        acc_ref[...] = jnp.zeros_like(acc_ref)

    acc_ref[...] += jnp.dot(a_ref[...], b_ref[...], preferred_element_type=F32)

    @pl.when(k == pl.num_programs(1) - 1)
    def _():
        y = DN_ALPHA * r_ref[...] + acc_ref[...]
        mu = jnp.mean(y, -1, keepdims=True)
        d = y - mu
        var = jnp.mean(d * d, -1, keepdims=True)
        out = d * lax.rsqrt(var + LN_EPS) * g_ref[...] + bt_ref[...]
        of_ref[...] = out
        ob_ref[...] = out.astype(BF16)


def matmul_residual_ln(a, b, resid, gain, bias, *, tm=528, tk):
    m, kk = a.shape
    n = b.shape[1]
    return pl.pallas_call(
        _mm_ln_kernel,
        grid=(m // tm, kk // tk),
        in_specs=[pl.BlockSpec((tm, tk), lambda i, k: (i, k)),
                  pl.BlockSpec((tk, n), lambda i, k: (k, 0)),
                  pl.BlockSpec((tm, n), lambda i, k: (i, 0)),
                  pl.BlockSpec((1, n), lambda i, k: (0, 0)),
                  pl.BlockSpec((1, n), lambda i, k: (0, 0))],
        out_specs=[pl.BlockSpec((tm, n), lambda i, k: (i, 0)),
                   pl.BlockSpec((tm, n), lambda i, k: (i, 0))],
        out_shape=[jax.ShapeDtypeStruct((m, n), F32), jax.ShapeDtypeStruct((m, n), BF16)],
        scratch_shapes=[pltpu.VMEM((tm, n), F32)],
        compiler_params=_params(("parallel", "arbitrary")),
        name="matmul_residual_ln",
    )(a, b, resid, gain.reshape(1, n), bias.reshape(1, n))


def _softmax_pv(pieces, sink):
    m = None
    for s, _ in pieces:
        mi = jnp.max(s, -1, keepdims=True)
        m = mi if m is None else jnp.maximum(m, mi)
    if sink is not None:
        m = jnp.maximum(m, sink)
    den = None
    acc = None
    for s, v in pieces:
        p = jnp.exp(s - m)
        di = jnp.sum(p, -1, keepdims=True)
        den = di if den is None else den + di
        oi = jnp.dot(p.astype(BF16), v, preferred_element_type=F32)
        acc = oi if acc is None else acc + oi
    if sink is not None:
        den = den + jnp.exp(sink - m)
    return acc / den


def _head_lanes(j):
    lane = lax.broadcasted_iota(jnp.int32, (1, LANES), 1)
    return (lane < HEAD_DIM) if j == 0 else (lane >= HEAD_DIM)


def _band_attn_kernel(sink_ref, q_ref, *refs, n_prev_blocks, n_prev_rows, has_sink):
    nk = n_prev_blocks + 1
    k_refs = refs[:nk]
    v_refs = refs[nk:2 * nk]
    bias_ref = refs[2 * nk]
    o_ref = refs[2 * nk + 1]
    hp = pl.program_id(0)
    i = pl.program_id(2)
    q = q_ref[...]
    k_all = jnp.concatenate([r[...] for r in k_refs], axis=0)
    v_all = jnp.concatenate([r[...] for r in v_refs], axis=0)
    kw = k_all.shape[0]
    col = lax.broadcasted_iota(jnp.int32, (1, kw), 1)
    col_ok = col >= jnp.maximum(n_prev_rows - i * QB, 0)
    outs = []
    for j in range(2):
        qj = jnp.where(_head_lanes(j), q, jnp.zeros_like(q)) * (HEAD_DIM ** -0.5)
        s = lax.dot_general(qj, k_all, NT, preferred_element_type=F32) + bias_ref[j]
        s = jnp.where(col_ok, s, NEG)
        sink = sink_ref[hp * 2 + j] if has_sink else None
        outs.append(_softmax_pv([(s, v_all)], sink))
    o_ref[...] = jnp.where(_head_lanes(0), outs[0], outs[1]).astype(o_ref.dtype)


def band_attention_prompt(proj, bias, sink, *, q_col, k_col, v_col, shared_kv, n_back, has_sink):
    n_prev_rows = n_back * CHUNK
    nqb = SEQ // QB
    if n_prev_rows >= QB:
        n_prev_blocks, pb = n_prev_rows // QB, QB
    else:
        n_prev_blocks, pb = 1, n_prev_rows
    per = QB // pb
    kvw = n_prev_rows + QB

    def kv_col(col):
        if shared_kv:
            return lambda hp: col // LANES + hp // (SWA_HEADS // SWA_KV_HEADS // 2)
        return lambda hp: col // LANES + hp

    def prev_spec(col, back):
        cf = kv_col(col)
        return pl.BlockSpec(
            (pb, LANES),
            lambda hp, b, i: (b * (SEQ // pb) + jnp.maximum(i * per - back, 0), cf(hp)))

    def own_spec(col):
        cf = kv_col(col)
        return pl.BlockSpec((QB, LANES), lambda hp, b, i: (b * nqb + i, cf(hp)))

    k_specs = [prev_spec(k_col, n_prev_blocks - t) for t in range(n_prev_blocks)] + [own_spec(k_col)]
    v_specs = [prev_spec(v_col, n_prev_blocks - t) for t in range(n_prev_blocks)] + [own_spec(v_col)]
    kern = functools.partial(_band_attn_kernel, n_prev_blocks=n_prev_blocks,
                             n_prev_rows=n_prev_rows, has_sink=has_sink)
    n_in = 2 * (n_prev_blocks + 1)
    return pl.pallas_call(
        kern,
        grid=(8, BATCH, nqb),
        in_specs=[pl.BlockSpec(memory_space=pltpu.SMEM),
                  pl.BlockSpec((QB, LANES), lambda hp, b, i: (b * nqb + i, q_col // LANES + hp))]
                 + k_specs + v_specs
                 + [pl.BlockSpec((2, QB, kvw), lambda hp, b, i: (hp, 0, 0))],
        out_specs=pl.BlockSpec((QB, LANES), lambda hp, b, i: (b * nqb + i, hp)),
        out_shape=jax.ShapeDtypeStruct((P_ROWS, 16 * HEAD_DIM), BF16),
        compiler_params=_params(("parallel", "parallel", "parallel")),
        name="band_attention_prompt",
    )(sink, proj, *([proj] * n_in), bias)


def _cached_attn_kernel(sink_ref, q_ref, kn_ref, vn_ref, kc_ref, vc_ref, bc_ref, bn_ref, o_ref,
                        *, shared_kv, has_sink):
    for p in range(8):
        c0 = (p // 4 if shared_kv else p) * LANES
        q = q_ref[:, p * LANES:(p + 1) * LANES]
        kc = kc_ref[0, :, c0:c0 + LANES].astype(BF16)
        vc = vc_ref[0, :, c0:c0 + LANES].astype(BF16)
        kn = kn_ref[:, c0:c0 + LANES]
        vn = vn_ref[:, c0:c0 + LANES]
        outs = []
        for j in range(2):
            h = 2 * p + j
            qj = jnp.where(_head_lanes(j), q, jnp.zeros_like(q)) * (HEAD_DIM ** -0.5)
            s_c = lax.dot_general(qj, kc, NT, preferred_element_type=F32) + bc_ref[h]
            s_n = lax.dot_general(qj, kn, NT, preferred_element_type=F32) + bn_ref[h]
            sink = sink_ref[h] if has_sink else None
            outs.append(_softmax_pv([(s_c, vc), (s_n, vn)], sink))
        o_ref[:, p * LANES:(p + 1) * LANES] = jnp.where(_head_lanes(0), outs[0], outs[1]).astype(o_ref.dtype)


def cached_attention_sample(proj, k_cache, v_cache, bias_c, bias_n, sink, *, q_col, k_col, v_col,
                            shared_kv, has_sink):
    lc, wc = k_cache.shape[1], k_cache.shape[2]
    r0 = P_ROWS // DEC_SEQ
    kern = functools.partial(_cached_attn_kernel, shared_kv=shared_kv, has_sink=has_sink)
    return pl.pallas_call(
        kern,
        grid=(DEC_BATCH,),
        in_specs=[pl.BlockSpec(memory_space=pltpu.SMEM),
                  pl.BlockSpec((DEC_SEQ, 1024), lambda s: (r0 + s, q_col // 1024)),
                  pl.BlockSpec((DEC_SEQ, wc), lambda s: (r0 + s, k_col // wc)),
                  pl.BlockSpec((DEC_SEQ, wc), lambda s: (r0 + s, v_col // wc)),
                  pl.BlockSpec((1, lc, wc), lambda s: (s, 0, 0)),
                  pl.BlockSpec((1, lc, wc), lambda s: (s, 0, 0)),
                  pl.BlockSpec((16, DEC_SEQ, lc), lambda s: (0, 0, 0)),
                  pl.BlockSpec((16, DEC_SEQ, DEC_SEQ), lambda s: (0, 0, 0))],
        out_specs=pl.BlockSpec((DEC_SEQ, 1024), lambda s: (s, 0)),
        out_shape=jax.ShapeDtypeStruct((S_ROWS, 1024), BF16),
        compiler_params=_params(("parallel",)),
        name="cached_attention_sample",
    )(sink, proj, proj, proj, k_cache, v_cache, bias_c, bias_n)


def _ret_kernel(q_ref, k_ref, v_ref, g_ref, cos_ref, sin_ref, s0_ref, o_ref, sout_ref, st_ref, *, blk_len):
    blk = pl.program_id(1)

    @pl.when(blk == 0)
    def _():
        st_ref[...] = s0_ref[0]

    cos = cos_ref[...]
    sin = sin_ref[...]
    lane = lax.broadcasted_iota(jnp.int32, (1, LANES), 1)
    low_half = (lane % RET_DK) < (RET_DK // 2)

    def rope(x):
        x = x.astype(F32)
        swapped = jnp.where(low_half, pltpu.roll(x, LANES - RET_DK // 2, 1), pltpu.roll(x, RET_DK // 2, 1))
        return x * cos + swapped * sin

    ii = lax.broadcasted_iota(jnp.int32, (blk_len, blk_len), 0)
    jj = lax.broadcasted_iota(jnp.int32, (blk_len, blk_len), 1)
    diff = (ii - jj).astype(F32)
    row = lax.broadcasted_iota(jnp.int32, (blk_len, 1), 0).astype(F32)
    srow = lax.broadcasted_iota(jnp.int32, (LANES, 1), 0)

    for p in range(RET_HEADS // 2):
        qr = rope(q_ref[:, p * LANES:(p + 1) * LANES])
        kr = rope(k_ref[:, p * LANES:(p + 1) * LANES]) * (RET_DK ** -0.5)
        kb = kr.astype(BF16)
        state = st_ref[p]
        state_b = state.astype(BF16)
        upd = None
        log_gs = []
        for j in range(2):
            h = 2 * p + j
            log_g = math.log(1.0 - 2.0 ** (-5.0 - h))
            log_gs.append(log_g)
            decay = jnp.where(diff >= 0, jnp.exp(log_g * jnp.maximum(diff, 0.0)), 0.0)
            qj = jnp.where(_head_lanes(j), qr, 0.0).astype(BF16)
            vh = v_ref[:, h * RET_DV:(h + 1) * RET_DV]
            qk = lax.dot_general(qj, kb, NT, preferred_element_type=F32) * decay
            o = jnp.dot(qk.astype(BF16), vh, preferred_element_type=F32)
            o = o + jnp.dot(qj, state_b, preferred_element_type=F32) * jnp.exp(log_g * (row + 1.0))
            mu = jnp.mean(o, -1, keepdims=True)
            d = o - mu
            var = jnp.mean(d * d, -1, keepdims=True)
            gate = g_ref[:, h * RET_DV:(h + 1) * RET_DV].astype(F32)
            o_ref[:, h * RET_DV:(h + 1) * RET_DV] = (
                d * lax.rsqrt(var + RET_NORM_EPS) * (gate * jax.nn.sigmoid(gate))).astype(o_ref.dtype)
            kwj = jnp.where(_head_lanes(j), kr * jnp.exp(log_g * (blk_len - 1.0 - row)), 0.0).astype(BF16)
            u = lax.dot_general(kwj, vh, TN, preferred_element_type=F32)
            upd = u if upd is None else upd + u
        carry = jnp.where(srow < RET_DK, math.exp(log_gs[0] * blk_len), math.exp(log_gs[1] * blk_len))
        st_ref[p] = carry * state + upd

    @pl.when(blk == pl.num_programs(1) - 1)
    def _():
        sout_ref[0] = st_ref[...]


def retention(proj, cos, sin, s0, *, row0, n_seq, seq_len, blk_len, pos_per_blk):
    nb = seq_len // blk_len
    rb0 = row0 // blk_len

    def rows(col, width):
        return pl.BlockSpec((blk_len, width), lambda b, t: (rb0 + b * nb + t, col // width))

    tab = pl.BlockSpec((blk_len, LANES), lambda b, t: (t if pos_per_blk else 0, 0))
    st = pl.BlockSpec((1, 4, LANES, LANES), lambda b, t: (b, 0, 0, 0))
    return pl.pallas_call(
        functools.partial(_ret_kernel, blk_len=blk_len),
        grid=(n_seq, nb),
        in_specs=[rows(COL_QB, 512), rows(COL_KB, 512), rows(COL_VB, 1024), rows(COL_GR, 1024),
                  tab, tab, st],
        out_specs=[pl.BlockSpec((blk_len, 1024), lambda b, t: (b * nb + t, 0)), st],
        out_shape=[jax.ShapeDtypeStruct((n_seq * seq_len, 1024), BF16),
                   jax.ShapeDtypeStruct((n_seq, 4, LANES, LANES), F32)],
        scratch_shapes=[pltpu.VMEM((4, LANES, LANES), F32)],
        compiler_params=_params(("parallel", "arbitrary")),
        name="retention",
    )(proj, proj, proj, proj, cos, sin, s0)


def _mem_attn_kernel(q_ref, k_ref, v_ref, o_ref):
    k = k_ref[0].astype(BF16)
    v = v_ref[0].astype(BF16)
    s = lax.dot_general(q_ref[...], k, NT, preferred_element_type=F32) * (MEM_HD ** -0.5)
    o_ref[...] = _softmax_pv([(s, v)], None).astype(o_ref.dtype)


def mem_attention(qm, k_src, v_src, *, row0, n_rows, tm, rows_per_kv, k_colblk, v_colblk):
    rb0 = row0 // tm
    return pl.pallas_call(
        _mem_attn_kernel,
        grid=(n_rows // tm, MEM_HEADS),
        in_specs=[pl.BlockSpec((tm, MEM_HD), lambda i, h: (rb0 + i, h)),
                  pl.BlockSpec((1, MEM_LEN, MEM_HD), lambda i, h: (i * tm // rows_per_kv, 0, k_colblk + h)),
                  pl.BlockSpec((1, MEM_LEN, MEM_HD), lambda i, h: (i * tm // rows_per_kv, 0, v_colblk + h))],
        out_specs=pl.BlockSpec((tm, MEM_HD), lambda i, h: (i, h)),
        out_shape=jax.ShapeDtypeStruct((n_rows, D_MODEL), BF16),
        compiler_params=_params(("parallel", "parallel")),
        name="mem_attention",
    )(qm, k_src, v_src)


def _gelu(x):
    return 0.5 * x * (1.0 + lax.erf(x * (2.0 ** -0.5)))


def _ffn_in_kernel(*refs, sample, seq_tiles, tail):
    if sample:
        a_ref, bg_ref, bv_ref, wg_ref, wv_ref, cbg_ref, cbv_ref, f1g_ref, f1v_ref, f2g_ref, f2v_ref, \
            h_ref, tg_ref, tv_ref = refs
    else:
        a_ref, ah_ref, bg_ref, bv_ref, wg_ref, wv_ref, cbg_ref, cbv_ref, h_ref, tg_ref, tv_ref = refs
    tm = a_ref.shape[0]
    a = a_ref[...]
    row = lax.broadcasted_iota(jnp.int32, (tm, 1), 0)
    if not sample:
        keep = (pl.program_id(0) % seq_tiles != 0).astype(F32)

    def conv(b_ref, w_ref, cb_ref, f1_ref, f2_ref, t_ref):
        u = jnp.dot(a, b_ref[...], preferred_element_type=F32)
        t_ref[0] = u[tm - tail:, :]
        r1 = pltpu.roll(u, 1, 0)
        r2 = pltpu.roll(u, 2, 0)
        if sample:
            pos = row % DEC_SEQ
            u1 = jnp.where(pos < 1, f1_ref[...], r1)
            u2 = jnp.where(pos < 2, f2_ref[...], r2)
        else:
            uh = jnp.dot(ah_ref[...], b_ref[...], preferred_element_type=F32) * keep
            n = uh.shape[0]
            u1 = jnp.where(row < 1, uh[n - 1:n, :], r1)
            u2 = jnp.where(row < 1, uh[n - 2:n - 1, :], jnp.where(row < 2, uh[n - 1:n, :], r2))
        w = w_ref[...]
        return w[0:1, :] * u2 + w[1:2, :] * u1 + w[2:3, :] * u + cb_ref[...]

    if sample:
        cg = conv(bg_ref, wg_ref, cbg_ref, f1g_ref, f2g_ref, tg_ref)
        cv = conv(bv_ref, wv_ref, cbv_ref, f1v_ref, f2v_ref, tv_ref)
    else:
        cg = conv(bg_ref, wg_ref, cbg_ref, None, None, tg_ref)
        cv = conv(bv_ref, wv_ref, cbv_ref, None, None, tv_ref)
    h_ref[...] = (_gelu(cg) * cv).astype(h_ref.dtype)


def ffn_in(x, w, conv_w, conv_b, *, row0, n_rows, tm, tn, tail, fix1=None, fix2=None):
    sample = fix1 is not None
    k = x.shape[1]
    nj = D_FF // tn
    ni = n_rows // tm
    rb0 = row0 // tm
    halo = 16
    a_spec = pl.BlockSpec((tm, k), lambda i, j: (rb0 + i, 0))
    bg = pl.BlockSpec((k, tn), lambda i, j: (0, j))
    bv = pl.BlockSpec((k, tn), lambda i, j: (0, nj + j))
    wg = pl.BlockSpec((CONV_W, tn), lambda i, j: (0, j))
    wv = pl.BlockSpec((CONV_W, tn), lambda i, j: (0, nj + j))
    cg = pl.BlockSpec((1, tn), lambda i, j: (0, j))
    cv = pl.BlockSpec((1, tn), lambda i, j: (0, nj + j))
    conv_b2 = conv_b.reshape(1, 2 * D_FF)
    if sample:
        fg = pl.BlockSpec((tm, tn), lambda i, j: (i, j))
        fv = pl.BlockSpec((tm, tn), lambda i, j: (i, nj + j))
        in_specs = [a_spec, bg, bv, wg, wv, cg, cv, fg, fv, fg, fv]
        args = (x, w, w, conv_w, conv_w, conv_b2, conv_b2, fix1, fix1, fix2, fix2)
    else:
        ah = pl.BlockSpec((halo, k), lambda i, j: (jnp.maximum((rb0 + i) * (tm // halo) - 1, 0), 0))
        in_specs = [a_spec, ah, bg, bv, wg, wv, cg, cv]
        args = (x, x, w, w, conv_w, conv_w, conv_b2, conv_b2)
    t_spec = pl.BlockSpec((1, tail, tn), lambda i, j: (i, 0, j))
    return pl.pallas_call(
        functools.partial(_ffn_in_kernel, sample=sample, seq_tiles=SEQ // tm if not sample else 1, tail=tail),
        grid=(ni, nj),
        in_specs=in_specs,
        out_specs=[pl.BlockSpec((tm, tn), lambda i, j: (i, j)), t_spec, t_spec],
        out_shape=[jax.ShapeDtypeStruct((n_rows, D_FF), BF16),
                   jax.ShapeDtypeStruct((ni, tail, D_FF), F32),
                   jax.ShapeDtypeStruct((ni, tail, D_FF), F32)],
        compiler_params=_params(("parallel", "parallel")),
        name="ffn_in",
    )(*args)


def _t5_bucket(rel):
    nb = T5_BUCKETS // 2
    max_exact = nb // 2
    n = jnp.abs(rel)
    nf = jnp.maximum(n, 1).astype(F32)
    large = max_exact + (jnp.log(nf / max_exact) / math.log(T5_MAX_DIST / max_exact)
                         * (nb - max_exact)).astype(jnp.int32)
    large = jnp.minimum(large, nb - 1)
    return jnp.where(rel > 0, nb, 0) + jnp.where(n < max_exact, n, large)


def _t5_bias(table, rel):
    return jnp.transpose(table[_t5_bucket(rel)], (2, 0, 1)).astype(F32)


def _clipped_bias(table, rel):
    return table[:, jnp.clip(rel, -BAND_MAX_REL, BAND_MAX_REL) + BAND_MAX_REL].astype(F32)


def _band_block_bias(bias_fn, n_back):
    n_prev = n_back * CHUNK
    r = np.arange(QB)[:, None]
    c = np.arange(n_prev + QB)[None, :] - n_prev
    qc = r // CHUNK
    kc = np.floor_divide(c, CHUNK)
    allowed = (kc <= qc) & (kc >= qc - n_back)
    return jnp.where(jnp.asarray(allowed)[None], bias_fn(jnp.asarray(c - r)), NEG)


def _rope_tables(pos):
    half = RET_DK // 2
    inv = ROPE_BASE ** (-jnp.arange(half, dtype=F32) / half)
    ang = pos.astype(F32)[:, None] * inv[None, :]
    cos, sin = jnp.cos(ang), jnp.sin(ang)
    cos_t = jnp.concatenate([cos, cos, cos, cos], -1)
    sin_t = jnp.concatenate([-sin, sin, -sin, sin], -1)
    return cos_t, sin_t


def _permute_w_in(w):
    seg = lambda off, width: w[:, off:off + width]
    qa, ka, va = seg(0, 1024), seg(1024, 128), seg(1152, 128)
    qb, kb, vb, gr = seg(1280, 512), seg(1792, 512), seg(2304, 1024), seg(3328, 1024)
    qc, kc, vc = seg(4352, 1024), seg(5376, 1024), seg(6400, 1024)
    ga, gb, gc = seg(7424, 2048), seg(9472, 2048), seg(11520, 2048)
    dup = lambda t: jnp.concatenate([t[:, :64], t[:, :64], t[:, 64:], t[:, 64:]], -1)
    return jnp.concatenate([qa, qc, kc, vc, vb, gr, ga, gb, gc, qb, kb, dup(ka), dup(va)], -1).astype(BF16)


def _dup_groups(t):
    g0, g1 = t[..., 0, :], t[..., 1, :]
    return jnp.concatenate([g0, g0, g1, g1], -1)


def _undup(t):
    return jnp.stack([t[:, 0:64], t[:, 128:192]], 1)


def kernel(x_prompt, x_sample, mem_prompt, cache_swa_k, cache_swa_v, state_ret, cache_band_k, cache_band_v, state_ffn_conv, cache_mem_k, cache_mem_v, w_in, t5_table, swa_sink, band_rel_table, w_br_a, w_br_b, w_br_c, w_mix_o, ln1_g, ln1_b, w_mq, w_mk, w_mv, w_mo, ln2_g, ln2_b, w_ffn_in, ffn_conv_w, ffn_conv_b, w_ffn_out, ln3_g, ln3_b):
    x = jnp.concatenate([x_prompt.reshape(P_ROWS, D_MODEL), x_sample.reshape(S_ROWS, D_MODEL)], 0)
    xb = x.astype(BF16)
    memb = mem_prompt.reshape(BATCH * MEM_LEN, D_MODEL).astype(BF16)

    bias_a = _band_block_bias(functools.partial(_t5_bias, t5_table), SWA_BACK)
    qpos = PAST_LEN + jnp.arange(DEC_SEQ)
    la, lc = cache_swa_k.shape[2], cache_band_k.shape[2]
    rel_ac = (PAST_LEN - la + jnp.arange(la))[None, :] - qpos[:, None]
    rel_cc = (PAST_LEN - lc + jnp.arange(lc))[None, :] - qpos[:, None]
    rel_n = qpos[None, :] - qpos[:, None]
    bias_a_c, bias_a_n = _t5_bias(t5_table, rel_ac), _t5_bias(t5_table, rel_n)
    cos_p, sin_p = _rope_tables(jnp.arange(SEQ))
    cos_s, sin_s = _rope_tables(qpos)
    zero_state = jnp.zeros((BATCH, 4, LANES, LANES), F32)
    no_sink = jnp.zeros((16,), F32)

    outs = {k: [] for k in ("p_ak", "p_av", "p_rs", "p_bk", "p_bv", "p_fc", "p_mk", "p_mv",
                            "s_ak", "s_av", "s_rs", "s_bk", "s_bv", "s_fc")}
    for l in range(DEPTH):
        proj = matmul(xb, _permute_w_in(w_in[l]), tm=1056, tn=1536, out_dtype=BF16)
        bias_c = _band_block_bias(functools.partial(_clipped_bias, band_rel_table[l]), BAND_BACK)
        oa_p = band_attention_prompt(proj, bias_a, swa_sink[l], q_col=COL_QA, k_col=COL_KA2, v_col=COL_VA2,
                                     shared_kv=True, n_back=SWA_BACK, has_sink=True)
        oc_p = band_attention_prompt(proj, bias_c, no_sink, q_col=COL_QC, k_col=COL_KC, v_col=COL_VC,
                                     shared_kv=False, n_back=BAND_BACK, has_sink=False)
        ob_p, rs_p = retention(proj, cos_p, sin_p, zero_state, row0=0, n_seq=BATCH, seq_len=SEQ,
                               blk_len=RET_L, pos_per_blk=True)
        oa_s = cached_attention_sample(
            proj, _dup_groups(cache_swa_k[l]), _dup_groups(cache_swa_v[l]), bias_a_c, bias_a_n, swa_sink[l],
            q_col=COL_QA, k_col=COL_KA2, v_col=COL_VA2, shared_kv=True, has_sink=True)
        oc_s = cached_attention_sample(
            proj, cache_band_k[l].reshape(DEC_BATCH, lc, 1024), cache_band_v[l].reshape(DEC_BATCH, lc, 1024),
            _clipped_bias(band_rel_table[l], rel_cc), _clipped_bias(band_rel_table[l], rel_n), no_sink,
            q_col=COL_QC, k_col=COL_KC, v_col=COL_VC, shared_kv=False, has_sink=False)
        ob_s, rs_s = retention(proj, cos_s, sin_s, state_ret[l].reshape(DEC_BATCH, 4, LANES, LANES),
                               row0=P_ROWS, n_seq=DEC_BATCH, seq_len=DEC_SEQ, blk_len=DEC_SEQ,
                               pos_per_blk=False)
        oa = jnp.concatenate([oa_p, oa_s], 0)
        ob = jnp.concatenate([ob_p, ob_s], 0)
        oc = jnp.concatenate([oc_p, oc_s], 0)
        mix = gated_branch_sum(oa, ob, oc, w_br_a[l].astype(BF16), w_br_b[l].astype(BF16),
                               w_br_c[l].astype(BF16), proj)
        x, xb = matmul_residual_ln(mix, w_mix_o[l].astype(BF16), x, ln1_g[l], ln1_b[l], tk=1024)

        mkv = matmul(memb, jnp.concatenate([w_mk[l], w_mv[l]], 1).astype(BF16), tm=512, tn=1024, out_dtype=F32)
        qm = matmul(xb, w_mq[l].astype(BF16), tm=1056, tn=2048, out_dtype=BF16)
        mkv3 = mkv.reshape(BATCH, MEM_LEN, 2 * D_MODEL)
        om_p = mem_attention(qm, mkv3, mkv3, row0=0, n_rows=P_ROWS, tm=1024, rows_per_kv=SEQ,
                             k_colblk=0, v_colblk=MEM_HEADS)
        om_s = mem_attention(qm, cache_mem_k[l].reshape(DEC_BATCH, MEM_LEN, D_MODEL),
                             cache_mem_v[l].reshape(DEC_BATCH, MEM_LEN, D_MODEL),
                             row0=P_ROWS, n_rows=S_ROWS, tm=DEC_SEQ, rows_per_kv=DEC_SEQ,
                             k_colblk=0, v_colblk=0)
        om = jnp.concatenate([om_p, om_s], 0)
        x, xb = matmul_residual_ln(om, w_mo[l].astype(BF16), x, ln2_g[l], ln2_b[l], tk=1024)

        wfi = w_ffn_in[l].astype(BF16)
        h_p, tg_p, tv_p = ffn_in(xb, wfi, ffn_conv_w[l], ffn_conv_b[l], row0=0, n_rows=P_ROWS,
                                 tm=1024, tn=512, tail=8)
        st = state_ffn_conv[l]
        fix1 = jnp.pad(st[:, 1:2], ((0, 0), (0, DEC_SEQ - 1), (0, 0))).reshape(S_ROWS, 2 * D_FF)
        fix2 = jnp.pad(st, ((0, 0), (0, DEC_SEQ - 2), (0, 0))).reshape(S_ROWS, 2 * D_FF)
        h_s, tg_s, tv_s = ffn_in(xb, wfi, ffn_conv_w[l], ffn_conv_b[l], row0=P_ROWS, n_rows=S_ROWS,
                                 tm=S_ROWS, tn=512, tail=S_ROWS, fix1=fix1, fix2=fix2)
        h = jnp.concatenate([h_p, h_s], 0)
        x, xb = matmul_residual_ln(h, w_ffn_out[l].astype(BF16), x, ln3_g[l], ln3_b[l], tk=512)

        pf = proj[:P_ROWS].reshape(BATCH, SEQ, PROJ_COLS)
        sf = proj[P_ROWS:]
        la_p, lc_p = min(SWA_BACK * CHUNK, SEQ), min(BAND_BACK * CHUNK, SEQ)
        tail_a = pf[:, SEQ - la_p:].reshape(BATCH * la_p, PROJ_COLS)
        outs["p_ak"].append(_undup(tail_a[:, COL_KA2:COL_KA2 + 256]).reshape(BATCH, la_p, 2, 64).astype(F32))
        outs["p_av"].append(_undup(tail_a[:, COL_VA2:COL_VA2 + 256]).reshape(BATCH, la_p, 2, 64).astype(F32))
        outs["p_rs"].append(rs_p.reshape(BATCH, RET_HEADS, RET_DK, RET_DV))
        outs["p_bk"].append(pf[:, SEQ - lc_p:, COL_KC:COL_KC + 1024].reshape(BATCH, lc_p, 16, 64).astype(F32))
        outs["p_bv"].append(pf[:, SEQ - lc_p:, COL_VC:COL_VC + 1024].reshape(BATCH, lc_p, 16, 64).astype(F32))
        last = [(b + 1) * (SEQ // 1024) - 1 for b in range(BATCH)]
        outs["p_fc"].append(jnp.stack(
            [jnp.concatenate([tg_p[t, 6:8], tv_p[t, 6:8]], -1) for t in last], 0))
        outs["p_mk"].append(mkv[:, :D_MODEL].reshape(BATCH, MEM_LEN, MEM_HEADS, MEM_HD))
        outs["p_mv"].append(mkv[:, D_MODEL:].reshape(BATCH, MEM_LEN, MEM_HEADS, MEM_HD))
        outs["s_ak"].append(_undup(sf[:, COL_KA2:COL_KA2 + 256]).reshape(DEC_BATCH, DEC_SEQ, 2, 64).astype(F32))
        outs["s_av"].append(_undup(sf[:, COL_VA2:COL_VA2 + 256]).reshape(DEC_BATCH, DEC_SEQ, 2, 64).astype(F32))
        outs["s_rs"].append(rs_s.reshape(DEC_BATCH, RET_HEADS, RET_DK, RET_DV))
        outs["s_bk"].append(sf[:, COL_KC:COL_KC + 1024].reshape(DEC_BATCH, DEC_SEQ, 16, 64).astype(F32))
        outs["s_bv"].append(sf[:, COL_VC:COL_VC + 1024].reshape(DEC_BATCH, DEC_SEQ, 16, 64).astype(F32))
        u_s = jnp.concatenate([tg_s[0], tv_s[0]], -1).reshape(DEC_BATCH, DEC_SEQ, 2 * D_FF)
        outs["s_fc"].append(u_s[:, DEC_SEQ - 2:])

    st = lambda name: jnp.stack(outs[name], 0)
    return (x[:P_ROWS].reshape(BATCH, SEQ, D_MODEL), x[P_ROWS:].reshape(DEC_BATCH, DEC_SEQ, D_MODEL),
            st("p_ak"), st("p_av"), st("p_rs"), st("p_bk"), st("p_bv"), st("p_fc"), st("p_mk"), st("p_mv"),
            st("s_ak"), st("s_av"), st("s_rs"), st("s_bk"), st("s_bv"), st("s_fc"))
```

```python
import functools
import math

import numpy as np
import jax
import jax.numpy as jnp
from jax import lax
from jax.experimental import pallas as pl
from jax.experimental.pallas import tpu as pltpu

F32 = jnp.float32
BF16 = jnp.bfloat16

D_MODEL = 2048
BATCH = 2
SEQ = 4096
DEPTH = 2
DEC_BATCH = 16
DEC_SEQ = 16
PAST_LEN = 2048
CHUNK = 64
HEAD_DIM = 64
SWA_BACK = 2
SWA_HEADS = 16
SWA_KV_HEADS = 2
T5_BUCKETS = 32
T5_MAX_DIST = 128
RET_HEADS = 8
RET_DK = 64
RET_DV = 128
ROPE_BASE = 10000.0
RET_NORM_EPS = 1e-5
BAND_BACK = 8
BAND_HEADS = 16
BAND_MAX_REL = 256
MEM_LEN = 256
MEM_HEADS = 4
MEM_HD = D_MODEL // MEM_HEADS
D_FF = 5632
CONV_W = 3
DN_ALPHA = (2 * DEPTH) ** 0.25
LN_EPS = 1e-5

P_ROWS = BATCH * SEQ
S_ROWS = DEC_BATCH * DEC_SEQ
ROWS = P_ROWS + S_ROWS

COL_QA = 0
COL_QC = 1024
COL_KC = 2048
COL_VC = 3072
COL_VB = 4096
COL_GR = 5120
COL_GA = 6144
COL_GB = 8192
COL_GC = 10240
COL_QB = 12288
COL_KB = 12800
COL_KA2 = 13312
COL_VA2 = 13568
PROJ_COLS = 13824

LANES = 128
QB = 256
RET_L = 256
NEG = -1e30
VMEM_LIMIT = 48 * 1024 * 1024

NT = (((1,), (1,)), ((), ()))
TN = (((0,), (0,)), ((), ()))


def _params(sem, vmem=VMEM_LIMIT):
    return pltpu.CompilerParams(dimension_semantics=sem, vmem_limit_bytes=vmem)


def _mm_kernel(a_ref, b_ref, o_ref):
    o_ref[...] = jnp.dot(a_ref[...], b_ref[...], preferred_element_type=F32).astype(o_ref.dtype)


def matmul(a, b, *, tm, tn, out_dtype):
    m, k = a.shape
    n = b.shape[1]
    return pl.pallas_call(
        _mm_kernel,
        grid=(m // tm, n // tn),
        in_specs=[pl.BlockSpec((tm, k), lambda i, j: (i, 0)),
                  pl.BlockSpec((k, tn), lambda i, j: (0, j))],
        out_specs=pl.BlockSpec((tm, tn), lambda i, j: (i, j)),
        out_shape=jax.ShapeDtypeStruct((m, n), out_dtype),
        compiler_params=_params(("parallel", "parallel")),
        name="matmul",
    )(a, b)


def _gate_mm_kernel(oa_ref, ob_ref, oc_ref, wa_ref, wb_ref, wc_ref, ga_ref, gb_ref, gc_ref, o_ref):
    acc = None
    for o, w, g in ((oa_ref, wa_ref, ga_ref), (ob_ref, wb_ref, gb_ref), (oc_ref, wc_ref, gc_ref)):
        t = jnp.dot(o[...], w[...], preferred_element_type=F32) * jax.nn.sigmoid(g[...].astype(F32))
        acc = t if acc is None else acc + t
    o_ref[...] = acc.astype(o_ref.dtype)


def gated_branch_sum(oa, ob, oc, wa, wb, wc, proj, *, tm=528, tn=1024):
    m, k = oa.shape
    n = wa.shape[1]
    o_spec = pl.BlockSpec((tm, k), lambda i, j: (i, 0))
    w_spec = pl.BlockSpec((k, tn), lambda i, j: (0, j))

    def g_spec(col):
        return pl.BlockSpec((tm, tn), lambda i, j: (i, col // tn + j))

    return pl.pallas_call(
        _gate_mm_kernel,
        grid=(m // tm, n // tn),
        in_specs=[o_spec, o_spec, o_spec, w_spec, w_spec, w_spec,
                  g_spec(COL_GA), g_spec(COL_GB), g_spec(COL_GC)],
        out_specs=pl.BlockSpec((tm, tn), lambda i, j: (i, j)),
        out_shape=jax.ShapeDtypeStruct((m, n), BF16),
        compiler_params=_params(("parallel", "parallel")),
        name="gated_branch_sum",
    )(oa, ob, oc, wa, wb, wc, proj, proj, proj)


def _mm_ln_kernel(a_ref, b_ref, r_ref, g_ref, bt_ref, of_ref, ob_ref, *, nk):
    part = jnp.dot(a_ref[...], b_ref[...], preferred_element_type=F32)

    def finish(acc):
        y = DN_ALPHA * r_ref[...] + acc
        mu = jnp.mean(y, -1, keepdims=True)
        d = y - mu
        var = jnp.mean(d * d, -1, keepdims=True)
        out = d * lax.rsqrt(var + LN_EPS) * g_ref[...] + bt_ref[...]
        of_ref[...] = out
        ob_ref[...] = out.astype(BF16)

    if nk == 1:
        finish(part)
        return
    k = pl.program_id(1)

    @pl.when(k == 0)
    def _():
        of_ref[...] = part

    @pl.when((k > 0) & (k < nk - 1))
    def _():
        of_ref[...] += part

    @pl.when(k == nk - 1)
    def _():
        finish(of_ref[...] + part)


def matmul_residual_ln(a, b, resid, gain, bias, *, tm, tk):
    m, kk = a.shape
    n = b.shape[1]
    nk = kk // tk
    b_mode = dict(pipeline_mode=pl.Buffered(1)) if nk == 1 else {}
    return pl.pallas_call(
        functools.partial(_mm_ln_kernel, nk=nk),
        grid=(m // tm, nk),
        in_specs=[pl.BlockSpec((tm, tk), lambda i, k: (i, k)),
                  pl.BlockSpec((tk, n), lambda i, k: (k, 0), **b_mode),
                  pl.BlockSpec((tm, n), lambda i, k: (i, 0)),
                  pl.BlockSpec((1, n), lambda i, k: (0, 0)),
                  pl.BlockSpec((1, n), lambda i, k: (0, 0))],
        out_specs=[pl.BlockSpec((tm, n), lambda i, k: (i, 0)),
                   pl.BlockSpec((tm, n), lambda i, k: (i, 0))],
        out_shape=[jax.ShapeDtypeStruct((m, n), F32), jax.ShapeDtypeStruct((m, n), BF16)],
        compiler_params=_params(("parallel", "arbitrary")),
        name="matmul_residual_ln",
    )(a, b, resid, gain.reshape(1, n), bias.reshape(1, n))


def _softmax_pv(pieces, sink):
    m = None
    for s, _ in pieces:
        mi = jnp.max(s, -1, keepdims=True)
        m = mi if m is None else jnp.maximum(m, mi)
    if sink is not None:
        m = jnp.maximum(m, sink)
    den = None
    acc = None
    for s, v in pieces:
        p = jnp.exp(s - m)
        di = jnp.sum(p, -1, keepdims=True)
        den = di if den is None else den + di
        oi = jnp.dot(p.astype(BF16), v, preferred_element_type=F32)
        acc = oi if acc is None else acc + oi
    if sink is not None:
        den = den + jnp.exp(sink - m)
    return acc / den


def _head_lanes(j):
    lane = lax.broadcasted_iota(jnp.int32, (1, LANES), 1)
    return (lane < HEAD_DIM) if j == 0 else (lane >= HEAD_DIM)


def _band_attn_kernel(sink_ref, diag_ref, q_ref, *refs, n_prev_blocks, n_back, shared_kv, has_sink):
    nk = n_prev_blocks + 1
    k_refs = refs[:nk]
    v_refs = refs[nk:2 * nk]
    o_ref = refs[2 * nk]
    bias_ref = refs[2 * nk + 1]
    b = pl.program_id(0)
    i = pl.program_id(1)
    n_prev_rows = n_back * CHUNK
    kw = n_prev_rows + QB

    @pl.when((b == 0) & (i == 0))
    def _():
        n = diag_ref.shape[2]
        qc = lax.broadcasted_iota(jnp.int32, (QB, kw), 0) // CHUNK
        cb = lax.broadcasted_iota(jnp.int32, (QB, kw), 1) // CHUNK
        allowed = (cb >= qc) & (cb - n_back <= qc)
        for h in range(16):
            t = pltpu.roll(jnp.broadcast_to(diag_ref[h, 0:1, :], (QB, n)), 0, 1, stride=1, stride_axis=0)
            bias_ref[h] = jnp.where(allowed, t[:, :kw], NEG)

    col = lax.broadcasted_iota(jnp.int32, (1, kw), 1)
    col_ok = col >= jnp.maximum(n_prev_rows - i * QB, 0)
    for p in range(8):
        c0 = (p // 4 if shared_kv else p) * LANES
        q = q_ref[:, p * LANES:(p + 1) * LANES]
        k_all = jnp.concatenate([r[:, c0:c0 + LANES] for r in k_refs], axis=0)
        v_all = jnp.concatenate([r[:, c0:c0 + LANES] for r in v_refs], axis=0)
        outs = []
        for j in range(2):
            h = 2 * p + j
            qj = jnp.where(_head_lanes(j), q, jnp.zeros_like(q)) * (HEAD_DIM ** -0.5)
            s = lax.dot_general(qj, k_all, NT, preferred_element_type=F32) + bias_ref[h]
            s = jnp.where(col_ok, s, NEG)
            sink = sink_ref[h] if has_sink else None
            outs.append(_softmax_pv([(s, v_all)], sink))
        o_ref[:, p * LANES:(p + 1) * LANES] = jnp.where(_head_lanes(0), outs[0], outs[1]).astype(o_ref.dtype)


def band_attention_prompt(proj, diag, sink, *, q_col, k_col, v_col, shared_kv, n_back, has_sink):
    n_prev_rows = n_back * CHUNK
    nqb = SEQ // QB
    if n_prev_rows >= QB:
        n_prev_blocks, pb = n_prev_rows // QB, QB
    else:
        n_prev_blocks, pb = 1, n_prev_rows
    per = QB // pb
    kvw = 256 if shared_kv else 1024

    def prev_spec(col, back):
        return pl.BlockSpec(
            (pb, kvw), lambda b, i: (b * (SEQ // pb) + jnp.maximum(i * per - back, 0), col // kvw))

    def own_spec(col):
        return pl.BlockSpec((QB, kvw), lambda b, i: (b * nqb + i, col // kvw))

    k_specs = [prev_spec(k_col, n_prev_blocks - t) for t in range(n_prev_blocks)] + [own_spec(k_col)]
    v_specs = [prev_spec(v_col, n_prev_blocks - t) for t in range(n_prev_blocks)] + [own_spec(v_col)]
    kern = functools.partial(_band_attn_kernel, n_prev_blocks=n_prev_blocks, n_back=n_back,
                             shared_kv=shared_kv, has_sink=has_sink)
    n_in = 2 * (n_prev_blocks + 1)
    return pl.pallas_call(
        kern,
        grid=(BATCH, nqb),
        in_specs=[pl.BlockSpec(memory_space=pltpu.SMEM),
                  pl.BlockSpec(diag.shape, lambda b, i: (0, 0, 0)),
                  pl.BlockSpec((QB, 1024), lambda b, i: (b * nqb + i, q_col // 1024))]
                 + k_specs + v_specs,
        out_specs=pl.BlockSpec((QB, 1024), lambda b, i: (b * nqb + i, 0)),
        out_shape=jax.ShapeDtypeStruct((ROWS, 16 * HEAD_DIM), BF16),
        scratch_shapes=[pltpu.VMEM((16, QB, n_prev_rows + QB), F32)],
        compiler_params=_params(("arbitrary", "arbitrary")),
        name="band_attention_prompt",
    )(sink, diag, proj, *([proj] * n_in))


def _cached_attn_kernel(sink_ref, q_ref, kn_ref, vn_ref, kc_ref, vc_ref, bc_ref, bn_ref, _, o_ref,
                        *, shared_kv, has_sink):
    for p in range(8):
        c0 = (p // 4 if shared_kv else p) * LANES
        q = q_ref[:, p * LANES:(p + 1) * LANES]
        kc = kc_ref[0, :, c0:c0 + LANES].astype(BF16)
        vc = vc_ref[0, :, c0:c0 + LANES].astype(BF16)
        kn = kn_ref[:, c0:c0 + LANES]
        vn = vn_ref[:, c0:c0 + LANES]
        outs = []
        for j in range(2):
            h = 2 * p + j
            qj = jnp.where(_head_lanes(j), q, jnp.zeros_like(q)) * (HEAD_DIM ** -0.5)
            s_c = lax.dot_general(qj, kc, NT, preferred_element_type=F32) + bc_ref[h]
            s_n = lax.dot_general(qj, kn, NT, preferred_element_type=F32) + bn_ref[h]
            sink = sink_ref[h] if has_sink else None
            outs.append(_softmax_pv([(s_c, vc), (s_n, vn)], sink))
        o_ref[:, p * LANES:(p + 1) * LANES] = jnp.where(_head_lanes(0), outs[0], outs[1]).astype(o_ref.dtype)


def cached_attention_sample(proj, k_cache, v_cache, bias_c, bias_n, sink, o_all, *, q_col, k_col, v_col,
                            shared_kv, has_sink):
    lc, wc = k_cache.shape[1], k_cache.shape[2]
    r0 = P_ROWS // DEC_SEQ
    kern = functools.partial(_cached_attn_kernel, shared_kv=shared_kv, has_sink=has_sink)
    return pl.pallas_call(
        kern,
        grid=(DEC_BATCH,),
        in_specs=[pl.BlockSpec(memory_space=pltpu.SMEM),
                  pl.BlockSpec((DEC_SEQ, 1024), lambda s: (r0 + s, q_col // 1024)),
                  pl.BlockSpec((DEC_SEQ, wc), lambda s: (r0 + s, k_col // wc)),
                  pl.BlockSpec((DEC_SEQ, wc), lambda s: (r0 + s, v_col // wc)),
                  pl.BlockSpec((1, lc, wc), lambda s: (s, 0, 0)),
                  pl.BlockSpec((1, lc, wc), lambda s: (s, 0, 0)),
                  pl.BlockSpec((16, DEC_SEQ, lc), lambda s: (0, 0, 0)),
                  pl.BlockSpec((16, DEC_SEQ, DEC_SEQ), lambda s: (0, 0, 0)),
                  pl.BlockSpec(memory_space=pl.ANY)],
        out_specs=pl.BlockSpec((DEC_SEQ, 1024), lambda s: (r0 + s, 0)),
        out_shape=jax.ShapeDtypeStruct(o_all.shape, o_all.dtype),
        input_output_aliases={8: 0},
        compiler_params=_params(("parallel",)),
        name="cached_attention_sample",
    )(sink, proj, proj, proj, k_cache, v_cache, bias_c, bias_n, o_all)


def _ret_kernel(q_ref, k_ref, v_ref, g_ref, cos_ref, sin_ref, s0_ref, *refs, blk_len):
    o_ref, sout_ref, st_ref = refs[-3:]
    blk = pl.program_id(1)

    @pl.when(blk == 0)
    def _():
        st_ref[...] = s0_ref[0]

    cos = cos_ref[...]
    sin = sin_ref[...]
    lane = lax.broadcasted_iota(jnp.int32, (1, LANES), 1)
    low_half = (lane % RET_DK) < (RET_DK // 2)

    def rope(x):
        x = x.astype(F32)
        swapped = jnp.where(low_half, pltpu.roll(x, LANES - RET_DK // 2, 1), pltpu.roll(x, RET_DK // 2, 1))
        return x * cos + swapped * sin

    ii = lax.broadcasted_iota(jnp.int32, (blk_len, blk_len), 0)
    jj = lax.broadcasted_iota(jnp.int32, (blk_len, blk_len), 1)
    diff = (ii - jj).astype(F32)
    row = lax.broadcasted_iota(jnp.int32, (blk_len, 1), 0).astype(F32)
    srow = lax.broadcasted_iota(jnp.int32, (LANES, 1), 0)

    for p in range(RET_HEADS // 2):
        qr = rope(q_ref[:, p * LANES:(p + 1) * LANES])
        kr = rope(k_ref[:, p * LANES:(p + 1) * LANES]) * (RET_DK ** -0.5)
        kb = kr.astype(BF16)
        state = st_ref[p]
        state_b = state.astype(BF16)
        upd = None
        log_gs = []
        for j in range(2):
            h = 2 * p + j
            log_g = math.log(1.0 - 2.0 ** (-5.0 - h))
            log_gs.append(log_g)
            decay = jnp.where(diff >= 0, jnp.exp(log_g * jnp.maximum(diff, 0.0)), 0.0)
            qj = jnp.where(_head_lanes(j), qr, 0.0).astype(BF16)
            vh = v_ref[:, h * RET_DV:(h + 1) * RET_DV]
            qk = lax.dot_general(qj, kb, NT, preferred_element_type=F32) * decay
            o = jnp.dot(qk.astype(BF16), vh, preferred_element_type=F32)
            o = o + jnp.dot(qj, state_b, preferred_element_type=F32) * jnp.exp(log_g * (row + 1.0))
            mu = jnp.mean(o, -1, keepdims=True)
            d = o - mu
            var = jnp.mean(d * d, -1, keepdims=True)
            gate = g_ref[:, h * RET_DV:(h + 1) * RET_DV].astype(F32)
            o_ref[:, h * RET_DV:(h + 1) * RET_DV] = (
                d * lax.rsqrt(var + RET_NORM_EPS) * (gate * jax.nn.sigmoid(gate))).astype(o_ref.dtype)
            kwj = jnp.where(_head_lanes(j), kr * jnp.exp(log_g * (blk_len - 1.0 - row)), 0.0).astype(BF16)
            u = lax.dot_general(kwj, vh, TN, preferred_element_type=F32)
            upd = u if upd is None else upd + u
        carry = jnp.where(srow < RET_DK, math.exp(log_gs[0] * blk_len), math.exp(log_gs[1] * blk_len))
        st_ref[p] = carry * state + upd

    @pl.when(blk == pl.num_programs(1) - 1)
    def _():
        sout_ref[0] = st_ref[...]


def retention(proj, cos, sin, s0, o_all=None, *, row0, n_seq, seq_len, blk_len, pos_per_blk):
    nb = seq_len // blk_len
    rb0 = row0 // blk_len

    def rows(col, width):
        return pl.BlockSpec((blk_len, width), lambda b, t: (rb0 + b * nb + t, col // width))

    tab = pl.BlockSpec((blk_len, LANES), lambda b, t: (t if pos_per_blk else 0, 0))
    st = pl.BlockSpec((1, 4, LANES, LANES), lambda b, t: (b, 0, 0, 0))
    in_specs = [rows(COL_QB, 512), rows(COL_KB, 512), rows(COL_VB, 1024), rows(COL_GR, 1024), tab, tab, st]
    args = [proj, proj, proj, proj, cos, sin, s0]
    aliases = {}
    if o_all is not None:
        in_specs.append(pl.BlockSpec(memory_space=pl.ANY))
        args.append(o_all)
        aliases = {len(args) - 1: 0}
    return pl.pallas_call(
        functools.partial(_ret_kernel, blk_len=blk_len),
        grid=(n_seq, nb),
        in_specs=in_specs,
        out_specs=[rows(0, 1024), st],
        out_shape=[jax.ShapeDtypeStruct((ROWS, 1024), BF16),
                   jax.ShapeDtypeStruct((n_seq, 4, LANES, LANES), F32)],
        scratch_shapes=[pltpu.VMEM((4, LANES, LANES), F32)],
        input_output_aliases=aliases,
        compiler_params=_params(("parallel", "arbitrary")),
        name="retention",
    )(*args)


def _mem_attn_kernel(q_ref, k_ref, v_ref, *refs, heads):
    o_ref = refs[-1]
    for h in range(heads):
        cols = slice(h * MEM_HD, (h + 1) * MEM_HD)
        k = k_ref[0, :, cols].astype(BF16)
        v = v_ref[0, :, cols].astype(BF16)
        s = lax.dot_general(q_ref[:, cols], k, NT, preferred_element_type=F32) * (MEM_HD ** -0.5)
        o_ref[:, cols] = _softmax_pv([(s, v)], None).astype(o_ref.dtype)


def mem_attention(qm, k_src, v_src, o_all=None, *, row0, n_rows, tm, heads, rows_per_kv, k_colblk, v_colblk):
    rb0 = row0 // tm
    w = heads * MEM_HD
    in_specs = [pl.BlockSpec((tm, w), lambda i, h: (rb0 + i, h)),
                pl.BlockSpec((1, MEM_LEN, w), lambda i, h: (i * tm // rows_per_kv, 0, k_colblk + h)),
                pl.BlockSpec((1, MEM_LEN, w), lambda i, h: (i * tm // rows_per_kv, 0, v_colblk + h))]
    args = [qm, k_src, v_src]
    aliases = {}
    if o_all is not None:
        in_specs.append(pl.BlockSpec(memory_space=pl.ANY))
        args.append(o_all)
        aliases = {3: 0}
    return pl.pallas_call(
        functools.partial(_mem_attn_kernel, heads=heads),
        grid=(n_rows // tm, MEM_HEADS // heads),
        in_specs=in_specs,
        out_specs=pl.BlockSpec((tm, w), lambda i, h: (rb0 + i, h)),
        out_shape=jax.ShapeDtypeStruct((ROWS, D_MODEL), BF16),
        input_output_aliases=aliases,
        compiler_params=_params(("parallel", "parallel")),
        name="mem_attention",
    )(*args)


def _gelu(x):
    return 0.5 * x * (1.0 + lax.erf(x * (2.0 ** -0.5)))


def _ffn_in_kernel(*refs, sample, seq_tiles, tail):
    h_ref, tg_ref, tv_ref = refs[-3:]
    if sample:
        a_ref, bg_ref, bv_ref, wg_ref, wv_ref, cbg_ref, cbv_ref, f1g_ref, f1v_ref, f2g_ref, f2v_ref = refs[:11]
    else:
        a_ref, ah_ref, bg_ref, bv_ref, wg_ref, wv_ref, cbg_ref, cbv_ref = refs[:8]
    tm = a_ref.shape[0]
    a = a_ref[...]
    if sample:
        pos = lax.broadcasted_iota(jnp.int32, (tm, 1), 0) % DEC_SEQ
    else:
        top = lax.broadcasted_iota(jnp.int32, (8, 1), 0)
        keep = (pl.program_id(0) % seq_tiles != 0).astype(F32)

    def conv(b_ref, w_ref, cb_ref, f1_ref, f2_ref, t_ref):
        u = jnp.dot(a, b_ref[...], preferred_element_type=F32)
        t_ref[0] = u[tm - tail:, :]
        r1 = pltpu.roll(u, 1, 0)
        r2 = pltpu.roll(u, 2, 0)
        if sample:
            u1 = jnp.where(pos < 1, f1_ref[...], r1)
            u2 = jnp.where(pos < 2, f2_ref[...], r2)
        else:
            uh = jnp.dot(ah_ref[...], b_ref[...], preferred_element_type=F32) * keep
            n = uh.shape[0]
            prev1, prev2 = uh[n - 1:n, :], uh[n - 2:n - 1, :]
            u1 = jnp.concatenate([jnp.where(top < 1, prev1, r1[:8]), r1[8:]], 0)
            top2 = jnp.where(top < 1, prev2, jnp.where(top < 2, prev1, r2[:8]))
            u2 = jnp.concatenate([top2, r2[8:]], 0)
        w = w_ref[...]
        return w[0:1, :] * u2 + w[1:2, :] * u1 + w[2:3, :] * u + cb_ref[...]

    if sample:
        cg = conv(bg_ref, wg_ref, cbg_ref, f1g_ref, f2g_ref, tg_ref)
        cv = conv(bv_ref, wv_ref, cbv_ref, f1v_ref, f2v_ref, tv_ref)
    else:
        cg = conv(bg_ref, wg_ref, cbg_ref, None, None, tg_ref)
        cv = conv(bv_ref, wv_ref, cbv_ref, None, None, tv_ref)
    h_ref[...] = (_gelu(cg) * cv).astype(h_ref.dtype)


def ffn_in(x, w, conv_w, conv_b, h_all=None, *, row0, n_rows, tm, tn, tail, fix1=None, fix2=None):
    sample = fix1 is not None
    k = x.shape[1]
    nj = D_FF // tn
    ni = n_rows // tm
    rb0 = row0 // tm
    halo = 16
    a_spec = pl.BlockSpec((tm, k), lambda i, j: (rb0 + i, 0))
    bg = pl.BlockSpec((k, tn), lambda i, j: (0, j))
    bv = pl.BlockSpec((k, tn), lambda i, j: (0, nj + j))
    wg = pl.BlockSpec((CONV_W, tn), lambda i, j: (0, j))
    wv = pl.BlockSpec((CONV_W, tn), lambda i, j: (0, nj + j))
    cg = pl.BlockSpec((1, tn), lambda i, j: (0, j))
    cv = pl.BlockSpec((1, tn), lambda i, j: (0, nj + j))
    conv_b2 = conv_b.reshape(1, 2 * D_FF)
    if sample:
        fg = pl.BlockSpec((tm, tn), lambda i, j: (i, j))
        fv = pl.BlockSpec((tm, tn), lambda i, j: (i, nj + j))
        in_specs = [a_spec, bg, bv, wg, wv, cg, cv, fg, fv, fg, fv]
        args = [x, w, w, conv_w, conv_w, conv_b2, conv_b2, fix1, fix1, fix2, fix2]
    else:
        ah = pl.BlockSpec((halo, k), lambda i, j: (jnp.maximum((rb0 + i) * (tm // halo) - 1, 0), 0))
        in_specs = [a_spec, ah, bg, bv, wg, wv, cg, cv]
        args = [x, x, w, w, conv_w, conv_w, conv_b2, conv_b2]
    aliases = {}
    if h_all is not None:
        in_specs.append(pl.BlockSpec(memory_space=pl.ANY))
        args.append(h_all)
        aliases = {len(args) - 1: 0}
    t_spec = pl.BlockSpec((1, tail, tn), lambda i, j: (i, 0, j))
    return pl.pallas_call(
        functools.partial(_ffn_in_kernel, sample=sample, seq_tiles=SEQ // tm if not sample else 1, tail=tail),
        grid=(ni, nj),
        in_specs=in_specs,
        out_specs=[pl.BlockSpec((tm, tn), lambda i, j: (rb0 + i, j)), t_spec, t_spec],
        out_shape=[jax.ShapeDtypeStruct((ROWS, D_FF), BF16),
                   jax.ShapeDtypeStruct((ni, tail, D_FF), F32),
                   jax.ShapeDtypeStruct((ni, tail, D_FF), F32)],
        input_output_aliases=aliases,
        compiler_params=_params(("parallel", "parallel")),
        name="ffn_in",
    )(*args)


def _t5_bucket(rel):
    nb = T5_BUCKETS // 2
    max_exact = nb // 2
    n = jnp.abs(rel)
    nf = jnp.maximum(n, 1).astype(F32)
    large = max_exact + (jnp.log(nf / max_exact) / math.log(T5_MAX_DIST / max_exact)
                         * (nb - max_exact)).astype(jnp.int32)
    large = jnp.minimum(large, nb - 1)
    return jnp.where(rel > 0, nb, 0) + jnp.where(n < max_exact, n, large)


def _t5_bias(table, rel):
    return jnp.transpose(table[_t5_bucket(rel)], (2, 0, 1)).astype(F32)


def _clipped_bias(table, rel):
    return table[:, jnp.clip(rel, -BAND_MAX_REL, BAND_MAX_REL) + BAND_MAX_REL].astype(F32)


DIAG_N = 1024


def _band_diag(bias_fn, n_back):
    kw = n_back * CHUNK + QB
    assert kw + QB - 1 <= DIAG_N
    k = np.arange(DIAG_N)
    rel = np.where(k < kw, k, k - DIAG_N) - n_back * CHUNK
    return jnp.broadcast_to(bias_fn(jnp.asarray(rel)[None, :]), (16, 8, DIAG_N))


def _rope_tables(pos):
    half = RET_DK // 2
    inv = ROPE_BASE ** (-jnp.arange(half, dtype=F32) / half)
    ang = pos.astype(F32)[:, None] * inv[None, :]
    cos, sin = jnp.cos(ang), jnp.sin(ang)
    cos_t = jnp.concatenate([cos, cos, cos, cos], -1)
    sin_t = jnp.concatenate([-sin, sin, -sin, sin], -1)
    return cos_t, sin_t


def _permute_w_in(w):
    seg = lambda off, width: w[:, off:off + width]
    qa, ka, va = seg(0, 1024), seg(1024, 128), seg(1152, 128)
    qb, kb, vb, gr = seg(1280, 512), seg(1792, 512), seg(2304, 1024), seg(3328, 1024)
    qc, kc, vc = seg(4352, 1024), seg(5376, 1024), seg(6400, 1024)
    ga, gb, gc = seg(7424, 2048), seg(9472, 2048), seg(11520, 2048)
    dup = lambda t: jnp.concatenate([t[:, :64], t[:, :64], t[:, 64:], t[:, 64:]], -1)
    return jnp.concatenate([qa, qc, kc, vc, vb, gr, ga, gb, gc, qb, kb, dup(ka), dup(va)], -1).astype(BF16)


def _dup_groups(t):
    g0, g1 = t[..., 0, :], t[..., 1, :]
    return jnp.concatenate([g0, g0, g1, g1], -1)


def _undup(t):
    return jnp.stack([t[:, 0:64], t[:, 128:192]], 1)


def kernel(x_prompt, x_sample, mem_prompt, cache_swa_k, cache_swa_v, state_ret, cache_band_k, cache_band_v, state_ffn_conv, cache_mem_k, cache_mem_v, w_in, t5_table, swa_sink, band_rel_table, w_br_a, w_br_b, w_br_c, w_mix_o, ln1_g, ln1_b, w_mq, w_mk, w_mv, w_mo, ln2_g, ln2_b, w_ffn_in, ffn_conv_w, ffn_conv_b, w_ffn_out, ln3_g, ln3_b):
    x = jnp.concatenate([x_prompt.reshape(P_ROWS, D_MODEL), x_sample.reshape(S_ROWS, D_MODEL)], 0)
    xb = x.astype(BF16)
    memb = mem_prompt.reshape(BATCH * MEM_LEN, D_MODEL).astype(BF16)

    diag_a = _band_diag(functools.partial(_t5_bias, t5_table), SWA_BACK)
    qpos = PAST_LEN + jnp.arange(DEC_SEQ)
    la, lc = cache_swa_k.shape[2], cache_band_k.shape[2]
    rel_ac = (PAST_LEN - la + jnp.arange(la))[None, :] - qpos[:, None]
    rel_cc = (PAST_LEN - lc + jnp.arange(lc))[None, :] - qpos[:, None]
    rel_n = qpos[None, :] - qpos[:, None]
    bias_a_c, bias_a_n = _t5_bias(t5_table, rel_ac), _t5_bias(t5_table, rel_n)
    cos_p, sin_p = _rope_tables(jnp.arange(SEQ))
    cos_s, sin_s = _rope_tables(qpos)
    zero_state = jnp.zeros((BATCH, 4, LANES, LANES), F32)
    no_sink = jnp.zeros((16,), F32)

    outs = {k: [] for k in ("p_ak", "p_av", "p_rs", "p_bk", "p_bv", "p_fc", "p_mk", "p_mv",
                            "s_ak", "s_av", "s_rs", "s_bk", "s_bv", "s_fc")}
    for l in range(DEPTH):
        proj = matmul(xb, _permute_w_in(w_in[l]), tm=1056, tn=1536, out_dtype=BF16)
        diag_c = _band_diag(functools.partial(_clipped_bias, band_rel_table[l]), BAND_BACK)
        oa = band_attention_prompt(proj, diag_a, swa_sink[l], q_col=COL_QA, k_col=COL_KA2, v_col=COL_VA2,
                                   shared_kv=True, n_back=SWA_BACK, has_sink=True)
        oc = band_attention_prompt(proj, diag_c, no_sink, q_col=COL_QC, k_col=COL_KC, v_col=COL_VC,
                                   shared_kv=False, n_back=BAND_BACK, has_sink=False)
        ob, rs_p = retention(proj, cos_p, sin_p, zero_state, row0=0, n_seq=BATCH, seq_len=SEQ,
                             blk_len=RET_L, pos_per_blk=True)
        oa = cached_attention_sample(
            proj, _dup_groups(cache_swa_k[l]), _dup_groups(cache_swa_v[l]), bias_a_c, bias_a_n, swa_sink[l], oa,
            q_col=COL_QA, k_col=COL_KA2, v_col=COL_VA2, shared_kv=True, has_sink=True)
        oc = cached_attention_sample(
            proj, cache_band_k[l].reshape(DEC_BATCH, lc, 1024), cache_band_v[l].reshape(DEC_BATCH, lc, 1024),
            _clipped_bias(band_rel_table[l], rel_cc), _clipped_bias(band_rel_table[l], rel_n), no_sink, oc,
            q_col=COL_QC, k_col=COL_KC, v_col=COL_VC, shared_kv=False, has_sink=False)
        ob, rs_s = retention(proj, cos_s, sin_s, state_ret[l].reshape(DEC_BATCH, 4, LANES, LANES), ob,
                             row0=P_ROWS, n_seq=DEC_BATCH, seq_len=DEC_SEQ, blk_len=DEC_SEQ,
                             pos_per_blk=False)
        mix = gated_branch_sum(oa, ob, oc, w_br_a[l].astype(BF16), w_br_b[l].astype(BF16),
                               w_br_c[l].astype(BF16), proj)
        x, xb = matmul_residual_ln(mix, w_mix_o[l].astype(BF16), x, ln1_g[l], ln1_b[l], tm=528, tk=D_MODEL)

        mkv = matmul(memb, jnp.concatenate([w_mk[l], w_mv[l]], 1).astype(BF16), tm=512, tn=1024, out_dtype=F32)
        qm = matmul(xb, w_mq[l].astype(BF16), tm=1056, tn=2048, out_dtype=BF16)
        mkv3 = mkv.reshape(BATCH, MEM_LEN, 2 * D_MODEL)
        om = mem_attention(qm, mkv3, mkv3, row0=0, n_rows=P_ROWS, tm=1024, heads=1, rows_per_kv=SEQ,
                           k_colblk=0, v_colblk=MEM_HEADS)
        om = mem_attention(qm, cache_mem_k[l].reshape(DEC_BATCH, MEM_LEN, D_MODEL),
                           cache_mem_v[l].reshape(DEC_BATCH, MEM_LEN, D_MODEL), om,
                           row0=P_ROWS, n_rows=S_ROWS, tm=DEC_SEQ, heads=MEM_HEADS, rows_per_kv=DEC_SEQ,
                           k_colblk=0, v_colblk=0)
        x, xb = matmul_residual_ln(om, w_mo[l].astype(BF16), x, ln2_g[l], ln2_b[l], tm=528, tk=D_MODEL)

        wfi = w_ffn_in[l].astype(BF16)
        h, tg_p, tv_p = ffn_in(xb, wfi, ffn_conv_w[l], ffn_conv_b[l], row0=0, n_rows=P_ROWS,
                               tm=1024, tn=512, tail=8)
        st = state_ffn_conv[l]
        fix1 = jnp.pad(st[:, 1:2], ((0, 0), (0, DEC_SEQ - 1), (0, 0))).reshape(S_ROWS, 2 * D_FF)
        fix2 = jnp.pad(st, ((0, 0), (0, DEC_SEQ - 2), (0, 0))).reshape(S_ROWS, 2 * D_FF)
        h, tg_s, tv_s = ffn_in(xb, wfi, ffn_conv_w[l], ffn_conv_b[l], h, row0=P_ROWS, n_rows=S_ROWS,
                               tm=S_ROWS, tn=512, tail=S_ROWS, fix1=fix1, fix2=fix2)
        x, xb = matmul_residual_ln(h, w_ffn_out[l].astype(BF16), x, ln3_g[l], ln3_b[l], tm=768, tk=512)

        sf = proj[P_ROWS:]
        la_p, lc_p = min(SWA_BACK * CHUNK, SEQ), min(BAND_BACK * CHUNK, SEQ)
        seq_tail = lambda n, c0, w: jnp.stack(
            [proj[(b + 1) * SEQ - n:(b + 1) * SEQ, c0:c0 + w] for b in range(BATCH)], 0)
        outs["p_ak"].append(_undup(seq_tail(la_p, COL_KA2, 256).reshape(BATCH * la_p, 256))
                            .reshape(BATCH, la_p, 2, 64).astype(F32))
        outs["p_av"].append(_undup(seq_tail(la_p, COL_VA2, 256).reshape(BATCH * la_p, 256))
                            .reshape(BATCH, la_p, 2, 64).astype(F32))
        outs["p_rs"].append(rs_p.reshape(BATCH, RET_HEADS, RET_DK, RET_DV))
        outs["p_bk"].append(seq_tail(lc_p, COL_KC, 1024).reshape(BATCH, lc_p, 16, 64).astype(F32))
        outs["p_bv"].append(seq_tail(lc_p, COL_VC, 1024).reshape(BATCH, lc_p, 16, 64).astype(F32))
        last = [(b + 1) * (SEQ // 1024) - 1 for b in range(BATCH)]
        outs["p_fc"].append(jnp.stack(
            [jnp.concatenate([tg_p[t, 6:8], tv_p[t, 6:8]], -1) for t in last], 0))
        outs["p_mk"].append(mkv[:, :D_MODEL].reshape(BATCH, MEM_LEN, MEM_HEADS, MEM_HD))
        outs["p_mv"].append(mkv[:, D_MODEL:].reshape(BATCH, MEM_LEN, MEM_HEADS, MEM_HD))
        outs["s_ak"].append(_undup(sf[:, COL_KA2:COL_KA2 + 256]).reshape(DEC_BATCH, DEC_SEQ, 2, 64).astype(F32))
        outs["s_av"].append(_undup(sf[:, COL_VA2:COL_VA2 + 256]).reshape(DEC_BATCH, DEC_SEQ, 2, 64).astype(F32))
        outs["s_rs"].append(rs_s.reshape(DEC_BATCH, RET_HEADS, RET_DK, RET_DV))
        outs["s_bk"].append(sf[:, COL_KC:COL_KC + 1024].reshape(DEC_BATCH, DEC_SEQ, 16, 64).astype(F32))
        outs["s_bv"].append(sf[:, COL_VC:COL_VC + 1024].reshape(DEC_BATCH, DEC_SEQ, 16, 64).astype(F32))
        u_s = jnp.concatenate([tg_s[0], tv_s[0]], -1).reshape(DEC_BATCH, DEC_SEQ, 2 * D_FF)
        outs["s_fc"].append(u_s[:, DEC_SEQ - 2:])

    st = lambda name: jnp.stack(outs[name], 0)
    return (x[:P_ROWS].reshape(BATCH, SEQ, D_MODEL), x[P_ROWS:].reshape(DEC_BATCH, DEC_SEQ, D_MODEL),
            st("p_ak"), st("p_av"), st("p_rs"), st("p_bk"), st("p_bv"), st("p_fc"), st("p_mk"), st("p_mv"),
            st("s_ak"), st("s_av"), st("s_rs"), st("s_bk"), st("s_bv"), st("s_fc"))
```

```python
import functools
import math

import numpy as np
import jax
import jax.numpy as jnp
from jax import lax
from jax.experimental import pallas as pl
from jax.experimental.pallas import tpu as pltpu

F32 = jnp.float32
BF16 = jnp.bfloat16

D_MODEL = 2048
BATCH = 2
SEQ = 4096
DEPTH = 2
DEC_BATCH = 16
DEC_SEQ = 16
PAST_LEN = 2048
CHUNK = 64
HEAD_DIM = 64
SWA_BACK = 2
SWA_HEADS = 16
SWA_KV_HEADS = 2
T5_BUCKETS = 32
T5_MAX_DIST = 128
RET_HEADS = 8
RET_DK = 64
RET_DV = 128
ROPE_BASE = 10000.0
RET_NORM_EPS = 1e-5
BAND_BACK = 8
BAND_HEADS = 16
BAND_MAX_REL = 256
MEM_LEN = 256
MEM_HEADS = 4
MEM_HD = D_MODEL // MEM_HEADS
D_FF = 5632
CONV_W = 3
DN_ALPHA = (2 * DEPTH) ** 0.25
LN_EPS = 1e-5

P_ROWS = BATCH * SEQ
S_ROWS = DEC_BATCH * DEC_SEQ
ROWS = P_ROWS + S_ROWS

COL_QA = 0
COL_QC = 1024
COL_KC = 2048
COL_VC = 3072
COL_VB = 4096
COL_GR = 5120
COL_GA = 6144
COL_GB = 8192
COL_GC = 10240
COL_QB = 12288
COL_KB = 12800
COL_KA2 = 13312
COL_VA2 = 13568
PROJ_COLS = 13824

LANES = 128
QB = 256
RET_L = 256
NEG = -1e30
VMEM_LIMIT = 48 * 1024 * 1024

NT = (((1,), (1,)), ((), ()))
TN = (((0,), (0,)), ((), ()))


def _params(sem, vmem=VMEM_LIMIT):
    return pltpu.CompilerParams(dimension_semantics=sem, vmem_limit_bytes=vmem)


def _w_in_permute_kernel(tbl_ref, src_ref, o_ref):
    mode = tbl_ref[1, pl.program_id(1)]
    t = src_ref[...]

    @pl.when(mode == 0)
    def _():
        o_ref[...] = t.astype(BF16)

    for m, half in ((1, t[:, :LANES]), (2, t[:, LANES:])):
        @pl.when(mode == m)
        def _():
            swapped = pltpu.roll(half, HEAD_DIM, 1)
            o_ref[...] = jnp.concatenate([jnp.where(_head_lanes(0), half, swapped),
                                          jnp.where(_head_lanes(0), swapped, half)], 1).astype(BF16)


def permute_w_in(w_in):
    blk = 256
    src = lambda off, width: list(range(off // blk, (off + width) // blk))
    order = (src(0, 1024) + src(4352, 1024) + src(5376, 1024) + src(6400, 1024) + src(2304, 1024)
             + src(3328, 1024) + src(7424, 2048) + src(9472, 2048) + src(11520, 2048) + src(1280, 512)
             + src(1792, 512))
    kv = 1024 // blk
    tbl = np.array([order + [kv, kv], [0] * len(order) + [1, 2]], np.int32)
    assert tbl.shape[1] * blk == PROJ_COLS
    d = w_in.shape[1]
    return pl.pallas_call(
        _w_in_permute_kernel,
        grid_spec=pltpu.PrefetchScalarGridSpec(
            num_scalar_prefetch=1,
            grid=(DEPTH, PROJ_COLS // blk),
            in_specs=[pl.BlockSpec((None, d, blk), lambda l, j, tbl: (l, 0, tbl[0, j]))],
            out_specs=pl.BlockSpec((None, d, blk), lambda l, j, tbl: (l, 0, j)),
        ),
        out_shape=jax.ShapeDtypeStruct((DEPTH, d, PROJ_COLS), BF16),
        compiler_params=_params(("parallel", "parallel")),
        name="permute_w_in",
    )(jnp.asarray(tbl), w_in)


def _mm_kernel(a_ref, b_ref, o_ref):
    o_ref[...] = jnp.dot(a_ref[...], b_ref[...], preferred_element_type=F32).astype(o_ref.dtype)


def matmul(a, b, layer, *, tm, tn, out_dtype):
    m, k = a.shape
    n = b.shape[2]
    return pl.pallas_call(
        _mm_kernel,
        grid=(m // tm, n // tn),
        in_specs=[pl.BlockSpec((tm, k), lambda i, j: (i, 0)),
                  pl.BlockSpec((None, k, tn), lambda i, j: (layer, 0, j))],
        out_specs=pl.BlockSpec((tm, tn), lambda i, j: (i, j)),
        out_shape=jax.ShapeDtypeStruct((m, n), out_dtype),
        compiler_params=_params(("parallel", "parallel")),
        name="matmul",
    )(a, b)


def _gate_mm_kernel(oa_ref, ob_ref, oc_ref, wa_ref, wb_ref, wc_ref, ga_ref, gb_ref, gc_ref, o_ref):
    acc = None
    for o, w, g in ((oa_ref, wa_ref, ga_ref), (ob_ref, wb_ref, gb_ref), (oc_ref, wc_ref, gc_ref)):
        t = jnp.dot(o[...], w[...], preferred_element_type=F32) * jax.nn.sigmoid(g[...].astype(F32))
        acc = t if acc is None else acc + t
    o_ref[...] = acc.astype(o_ref.dtype)


def gated_branch_sum(oa, ob, oc, wa, wb, wc, proj, layer, *, tm=528, tn=1024):
    m, k = oa.shape
    n = wa.shape[2]
    o_spec = pl.BlockSpec((tm, k), lambda i, j: (i, 0))
    w_spec = pl.BlockSpec((None, k, tn), lambda i, j: (layer, 0, j))

    def g_spec(col):
        return pl.BlockSpec((tm, tn), lambda i, j: (i, col // tn + j))

    return pl.pallas_call(
        _gate_mm_kernel,
        grid=(m // tm, n // tn),
        in_specs=[o_spec, o_spec, o_spec, w_spec, w_spec, w_spec,
                  g_spec(COL_GA), g_spec(COL_GB), g_spec(COL_GC)],
        out_specs=pl.BlockSpec((tm, tn), lambda i, j: (i, j)),
        out_shape=jax.ShapeDtypeStruct((m, n), BF16),
        compiler_params=_params(("parallel", "parallel")),
        name="gated_branch_sum",
    )(oa, ob, oc, wa, wb, wc, proj, proj, proj)


def _mm_ln_kernel(a_ref, b_ref, r_ref, g_ref, bt_ref, of_ref, ob_ref, *, nk):
    def finish(acc):
        y = DN_ALPHA * r_ref[...] + acc
        mu = jnp.mean(y, -1, keepdims=True)
        d = y - mu
        var = jnp.mean(d * d, -1, keepdims=True)
        out = d * lax.rsqrt(var + LN_EPS) * g_ref[...] + bt_ref[...]
        of_ref[...] = out
        ob_ref[...] = out.astype(BF16)

    if nk == 1:
        finish(jnp.dot(a_ref[...], b_ref[...], preferred_element_type=F32))
        return
    k = pl.program_id(1)

    @pl.when(k == 0)
    def _():
        of_ref[...] = jnp.zeros_like(of_ref)

    of_ref[...] += jnp.dot(a_ref[...], b_ref[...], preferred_element_type=F32)

    @pl.when(k == nk - 1)
    def _():
        finish(of_ref[...])


def matmul_residual_ln(a, b, layer, resid, gain, bias, *, tm, tk):
    m, kk = a.shape
    n = b.shape[2]
    nk = kk // tk
    b_mode = dict(pipeline_mode=pl.Buffered(1)) if nk == 1 else {}
    return pl.pallas_call(
        functools.partial(_mm_ln_kernel, nk=nk),
        grid=(m // tm, nk),
        in_specs=[pl.BlockSpec((tm, tk), lambda i, k: (i, k)),
                  pl.BlockSpec((None, tk, n), lambda i, k: (layer, k, 0), **b_mode),
                  pl.BlockSpec((tm, n), lambda i, k: (i, 0)),
                  pl.BlockSpec((None, 1, n), lambda i, k: (layer, 0, 0)),
                  pl.BlockSpec((None, 1, n), lambda i, k: (layer, 0, 0))],
        out_specs=[pl.BlockSpec((tm, n), lambda i, k: (i, 0)),
                   pl.BlockSpec((tm, n), lambda i, k: (i, 0))],
        out_shape=[jax.ShapeDtypeStruct((m, n), F32), jax.ShapeDtypeStruct((m, n), BF16)],
        compiler_params=_params(("parallel", "arbitrary")),
        name="matmul_residual_ln",
    )(a, b, resid, gain.reshape(DEPTH, 1, n), bias.reshape(DEPTH, 1, n))


def _softmax_pv(pieces, sink):
    m = None
    for s, _ in pieces:
        mi = jnp.max(s, -1, keepdims=True)
        m = mi if m is None else jnp.maximum(m, mi)
    if sink is not None:
        m = jnp.maximum(m, sink)
    den = None
    acc = None
    for s, v in pieces:
        p = jnp.exp(s - m)
        di = jnp.sum(p, -1, keepdims=True)
        den = di if den is None else den + di
        oi = jnp.dot(p.astype(BF16), v, preferred_element_type=F32)
        acc = oi if acc is None else acc + oi
    if sink is not None:
        den = den + jnp.exp(sink - m)
    return acc / den


def _head_lanes(j):
    lane = lax.broadcasted_iota(jnp.int32, (1, LANES), 1)
    return (lane < HEAD_DIM) if j == 0 else (lane >= HEAD_DIM)


def _band_attn_kernel(sink_ref, diag_ref, q_ref, *refs, n_prev_blocks, n_back, shared_kv, has_sink):
    nk = n_prev_blocks + 1
    k_refs = refs[:nk]
    v_refs = refs[nk:2 * nk]
    o_ref = refs[2 * nk]
    bias_ref = refs[2 * nk + 1]
    b = pl.program_id(0)
    i = pl.program_id(1)
    n_prev_rows = n_back * CHUNK
    kw = n_prev_rows + QB

    @pl.when((b == 0) & (i == 0))
    def _():
        n = diag_ref.shape[2]
        qc = lax.broadcasted_iota(jnp.int32, (QB, kw), 0) // CHUNK
        cb = lax.broadcasted_iota(jnp.int32, (QB, kw), 1) // CHUNK
        allowed = (cb >= qc) & (cb - n_back <= qc)
        for h in range(16):
            t = pltpu.roll(jnp.broadcast_to(diag_ref[h, 0:1, :], (QB, n)), 0, 1, stride=1, stride_axis=0)
            bias_ref[h] = jnp.where(allowed, t[:, :kw], NEG)

    krow = lax.broadcasted_iota(jnp.int32, (kw, 1), 0)
    kmask = jnp.where(krow >= jnp.maximum(n_prev_rows - i * QB, 0), 0.0, NEG).astype(BF16)
    lane = lax.broadcasted_iota(jnp.int32, (1, LANES), 1)
    for p in range(8):
        c0 = (p // 4 if shared_kv else p) * LANES
        q = q_ref[:, p * LANES:(p + 1) * LANES] * (HEAD_DIM ** -0.5)
        k_all = jnp.concatenate([r[:, c0:c0 + LANES] for r in k_refs], axis=0)
        v_all = jnp.concatenate([r[:, c0:c0 + LANES] for r in v_refs], axis=0)
        outs = []
        for j in range(2):
            h = 2 * p + j
            mask_lane = lane == (HEAD_DIM if j == 0 else 0)
            qj = jnp.where(mask_lane, jnp.ones_like(q), jnp.where(_head_lanes(j), q, jnp.zeros_like(q)))
            kj = jnp.where(mask_lane, kmask, k_all)
            s_all = lax.dot_general(qj, kj, NT, preferred_element_type=F32)
            sink = sink_ref[h] if has_sink else None
            ps, dens = [], []
            for r in range(QB // CHUNK):
                rows = slice(r * CHUNK, (r + 1) * CHUNK)
                lo = r * CHUNK // LANES * LANES
                hi = min(kw, -(-(r * CHUNK + n_prev_rows + CHUNK) // LANES) * LANES)
                s = s_all[rows, lo:hi] + bias_ref[h, rows, lo:hi]
                m = jnp.max(s, -1, keepdims=True)
                if has_sink:
                    m = jnp.maximum(m, sink)
                e = jnp.exp(s - m)
                den = jnp.sum(e, -1, keepdims=True)
                if has_sink:
                    den = den + jnp.exp(sink - m)
                parts = [e.astype(BF16)]
                if lo:
                    parts.insert(0, jnp.zeros((CHUNK, lo), BF16))
                if hi < kw:
                    parts.append(jnp.zeros((CHUNK, kw - hi), BF16))
                ps.append(jnp.concatenate(parts, 1))
                dens.append(den)
            pv = jnp.dot(jnp.concatenate(ps, 0), v_all, preferred_element_type=F32)
            outs.append(pv / jnp.concatenate(dens, 0))
        o_ref[:, p * LANES:(p + 1) * LANES] = jnp.where(_head_lanes(0), outs[0], outs[1]).astype(o_ref.dtype)


def band_attention_prompt(proj, diag, sink, *, q_col, k_col, v_col, shared_kv, n_back, has_sink):
    n_prev_rows = n_back * CHUNK
    nqb = SEQ // QB
    if n_prev_rows >= QB:
        n_prev_blocks, pb = n_prev_rows // QB, QB
    else:
        n_prev_blocks, pb = 1, n_prev_rows
    per = QB // pb
    kvw = 256 if shared_kv else 1024

    def prev_spec(col, back):
        return pl.BlockSpec(
            (pb, kvw), lambda b, i: (b * (SEQ // pb) + jnp.maximum(i * per - back, 0), col // kvw))

    def own_spec(col):
        return pl.BlockSpec((QB, kvw), lambda b, i: (b * nqb + i, col // kvw))

    k_specs = [prev_spec(k_col, n_prev_blocks - t) for t in range(n_prev_blocks)] + [own_spec(k_col)]
    v_specs = [prev_spec(v_col, n_prev_blocks - t) for t in range(n_prev_blocks)] + [own_spec(v_col)]
    kern = functools.partial(_band_attn_kernel, n_prev_blocks=n_prev_blocks, n_back=n_back,
                             shared_kv=shared_kv, has_sink=has_sink)
    n_in = 2 * (n_prev_blocks + 1)
    return pl.pallas_call(
        kern,
        grid=(BATCH, nqb),
        in_specs=[pl.BlockSpec(memory_space=pltpu.SMEM),
                  pl.BlockSpec(diag.shape, lambda b, i: (0, 0, 0)),
                  pl.BlockSpec((QB, 1024), lambda b, i: (b * nqb + i, q_col // 1024))]
                 + k_specs + v_specs,
        out_specs=pl.BlockSpec((QB, 1024), lambda b, i: (b * nqb + i, 0)),
        out_shape=jax.ShapeDtypeStruct((ROWS, 16 * HEAD_DIM), BF16),
        scratch_shapes=[pltpu.VMEM((16, QB, n_prev_rows + QB), F32)],
        compiler_params=_params(("arbitrary", "arbitrary")),
        name="band_attention_prompt",
    )(sink, diag, proj, *([proj] * n_in))


def _cached_attn_kernel(sink_ref, q_ref, kn_ref, vn_ref, kc_ref, vc_ref, diag_ref, bn_ref, _, o_ref, bc_ref,
                        *, shared_kv, has_sink):
    lc = kc_ref.shape[1]

    @pl.when(pl.program_id(0) == 0)
    def _():
        n = diag_ref.shape[2]
        for h in range(16):
            t = pltpu.roll(jnp.broadcast_to(diag_ref[h, 0:1, :], (DEC_SEQ, n)), 0, 1, stride=1, stride_axis=0)
            bc_ref[h] = t[:, :lc]

    for p in range(8):
        c0 = (p // 4 if shared_kv else p) * LANES
        q = q_ref[:, p * LANES:(p + 1) * LANES]
        kc = kc_ref[0, :, c0:c0 + LANES].astype(BF16)
        vc = vc_ref[0, :, c0:c0 + LANES].astype(BF16)
        kn = kn_ref[:, c0:c0 + LANES]
        vn = vn_ref[:, c0:c0 + LANES]
        outs = []
        for j in range(2):
            h = 2 * p + j
            qj = jnp.where(_head_lanes(j), q, jnp.zeros_like(q)) * (HEAD_DIM ** -0.5)
            s_c = lax.dot_general(qj, kc, NT, preferred_element_type=F32) + bc_ref[h]
            s_n = lax.dot_general(qj, kn, NT, preferred_element_type=F32) + bn_ref[h]
            sink = sink_ref[h] if has_sink else None
            outs.append(_softmax_pv([(s_c, vc), (s_n, vn)], sink))
        o_ref[:, p * LANES:(p + 1) * LANES] = jnp.where(_head_lanes(0), outs[0], outs[1]).astype(o_ref.dtype)


def cached_attention_sample(proj, k_cache, v_cache, layer, diag_c, bias_n, sink, o_all, *, q_col, k_col, v_col,
                            shared_kv, has_sink):
    lc, wc = k_cache.shape[2], k_cache.shape[3]
    r0 = P_ROWS // DEC_SEQ
    kern = functools.partial(_cached_attn_kernel, shared_kv=shared_kv, has_sink=has_sink)
    cache = pl.BlockSpec((None, 1, lc, wc), lambda s: (layer, s, 0, 0))
    return pl.pallas_call(
        kern,
        grid=(DEC_BATCH,),
        in_specs=[pl.BlockSpec(memory_space=pltpu.SMEM),
                  pl.BlockSpec((DEC_SEQ, 1024), lambda s: (r0 + s, q_col // 1024)),
                  pl.BlockSpec((DEC_SEQ, wc), lambda s: (r0 + s, k_col // wc)),
                  pl.BlockSpec((DEC_SEQ, wc), lambda s: (r0 + s, v_col // wc)),
                  cache, cache,
                  pl.BlockSpec(diag_c.shape, lambda s: (0, 0, 0)),
                  pl.BlockSpec((16, DEC_SEQ, DEC_SEQ), lambda s: (0, 0, 0)),
                  pl.BlockSpec(memory_space=pl.ANY)],
        out_specs=pl.BlockSpec((DEC_SEQ, 1024), lambda s: (r0 + s, 0)),
        out_shape=jax.ShapeDtypeStruct(o_all.shape, o_all.dtype),
        scratch_shapes=[pltpu.VMEM((16, DEC_SEQ, lc), F32)],
        input_output_aliases={8: 0},
        compiler_params=_params(("arbitrary",)),
        name="cached_attention_sample",
    )(sink, proj, proj, proj, k_cache, v_cache, diag_c, bias_n, o_all)


def _ret_kernel(q_ref, k_ref, v_ref, g_ref, cos_ref, sin_ref, s0_ref, *refs, blk_len):
    o_ref, sout_ref, st_ref = refs[-3:]
    blk = pl.program_id(1)

    @pl.when(blk == 0)
    def _():
        st_ref[...] = s0_ref[0]

    cos = cos_ref[...]
    sin = sin_ref[...]
    lane = lax.broadcasted_iota(jnp.int32, (1, LANES), 1)
    low_half = (lane % RET_DK) < (RET_DK // 2)

    def rope(x):
        x = x.astype(F32)
        swapped = jnp.where(low_half, pltpu.roll(x, LANES - RET_DK // 2, 1), pltpu.roll(x, RET_DK // 2, 1))
        return x * cos + swapped * sin

    ii = lax.broadcasted_iota(jnp.int32, (blk_len, blk_len), 0)
    jj = lax.broadcasted_iota(jnp.int32, (blk_len, blk_len), 1)
    diff = (ii - jj).astype(F32)
    row = lax.broadcasted_iota(jnp.int32, (blk_len, 1), 0).astype(F32)
    srow = lax.broadcasted_iota(jnp.int32, (LANES, 1), 0)

    for p in range(RET_HEADS // 2):
        qr = rope(q_ref[:, p * LANES:(p + 1) * LANES])
        kr = rope(k_ref[:, p * LANES:(p + 1) * LANES]) * (RET_DK ** -0.5)
        kb = kr.astype(BF16)
        state = st_ref[p]
        state_b = state.astype(BF16)
        upd = None
        log_gs = []
        for j in range(2):
            h = 2 * p + j
            log_g = math.log(1.0 - 2.0 ** (-5.0 - h))
            log_gs.append(log_g)
            decay = jnp.where(diff >= 0, jnp.exp(log_g * jnp.maximum(diff, 0.0)), 0.0)
            qj = jnp.where(_head_lanes(j), qr, 0.0).astype(BF16)
            vh = v_ref[:, h * RET_DV:(h + 1) * RET_DV]
            qk = lax.dot_general(qj, kb, NT, preferred_element_type=F32) * decay
            o = jnp.dot(qk.astype(BF16), vh, preferred_element_type=F32)
            o = o + jnp.dot(qj, state_b, preferred_element_type=F32) * jnp.exp(log_g * (row + 1.0))
            mu = jnp.mean(o, -1, keepdims=True)
            d = o - mu
            var = jnp.mean(d * d, -1, keepdims=True)
            gate = g_ref[:, h * RET_DV:(h + 1) * RET_DV].astype(F32)
            o_ref[:, h * RET_DV:(h + 1) * RET_DV] = (
                d * lax.rsqrt(var + RET_NORM_EPS) * (gate * jax.nn.sigmoid(gate))).astype(o_ref.dtype)
            kwj = jnp.where(_head_lanes(j), kr * jnp.exp(log_g * (blk_len - 1.0 - row)), 0.0).astype(BF16)
            u = lax.dot_general(kwj, vh, TN, preferred_element_type=F32)
            upd = u if upd is None else upd + u
        carry = jnp.where(srow < RET_DK, math.exp(log_gs[0] * blk_len), math.exp(log_gs[1] * blk_len))
        st_ref[p] = carry * state + upd

    @pl.when(blk == pl.num_programs(1) - 1)
    def _():
        sout_ref[0] = st_ref[...]


def retention(proj, cos, sin, s0, layer, o_all=None, *, row0, n_seq, seq_len, blk_len, pos_per_blk):
    nb = seq_len // blk_len
    rb0 = row0 // blk_len

    def rows(col, width):
        return pl.BlockSpec((blk_len, width), lambda b, t: (rb0 + b * nb + t, col // width))

    tab = pl.BlockSpec((blk_len, LANES), lambda b, t: (t if pos_per_blk else 0, 0))
    st = pl.BlockSpec((1, 4, LANES, LANES), lambda b, t: (b, 0, 0, 0))
    st_in = pl.BlockSpec((None, 1, 4, LANES, LANES), lambda b, t: (layer, b, 0, 0, 0))
    in_specs = [rows(COL_QB, 512), rows(COL_KB, 512), rows(COL_VB, 1024), rows(COL_GR, 1024), tab, tab, st_in]
    args = [proj, proj, proj, proj, cos, sin, s0]
    aliases = {}
    if o_all is not None:
        in_specs.append(pl.BlockSpec(memory_space=pl.ANY))
        args.append(o_all)
        aliases = {len(args) - 1: 0}
    return pl.pallas_call(
        functools.partial(_ret_kernel, blk_len=blk_len),
        grid=(n_seq, nb),
        in_specs=in_specs,
        out_specs=[rows(0, 1024), st],
        out_shape=[jax.ShapeDtypeStruct((ROWS, 1024), BF16),
                   jax.ShapeDtypeStruct((n_seq, 4, LANES, LANES), F32)],
        scratch_shapes=[pltpu.VMEM((4, LANES, LANES), F32)],
        input_output_aliases=aliases,
        compiler_params=_params(("parallel", "arbitrary")),
        name="retention",
    )(*args)


def _mem_attn_kernel(q_ref, k_ref, v_ref, *refs, heads):
    o_ref = refs[-1]
    for h in range(heads):
        cols = slice(h * MEM_HD, (h + 1) * MEM_HD)
        k = k_ref[0, :, cols].astype(BF16)
        v = v_ref[0, :, cols].astype(BF16)
        s = lax.dot_general(q_ref[:, cols], k, NT, preferred_element_type=F32) * (MEM_HD ** -0.5)
        o_ref[:, cols] = _softmax_pv([(s, v)], None).astype(o_ref.dtype)


def mem_attention(qm, k_src, v_src, layer, o_all=None, *, row0, n_rows, tm, heads, rows_per_kv):
    rb0 = row0 // tm
    w = heads * MEM_HD
    kv = pl.BlockSpec((None, 1, MEM_LEN, w), lambda i, h: (layer, i * tm // rows_per_kv, 0, h))
    in_specs = [pl.BlockSpec((tm, w), lambda i, h: (rb0 + i, h)), kv, kv]
    args = [qm, k_src, v_src]
    aliases = {}
    if o_all is not None:
        in_specs.append(pl.BlockSpec(memory_space=pl.ANY))
        args.append(o_all)
        aliases = {3: 0}
    return pl.pallas_call(
        functools.partial(_mem_attn_kernel, heads=heads),
        grid=(n_rows // tm, MEM_HEADS // heads),
        in_specs=in_specs,
        out_specs=pl.BlockSpec((tm, w), lambda i, h: (rb0 + i, h)),
        out_shape=jax.ShapeDtypeStruct((ROWS, D_MODEL), BF16),
        input_output_aliases=aliases,
        compiler_params=_params(("parallel", "parallel")),
        name="mem_attention",
    )(*args)


def _gelu(x):
    return 0.5 * x * (1.0 + lax.erf(x * (2.0 ** -0.5)))


def _ffn_in_kernel(*refs, sample, seq_tiles, tail):
    h_ref, tg_ref, tv_ref, bg_ref, bv_ref = refs[-5:]
    if sample:
        a_ref, wfg_ref, wfv_ref, wg_ref, wv_ref, cbg_ref, cbv_ref, f1g_ref, f1v_ref, f2g_ref, f2v_ref = refs[:11]
    else:
        a_ref, ah_ref, wfg_ref, wfv_ref, wg_ref, wv_ref, cbg_ref, cbv_ref = refs[:8]
    tm = a_ref.shape[0]
    i = pl.program_id(1)

    @pl.when(i == 0)
    def _():
        bg_ref[...] = wfg_ref[...].astype(BF16)
        bv_ref[...] = wfv_ref[...].astype(BF16)

    a = a_ref[...]
    if sample:
        pos = lax.broadcasted_iota(jnp.int32, (tm, 1), 0) % DEC_SEQ
    else:
        top = lax.broadcasted_iota(jnp.int32, (8, 1), 0)
        keep = (i % seq_tiles != 0).astype(F32)

    def conv(b_ref, w_ref, cb_ref, f1_ref, f2_ref, t_ref):
        u = jnp.dot(a, b_ref[...], preferred_element_type=F32)
        t_ref[0] = u[tm - tail:, :]
        r1 = pltpu.roll(u, 1, 0)
        r2 = pltpu.roll(u, 2, 0)
        if sample:
            u1 = jnp.where(pos < 1, f1_ref[...], r1)
            u2 = jnp.where(pos < 2, f2_ref[...], r2)
        else:
            uh = jnp.dot(ah_ref[...], b_ref[...], preferred_element_type=F32) * keep
            n = uh.shape[0]
            prev1, prev2 = uh[n - 1:n, :], uh[n - 2:n - 1, :]
            u1 = jnp.concatenate([jnp.where(top < 1, prev1, r1[:8]), r1[8:]], 0)
            top2 = jnp.where(top < 1, prev2, jnp.where(top < 2, prev1, r2[:8]))
            u2 = jnp.concatenate([top2, r2[8:]], 0)
        w = w_ref[...]
        return w[0:1, :] * u2 + w[1:2, :] * u1 + w[2:3, :] * u + cb_ref[...]

    if sample:
        cg = conv(bg_ref, wg_ref, cbg_ref, f1g_ref, f2g_ref, tg_ref)
        cv = conv(bv_ref, wv_ref, cbv_ref, f1v_ref, f2v_ref, tv_ref)
    else:
        cg = conv(bg_ref, wg_ref, cbg_ref, None, None, tg_ref)
        cv = conv(bv_ref, wv_ref, cbv_ref, None, None, tv_ref)
    h_ref[...] = (_gelu(cg) * cv).astype(h_ref.dtype)


def ffn_in(x, w, conv_w, conv_b, layer, h_all=None, *, row0, n_rows, tm, tn, tail, fix1=None, fix2=None):
    sample = fix1 is not None
    k = x.shape[1]
    nj = D_FF // tn
    ni = n_rows // tm
    rb0 = row0 // tm
    halo = 16
    a_spec = pl.BlockSpec((tm, k), lambda j, i: (rb0 + i, 0))
    bg = pl.BlockSpec((None, k, tn), lambda j, i: (layer, 0, j))
    bv = pl.BlockSpec((None, k, tn), lambda j, i: (layer, 0, nj + j))
    wg = pl.BlockSpec((None, CONV_W, tn), lambda j, i: (layer, 0, j))
    wv = pl.BlockSpec((None, CONV_W, tn), lambda j, i: (layer, 0, nj + j))
    cg = pl.BlockSpec((None, 1, tn), lambda j, i: (layer, 0, j))
    cv = pl.BlockSpec((None, 1, tn), lambda j, i: (layer, 0, nj + j))
    conv_b3 = conv_b.reshape(DEPTH, 1, 2 * D_FF)
    if sample:
        fg = pl.BlockSpec((tm, tn), lambda j, i: (i, j))
        fv = pl.BlockSpec((tm, tn), lambda j, i: (i, nj + j))
        in_specs = [a_spec, bg, bv, wg, wv, cg, cv, fg, fv, fg, fv]
        args = [x, w, w, conv_w, conv_w, conv_b3, conv_b3, fix1, fix1, fix2, fix2]
    else:
        ah = pl.BlockSpec((halo, k), lambda j, i: (jnp.maximum((rb0 + i) * (tm // halo) - 1, 0), 0))
        in_specs = [a_spec, ah, bg, bv, wg, wv, cg, cv]
        args = [x, x, w, w, conv_w, conv_w, conv_b3, conv_b3]
    aliases = {}
    if h_all is not None:
        in_specs.append(pl.BlockSpec(memory_space=pl.ANY))
        args.append(h_all)
        aliases = {len(args) - 1: 0}
    t_spec = pl.BlockSpec((1, tail, tn), lambda j, i: (i, 0, j))
    return pl.pallas_call(
        functools.partial(_ffn_in_kernel, sample=sample, seq_tiles=SEQ // tm if not sample else 1, tail=tail),
        grid=(nj, ni),
        in_specs=in_specs,
        out_specs=[pl.BlockSpec((tm, tn), lambda j, i: (rb0 + i, j)), t_spec, t_spec],
        out_shape=[jax.ShapeDtypeStruct((ROWS, D_FF), BF16),
                   jax.ShapeDtypeStruct((ni, tail, D_FF), F32),
                   jax.ShapeDtypeStruct((ni, tail, D_FF), F32)],
        scratch_shapes=[pltpu.VMEM((k, tn), BF16), pltpu.VMEM((k, tn), BF16)],
        input_output_aliases=aliases,
        compiler_params=_params(("parallel", "arbitrary")),
        name="ffn_in",
    )(*args)


def _t5_bucket(rel):
    nb = T5_BUCKETS // 2
    max_exact = nb // 2
    n = jnp.abs(rel)
    nf = jnp.maximum(n, 1).astype(F32)
    large = max_exact + (jnp.log(nf / max_exact) / math.log(T5_MAX_DIST / max_exact)
                         * (nb - max_exact)).astype(jnp.int32)
    large = jnp.minimum(large, nb - 1)
    return jnp.where(rel > 0, nb, 0) + jnp.where(n < max_exact, n, large)


def _t5_bias(table, rel):
    return jnp.transpose(table[_t5_bucket(rel)], (2, 0, 1)).astype(F32)


def _clipped_bias(table, rel):
    return table[:, jnp.clip(rel, -BAND_MAX_REL, BAND_MAX_REL) + BAND_MAX_REL].astype(F32)


def _toeplitz_diag(bias_fn, n_rows, n_cols, rel0, n):
    assert n_cols + n_rows - 1 <= n
    k = np.arange(n)
    rel = np.where(k < n_cols, k, k - n) + rel0
    return jnp.broadcast_to(bias_fn(jnp.asarray(rel)[None, :]), (16, 8, n))


def _band_diag(bias_fn, n_back):
    return _toeplitz_diag(bias_fn, QB, n_back * CHUNK + QB, -n_back * CHUNK, 1024)


def _cache_diag(bias_fn, cache_len):
    n = max(256, cache_len * 2)
    return _toeplitz_diag(bias_fn, DEC_SEQ, cache_len, -cache_len, n)


def _rope_tables(pos):
    half = RET_DK // 2
    inv = ROPE_BASE ** (-jnp.arange(half, dtype=F32) / half)
    ang = pos.astype(F32)[:, None] * inv[None, :]
    cos, sin = jnp.cos(ang), jnp.sin(ang)
    cos_t = jnp.concatenate([cos, cos, cos, cos], -1)
    sin_t = jnp.concatenate([-sin, sin, -sin, sin], -1)
    return cos_t, sin_t


def _dup_groups(t):
    g0, g1 = t[..., 0, :], t[..., 1, :]
    return jnp.concatenate([g0, g0, g1, g1], -1)


def _undup(t):
    return jnp.stack([t[:, 0:64], t[:, 128:192]], 1)


def kernel(x_prompt, x_sample, mem_prompt, cache_swa_k, cache_swa_v, state_ret, cache_band_k, cache_band_v, state_ffn_conv, cache_mem_k, cache_mem_v, w_in, t5_table, swa_sink, band_rel_table, w_br_a, w_br_b, w_br_c, w_mix_o, ln1_g, ln1_b, w_mq, w_mk, w_mv, w_mo, ln2_g, ln2_b, w_ffn_in, ffn_conv_w, ffn_conv_b, w_ffn_out, ln3_g, ln3_b):
    x = jnp.concatenate([x_prompt.reshape(P_ROWS, D_MODEL), x_sample.reshape(S_ROWS, D_MODEL)], 0)
    xb = x.astype(BF16)
    memb = mem_prompt.reshape(BATCH * MEM_LEN, D_MODEL).astype(BF16)

    t5 = functools.partial(_t5_bias, t5_table)
    diag_a = _band_diag(t5, SWA_BACK)
    qpos = PAST_LEN + jnp.arange(DEC_SEQ)
    la, lc = cache_swa_k.shape[2], cache_band_k.shape[2]
    rel_n = qpos[None, :] - qpos[:, None]
    diag_a_c, bias_a_n = _cache_diag(t5, la), t5(rel_n)
    cos_p, sin_p = _rope_tables(jnp.arange(SEQ))
    cos_s, sin_s = _rope_tables(qpos)
    zero_state = jnp.zeros((1, BATCH, 4, LANES, LANES), F32)
    no_sink = jnp.zeros((16,), F32)

    w_in_b = permute_w_in(w_in)
    w_br_a_b, w_br_b_b, w_br_c_b = w_br_a.astype(BF16), w_br_b.astype(BF16), w_br_c.astype(BF16)
    w_mix_o_b, w_mq_b, w_mk_b, w_mv_b, w_mo_b = (w.astype(BF16) for w in (w_mix_o, w_mq, w_mk, w_mv, w_mo))
    w_ffn_out_b = w_ffn_out.astype(BF16)
    swa_k2, swa_v2 = _dup_groups(cache_swa_k), _dup_groups(cache_swa_v)
    band_k2 = cache_band_k.reshape(DEPTH, DEC_BATCH, lc, BAND_HEADS * HEAD_DIM)
    band_v2 = cache_band_v.reshape(DEPTH, DEC_BATCH, lc, BAND_HEADS * HEAD_DIM)
    mem_k2 = cache_mem_k.reshape(DEPTH, DEC_BATCH, MEM_LEN, D_MODEL)
    mem_v2 = cache_mem_v.reshape(DEPTH, DEC_BATCH, MEM_LEN, D_MODEL)
    ret_s0 = state_ret.reshape(DEPTH, DEC_BATCH, 4, LANES, LANES)

    outs = {k: [] for k in ("p_ak", "p_av", "p_rs", "p_bk", "p_bv", "p_fc", "p_mk", "p_mv",
                            "s_ak", "s_av", "s_rs", "s_bk", "s_bv", "s_fc")}
    for l in range(DEPTH):
        proj = matmul(xb, w_in_b, l, tm=1056, tn=1536, out_dtype=BF16)
        clipped = functools.partial(_clipped_bias, band_rel_table[l])
        oa = band_attention_prompt(proj, diag_a, swa_sink[l], q_col=COL_QA, k_col=COL_KA2, v_col=COL_VA2,
                                   shared_kv=True, n_back=SWA_BACK, has_sink=True)
        oc = band_attention_prompt(proj, _band_diag(clipped, BAND_BACK), no_sink, q_col=COL_QC, k_col=COL_KC,
                                   v_col=COL_VC, shared_kv=False, n_back=BAND_BACK, has_sink=False)
        ob, rs_p = retention(proj, cos_p, sin_p, zero_state, 0, row0=0, n_seq=BATCH, seq_len=SEQ,
                             blk_len=RET_L, pos_per_blk=True)
        oa = cached_attention_sample(
            proj, swa_k2, swa_v2, l, diag_a_c, bias_a_n, swa_sink[l], oa,
            q_col=COL_QA, k_col=COL_KA2, v_col=COL_VA2, shared_kv=True, has_sink=True)
        oc = cached_attention_sample(
            proj, band_k2, band_v2, l, _cache_diag(clipped, lc), clipped(rel_n), no_sink, oc,
            q_col=COL_QC, k_col=COL_KC, v_col=COL_VC, shared_kv=False, has_sink=False)
        ob, rs_s = retention(proj, cos_s, sin_s, ret_s0, l, ob, row0=P_ROWS, n_seq=DEC_BATCH, seq_len=DEC_SEQ,
                             blk_len=DEC_SEQ, pos_per_blk=False)
        mix = gated_branch_sum(oa, ob, oc, w_br_a_b, w_br_b_b, w_br_c_b, proj, l)
        x, xb = matmul_residual_ln(mix, w_mix_o_b, l, x, ln1_g, ln1_b, tm=528, tk=D_MODEL)

        mk = matmul(memb, w_mk_b, l, tm=512, tn=1024, out_dtype=F32)
        mv = matmul(memb, w_mv_b, l, tm=512, tn=1024, out_dtype=F32)
        qm = matmul(xb, w_mq_b, l, tm=1056, tn=2048, out_dtype=BF16)
        om = mem_attention(qm, mk.reshape(1, BATCH, MEM_LEN, D_MODEL), mv.reshape(1, BATCH, MEM_LEN, D_MODEL), 0,
                           row0=0, n_rows=P_ROWS, tm=1024, heads=1, rows_per_kv=SEQ)
        om = mem_attention(qm, mem_k2, mem_v2, l, om, row0=P_ROWS, n_rows=S_ROWS, tm=DEC_SEQ,
                           heads=MEM_HEADS, rows_per_kv=DEC_SEQ)
        x, xb = matmul_residual_ln(om, w_mo_b, l, x, ln2_g, ln2_b, tm=528, tk=D_MODEL)

        h, tg_p, tv_p = ffn_in(xb, w_ffn_in, ffn_conv_w, ffn_conv_b, l, row0=0, n_rows=P_ROWS,
                               tm=1024, tn=512, tail=8)
        st = state_ffn_conv[l]
        fix1 = jnp.pad(st[:, 1:2], ((0, 0), (0, DEC_SEQ - 1), (0, 0))).reshape(S_ROWS, 2 * D_FF)
        fix2 = jnp.pad(st, ((0, 0), (0, DEC_SEQ - 2), (0, 0))).reshape(S_ROWS, 2 * D_FF)
        h, tg_s, tv_s = ffn_in(xb, w_ffn_in, ffn_conv_w, ffn_conv_b, l, h, row0=P_ROWS, n_rows=S_ROWS,
                               tm=S_ROWS, tn=512, tail=S_ROWS, fix1=fix1, fix2=fix2)
        x, xb = matmul_residual_ln(h, w_ffn_out_b, l, x, ln3_g, ln3_b, tm=768, tk=512)

        sf = proj[P_ROWS:]
        la_p, lc_p = min(SWA_BACK * CHUNK, SEQ), min(BAND_BACK * CHUNK, SEQ)
        seq_tail = lambda n, c0, w: jnp.stack(
            [proj[(b + 1) * SEQ - n:(b + 1) * SEQ, c0:c0 + w] for b in range(BATCH)], 0)
        outs["p_ak"].append(_undup(seq_tail(la_p, COL_KA2, 256).reshape(BATCH * la_p, 256))
                            .reshape(BATCH, la_p, 2, 64).astype(F32))
        outs["p_av"].append(_undup(seq_tail(la_p, COL_VA2, 256).reshape(BATCH * la_p, 256))
                            .reshape(BATCH, la_p, 2, 64).astype(F32))
        outs["p_rs"].append(rs_p.reshape(BATCH, RET_HEADS, RET_DK, RET_DV))
        outs["p_bk"].append(seq_tail(lc_p, COL_KC, 1024).reshape(BATCH, lc_p, 16, 64).astype(F32))
        outs["p_bv"].append(seq_tail(lc_p, COL_VC, 1024).reshape(BATCH, lc_p, 16, 64).astype(F32))
        last = [(b + 1) * (SEQ // 1024) - 1 for b in range(BATCH)]
        outs["p_fc"].append(jnp.stack(
            [jnp.concatenate([tg_p[t, 6:8], tv_p[t, 6:8]], -1) for t in last], 0))
        outs["p_mk"].append(mk.reshape(BATCH, MEM_LEN, MEM_HEADS, MEM_HD))
        outs["p_mv"].append(mv.reshape(BATCH, MEM_LEN, MEM_HEADS, MEM_HD))
        outs["s_ak"].append(_undup(sf[:, COL_KA2:COL_KA2 + 256]).reshape(DEC_BATCH, DEC_SEQ, 2, 64).astype(F32))
        outs["s_av"].append(_undup(sf[:, COL_VA2:COL_VA2 + 256]).reshape(DEC_BATCH, DEC_SEQ, 2, 64).astype(F32))
        outs["s_rs"].append(rs_s.reshape(DEC_BATCH, RET_HEADS, RET_DK, RET_DV))
        outs["s_bk"].append(sf[:, COL_KC:COL_KC + 1024].reshape(DEC_BATCH, DEC_SEQ, 16, 64).astype(F32))
        outs["s_bv"].append(sf[:, COL_VC:COL_VC + 1024].reshape(DEC_BATCH, DEC_SEQ, 16, 64).astype(F32))
        u_s = jnp.concatenate([tg_s[0], tv_s[0]], -1).reshape(DEC_BATCH, DEC_SEQ, 2 * D_FF)
        outs["s_fc"].append(u_s[:, DEC_SEQ - 2:])

    st = lambda name: jnp.stack(outs[name], 0)
    return (x[:P_ROWS].reshape(BATCH, SEQ, D_MODEL), x[P_ROWS:].reshape(DEC_BATCH, DEC_SEQ, D_MODEL),
            st("p_ak"), st("p_av"), st("p_rs"), st("p_bk"), st("p_bv"), st("p_fc"), st("p_mk"), st("p_mv"),
            st("s_ak"), st("s_av"), st("s_rs"), st("s_bk"), st("s_bv"), st("s_fc"))
```

```python
import functools
import math

import numpy as np
import jax
import jax.numpy as jnp
from jax import lax
from jax.experimental import pallas as pl
from jax.experimental.pallas import tpu as pltpu

F32 = jnp.float32
BF16 = jnp.bfloat16

D_MODEL = 2048
BATCH = 2
SEQ = 4096
DEPTH = 2
DEC_BATCH = 16
DEC_SEQ = 16
PAST_LEN = 2048
CHUNK = 64
HEAD_DIM = 64
SWA_BACK = 2
SWA_HEADS = 16
SWA_KV_HEADS = 2
T5_BUCKETS = 32
T5_MAX_DIST = 128
RET_HEADS = 8
RET_DK = 64
RET_DV = 128
ROPE_BASE = 10000.0
RET_NORM_EPS = 1e-5
BAND_BACK = 8
BAND_HEADS = 16
BAND_MAX_REL = 256
MEM_LEN = 256
MEM_HEADS = 4
MEM_HD = D_MODEL // MEM_HEADS
D_FF = 5632
CONV_W = 3
DN_ALPHA = (2 * DEPTH) ** 0.25
LN_EPS = 1e-5

P_ROWS = BATCH * SEQ
S_ROWS = DEC_BATCH * DEC_SEQ
ROWS = P_ROWS + S_ROWS

COL_QA = 0
COL_QC = 1024
COL_KC = 2048
COL_VC = 3072
COL_VB = 4096
COL_GR = 5120
COL_GA = 6144
COL_GB = 8192
COL_GC = 10240
COL_QB = 12288
COL_KB = 12800
COL_KA2 = 13312
COL_VA2 = 13568
PROJ_COLS = 13824

LANES = 128
MXU_COLS = 256
QB = 256
RET_L = 256
NEG = -1e30
VMEM_LIMIT = 48 * 1024 * 1024

NT = (((1,), (1,)), ((), ()))
TN = (((0,), (0,)), ((), ()))


def _params(sem, vmem=VMEM_LIMIT):
    return pltpu.CompilerParams(dimension_semantics=sem, vmem_limit_bytes=vmem)


def _w_in_permute_kernel(tbl_ref, src_ref, o_ref):
    mode = tbl_ref[1, pl.program_id(1)]
    t = src_ref[...]

    @pl.when(mode == 0)
    def _():
        o_ref[...] = t.astype(BF16)

    for m, half in ((1, t[:, :LANES]), (2, t[:, LANES:])):
        @pl.when(mode == m)
        def _():
            swapped = pltpu.roll(half, HEAD_DIM, 1)
            o_ref[...] = jnp.concatenate([jnp.where(_head_lanes(0), half, swapped),
                                          jnp.where(_head_lanes(0), swapped, half)], 1).astype(BF16)


def permute_w_in(w_in):
    blk = 256
    src = lambda off, width: list(range(off // blk, (off + width) // blk))
    order = (src(0, 1024) + src(4352, 1024) + src(5376, 1024) + src(6400, 1024) + src(2304, 1024)
             + src(3328, 1024) + src(7424, 2048) + src(9472, 2048) + src(11520, 2048) + src(1280, 512)
             + src(1792, 512))
    kv = 1024 // blk
    tbl = np.array([order + [kv, kv], [0] * len(order) + [1, 2]], np.int32)
    assert tbl.shape[1] * blk == PROJ_COLS
    d = w_in.shape[1]
    return pl.pallas_call(
        _w_in_permute_kernel,
        grid_spec=pltpu.PrefetchScalarGridSpec(
            num_scalar_prefetch=1,
            grid=(DEPTH, PROJ_COLS // blk),
            in_specs=[pl.BlockSpec((None, d, blk), lambda l, j, tbl: (l, 0, tbl[0, j]))],
            out_specs=pl.BlockSpec((None, d, blk), lambda l, j, tbl: (l, 0, j)),
        ),
        out_shape=jax.ShapeDtypeStruct((DEPTH, d, PROJ_COLS), BF16),
        compiler_params=_params(("parallel", "parallel")),
        name="permute_w_in",
    )(jnp.asarray(tbl), w_in)


def _mm_kernel(a_ref, b_ref, o_ref):
    o_ref[...] = jnp.dot(a_ref[...], b_ref[...], preferred_element_type=F32).astype(o_ref.dtype)


def matmul(a, b, layer, *, tm, tn, out_dtype):
    m, k = a.shape
    n = b.shape[2]
    return pl.pallas_call(
        _mm_kernel,
        grid=(m // tm, n // tn),
        in_specs=[pl.BlockSpec((tm, k), lambda i, j: (i, 0)),
                  pl.BlockSpec((None, k, tn), lambda i, j: (layer, 0, j))],
        out_specs=pl.BlockSpec((tm, tn), lambda i, j: (i, j)),
        out_shape=jax.ShapeDtypeStruct((m, n), out_dtype),
        compiler_params=_params(("parallel", "parallel")),
        name="matmul",
    )(a, b)


def _gate_mm_kernel(oa_ref, ob_ref, oc_ref, wa_ref, wb_ref, wc_ref, ga_ref, gb_ref, gc_ref, o_ref):
    acc = None
    for o, w, g in ((oa_ref, wa_ref, ga_ref), (ob_ref, wb_ref, gb_ref), (oc_ref, wc_ref, gc_ref)):
        t = jnp.dot(o[...], w[...], preferred_element_type=F32) * jax.nn.sigmoid(g[...].astype(F32))
        acc = t if acc is None else acc + t
    o_ref[...] = acc.astype(o_ref.dtype)


def gated_branch_sum(oa, ob, oc, wa, wb, wc, proj, layer, *, tm=528):
    m, k = oa.shape
    n = wa.shape[2]
    o_spec = pl.BlockSpec((tm, k), lambda i: (i, 0))
    w_spec = pl.BlockSpec((None, k, n), lambda i: (layer, 0, 0), pipeline_mode=pl.Buffered(1))

    def g_spec(col):
        return pl.BlockSpec((tm, n), lambda i: (i, col // n))

    return pl.pallas_call(
        _gate_mm_kernel,
        grid=(m // tm,),
        in_specs=[o_spec, o_spec, o_spec, w_spec, w_spec, w_spec,
                  g_spec(COL_GA), g_spec(COL_GB), g_spec(COL_GC)],
        out_specs=pl.BlockSpec((tm, n), lambda i: (i, 0)),
        out_shape=jax.ShapeDtypeStruct((m, n), BF16),
        compiler_params=_params(("parallel",)),
        name="gated_branch_sum",
    )(oa, ob, oc, wa, wb, wc, proj, proj, proj)


def _mm_ln_kernel(a_ref, b_ref, r_ref, g_ref, bt_ref, of_ref, ob_ref, acc_ref):
    @pl.when(pl.program_id(0) == 0)
    def _():
        acc_ref[...] = jnp.zeros_like(acc_ref)

    y = DN_ALPHA * r_ref[...] + acc_ref[...]
    mu = jnp.mean(y, -1, keepdims=True)
    d = y - mu
    var = jnp.mean(d * d, -1, keepdims=True)
    out = d * lax.rsqrt(var + LN_EPS) * g_ref[...] + bt_ref[...]
    of_ref[...] = out
    ob_ref[...] = out.astype(BF16)
    acc_ref[...] = jnp.dot(a_ref[...], b_ref[...], preferred_element_type=F32)


def matmul_residual_ln(a, b, layer, resid, gain, bias, *, tm):
    m, kk = a.shape
    n = b.shape[2]
    nt = m // tm
    cur = lambda i: (jnp.minimum(i, nt - 1), 0)
    prev = lambda i: (jnp.maximum(i - 1, 0), 0)
    return pl.pallas_call(
        _mm_ln_kernel,
        grid=(nt + 1,),
        in_specs=[pl.BlockSpec((tm, kk), cur),
                  pl.BlockSpec((None, kk, n), lambda i: (layer, 0, 0), pipeline_mode=pl.Buffered(1)),
                  pl.BlockSpec((tm, n), prev),
                  pl.BlockSpec((None, 1, n), lambda i: (layer, 0, 0)),
                  pl.BlockSpec((None, 1, n), lambda i: (layer, 0, 0))],
        out_specs=[pl.BlockSpec((tm, n), prev), pl.BlockSpec((tm, n), prev)],
        out_shape=[jax.ShapeDtypeStruct((m, n), F32), jax.ShapeDtypeStruct((m, n), BF16)],
        scratch_shapes=[pltpu.VMEM((tm, n), F32)],
        compiler_params=_params(("arbitrary",)),
        name="matmul_residual_ln",
    )(a, b, resid, gain.reshape(DEPTH, 1, n), bias.reshape(DEPTH, 1, n))


def _softmax_pv(pieces, sink):
    m = None
    for s, _ in pieces:
        mi = jnp.max(s, -1, keepdims=True)
        m = mi if m is None else jnp.maximum(m, mi)
    if sink is not None:
        m = jnp.maximum(m, sink)
    den = None
    acc = None
    for s, v in pieces:
        p = jnp.exp(s - m)
        di = jnp.sum(p, -1, keepdims=True)
        den = di if den is None else den + di
        oi = jnp.dot(p.astype(BF16), v, preferred_element_type=F32)
        acc = oi if acc is None else acc + oi
    if sink is not None:
        den = den + jnp.exp(sink - m)
    return acc / den


def _head_lanes(j):
    lane = lax.broadcasted_iota(jnp.int32, (1, LANES), 1)
    return (lane < HEAD_DIM) if j == 0 else (lane >= HEAD_DIM)


def _band_attn_kernel(sink_ref, diag_ref, q_ref, *refs, n_prev_blocks, n_back, shared_kv, has_sink):
    nk = n_prev_blocks + 1
    k_refs = refs[:nk]
    v_refs = refs[nk:2 * nk]
    o_ref = refs[2 * nk]
    bias_ref = refs[2 * nk + 1]
    b = pl.program_id(0)
    i = pl.program_id(1)
    n_prev_rows = n_back * CHUNK
    kw = n_prev_rows + QB

    @pl.when((b == 0) & (i == 0))
    def _():
        n = diag_ref.shape[2]
        qc = lax.broadcasted_iota(jnp.int32, (QB, kw), 0) // CHUNK
        cb = lax.broadcasted_iota(jnp.int32, (QB, kw), 1) // CHUNK
        allowed = (cb >= qc) & (cb - n_back <= qc)
        for h in range(16):
            t = pltpu.roll(jnp.broadcast_to(diag_ref[h, 0:1, :], (QB, n)), 0, 1, stride=1, stride_axis=0)
            bias_ref[h] = jnp.where(allowed, t[:, :kw], NEG)

    krow = lax.broadcasted_iota(jnp.int32, (kw, 1), 0)
    kmask = jnp.where(krow >= jnp.maximum(n_prev_rows - i * QB, 0), 0.0, NEG).astype(BF16)
    lane = lax.broadcasted_iota(jnp.int32, (1, LANES), 1)
    for p in range(8):
        c0 = (p // 4 if shared_kv else p) * LANES
        q = q_ref[:, p * LANES:(p + 1) * LANES] * (HEAD_DIM ** -0.5)
        k_all = jnp.concatenate([r[:, c0:c0 + LANES] for r in k_refs], axis=0)
        v_all = jnp.concatenate([r[:, c0:c0 + LANES] for r in v_refs], axis=0)
        outs = []
        for j in range(2):
            h = 2 * p + j
            mask_lane = lane == (HEAD_DIM if j == 0 else 0)
            qj = jnp.where(mask_lane, jnp.ones_like(q), jnp.where(_head_lanes(j), q, jnp.zeros_like(q)))
            kj = jnp.where(mask_lane, kmask, k_all)
            s_all = lax.dot_general(qj, kj, NT, preferred_element_type=F32)
            sink = sink_ref[h] if has_sink else None
            ps, dens = [], []
            for r in range(QB // CHUNK):
                rows = slice(r * CHUNK, (r + 1) * CHUNK)
                lo = r * CHUNK // LANES * LANES
                hi = min(kw, -(-(r * CHUNK + n_prev_rows + CHUNK) // LANES) * LANES)
                s = s_all[rows, lo:hi] + bias_ref[h, rows, lo:hi]
                m = jnp.max(s, -1, keepdims=True)
                if has_sink:
                    m = jnp.maximum(m, sink)
                e = jnp.exp(s - m)
                den = jnp.sum(e, -1, keepdims=True)
                if has_sink:
                    den = den + jnp.exp(sink - m)
                parts = [e.astype(BF16)]
                if lo:
                    parts.insert(0, jnp.zeros((CHUNK, lo), BF16))
                if hi < kw:
                    parts.append(jnp.zeros((CHUNK, kw - hi), BF16))
                ps.append(jnp.concatenate(parts, 1))
                dens.append(den)
            pv = jnp.dot(jnp.concatenate(ps, 0), v_all, preferred_element_type=F32)
            outs.append(pv / jnp.concatenate(dens, 0))
        o_ref[:, p * LANES:(p + 1) * LANES] = jnp.where(_head_lanes(0), outs[0], outs[1]).astype(o_ref.dtype)


def band_attention_prompt(proj, diag, sink, *, q_col, k_col, v_col, shared_kv, n_back, has_sink):
    n_prev_rows = n_back * CHUNK
    nqb = SEQ // QB
    if n_prev_rows >= QB:
        n_prev_blocks, pb = n_prev_rows // QB, QB
    else:
        n_prev_blocks, pb = 1, n_prev_rows
    per = QB // pb
    kvw = 256 if shared_kv else 1024

    def prev_spec(col, back):
        return pl.BlockSpec(
            (pb, kvw), lambda b, i: (b * (SEQ // pb) + jnp.maximum(i * per - back, 0), col // kvw))

    def own_spec(col):
        return pl.BlockSpec((QB, kvw), lambda b, i: (b * nqb + i, col // kvw))

    k_specs = [prev_spec(k_col, n_prev_blocks - t) for t in range(n_prev_blocks)] + [own_spec(k_col)]
    v_specs = [prev_spec(v_col, n_prev_blocks - t) for t in range(n_prev_blocks)] + [own_spec(v_col)]
    kern = functools.partial(_band_attn_kernel, n_prev_blocks=n_prev_blocks, n_back=n_back,
                             shared_kv=shared_kv, has_sink=has_sink)
    n_in = 2 * (n_prev_blocks + 1)
    return pl.pallas_call(
        kern,
        grid=(BATCH, nqb),
        in_specs=[pl.BlockSpec(memory_space=pltpu.SMEM),
                  pl.BlockSpec(diag.shape, lambda b, i: (0, 0, 0)),
                  pl.BlockSpec((QB, 1024), lambda b, i: (b * nqb + i, q_col // 1024))]
                 + k_specs + v_specs,
        out_specs=pl.BlockSpec((QB, 1024), lambda b, i: (b * nqb + i, 0)),
        out_shape=jax.ShapeDtypeStruct((ROWS, 16 * HEAD_DIM), BF16),
        scratch_shapes=[pltpu.VMEM((16, QB, n_prev_rows + QB), F32)],
        compiler_params=_params(("arbitrary", "arbitrary")),
        name="band_attention_prompt",
    )(sink, diag, proj, *([proj] * n_in))


def _cached_attn_kernel(sink_ref, q_ref, kn_ref, vn_ref, kc_ref, vc_ref, diag_ref, bn_ref, _, o_ref, bc_ref,
                        *, shared_kv, has_sink):
    lc = kc_ref.shape[1]

    @pl.when(pl.program_id(0) == 0)
    def _():
        n = diag_ref.shape[2]
        for h in range(16):
            t = pltpu.roll(jnp.broadcast_to(diag_ref[h, 0:1, :], (DEC_SEQ, n)), 0, 1, stride=1, stride_axis=0)
            bc_ref[h] = t[:, :lc]

    kv_cols = [(p // 4 if shared_kv else p) * LANES for p in range(8)]
    scores = []
    for p in range(8):
        c0 = kv_cols[p]
        q = q_ref[:, p * LANES:(p + 1) * LANES] * (HEAD_DIM ** -0.5)
        kc = kc_ref[0, :, c0:c0 + LANES].astype(BF16)
        kn = kn_ref[:, c0:c0 + LANES]
        for j in range(2):
            h = 2 * p + j
            qj = jnp.where(_head_lanes(j), q, jnp.zeros_like(q))
            scores.append((lax.dot_general(qj, kc, NT, preferred_element_type=F32) + bc_ref[h],
                           lax.dot_general(qj, kn, NT, preferred_element_type=F32) + bn_ref[h]))
    probs = []
    for h, (s_c, s_n) in enumerate(scores):
        m = jnp.maximum(jnp.max(s_c, -1, keepdims=True), jnp.max(s_n, -1, keepdims=True))
        if has_sink:
            m = jnp.maximum(m, sink_ref[h])
        e_c, e_n = jnp.exp(s_c - m), jnp.exp(s_n - m)
        den = jnp.sum(e_c, -1, keepdims=True) + jnp.sum(e_n, -1, keepdims=True)
        if has_sink:
            den = den + jnp.exp(sink_ref[h] - m)
        probs.append((e_c.astype(BF16), e_n.astype(BF16), den))
    for p in range(8):
        c0 = kv_cols[p]
        vc = vc_ref[0, :, c0:c0 + LANES].astype(BF16)
        vn = vn_ref[:, c0:c0 + LANES]
        outs = []
        for j in range(2):
            e_c, e_n, den = probs[2 * p + j]
            outs.append((jnp.dot(e_c, vc, preferred_element_type=F32)
                         + jnp.dot(e_n, vn, preferred_element_type=F32)) / den)
        o_ref[:, p * LANES:(p + 1) * LANES] = jnp.where(_head_lanes(0), outs[0], outs[1]).astype(o_ref.dtype)


def cached_attention_sample(proj, k_cache, v_cache, layer, diag_c, bias_n, sink, o_all, *, q_col, k_col, v_col,
                            shared_kv, has_sink):
    lc, wc = k_cache.shape[2], k_cache.shape[3]
    r0 = P_ROWS // DEC_SEQ
    kern = functools.partial(_cached_attn_kernel, shared_kv=shared_kv, has_sink=has_sink)
    cache = pl.BlockSpec((None, 1, lc, wc), lambda s: (layer, s, 0, 0))
    return pl.pallas_call(
        kern,
        grid=(DEC_BATCH,),
        in_specs=[pl.BlockSpec(memory_space=pltpu.SMEM),
                  pl.BlockSpec((DEC_SEQ, 1024), lambda s: (r0 + s, q_col // 1024)),
                  pl.BlockSpec((DEC_SEQ, wc), lambda s: (r0 + s, k_col // wc)),
                  pl.BlockSpec((DEC_SEQ, wc), lambda s: (r0 + s, v_col // wc)),
                  cache, cache,
                  pl.BlockSpec(diag_c.shape, lambda s: (0, 0, 0)),
                  pl.BlockSpec((16, DEC_SEQ, DEC_SEQ), lambda s: (0, 0, 0)),
                  pl.BlockSpec(memory_space=pl.ANY)],
        out_specs=pl.BlockSpec((DEC_SEQ, 1024), lambda s: (r0 + s, 0)),
        out_shape=jax.ShapeDtypeStruct(o_all.shape, o_all.dtype),
        scratch_shapes=[pltpu.VMEM((16, DEC_SEQ, lc), F32)],
        input_output_aliases={8: 0},
        compiler_params=_params(("arbitrary",)),
        name="cached_attention_sample",
    )(sink, proj, proj, proj, k_cache, v_cache, diag_c, bias_n, o_all)


def _ret_kernel(q_ref, k_ref, v_ref, g_ref, cos_ref, sin_ref, s0_ref, *refs, blk_len):
    o_ref, sout_ref, st_ref = refs[-3:]
    blk = pl.program_id(1)

    @pl.when(blk == 0)
    def _():
        st_ref[...] = s0_ref[0]

    cos = cos_ref[...]
    sin = sin_ref[...]
    lane = lax.broadcasted_iota(jnp.int32, (1, LANES), 1)
    low_half = (lane % RET_DK) < (RET_DK // 2)

    def rope(x):
        x = x.astype(F32)
        swapped = jnp.where(low_half, pltpu.roll(x, LANES - RET_DK // 2, 1), pltpu.roll(x, RET_DK // 2, 1))
        return x * cos + swapped * sin

    ii = lax.broadcasted_iota(jnp.int32, (blk_len, blk_len), 0)
    jj = lax.broadcasted_iota(jnp.int32, (blk_len, blk_len), 1)
    diff = (ii - jj).astype(F32)
    row = lax.broadcasted_iota(jnp.int32, (blk_len, 1), 0).astype(F32)
    srow = lax.broadcasted_iota(jnp.int32, (LANES, 1), 0)

    for p in range(RET_HEADS // 2):
        qr = rope(q_ref[:, p * LANES:(p + 1) * LANES])
        kr = rope(k_ref[:, p * LANES:(p + 1) * LANES]) * (RET_DK ** -0.5)
        kb = kr.astype(BF16)
        state = st_ref[p]
        state_b = state.astype(BF16)
        upd = None
        log_gs = []
        for j in range(2):
            h = 2 * p + j
            log_g = math.log(1.0 - 2.0 ** (-5.0 - h))
            log_gs.append(log_g)
            decay = jnp.where(diff >= 0, jnp.exp(log_g * jnp.maximum(diff, 0.0)), 0.0)
            qj = jnp.where(_head_lanes(j), qr, 0.0).astype(BF16)
            vh = v_ref[:, h * RET_DV:(h + 1) * RET_DV]
            qk = lax.dot_general(qj, kb, NT, preferred_element_type=F32) * decay
            o = jnp.dot(qk.astype(BF16), vh, preferred_element_type=F32)
            o = o + jnp.dot(qj, state_b, preferred_element_type=F32) * jnp.exp(log_g * (row + 1.0))
            mu = jnp.mean(o, -1, keepdims=True)
            d = o - mu
            var = jnp.mean(d * d, -1, keepdims=True)
            gate = g_ref[:, h * RET_DV:(h + 1) * RET_DV].astype(F32)
            o_ref[:, h * RET_DV:(h + 1) * RET_DV] = (
                d * lax.rsqrt(var + RET_NORM_EPS) * (gate * jax.nn.sigmoid(gate))).astype(o_ref.dtype)
            kwj = jnp.where(_head_lanes(j), kr * jnp.exp(log_g * (blk_len - 1.0 - row)), 0.0).astype(BF16)
            u = lax.dot_general(kwj, vh, TN, preferred_element_type=F32)
            upd = u if upd is None else upd + u
        carry = jnp.where(srow < RET_DK, math.exp(log_gs[0] * blk_len), math.exp(log_gs[1] * blk_len))
        st_ref[p] = carry * state + upd

    @pl.when(blk == pl.num_programs(1) - 1)
    def _():
        sout_ref[0] = st_ref[...]


def retention(proj, cos, sin, s0, layer, o_all=None, *, row0, n_seq, seq_len, blk_len, pos_per_blk):
    nb = seq_len // blk_len
    rb0 = row0 // blk_len

    def rows(col, width):
        return pl.BlockSpec((blk_len, width), lambda b, t: (rb0 + b * nb + t, col // width))

    tab = pl.BlockSpec((blk_len, LANES), lambda b, t: (t if pos_per_blk else 0, 0))
    st = pl.BlockSpec((1, 4, LANES, LANES), lambda b, t: (b, 0, 0, 0))
    st_in = pl.BlockSpec((None, 1, 4, LANES, LANES), lambda b, t: (layer, b, 0, 0, 0))
    in_specs = [rows(COL_QB, 512), rows(COL_KB, 512), rows(COL_VB, 1024), rows(COL_GR, 1024), tab, tab, st_in]
    args = [proj, proj, proj, proj, cos, sin, s0]
    aliases = {}
    if o_all is not None:
        in_specs.append(pl.BlockSpec(memory_space=pl.ANY))
        args.append(o_all)
        aliases = {len(args) - 1: 0}
    return pl.pallas_call(
        functools.partial(_ret_kernel, blk_len=blk_len),
        grid=(n_seq, nb),
        in_specs=in_specs,
        out_specs=[rows(0, 1024), st],
        out_shape=[jax.ShapeDtypeStruct((ROWS, 1024), BF16),
                   jax.ShapeDtypeStruct((n_seq, 4, LANES, LANES), F32)],
        scratch_shapes=[pltpu.VMEM((4, LANES, LANES), F32)],
        input_output_aliases=aliases,
        compiler_params=_params(("parallel", "arbitrary")),
        name="retention",
    )(*args)


def _mem_attn_kernel(q_ref, k_ref, v_ref, *refs, heads):
    o_ref = refs[-1]
    for h in range(heads):
        cols = slice(h * MEM_HD, (h + 1) * MEM_HD)
        k = k_ref[0, :, cols].astype(BF16)
        v = v_ref[0, :, cols].astype(BF16)
        s = lax.dot_general(q_ref[:, cols], k, NT, preferred_element_type=F32) * (MEM_HD ** -0.5)
        o_ref[:, cols] = _softmax_pv([(s, v)], None).astype(o_ref.dtype)


def mem_attention(qm, k_src, v_src, layer, o_all=None, *, row0, n_rows, tm, heads, rows_per_kv):
    rb0 = row0 // tm
    w = heads * MEM_HD
    kv = pl.BlockSpec((None, 1, MEM_LEN, w), lambda i, h: (layer, i * tm // rows_per_kv, 0, h))
    in_specs = [pl.BlockSpec((tm, w), lambda i, h: (rb0 + i, h)), kv, kv]
    args = [qm, k_src, v_src]
    aliases = {}
    if o_all is not None:
        in_specs.append(pl.BlockSpec(memory_space=pl.ANY))
        args.append(o_all)
        aliases = {3: 0}
    return pl.pallas_call(
        functools.partial(_mem_attn_kernel, heads=heads),
        grid=(n_rows // tm, MEM_HEADS // heads),
        in_specs=in_specs,
        out_specs=pl.BlockSpec((tm, w), lambda i, h: (rb0 + i, h)),
        out_shape=jax.ShapeDtypeStruct((ROWS, D_MODEL), BF16),
        input_output_aliases=aliases,
        compiler_params=_params(("parallel", "parallel")),
        name="mem_attention",
    )(*args)


def _gelu(x):
    return 0.5 * x * (1.0 + lax.erf(x * (2.0 ** -0.5)))


def _ffn_in_kernel(*refs, sample, seq_tiles, tail):
    h_ref, tg_ref, tv_ref, bg_ref, bv_ref = refs[-5:]
    if sample:
        a_ref, wfg_ref, wfv_ref, wg_ref, wv_ref, cbg_ref, cbv_ref, f1g_ref, f1v_ref, f2g_ref, f2v_ref = refs[:11]
    else:
        a_ref, ah_ref, wfg_ref, wfv_ref, wg_ref, wv_ref, cbg_ref, cbv_ref = refs[:8]
    tm = a_ref.shape[0]
    i = pl.program_id(1)

    @pl.when(i == 0)
    def _():
        bg_ref[...] = wfg_ref[...].astype(BF16)
        bv_ref[...] = wfv_ref[...].astype(BF16)

    a = a_ref[...]
    if sample:
        pos = lax.broadcasted_iota(jnp.int32, (tm, 1), 0) % DEC_SEQ
    else:
        top = lax.broadcasted_iota(jnp.int32, (8, 1), 0)
        keep = (i % seq_tiles != 0).astype(F32)

    def conv(cols, b_ref, w_ref, cb_ref, f1_ref, f2_ref, t_ref):
        u = jnp.dot(a, b_ref[:, cols], preferred_element_type=F32)
        t_ref[0, :, cols] = u[tm - tail:, :]
        r1 = pltpu.roll(u, 1, 0)
        r2 = pltpu.roll(u, 2, 0)
        if sample:
            u1 = jnp.where(pos < 1, f1_ref[:, cols], r1)
            u2 = jnp.where(pos < 2, f2_ref[:, cols], r2)
        else:
            uh = jnp.dot(ah_ref[...], b_ref[:, cols], preferred_element_type=F32) * keep
            n = uh.shape[0]
            prev1, prev2 = uh[n - 1:n, :], uh[n - 2:n - 1, :]
            u1 = jnp.concatenate([jnp.where(top < 1, prev1, r1[:8]), r1[8:]], 0)
            top2 = jnp.where(top < 1, prev2, jnp.where(top < 2, prev1, r2[:8]))
            u2 = jnp.concatenate([top2, r2[8:]], 0)
        w = w_ref[:, cols]
        return w[0:1, :] * u2 + w[1:2, :] * u1 + w[2:3, :] * u + cb_ref[:, cols]

    for c0 in range(0, h_ref.shape[1], MXU_COLS):
        cols = slice(c0, c0 + MXU_COLS)
        if sample:
            cg = conv(cols, bg_ref, wg_ref, cbg_ref, f1g_ref, f2g_ref, tg_ref)
            cv = conv(cols, bv_ref, wv_ref, cbv_ref, f1v_ref, f2v_ref, tv_ref)
        else:
            cg = conv(cols, bg_ref, wg_ref, cbg_ref, None, None, tg_ref)
            cv = conv(cols, bv_ref, wv_ref, cbv_ref, None, None, tv_ref)
        h_ref[:, cols] = (_gelu(cg) * cv).astype(h_ref.dtype)


def ffn_in(x, w, conv_w, conv_b, layer, h_all=None, *, row0, n_rows, tm, tn, tail, fix1=None, fix2=None):
    sample = fix1 is not None
    k = x.shape[1]
    nj = D_FF // tn
    ni = n_rows // tm
    rb0 = row0 // tm
    halo = 16
    a_spec = pl.BlockSpec((tm, k), lambda j, i: (rb0 + i, 0))
    bg = pl.BlockSpec((None, k, tn), lambda j, i: (layer, 0, j))
    bv = pl.BlockSpec((None, k, tn), lambda j, i: (layer, 0, nj + j))
    wg = pl.BlockSpec((None, CONV_W, tn), lambda j, i: (layer, 0, j))
    wv = pl.BlockSpec((None, CONV_W, tn), lambda j, i: (layer, 0, nj + j))
    cg = pl.BlockSpec((None, 1, tn), lambda j, i: (layer, 0, j))
    cv = pl.BlockSpec((None, 1, tn), lambda j, i: (layer, 0, nj + j))
    conv_b3 = conv_b.reshape(DEPTH, 1, 2 * D_FF)
    if sample:
        fg = pl.BlockSpec((tm, tn), lambda j, i: (i, j))
        fv = pl.BlockSpec((tm, tn), lambda j, i: (i, nj + j))
        in_specs = [a_spec, bg, bv, wg, wv, cg, cv, fg, fv, fg, fv]
        args = [x, w, w, conv_w, conv_w, conv_b3, conv_b3, fix1, fix1, fix2, fix2]
    else:
        ah = pl.BlockSpec((halo, k), lambda j, i: (jnp.maximum((rb0 + i) * (tm // halo) - 1, 0), 0))
        in_specs = [a_spec, ah, bg, bv, wg, wv, cg, cv]
        args = [x, x, w, w, conv_w, conv_w, conv_b3, conv_b3]
    aliases = {}
    if h_all is not None:
        in_specs.append(pl.BlockSpec(memory_space=pl.ANY))
        args.append(h_all)
        aliases = {len(args) - 1: 0}
    t_spec = pl.BlockSpec((1, tail, tn), lambda j, i: (i, 0, j))
    return pl.pallas_call(
        functools.partial(_ffn_in_kernel, sample=sample, seq_tiles=SEQ // tm if not sample else 1, tail=tail),
        grid=(nj, ni),
        in_specs=in_specs,
        out_specs=[pl.BlockSpec((tm, tn), lambda j, i: (rb0 + i, j)), t_spec, t_spec],
        out_shape=[jax.ShapeDtypeStruct((ROWS, D_FF), BF16),
                   jax.ShapeDtypeStruct((ni, tail, D_FF), F32),
                   jax.ShapeDtypeStruct((ni, tail, D_FF), F32)],
        scratch_shapes=[pltpu.VMEM((k, tn), BF16), pltpu.VMEM((k, tn), BF16)],
        input_output_aliases=aliases,
        compiler_params=_params(("parallel", "arbitrary")),
        name="ffn_in",
    )(*args)


def _t5_bucket(rel):
    nb = T5_BUCKETS // 2
    max_exact = nb // 2
    n = jnp.abs(rel)
    nf = jnp.maximum(n, 1).astype(F32)
    large = max_exact + (jnp.log(nf / max_exact) / math.log(T5_MAX_DIST / max_exact)
                         * (nb - max_exact)).astype(jnp.int32)
    large = jnp.minimum(large, nb - 1)
    return jnp.where(rel > 0, nb, 0) + jnp.where(n < max_exact, n, large)


def _t5_bias(table, rel):
    return jnp.transpose(table[_t5_bucket(rel)], (2, 0, 1)).astype(F32)


def _clipped_bias(table, rel):
    return table[:, jnp.clip(rel, -BAND_MAX_REL, BAND_MAX_REL) + BAND_MAX_REL].astype(F32)


def _toeplitz_diag(bias_fn, n_rows, n_cols, rel0, n):
    assert n_cols + n_rows - 1 <= n
    k = np.arange(n)
    rel = np.where(k < n_cols, k, k - n) + rel0
    return jnp.broadcast_to(bias_fn(jnp.asarray(rel)[None, :]), (16, 8, n))


def _band_diag(bias_fn, n_back):
    return _toeplitz_diag(bias_fn, QB, n_back * CHUNK + QB, -n_back * CHUNK, 1024)


def _cache_diag(bias_fn, cache_len):
    n = max(256, cache_len * 2)
    return _toeplitz_diag(bias_fn, DEC_SEQ, cache_len, -cache_len, n)


def _rope_tables(pos):
    half = RET_DK // 2
    inv = ROPE_BASE ** (-jnp.arange(half, dtype=F32) / half)
    ang = pos.astype(F32)[:, None] * inv[None, :]
    cos, sin = jnp.cos(ang), jnp.sin(ang)
    cos_t = jnp.concatenate([cos, cos, cos, cos], -1)
    sin_t = jnp.concatenate([-sin, sin, -sin, sin], -1)
    return cos_t, sin_t


def _dup_groups(t):
    g0, g1 = t[..., 0, :], t[..., 1, :]
    return jnp.concatenate([g0, g0, g1, g1], -1)


def _undup(t):
    return jnp.stack([t[:, 0:64], t[:, 128:192]], 1)


def kernel(x_prompt, x_sample, mem_prompt, cache_swa_k, cache_swa_v, state_ret, cache_band_k, cache_band_v, state_ffn_conv, cache_mem_k, cache_mem_v, w_in, t5_table, swa_sink, band_rel_table, w_br_a, w_br_b, w_br_c, w_mix_o, ln1_g, ln1_b, w_mq, w_mk, w_mv, w_mo, ln2_g, ln2_b, w_ffn_in, ffn_conv_w, ffn_conv_b, w_ffn_out, ln3_g, ln3_b):
    x = jnp.concatenate([x_prompt.reshape(P_ROWS, D_MODEL), x_sample.reshape(S_ROWS, D_MODEL)], 0)
    xb = x.astype(BF16)
    memb = mem_prompt.reshape(BATCH * MEM_LEN, D_MODEL).astype(BF16)

    t5 = functools.partial(_t5_bias, t5_table)
    diag_a = _band_diag(t5, SWA_BACK)
    qpos = PAST_LEN + jnp.arange(DEC_SEQ)
    la, lc = cache_swa_k.shape[2], cache_band_k.shape[2]
    rel_n = qpos[None, :] - qpos[:, None]
    diag_a_c, bias_a_n = _cache_diag(t5, la), t5(rel_n)
    cos_p, sin_p = _rope_tables(jnp.arange(SEQ))
    cos_s, sin_s = _rope_tables(qpos)
    zero_state = jnp.zeros((1, BATCH, 4, LANES, LANES), F32)
    no_sink = jnp.zeros((16,), F32)

    w_in_b = permute_w_in(w_in)
    w_br_a_b, w_br_b_b, w_br_c_b = w_br_a.astype(BF16), w_br_b.astype(BF16), w_br_c.astype(BF16)
    w_mix_o_b, w_mq_b, w_mk_b, w_mv_b, w_mo_b = (w.astype(BF16) for w in (w_mix_o, w_mq, w_mk, w_mv, w_mo))
    w_ffn_out_b = w_ffn_out.astype(BF16)
    swa_k2, swa_v2 = _dup_groups(cache_swa_k), _dup_groups(cache_swa_v)
    band_k2 = cache_band_k.reshape(DEPTH, DEC_BATCH, lc, BAND_HEADS * HEAD_DIM)
    band_v2 = cache_band_v.reshape(DEPTH, DEC_BATCH, lc, BAND_HEADS * HEAD_DIM)
    mem_k2 = cache_mem_k.reshape(DEPTH, DEC_BATCH, MEM_LEN, D_MODEL)
    mem_v2 = cache_mem_v.reshape(DEPTH, DEC_BATCH, MEM_LEN, D_MODEL)
    ret_s0 = state_ret.reshape(DEPTH, DEC_BATCH, 4, LANES, LANES)

    outs = {k: [] for k in ("p_ak", "p_av", "p_rs", "p_bk", "p_bv", "p_fc", "p_mk", "p_mv",
                            "s_ak", "s_av", "s_rs", "s_bk", "s_bv", "s_fc")}
    for l in range(DEPTH):
        proj = matmul(xb, w_in_b, l, tm=1056, tn=1536, out_dtype=BF16)
        clipped = functools.partial(_clipped_bias, band_rel_table[l])
        oa = band_attention_prompt(proj, diag_a, swa_sink[l], q_col=COL_QA, k_col=COL_KA2, v_col=COL_VA2,
                                   shared_kv=True, n_back=SWA_BACK, has_sink=True)
        oc = band_attention_prompt(proj, _band_diag(clipped, BAND_BACK), no_sink, q_col=COL_QC, k_col=COL_KC,
                                   v_col=COL_VC, shared_kv=False, n_back=BAND_BACK, has_sink=False)
        ob, rs_p = retention(proj, cos_p, sin_p, zero_state, 0, row0=0, n_seq=BATCH, seq_len=SEQ,
                             blk_len=RET_L, pos_per_blk=True)
        oa = cached_attention_sample(
            proj, swa_k2, swa_v2, l, diag_a_c, bias_a_n, swa_sink[l], oa,
            q_col=COL_QA, k_col=COL_KA2, v_col=COL_VA2, shared_kv=True, has_sink=True)
        oc = cached_attention_sample(
            proj, band_k2, band_v2, l, _cache_diag(clipped, lc), clipped(rel_n), no_sink, oc,
            q_col=COL_QC, k_col=COL_KC, v_col=COL_VC, shared_kv=False, has_sink=False)
        ob, rs_s = retention(proj, cos_s, sin_s, ret_s0, l, ob, row0=P_ROWS, n_seq=DEC_BATCH, seq_len=DEC_SEQ,
                             blk_len=DEC_SEQ, pos_per_blk=False)
        mix = gated_branch_sum(oa, ob, oc, w_br_a_b, w_br_b_b, w_br_c_b, proj, l)
        x, xb = matmul_residual_ln(mix, w_mix_o_b, l, x, ln1_g, ln1_b, tm=528)

        mk = matmul(memb, w_mk_b, l, tm=512, tn=1024, out_dtype=F32)
        mv = matmul(memb, w_mv_b, l, tm=512, tn=1024, out_dtype=F32)
        qm = matmul(xb, w_mq_b, l, tm=1056, tn=2048, out_dtype=BF16)
        om = mem_attention(qm, mk.reshape(1, BATCH, MEM_LEN, D_MODEL), mv.reshape(1, BATCH, MEM_LEN, D_MODEL), 0,
                           row0=0, n_rows=P_ROWS, tm=1024, heads=1, rows_per_kv=SEQ)
        om = mem_attention(qm, mem_k2, mem_v2, l, om, row0=P_ROWS, n_rows=S_ROWS, tm=DEC_SEQ,
                           heads=MEM_HEADS, rows_per_kv=DEC_SEQ)
        x, xb = matmul_residual_ln(om, w_mo_b, l, x, ln2_g, ln2_b, tm=528)

        h, tg_p, tv_p = ffn_in(xb, w_ffn_in, ffn_conv_w, ffn_conv_b, l, row0=0, n_rows=P_ROWS,
                               tm=1024, tn=512, tail=8)
        st = state_ffn_conv[l]
        fix1 = jnp.pad(st[:, 1:2], ((0, 0), (0, DEC_SEQ - 1), (0, 0))).reshape(S_ROWS, 2 * D_FF)
        fix2 = jnp.pad(st, ((0, 0), (0, DEC_SEQ - 2), (0, 0))).reshape(S_ROWS, 2 * D_FF)
        h, tg_s, tv_s = ffn_in(xb, w_ffn_in, ffn_conv_w, ffn_conv_b, l, h, row0=P_ROWS, n_rows=S_ROWS,
                               tm=S_ROWS, tn=512, tail=S_ROWS, fix1=fix1, fix2=fix2)
        x, xb = matmul_residual_ln(h, w_ffn_out_b, l, x, ln3_g, ln3_b, tm=256)

        sf = proj[P_ROWS:]
        la_p, lc_p = min(SWA_BACK * CHUNK, SEQ), min(BAND_BACK * CHUNK, SEQ)
        seq_tail = lambda n, c0, w: jnp.stack(
            [proj[(b + 1) * SEQ - n:(b + 1) * SEQ, c0:c0 + w] for b in range(BATCH)], 0)
        outs["p_ak"].append(_undup(seq_tail(la_p, COL_KA2, 256).reshape(BATCH * la_p, 256))
                            .reshape(BATCH, la_p, 2, 64).astype(F32))
        outs["p_av"].append(_undup(seq_tail(la_p, COL_VA2, 256).reshape(BATCH * la_p, 256))
                            .reshape(BATCH, la_p, 2, 64).astype(F32))
        outs["p_rs"].append(rs_p.reshape(BATCH, RET_HEADS, RET_DK, RET_DV))
        outs["p_bk"].append(seq_tail(lc_p, COL_KC, 1024).reshape(BATCH, lc_p, 16, 64).astype(F32))
        outs["p_bv"].append(seq_tail(lc_p, COL_VC, 1024).reshape(BATCH, lc_p, 16, 64).astype(F32))
        last = [(b + 1) * (SEQ // 1024) - 1 for b in range(BATCH)]
        outs["p_fc"].append(jnp.stack(
            [jnp.concatenate([tg_p[t, 6:8], tv_p[t, 6:8]], -1) for t in last], 0))
        outs["p_mk"].append(mk.reshape(BATCH, MEM_LEN, MEM_HEADS, MEM_HD))
        outs["p_mv"].append(mv.reshape(BATCH, MEM_LEN, MEM_HEADS, MEM_HD))
        outs["s_ak"].append(_undup(sf[:, COL_KA2:COL_KA2 + 256]).reshape(DEC_BATCH, DEC_SEQ, 2, 64).astype(F32))
        outs["s_av"].append(_undup(sf[:, COL_VA2:COL_VA2 + 256]).reshape(DEC_BATCH, DEC_SEQ, 2, 64).astype(F32))
        outs["s_rs"].append(rs_s.reshape(DEC_BATCH, RET_HEADS, RET_DK, RET_DV))
        outs["s_bk"].append(sf[:, COL_KC:COL_KC + 1024].reshape(DEC_BATCH, DEC_SEQ, 16, 64).astype(F32))
        outs["s_bv"].append(sf[:, COL_VC:COL_VC + 1024].reshape(DEC_BATCH, DEC_SEQ, 16, 64).astype(F32))
        u_s = jnp.concatenate([tg_s[0], tv_s[0]], -1).reshape(DEC_BATCH, DEC_SEQ, 2 * D_FF)
        outs["s_fc"].append(u_s[:, DEC_SEQ - 2:])

    st = lambda name: jnp.stack(outs[name], 0)
    return (x[:P_ROWS].reshape(BATCH, SEQ, D_MODEL), x[P_ROWS:].reshape(DEC_BATCH, DEC_SEQ, D_MODEL),
            st("p_ak"), st("p_av"), st("p_rs"), st("p_bk"), st("p_bv"), st("p_fc"), st("p_mk"), st("p_mv"),
            st("s_ak"), st("s_av"), st("s_rs"), st("s_bk"), st("s_bv"), st("s_fc"))
```

```python
import functools
import math

import numpy as np
import jax
import jax.numpy as jnp
from jax import lax
from jax.experimental import pallas as pl
from jax.experimental.pallas import tpu as pltpu

F32 = jnp.float32
BF16 = jnp.bfloat16

D_MODEL = 2048
BATCH = 2
SEQ = 4096
DEPTH = 2
DEC_BATCH = 16
DEC_SEQ = 16
PAST_LEN = 2048
CHUNK = 64
HEAD_DIM = 64
SWA_BACK = 2
SWA_HEADS = 16
SWA_KV_HEADS = 2
T5_BUCKETS = 32
T5_MAX_DIST = 128
RET_HEADS = 8
RET_DK = 64
RET_DV = 128
ROPE_BASE = 10000.0
RET_NORM_EPS = 1e-5
BAND_BACK = 8
BAND_HEADS = 16
BAND_MAX_REL = 256
MEM_LEN = 256
MEM_HEADS = 4
MEM_HD = D_MODEL // MEM_HEADS
D_FF = 5632
CONV_W = 3
DN_ALPHA = (2 * DEPTH) ** 0.25
LN_EPS = 1e-5

P_ROWS = BATCH * SEQ
S_ROWS = DEC_BATCH * DEC_SEQ
ROWS = P_ROWS + S_ROWS

COL_QA = 0
COL_QC = 1024
COL_KC = 2048
COL_VC = 3072
COL_VB = 4096
COL_GR = 5120
COL_GA = 6144
COL_GB = 8192
COL_GC = 10240
COL_QB = 12288
COL_KB = 12800
COL_KA2 = 13312
COL_VA2 = 13568
PROJ_COLS = 13824

LANES = 128
MXU_COLS = 256
QB = 256
RET_L = 256
NEG = -1e30
VMEM_LIMIT = 48 * 1024 * 1024

NT = (((1,), (1,)), ((), ()))
TN = (((0,), (0,)), ((), ()))


def _params(sem, vmem=VMEM_LIMIT):
    return pltpu.CompilerParams(dimension_semantics=sem, vmem_limit_bytes=vmem)


def _w_in_permute_kernel(tbl_ref, src_ref, o_ref):
    mode = tbl_ref[1, pl.program_id(1)]
    t = src_ref[...]

    @pl.when(mode == 0)
    def _():
        o_ref[...] = t.astype(BF16)

    for m, half in ((1, t[:, :LANES]), (2, t[:, LANES:])):
        @pl.when(mode == m)
        def _():
            swapped = pltpu.roll(half, HEAD_DIM, 1)
            o_ref[...] = jnp.concatenate([jnp.where(_head_lanes(0), half, swapped),
                                          jnp.where(_head_lanes(0), swapped, half)], 1).astype(BF16)


def permute_w_in(w_in):
    blk = 256
    src = lambda off, width: list(range(off // blk, (off + width) // blk))
    order = (src(0, 1024) + src(4352, 1024) + src(5376, 1024) + src(6400, 1024) + src(2304, 1024)
             + src(3328, 1024) + src(7424, 2048) + src(9472, 2048) + src(11520, 2048) + src(1280, 512)
             + src(1792, 512))
    kv = 1024 // blk
    tbl = np.array([order + [kv, kv], [0] * len(order) + [1, 2]], np.int32)
    assert tbl.shape[1] * blk == PROJ_COLS
    d = w_in.shape[1]
    return pl.pallas_call(
        _w_in_permute_kernel,
        grid_spec=pltpu.PrefetchScalarGridSpec(
            num_scalar_prefetch=1,
            grid=(DEPTH, PROJ_COLS // blk),
            in_specs=[pl.BlockSpec((None, d, blk), lambda l, j, tbl: (l, 0, tbl[0, j]))],
            out_specs=pl.BlockSpec((None, d, blk), lambda l, j, tbl: (l, 0, j)),
        ),
        out_shape=jax.ShapeDtypeStruct((DEPTH, d, PROJ_COLS), BF16),
        compiler_params=_params(("parallel", "parallel")),
        name="permute_w_in",
    )(jnp.asarray(tbl), w_in)


def _mm_kernel(a_ref, b_ref, o_ref, *scratch):
    if scratch:
        bb_ref, = scratch

        @pl.when(pl.program_id(0) == 0)
        def _():
            bb_ref[...] = b_ref[...].astype(BF16)

        b = bb_ref[...]
    else:
        b = b_ref[...].astype(BF16)
    o_ref[...] = jnp.dot(a_ref[...], b, preferred_element_type=F32).astype(o_ref.dtype)


def matmul(a, b, layer, *, tm, tn, out_dtype):
    m, k = a.shape
    n = b.shape[2]
    ni, nj = m // tm, n // tn
    resident = b.dtype == F32 and ni > 1
    assert b.dtype == BF16 or ni == 1 or nj == 1
    b_mode = dict(pipeline_mode=pl.Buffered(1)) if resident else {}
    return pl.pallas_call(
        _mm_kernel,
        grid=(ni, nj),
        in_specs=[pl.BlockSpec((tm, k), lambda i, j: (i, 0)),
                  pl.BlockSpec((None, k, tn), lambda i, j: (layer, 0, j), **b_mode)],
        out_specs=pl.BlockSpec((tm, tn), lambda i, j: (i, j)),
        out_shape=jax.ShapeDtypeStruct((m, n), out_dtype),
        scratch_shapes=[pltpu.VMEM((k, tn), BF16)] if resident else [],
        compiler_params=_params(("arbitrary", "arbitrary") if resident else ("parallel", "parallel")),
        name="matmul",
    )(a, b)


def _gate_mm_kernel(oa_ref, ob_ref, oc_ref, wa_ref, wb_ref, wc_ref, ga_ref, gb_ref, gc_ref, o_ref):
    acc = None
    for o, w, g in ((oa_ref, wa_ref, ga_ref), (ob_ref, wb_ref, gb_ref), (oc_ref, wc_ref, gc_ref)):
        t = jnp.dot(o[...], w[...], preferred_element_type=F32) * jax.nn.sigmoid(g[...].astype(F32))
        acc = t if acc is None else acc + t
    o_ref[...] = acc.astype(o_ref.dtype)


def gated_branch_sum(oa, ob, oc, wa, wb, wc, proj, layer, *, tm=528):
    m, k = oa.shape
    n = wa.shape[2]
    o_spec = pl.BlockSpec((tm, k), lambda i: (i, 0))
    w_spec = pl.BlockSpec((None, k, n), lambda i: (layer, 0, 0), pipeline_mode=pl.Buffered(1))

    def g_spec(col):
        return pl.BlockSpec((tm, n), lambda i: (i, col // n))

    return pl.pallas_call(
        _gate_mm_kernel,
        grid=(m // tm,),
        in_specs=[o_spec, o_spec, o_spec, w_spec, w_spec, w_spec,
                  g_spec(COL_GA), g_spec(COL_GB), g_spec(COL_GC)],
        out_specs=pl.BlockSpec((tm, n), lambda i: (i, 0)),
        out_shape=jax.ShapeDtypeStruct((m, n), BF16),
        compiler_params=_params(("parallel",)),
        name="gated_branch_sum",
    )(oa, ob, oc, wa, wb, wc, proj, proj, proj)


def _mm_ln_kernel(a_ref, b_ref, r_ref, g_ref, bt_ref, of_ref, ob_ref, acc_ref):
    @pl.when(pl.program_id(0) == 0)
    def _():
        acc_ref[...] = jnp.zeros_like(acc_ref)

    y = DN_ALPHA * r_ref[...] + acc_ref[...]
    mu = jnp.mean(y, -1, keepdims=True)
    d = y - mu
    var = jnp.mean(d * d, -1, keepdims=True)
    out = d * lax.rsqrt(var + LN_EPS) * g_ref[...] + bt_ref[...]
    of_ref[...] = out
    ob_ref[...] = out.astype(BF16)
    acc_ref[...] = jnp.dot(a_ref[...], b_ref[...], preferred_element_type=F32)


def matmul_residual_ln(a, b, layer, resid, gain, bias, *, tm):
    m, kk = a.shape
    n = b.shape[2]
    nt = m // tm
    cur = lambda i: (jnp.minimum(i, nt - 1), 0)
    prev = lambda i: (jnp.maximum(i - 1, 0), 0)
    return pl.pallas_call(
        _mm_ln_kernel,
        grid=(nt + 1,),
        in_specs=[pl.BlockSpec((tm, kk), cur),
                  pl.BlockSpec((None, kk, n), lambda i: (layer, 0, 0), pipeline_mode=pl.Buffered(1)),
                  pl.BlockSpec((tm, n), prev),
                  pl.BlockSpec((None, 1, n), lambda i: (layer, 0, 0)),
                  pl.BlockSpec((None, 1, n), lambda i: (layer, 0, 0))],
        out_specs=[pl.BlockSpec((tm, n), prev), pl.BlockSpec((tm, n), prev)],
        out_shape=[jax.ShapeDtypeStruct((m, n), F32), jax.ShapeDtypeStruct((m, n), BF16)],
        scratch_shapes=[pltpu.VMEM((tm, n), F32)],
        compiler_params=_params(("arbitrary",)),
        name="matmul_residual_ln",
    )(a, b, resid, gain.reshape(DEPTH, 1, n), bias.reshape(DEPTH, 1, n))


def _head_lanes(j):
    lane = lax.broadcasted_iota(jnp.int32, (1, LANES), 1)
    return (lane < HEAD_DIM) if j == 0 else (lane >= HEAD_DIM)


def _band_attn_kernel(sink_ref, diag_ref, q_ref, *refs, n_prev_blocks, n_back, shared_kv, has_sink, skew):
    nk = n_prev_blocks + 1
    k_refs = refs[:nk]
    v_refs = refs[nk:2 * nk]
    o_ref = refs[2 * nk]
    bias_ref = refs[2 * nk + 1]
    b = pl.program_id(0)
    i = pl.program_id(1)
    n_prev_rows = n_back * CHUNK
    kw = n_prev_rows + QB

    @pl.when((b == 0) & (i == 0))
    def _():
        n = diag_ref.shape[2]
        qc = lax.broadcasted_iota(jnp.int32, (QB, kw), 0) // CHUNK
        cb = lax.broadcasted_iota(jnp.int32, (QB, kw), 1) // CHUNK
        allowed = (cb >= qc) & (cb - n_back <= qc)
        for h in range(16):
            t = pltpu.roll(jnp.broadcast_to(diag_ref[h, 0:1, :], (QB, n)), 0, 1, stride=1, stride_axis=0)
            bias_ref[h] = jnp.where(allowed, t[:, :kw], NEG)

    krow = lax.broadcasted_iota(jnp.int32, (kw, 1), 0)
    kmask = jnp.where(krow >= jnp.maximum(n_prev_rows - i * QB, 0), 0.0, NEG).astype(BF16)
    lane = lax.broadcasted_iota(jnp.int32, (1, LANES), 1)

    def kv_cols(h):
        c0 = (h // 8 if shared_kv else h // 2) * LANES
        return slice(c0, c0 + LANES)

    def scores(h):
        p, j = divmod(h, 2)
        q = q_ref[:, p * LANES:(p + 1) * LANES] * (HEAD_DIM ** -0.5)
        k_all = jnp.concatenate([r[:, kv_cols(h)] for r in k_refs], axis=0)
        mask_lane = lane == (HEAD_DIM if j == 0 else 0)
        qj = jnp.where(mask_lane, jnp.ones_like(q), jnp.where(_head_lanes(j), q, jnp.zeros_like(q)))
        kj = jnp.where(mask_lane, kmask, k_all)
        return lax.dot_general(qj, kj, NT, preferred_element_type=F32)

    def softmax(h, s_all):
        sink = sink_ref[h] if has_sink else None
        ps, dens = [], []
        for r in range(QB // CHUNK):
            rows = slice(r * CHUNK, (r + 1) * CHUNK)
            lo = r * CHUNK // LANES * LANES
            hi = min(kw, -(-(r * CHUNK + n_prev_rows + CHUNK) // LANES) * LANES)
            s = s_all[rows, lo:hi] + bias_ref[h, rows, lo:hi]
            m = jnp.max(s, -1, keepdims=True)
            if has_sink:
                m = jnp.maximum(m, sink)
            e = jnp.exp(s - m)
            den = jnp.sum(e, -1, keepdims=True)
            if has_sink:
                den = den + jnp.exp(sink - m)
            parts = [e.astype(BF16)]
            if lo:
                parts.insert(0, jnp.zeros((CHUNK, lo), BF16))
            if hi < kw:
                parts.append(jnp.zeros((CHUNK, kw - hi), BF16))
            ps.append(jnp.concatenate(parts, 1))
            dens.append(den)
        return jnp.concatenate(ps, 0), jnp.concatenate(dens, 0)

    def weighted_values(h, p_all, den):
        v_all = jnp.concatenate([r[:, kv_cols(h)] for r in v_refs], axis=0)
        return jnp.dot(p_all, v_all, preferred_element_type=F32) / den

    lag = 16 if skew is None else skew
    s_q, p_q, outs = {}, {}, {}
    for t in range(16 + 2 * lag):
        if t < 16:
            s_q[t] = scores(t)
        if lag <= t < 16 + lag:
            p_q[t - lag] = softmax(t - lag, s_q.pop(t - lag))
        if t >= 2 * lag:
            h = t - 2 * lag
            outs[h] = weighted_values(h, *p_q.pop(h))
            if h % 2:
                p = h // 2
                o_ref[:, p * LANES:(p + 1) * LANES] = jnp.where(
                    _head_lanes(0), outs.pop(h - 1), outs.pop(h)).astype(o_ref.dtype)


def band_attention_prompt(proj, diag, sink, *, q_col, k_col, v_col, shared_kv, n_back, has_sink, skew):
    n_prev_rows = n_back * CHUNK
    nqb = SEQ // QB
    if n_prev_rows >= QB:
        n_prev_blocks, pb = n_prev_rows // QB, QB
    else:
        n_prev_blocks, pb = 1, n_prev_rows
    per = QB // pb
    kvw = 256 if shared_kv else 1024

    def prev_spec(col, back):
        return pl.BlockSpec(
            (pb, kvw), lambda b, i: (b * (SEQ // pb) + jnp.maximum(i * per - back, 0), col // kvw))

    def own_spec(col):
        return pl.BlockSpec((QB, kvw), lambda b, i: (b * nqb + i, col // kvw))

    k_specs = [prev_spec(k_col, n_prev_blocks - t) for t in range(n_prev_blocks)] + [own_spec(k_col)]
    v_specs = [prev_spec(v_col, n_prev_blocks - t) for t in range(n_prev_blocks)] + [own_spec(v_col)]
    kern = functools.partial(_band_attn_kernel, n_prev_blocks=n_prev_blocks, n_back=n_back,
                             shared_kv=shared_kv, has_sink=has_sink, skew=skew)
    n_in = 2 * (n_prev_blocks + 1)
    return pl.pallas_call(
        kern,
        grid=(BATCH, nqb),
        in_specs=[pl.BlockSpec(memory_space=pltpu.SMEM),
                  pl.BlockSpec(diag.shape, lambda b, i: (0, 0, 0)),
                  pl.BlockSpec((QB, 1024), lambda b, i: (b * nqb + i, q_col // 1024))]
                 + k_specs + v_specs,
        out_specs=pl.BlockSpec((QB, 1024), lambda b, i: (b * nqb + i, 0)),
        out_shape=jax.ShapeDtypeStruct((ROWS, 16 * HEAD_DIM), BF16),
        scratch_shapes=[pltpu.VMEM((16, QB, n_prev_rows + QB), F32)],
        compiler_params=_params(("arbitrary", "arbitrary")),
        name="band_attention_prompt",
    )(sink, diag, proj, *([proj] * n_in))


def _cached_attn_kernel(sink_ref, q_ref, kn_ref, vn_ref, kc_ref, vc_ref, diag_ref, bn_ref, _, o_ref, bc_ref,
                        *, shared_kv, has_sink):
    lc = kc_ref.shape[1]

    @pl.when(pl.program_id(0) == 0)
    def _():
        n = diag_ref.shape[2]
        for h in range(16):
            t = pltpu.roll(jnp.broadcast_to(diag_ref[h, 0:1, :], (DEC_SEQ, n)), 0, 1, stride=1, stride_axis=0)
            bc_ref[h] = t[:, :lc]

    kv_cols = [(p // 4 if shared_kv else p) * LANES for p in range(8)]
    scores = []
    for p in range(8):
        c0 = kv_cols[p]
        q = q_ref[:, p * LANES:(p + 1) * LANES] * (HEAD_DIM ** -0.5)
        kc = kc_ref[0, :, c0:c0 + LANES].astype(BF16)
        kn = kn_ref[:, c0:c0 + LANES]
        for j in range(2):
            h = 2 * p + j
            qj = jnp.where(_head_lanes(j), q, jnp.zeros_like(q))
            scores.append((lax.dot_general(qj, kc, NT, preferred_element_type=F32) + bc_ref[h],
                           lax.dot_general(qj, kn, NT, preferred_element_type=F32) + bn_ref[h]))
    probs = []
    for h, (s_c, s_n) in enumerate(scores):
        m = jnp.maximum(jnp.max(s_c, -1, keepdims=True), jnp.max(s_n, -1, keepdims=True))
        if has_sink:
            m = jnp.maximum(m, sink_ref[h])
        e_c, e_n = jnp.exp(s_c - m), jnp.exp(s_n - m)
        den = jnp.sum(e_c, -1, keepdims=True) + jnp.sum(e_n, -1, keepdims=True)
        if has_sink:
            den = den + jnp.exp(sink_ref[h] - m)
        probs.append((e_c.astype(BF16), e_n.astype(BF16), den))
    for p in range(8):
        c0 = kv_cols[p]
        vc = vc_ref[0, :, c0:c0 + LANES].astype(BF16)
        vn = vn_ref[:, c0:c0 + LANES]
        outs = []
        for j in range(2):
            e_c, e_n, den = probs[2 * p + j]
            outs.append((jnp.dot(e_c, vc, preferred_element_type=F32)
                         + jnp.dot(e_n, vn, preferred_element_type=F32)) / den)
        o_ref[:, p * LANES:(p + 1) * LANES] = jnp.where(_head_lanes(0), outs[0], outs[1]).astype(o_ref.dtype)


def cached_attention_sample(proj, k_cache, v_cache, layer, diag_c, bias_n, sink, o_all, *, q_col, k_col, v_col,
                            shared_kv, has_sink):
    lc, wc = k_cache.shape[2], k_cache.shape[3]
    r0 = P_ROWS // DEC_SEQ
    kern = functools.partial(_cached_attn_kernel, shared_kv=shared_kv, has_sink=has_sink)
    cache = pl.BlockSpec((None, 1, lc, wc), lambda s: (layer, s, 0, 0))
    return pl.pallas_call(
        kern,
        grid=(DEC_BATCH,),
        in_specs=[pl.BlockSpec(memory_space=pltpu.SMEM),
                  pl.BlockSpec((DEC_SEQ, 1024), lambda s: (r0 + s, q_col // 1024)),
                  pl.BlockSpec((DEC_SEQ, wc), lambda s: (r0 + s, k_col // wc)),
                  pl.BlockSpec((DEC_SEQ, wc), lambda s: (r0 + s, v_col // wc)),
                  cache, cache,
                  pl.BlockSpec(diag_c.shape, lambda s: (0, 0, 0)),
                  pl.BlockSpec((16, DEC_SEQ, DEC_SEQ), lambda s: (0, 0, 0)),
                  pl.BlockSpec(memory_space=pl.ANY)],
        out_specs=pl.BlockSpec((DEC_SEQ, 1024), lambda s: (r0 + s, 0)),
        out_shape=jax.ShapeDtypeStruct(o_all.shape, o_all.dtype),
        scratch_shapes=[pltpu.VMEM((16, DEC_SEQ, lc), F32)],
        input_output_aliases={8: 0},
        compiler_params=_params(("arbitrary",)),
        name="cached_attention_sample",
    )(sink, proj, proj, proj, k_cache, v_cache, diag_c, bias_n, o_all)


def _ret_kernel(q_ref, k_ref, v_ref, g_ref, cos_ref, sin_ref, s0_ref, *refs, blk_len):
    o_ref, sout_ref, st_ref, dec_ref = refs[-4:]
    blk = pl.program_id(1)
    log_gs = [math.log(1.0 - 2.0 ** (-5.0 - h)) for h in range(RET_HEADS)]

    @pl.when((pl.program_id(0) == 0) & (blk == 0))
    def _():
        ii = lax.broadcasted_iota(jnp.int32, (blk_len, blk_len), 0)
        jj = lax.broadcasted_iota(jnp.int32, (blk_len, blk_len), 1)
        diff = (ii - jj).astype(F32)
        for h in range(RET_HEADS):
            dec_ref[h] = jnp.where(diff >= 0, jnp.exp(log_gs[h] * jnp.maximum(diff, 0.0)), 0.0)

    @pl.when(blk == 0)
    def _():
        st_ref[...] = s0_ref[0]

    cos = cos_ref[...]
    sin = sin_ref[...]
    lane = lax.broadcasted_iota(jnp.int32, (1, LANES), 1)
    low_half = (lane % RET_DK) < (RET_DK // 2)

    def rope(x):
        x = x.astype(F32)
        swapped = jnp.where(low_half, pltpu.roll(x, LANES - RET_DK // 2, 1), pltpu.roll(x, RET_DK // 2, 1))
        return x * cos + swapped * sin

    row = lax.broadcasted_iota(jnp.int32, (blk_len, 1), 0).astype(F32)
    srow = lax.broadcasted_iota(jnp.int32, (LANES, 1), 0)

    for p in range(RET_HEADS // 2):
        qr = rope(q_ref[:, p * LANES:(p + 1) * LANES])
        kr = rope(k_ref[:, p * LANES:(p + 1) * LANES]) * (RET_DK ** -0.5)
        kb = kr.astype(BF16)
        state = st_ref[p]
        state_b = state.astype(BF16)
        upd = None
        for j in range(2):
            h = 2 * p + j
            log_g = log_gs[h]
            qj = jnp.where(_head_lanes(j), qr, 0.0).astype(BF16)
            vh = v_ref[:, h * RET_DV:(h + 1) * RET_DV]
            qk = lax.dot_general(qj, kb, NT, preferred_element_type=F32) * dec_ref[h]
            o = jnp.dot(qk.astype(BF16), vh, preferred_element_type=F32)
            o = o + jnp.dot(qj, state_b, preferred_element_type=F32) * jnp.exp(log_g * (row + 1.0))
            mu = jnp.mean(o, -1, keepdims=True)
            d = o - mu
            var = jnp.mean(d * d, -1, keepdims=True)
            gate = g_ref[:, h * RET_DV:(h + 1) * RET_DV].astype(F32)
            o_ref[:, h * RET_DV:(h + 1) * RET_DV] = (
                d * lax.rsqrt(var + RET_NORM_EPS) * (gate * jax.nn.sigmoid(gate))).astype(o_ref.dtype)
            kwj = jnp.where(_head_lanes(j), kr * jnp.exp(log_g * (blk_len - 1.0 - row)), 0.0).astype(BF16)
            u = lax.dot_general(kwj, vh, TN, preferred_element_type=F32)
            upd = u if upd is None else upd + u
        carry = jnp.where(srow < RET_DK, math.exp(log_gs[2 * p] * blk_len), math.exp(log_gs[2 * p + 1] * blk_len))
        st_ref[p] = carry * state + upd

    @pl.when(blk == pl.num_programs(1) - 1)
    def _():
        sout_ref[0] = st_ref[...]


def retention(proj, cos, sin, s0, layer, o_all=None, *, row0, n_seq, seq_len, blk_len, pos_per_blk):
    nb = seq_len // blk_len
    rb0 = row0 // blk_len

    def rows(col, width):
        return pl.BlockSpec((blk_len, width), lambda b, t: (rb0 + b * nb + t, col // width))

    tab = pl.BlockSpec((blk_len, LANES), lambda b, t: (t if pos_per_blk else 0, 0))
    st = pl.BlockSpec((1, 4, LANES, LANES), lambda b, t: (b, 0, 0, 0))
    st_in = pl.BlockSpec((None, 1, 4, LANES, LANES), lambda b, t: (layer, b, 0, 0, 0))
    in_specs = [rows(COL_QB, 512), rows(COL_KB, 512), rows(COL_VB, 1024), rows(COL_GR, 1024), tab, tab, st_in]
    args = [proj, proj, proj, proj, cos, sin, s0]
    aliases = {}
    if o_all is not None:
        in_specs.append(pl.BlockSpec(memory_space=pl.ANY))
        args.append(o_all)
        aliases = {len(args) - 1: 0}
    return pl.pallas_call(
        functools.partial(_ret_kernel, blk_len=blk_len),
        grid=(n_seq, nb),
        in_specs=in_specs,
        out_specs=[rows(0, 1024), st],
        out_shape=[jax.ShapeDtypeStruct((ROWS, 1024), BF16),
                   jax.ShapeDtypeStruct((n_seq, 4, LANES, LANES), F32)],
        scratch_shapes=[pltpu.VMEM((4, LANES, LANES), F32), pltpu.VMEM((RET_HEADS, blk_len, blk_len), F32)],
        input_output_aliases=aliases,
        compiler_params=_params(("arbitrary", "arbitrary")),
        name="retention",
    )(*args)


def _mem_attn_kernel(q_ref, k_ref, v_ref, *refs, heads):
    o_ref = refs[-1]
    cols = [slice(h * MEM_HD, (h + 1) * MEM_HD) for h in range(heads)]
    scores = [lax.dot_general(q_ref[:, c], k_ref[0, :, c].astype(BF16), NT, preferred_element_type=F32)
              * (MEM_HD ** -0.5) for c in cols]
    probs = []
    for s in scores:
        e = jnp.exp(s - jnp.max(s, -1, keepdims=True))
        probs.append((e.astype(BF16), jnp.sum(e, -1, keepdims=True)))
    for c, (e, den) in zip(cols, probs):
        o_ref[:, c] = (jnp.dot(e, v_ref[0, :, c].astype(BF16), preferred_element_type=F32) / den
                       ).astype(o_ref.dtype)


def mem_attention(qm, k_src, v_src, layer, o_all=None, *, row0, n_rows, tm, heads, rows_per_kv):
    rb0 = row0 // tm
    w = heads * MEM_HD
    kv = pl.BlockSpec((None, 1, MEM_LEN, w), lambda i, h: (layer, i * tm // rows_per_kv, 0, h))
    in_specs = [pl.BlockSpec((tm, w), lambda i, h: (rb0 + i, h)), kv, kv]
    args = [qm, k_src, v_src]
    aliases = {}
    if o_all is not None:
        in_specs.append(pl.BlockSpec(memory_space=pl.ANY))
        args.append(o_all)
        aliases = {3: 0}
    return pl.pallas_call(
        functools.partial(_mem_attn_kernel, heads=heads),
        grid=(n_rows // tm, MEM_HEADS // heads),
        in_specs=in_specs,
        out_specs=pl.BlockSpec((tm, w), lambda i, h: (rb0 + i, h)),
        out_shape=jax.ShapeDtypeStruct((ROWS, D_MODEL), BF16),
        input_output_aliases=aliases,
        compiler_params=_params(("parallel", "parallel")),
        name="mem_attention",
    )(*args)


def _gelu(x):
    return 0.5 * x * (1.0 + lax.erf(x * (2.0 ** -0.5)))


def _ffn_in_kernel(*refs, sample, seq_tiles, tail):
    h_ref, tg_ref, tv_ref, bg_ref, bv_ref = refs[-5:]
    if sample:
        a_ref, wfg_ref, wfv_ref, wg_ref, wv_ref, cbg_ref, cbv_ref, f1g_ref, f1v_ref, f2g_ref, f2v_ref = refs[:11]
    else:
        a_ref, ah_ref, wfg_ref, wfv_ref, wg_ref, wv_ref, cbg_ref, cbv_ref = refs[:8]
    tm = a_ref.shape[0]
    i = pl.program_id(1)

    @pl.when(i == 0)
    def _():
        bg_ref[...] = wfg_ref[...].astype(BF16)
        bv_ref[...] = wfv_ref[...].astype(BF16)

    a = a_ref[...]
    if sample:
        pos = lax.broadcasted_iota(jnp.int32, (tm, 1), 0) % DEC_SEQ
    else:
        top = lax.broadcasted_iota(jnp.int32, (8, 1), 0)
        keep = (i % seq_tiles != 0).astype(F32)

    def project(cols, b_ref):
        u = jnp.dot(a, b_ref[:, cols], preferred_element_type=F32)
        uh = None if sample else jnp.dot(ah_ref[...], b_ref[:, cols], preferred_element_type=F32)
        return u, uh

    def conv(cols, u, uh, w_ref, cb_ref, f1_ref, f2_ref, t_ref):
        t_ref[0, :, cols] = u[tm - tail:, :]
        r1 = pltpu.roll(u, 1, 0)
        r2 = pltpu.roll(u, 2, 0)
        if sample:
            u1 = jnp.where(pos < 1, f1_ref[:, cols], r1)
            u2 = jnp.where(pos < 2, f2_ref[:, cols], r2)
        else:
            n = uh.shape[0]
            prev1, prev2 = uh[n - 1:n, :] * keep, uh[n - 2:n - 1, :] * keep
            u1 = jnp.concatenate([jnp.where(top < 1, prev1, r1[:8]), r1[8:]], 0)
            top2 = jnp.where(top < 1, prev2, jnp.where(top < 2, prev1, r2[:8]))
            u2 = jnp.concatenate([top2, r2[8:]], 0)
        w = w_ref[:, cols]
        return w[0:1, :] * u2 + w[1:2, :] * u1 + w[2:3, :] * u + cb_ref[:, cols]

    sub = [slice(c0, c0 + MXU_COLS) for c0 in range(0, h_ref.shape[1], MXU_COLS)]
    prods = [(project(cols, bg_ref), project(cols, bv_ref)) for cols in sub]
    for cols, ((ug, uhg), (uv, uhv)) in zip(sub, prods):
        if sample:
            cg = conv(cols, ug, uhg, wg_ref, cbg_ref, f1g_ref, f2g_ref, tg_ref)
            cv = conv(cols, uv, uhv, wv_ref, cbv_ref, f1v_ref, f2v_ref, tv_ref)
        else:
            cg = conv(cols, ug, uhg, wg_ref, cbg_ref, None, None, tg_ref)
            cv = conv(cols, uv, uhv, wv_ref, cbv_ref, None, None, tv_ref)
        h_ref[:, cols] = (_gelu(cg) * cv).astype(h_ref.dtype)


def ffn_in(x, w, conv_w, conv_b, layer, h_all=None, *, row0, n_rows, tm, tn, tail, fix1=None, fix2=None):
    sample = fix1 is not None
    k = x.shape[1]
    nj = D_FF // tn
    ni = n_rows // tm
    rb0 = row0 // tm
    halo = 16
    a_spec = pl.BlockSpec((tm, k), lambda j, i: (rb0 + i, 0))
    bg = pl.BlockSpec((None, k, tn), lambda j, i: (layer, 0, j))
    bv = pl.BlockSpec((None, k, tn), lambda j, i: (layer, 0, nj + j))
    wg = pl.BlockSpec((None, CONV_W, tn), lambda j, i: (layer, 0, j))
    wv = pl.BlockSpec((None, CONV_W, tn), lambda j, i: (layer, 0, nj + j))
    cg = pl.BlockSpec((None, 1, tn), lambda j, i: (layer, 0, j))
    cv = pl.BlockSpec((None, 1, tn), lambda j, i: (layer, 0, nj + j))
    conv_b3 = conv_b.reshape(DEPTH, 1, 2 * D_FF)
    if sample:
        fg = pl.BlockSpec((tm, tn), lambda j, i: (i, j))
        fv = pl.BlockSpec((tm, tn), lambda j, i: (i, nj + j))
        in_specs = [a_spec, bg, bv, wg, wv, cg, cv, fg, fv, fg, fv]
        args = [x, w, w, conv_w, conv_w, conv_b3, conv_b3, fix1, fix1, fix2, fix2]
    else:
        ah = pl.BlockSpec((halo, k), lambda j, i: (jnp.maximum((rb0 + i) * (tm // halo) - 1, 0), 0))
        in_specs = [a_spec, ah, bg, bv, wg, wv, cg, cv]
        args = [x, x, w, w, conv_w, conv_w, conv_b3, conv_b3]
    aliases = {}
    if h_all is not None:
        in_specs.append(pl.BlockSpec(memory_space=pl.ANY))
        args.append(h_all)
        aliases = {len(args) - 1: 0}
    t_spec = pl.BlockSpec((1, tail, tn), lambda j, i: (i, 0, j))
    return pl.pallas_call(
        functools.partial(_ffn_in_kernel, sample=sample, seq_tiles=SEQ // tm if not sample else 1, tail=tail),
        grid=(nj, ni),
        in_specs=in_specs,
        out_specs=[pl.BlockSpec((tm, tn), lambda j, i: (rb0 + i, j)), t_spec, t_spec],
        out_shape=[jax.ShapeDtypeStruct((ROWS, D_FF), BF16),
                   jax.ShapeDtypeStruct((ni, tail, D_FF), F32),
                   jax.ShapeDtypeStruct((ni, tail, D_FF), F32)],
        scratch_shapes=[pltpu.VMEM((k, tn), BF16), pltpu.VMEM((k, tn), BF16)],
        input_output_aliases=aliases,
        compiler_params=_params(("parallel", "arbitrary")),
        name="ffn_in",
    )(*args)


def _t5_bucket(rel):
    nb = T5_BUCKETS // 2
    max_exact = nb // 2
    n = jnp.abs(rel)
    nf = jnp.maximum(n, 1).astype(F32)
    large = max_exact + (jnp.log(nf / max_exact) / math.log(T5_MAX_DIST / max_exact)
                         * (nb - max_exact)).astype(jnp.int32)
    large = jnp.minimum(large, nb - 1)
    return jnp.where(rel > 0, nb, 0) + jnp.where(n < max_exact, n, large)


def _t5_bias(table, rel):
    return jnp.transpose(table[_t5_bucket(rel)], (2, 0, 1)).astype(F32)


def _clipped_bias(table, rel):
    return table[:, jnp.clip(rel, -BAND_MAX_REL, BAND_MAX_REL) + BAND_MAX_REL].astype(F32)


def _toeplitz_diag(bias_fn, n_rows, n_cols, rel0, n):
    assert n_cols + n_rows - 1 <= n
    k = np.arange(n)
    rel = np.where(k < n_cols, k, k - n) + rel0
    return jnp.broadcast_to(bias_fn(jnp.asarray(rel)[None, :]), (16, 8, n))


def _band_diag(bias_fn, n_back):
    return _toeplitz_diag(bias_fn, QB, n_back * CHUNK + QB, -n_back * CHUNK, 1024)


def _cache_diag(bias_fn, cache_len):
    n = max(256, cache_len * 2)
    return _toeplitz_diag(bias_fn, DEC_SEQ, cache_len, -cache_len, n)


def _rope_tables(pos):
    half = RET_DK // 2
    inv = ROPE_BASE ** (-jnp.arange(half, dtype=F32) / half)
    ang = pos.astype(F32)[:, None] * inv[None, :]
    cos, sin = jnp.cos(ang), jnp.sin(ang)
    cos_t = jnp.concatenate([cos, cos, cos, cos], -1)
    sin_t = jnp.concatenate([-sin, sin, -sin, sin], -1)
    return cos_t, sin_t


def _dup_groups(t):
    g0, g1 = t[..., 0, :], t[..., 1, :]
    return jnp.concatenate([g0, g0, g1, g1], -1)


def _undup(t):
    return jnp.stack([t[:, 0:64], t[:, 128:192]], 1)


def kernel(x_prompt, x_sample, mem_prompt, cache_swa_k, cache_swa_v, state_ret, cache_band_k, cache_band_v, state_ffn_conv, cache_mem_k, cache_mem_v, w_in, t5_table, swa_sink, band_rel_table, w_br_a, w_br_b, w_br_c, w_mix_o, ln1_g, ln1_b, w_mq, w_mk, w_mv, w_mo, ln2_g, ln2_b, w_ffn_in, ffn_conv_w, ffn_conv_b, w_ffn_out, ln3_g, ln3_b):
    x = jnp.concatenate([x_prompt.reshape(P_ROWS, D_MODEL), x_sample.reshape(S_ROWS, D_MODEL)], 0)
    xb = x.astype(BF16)
    memb = mem_prompt.reshape(BATCH * MEM_LEN, D_MODEL).astype(BF16)

    t5 = functools.partial(_t5_bias, t5_table)
    diag_a = _band_diag(t5, SWA_BACK)
    qpos = PAST_LEN + jnp.arange(DEC_SEQ)
    la, lc = cache_swa_k.shape[2], cache_band_k.shape[2]
    rel_n = qpos[None, :] - qpos[:, None]
    diag_a_c, bias_a_n = _cache_diag(t5, la), t5(rel_n)
    cos_p, sin_p = _rope_tables(jnp.arange(SEQ))
    cos_s, sin_s = _rope_tables(qpos)
    zero_state = jnp.zeros((1, BATCH, 4, LANES, LANES), F32)
    no_sink = jnp.zeros((16,), F32)

    w_in_b = permute_w_in(w_in)
    w_br_a_b, w_br_b_b, w_br_c_b = w_br_a.astype(BF16), w_br_b.astype(BF16), w_br_c.astype(BF16)
    w_mix_o_b, w_mo_b = w_mix_o.astype(BF16), w_mo.astype(BF16)
    w_ffn_out_b = w_ffn_out.astype(BF16)
    swa_k2, swa_v2 = _dup_groups(cache_swa_k), _dup_groups(cache_swa_v)
    band_k2 = cache_band_k.reshape(DEPTH, DEC_BATCH, lc, BAND_HEADS * HEAD_DIM)
    band_v2 = cache_band_v.reshape(DEPTH, DEC_BATCH, lc, BAND_HEADS * HEAD_DIM)
    mem_k2 = cache_mem_k.reshape(DEPTH, DEC_BATCH, MEM_LEN, D_MODEL)
    mem_v2 = cache_mem_v.reshape(DEPTH, DEC_BATCH, MEM_LEN, D_MODEL)
    ret_s0 = state_ret.reshape(DEPTH, DEC_BATCH, 4, LANES, LANES)

    outs = {k: [] for k in ("p_ak", "p_av", "p_rs", "p_bk", "p_bv", "p_fc", "p_mk", "p_mv",
                            "s_ak", "s_av", "s_rs", "s_bk", "s_bv", "s_fc")}
    for l in range(DEPTH):
        proj = matmul(xb, w_in_b, l, tm=1056, tn=1536, out_dtype=BF16)
        clipped = functools.partial(_clipped_bias, band_rel_table[l])
        oa = band_attention_prompt(proj, diag_a, swa_sink[l], q_col=COL_QA, k_col=COL_KA2, v_col=COL_VA2,
                                   shared_kv=True, n_back=SWA_BACK, has_sink=True, skew=None)
        oc = band_attention_prompt(proj, _band_diag(clipped, BAND_BACK), no_sink, q_col=COL_QC, k_col=COL_KC,
                                   v_col=COL_VC, shared_kv=False, n_back=BAND_BACK, has_sink=False, skew=1)
        ob, rs_p = retention(proj, cos_p, sin_p, zero_state, 0, row0=0, n_seq=BATCH, seq_len=SEQ,
                             blk_len=RET_L, pos_per_blk=True)
        oa = cached_attention_sample(
            proj, swa_k2, swa_v2, l, diag_a_c, bias_a_n, swa_sink[l], oa,
            q_col=COL_QA, k_col=COL_KA2, v_col=COL_VA2, shared_kv=True, has_sink=True)
        oc = cached_attention_sample(
            proj, band_k2, band_v2, l, _cache_diag(clipped, lc), clipped(rel_n), no_sink, oc,
            q_col=COL_QC, k_col=COL_KC, v_col=COL_VC, shared_kv=False, has_sink=False)
        ob, rs_s = retention(proj, cos_s, sin_s, ret_s0, l, ob, row0=P_ROWS, n_seq=DEC_BATCH, seq_len=DEC_SEQ,
                             blk_len=DEC_SEQ, pos_per_blk=False)
        mix = gated_branch_sum(oa, ob, oc, w_br_a_b, w_br_b_b, w_br_c_b, proj, l)
        x, xb = matmul_residual_ln(mix, w_mix_o_b, l, x, ln1_g, ln1_b, tm=528)

        mk = matmul(memb, w_mk, l, tm=512, tn=1024, out_dtype=F32)
        mv = matmul(memb, w_mv, l, tm=512, tn=1024, out_dtype=F32)
        qm = matmul(xb, w_mq, l, tm=1056, tn=2048, out_dtype=BF16)
        om = mem_attention(qm, mk.reshape(1, BATCH, MEM_LEN, D_MODEL), mv.reshape(1, BATCH, MEM_LEN, D_MODEL), 0,
                           row0=0, n_rows=P_ROWS, tm=1024, heads=MEM_HEADS, rows_per_kv=SEQ)
        om = mem_attention(qm, mem_k2, mem_v2, l, om, row0=P_ROWS, n_rows=S_ROWS, tm=DEC_SEQ,
                           heads=MEM_HEADS, rows_per_kv=DEC_SEQ)
        x, xb = matmul_residual_ln(om, w_mo_b, l, x, ln2_g, ln2_b, tm=528)

        h, tg_p, tv_p = ffn_in(xb, w_ffn_in, ffn_conv_w, ffn_conv_b, l, row0=0, n_rows=P_ROWS,
                               tm=1024, tn=512, tail=8)
        st = state_ffn_conv[l]
        fix1 = jnp.pad(st[:, 1:2], ((0, 0), (0, DEC_SEQ - 1), (0, 0))).reshape(S_ROWS, 2 * D_FF)
        fix2 = jnp.pad(st, ((0, 0), (0, DEC_SEQ - 2), (0, 0))).reshape(S_ROWS, 2 * D_FF)
        h, tg_s, tv_s = ffn_in(xb, w_ffn_in, ffn_conv_w, ffn_conv_b, l, h, row0=P_ROWS, n_rows=S_ROWS,
                               tm=S_ROWS, tn=512, tail=S_ROWS, fix1=fix1, fix2=fix2)
        x, xb = matmul_residual_ln(h, w_ffn_out_b, l, x, ln3_g, ln3_b, tm=256)

        sf = proj[P_ROWS:]
        la_p, lc_p = min(SWA_BACK * CHUNK, SEQ), min(BAND_BACK * CHUNK, SEQ)
        seq_tail = lambda n, c0, w: jnp.stack(
            [proj[(b + 1) * SEQ - n:(b + 1) * SEQ, c0:c0 + w] for b in range(BATCH)], 0)
        outs["p_ak"].append(_undup(seq_tail(la_p, COL_KA2, 256).reshape(BATCH * la_p, 256))
                            .reshape(BATCH, la_p, 2, 64).astype(F32))
        outs["p_av"].append(_undup(seq_tail(la_p, COL_VA2, 256).reshape(BATCH * la_p, 256))
                            .reshape(BATCH, la_p, 2, 64).astype(F32))
        outs["p_rs"].append(rs_p.reshape(BATCH, RET_HEADS, RET_DK, RET_DV))
        outs["p_bk"].append(seq_tail(lc_p, COL_KC, 1024).reshape(BATCH, lc_p, 16, 64).astype(F32))
        outs["p_bv"].append(seq_tail(lc_p, COL_VC, 1024).reshape(BATCH, lc_p, 16, 64).astype(F32))
        last = [(b + 1) * (SEQ // 1024) - 1 for b in range(BATCH)]
        outs["p_fc"].append(jnp.stack(
            [jnp.concatenate([tg_p[t, 6:8], tv_p[t, 6:8]], -1) for t in last], 0))
        outs["p_mk"].append(mk.reshape(BATCH, MEM_LEN, MEM_HEADS, MEM_HD))
        outs["p_mv"].append(mv.reshape(BATCH, MEM_LEN, MEM_HEADS, MEM_HD))
        outs["s_ak"].append(_undup(sf[:, COL_KA2:COL_KA2 + 256]).reshape(DEC_BATCH, DEC_SEQ, 2, 64).astype(F32))
        outs["s_av"].append(_undup(sf[:, COL_VA2:COL_VA2 + 256]).reshape(DEC_BATCH, DEC_SEQ, 2, 64).astype(F32))
        outs["s_rs"].append(rs_s.reshape(DEC_BATCH, RET_HEADS, RET_DK, RET_DV))
        outs["s_bk"].append(sf[:, COL_KC:COL_KC + 1024].reshape(DEC_BATCH, DEC_SEQ, 16, 64).astype(F32))
        outs["s_bv"].append(sf[:, COL_VC:COL_VC + 1024].reshape(DEC_BATCH, DEC_SEQ, 16, 64).astype(F32))
        u_s = jnp.concatenate([tg_s[0], tv_s[0]], -1).reshape(DEC_BATCH, DEC_SEQ, 2 * D_FF)
        outs["s_fc"].append(u_s[:, DEC_SEQ - 2:])

    st = lambda name: jnp.stack(outs[name], 0)
    return (x[:P_ROWS].reshape(BATCH, SEQ, D_MODEL), x[P_ROWS:].reshape(DEC_BATCH, DEC_SEQ, D_MODEL),
            st("p_ak"), st("p_av"), st("p_rs"), st("p_bk"), st("p_bv"), st("p_fc"), st("p_mk"), st("p_mv"),
            st("s_ak"), st("s_av"), st("s_rs"), st("s_bk"), st("s_bv"), st("s_fc"))
```

```python
import functools
import math

import numpy as np
import jax
import jax.numpy as jnp
from jax import lax
from jax.experimental import pallas as pl
from jax.experimental.pallas import tpu as pltpu

F32 = jnp.float32
BF16 = jnp.bfloat16

D_MODEL = 2048
BATCH = 2
SEQ = 4096
DEPTH = 2
DEC_BATCH = 16
DEC_SEQ = 16
PAST_LEN = 2048
CHUNK = 64
HEAD_DIM = 64
SWA_BACK = 2
SWA_HEADS = 16
SWA_KV_HEADS = 2
T5_BUCKETS = 32
T5_MAX_DIST = 128
RET_HEADS = 8
RET_DK = 64
RET_DV = 128
ROPE_BASE = 10000.0
RET_NORM_EPS = 1e-5
BAND_BACK = 8
BAND_HEADS = 16
BAND_MAX_REL = 256
MEM_LEN = 256
MEM_HEADS = 4
MEM_HD = D_MODEL // MEM_HEADS
D_FF = 5632
CONV_W = 3
DN_ALPHA = (2 * DEPTH) ** 0.25
LN_EPS = 1e-5

P_ROWS = BATCH * SEQ
S_ROWS = DEC_BATCH * DEC_SEQ
ROWS = P_ROWS + S_ROWS

COL_QA = 0
COL_QC = 1024
COL_KC = 2048
COL_VC = 3072
COL_VB = 4096
COL_GR = 5120
COL_GA = 6144
COL_GB = 8192
COL_GC = 10240
COL_QB = 12288
COL_KB = 12800
COL_KA2 = 13312
COL_VA2 = 13568
PROJ_COLS = 13824

LANES = 128
MXU_COLS = 256
QB = 256
RET_L = 256
NEG = -1e30
LOG2E = math.log2(math.e)
VMEM_LIMIT = 48 * 1024 * 1024

NT = (((1,), (1,)), ((), ()))
TN = (((0,), (0,)), ((), ()))


def _params(sem, vmem=VMEM_LIMIT):
    return pltpu.CompilerParams(dimension_semantics=sem, vmem_limit_bytes=vmem)


def _pack_rows_kernel(p_ref, s_ref, of_ref, ob_ref, *, n_prompt_tiles):
    src = jnp.where(pl.program_id(0) < n_prompt_tiles, p_ref[...], s_ref[...])
    of_ref[...] = src
    ob_ref[...] = src.astype(BF16)


def pack_rows(x_prompt, x_sample):
    tm = S_ROWS
    npt = P_ROWS // tm
    return pl.pallas_call(
        functools.partial(_pack_rows_kernel, n_prompt_tiles=npt),
        grid=(npt + 1,),
        in_specs=[pl.BlockSpec((tm, D_MODEL), lambda i: (jnp.minimum(i, npt - 1), 0)),
                  pl.BlockSpec((tm, D_MODEL), lambda i: (0, 0))],
        out_specs=[pl.BlockSpec((tm, D_MODEL), lambda i: (i, 0)), pl.BlockSpec((tm, D_MODEL), lambda i: (i, 0))],
        out_shape=[jax.ShapeDtypeStruct((ROWS, D_MODEL), F32), jax.ShapeDtypeStruct((ROWS, D_MODEL), BF16)],
        compiler_params=_params(("parallel",)),
        name="pack_rows",
    )(x_prompt.reshape(P_ROWS, D_MODEL), x_sample.reshape(S_ROWS, D_MODEL))


def _w_in_permute_kernel(tbl_ref, src_ref, o_ref):
    mode = tbl_ref[1, pl.program_id(1)]
    t = src_ref[...]

    @pl.when(mode == 0)
    def _():
        o_ref[...] = t.astype(BF16)

    for m, half in ((1, t[:, :LANES]), (2, t[:, LANES:])):
        @pl.when(mode == m)
        def _():
            swapped = pltpu.roll(half, HEAD_DIM, 1)
            o_ref[...] = jnp.concatenate([jnp.where(_head_lanes(0), half, swapped),
                                          jnp.where(_head_lanes(0), swapped, half)], 1).astype(BF16)


def permute_w_in(w_in):
    blk = 256
    src = lambda off, width: list(range(off // blk, (off + width) // blk))
    order = (src(0, 1024) + src(4352, 1024) + src(5376, 1024) + src(6400, 1024) + src(2304, 1024)
             + src(3328, 1024) + src(7424, 2048) + src(9472, 2048) + src(11520, 2048) + src(1280, 512)
             + src(1792, 512))
    kv = 1024 // blk
    tbl = np.array([order + [kv, kv], [0] * len(order) + [1, 2]], np.int32)
    assert tbl.shape[1] * blk == PROJ_COLS
    d = w_in.shape[1]
    return pl.pallas_call(
        _w_in_permute_kernel,
        grid_spec=pltpu.PrefetchScalarGridSpec(
            num_scalar_prefetch=1,
            grid=(DEPTH, PROJ_COLS // blk),
            in_specs=[pl.BlockSpec((None, d, blk), lambda l, j, tbl: (l, 0, tbl[0, j]))],
            out_specs=pl.BlockSpec((None, d, blk), lambda l, j, tbl: (l, 0, j)),
        ),
        out_shape=jax.ShapeDtypeStruct((DEPTH, d, PROJ_COLS), BF16),
        compiler_params=_params(("parallel", "parallel")),
        name="permute_w_in",
    )(jnp.asarray(tbl), w_in)


def _mm_kernel(a_ref, b_ref, o_ref, *scratch):
    if scratch:
        bb_ref, = scratch

        @pl.when(pl.program_id(0) == 0)
        def _():
            bb_ref[...] = b_ref[...].astype(BF16)

        b = bb_ref[...]
    else:
        b = b_ref[...].astype(BF16)
    o_ref[...] = jnp.dot(a_ref[...], b, preferred_element_type=F32).astype(o_ref.dtype)


def matmul(a, b, layer, *, tm, tn, out_dtype):
    m, k = a.shape
    n = b.shape[2]
    ni, nj = m // tm, n // tn
    resident = b.dtype == F32 and ni > 1
    assert b.dtype == BF16 or ni == 1 or nj == 1
    b_mode = dict(pipeline_mode=pl.Buffered(1)) if resident else {}
    return pl.pallas_call(
        _mm_kernel,
        grid=(ni, nj),
        in_specs=[pl.BlockSpec((tm, k), lambda i, j: (i, 0)),
                  pl.BlockSpec((None, k, tn), lambda i, j: (layer, 0, j), **b_mode)],
        out_specs=pl.BlockSpec((tm, tn), lambda i, j: (i, j)),
        out_shape=jax.ShapeDtypeStruct((m, n), out_dtype),
        scratch_shapes=[pltpu.VMEM((k, tn), BF16)] if resident else [],
        compiler_params=_params(("arbitrary", "arbitrary") if resident else ("parallel", "parallel")),
        name="matmul",
    )(a, b)


def _gate_mm_kernel(oa_ref, ob_ref, oc_ref, wa_ref, wb_ref, wc_ref, ga_ref, gb_ref, gc_ref, o_ref):
    acc = None
    for o, w, g in ((oa_ref, wa_ref, ga_ref), (ob_ref, wb_ref, gb_ref), (oc_ref, wc_ref, gc_ref)):
        t = jnp.dot(o[...], w[...], preferred_element_type=F32) * jax.nn.sigmoid(g[...].astype(F32))
        acc = t if acc is None else acc + t
    o_ref[...] = acc.astype(o_ref.dtype)


def gated_branch_sum(oa, ob, oc, wa, wb, wc, proj, layer, *, tm=528):
    m, k = oa.shape
    n = wa.shape[2]
    o_spec = pl.BlockSpec((tm, k), lambda i: (i, 0))
    w_spec = pl.BlockSpec((None, k, n), lambda i: (layer, 0, 0), pipeline_mode=pl.Buffered(1))

    def g_spec(col):
        return pl.BlockSpec((tm, n), lambda i: (i, col // n))

    return pl.pallas_call(
        _gate_mm_kernel,
        grid=(m // tm,),
        in_specs=[o_spec, o_spec, o_spec, w_spec, w_spec, w_spec,
                  g_spec(COL_GA), g_spec(COL_GB), g_spec(COL_GC)],
        out_specs=pl.BlockSpec((tm, n), lambda i: (i, 0)),
        out_shape=jax.ShapeDtypeStruct((m, n), BF16),
        compiler_params=_params(("parallel",)),
        name="gated_branch_sum",
    )(oa, ob, oc, wa, wb, wc, proj, proj, proj)


def _mm_ln_kernel(a_ref, b_ref, r_ref, g_ref, bt_ref, o1_ref, o2_ref, acc_ref, *, split_tiles):
    i = pl.program_id(0)

    @pl.when(i == 0)
    def _():
        acc_ref[...] = jnp.zeros_like(acc_ref)

    y = DN_ALPHA * r_ref[...] + acc_ref[...]
    mu = jnp.mean(y, -1, keepdims=True)
    d = y - mu
    var = jnp.mean(d * d, -1, keepdims=True)
    out = d * lax.rsqrt(var + LN_EPS) * g_ref[...] + bt_ref[...]
    if split_tiles is None:
        o1_ref[...] = out
        o2_ref[...] = out.astype(BF16)
    else:
        o2_ref[...] = out

        @pl.when(i <= split_tiles)
        def _():
            o1_ref[...] = out
    acc_ref[...] = jnp.dot(a_ref[...], b_ref[...], preferred_element_type=F32)


def matmul_residual_ln(a, b, layer, resid, gain, bias, *, tm, split_rows=None):
    m, kk = a.shape
    n = b.shape[2]
    nt = m // tm
    cur = lambda i: (jnp.minimum(i, nt - 1), 0)
    prev = lambda i: (jnp.maximum(i - 1, 0), 0)
    if split_rows is None:
        split_tiles = None
        out_specs = [pl.BlockSpec((tm, n), prev), pl.BlockSpec((tm, n), prev)]
        out_shape = [jax.ShapeDtypeStruct((m, n), F32), jax.ShapeDtypeStruct((m, n), BF16)]
    else:
        split_tiles = split_rows // tm
        assert split_tiles == nt - 1
        out_specs = [pl.BlockSpec((tm, n), lambda i: (jnp.clip(i - 1, 0, split_tiles - 1), 0)),
                     pl.BlockSpec((tm, n), lambda i: (0, 0))]
        out_shape = [jax.ShapeDtypeStruct((split_rows, n), F32), jax.ShapeDtypeStruct((tm, n), F32)]
    return pl.pallas_call(
        functools.partial(_mm_ln_kernel, split_tiles=split_tiles),
        grid=(nt + 1,),
        in_specs=[pl.BlockSpec((tm, kk), cur),
                  pl.BlockSpec((None, kk, n), lambda i: (layer, 0, 0), pipeline_mode=pl.Buffered(1)),
                  pl.BlockSpec((tm, n), prev),
                  pl.BlockSpec((None, 1, n), lambda i: (layer, 0, 0)),
                  pl.BlockSpec((None, 1, n), lambda i: (layer, 0, 0))],
        out_specs=out_specs,
        out_shape=out_shape,
        scratch_shapes=[pltpu.VMEM((tm, n), F32)],
        compiler_params=_params(("arbitrary",)),
        name="matmul_residual_ln",
    )(a, b, resid, gain.reshape(DEPTH, 1, n), bias.reshape(DEPTH, 1, n))


def _head_lanes(j):
    lane = lax.broadcasted_iota(jnp.int32, (1, LANES), 1)
    return (lane < HEAD_DIM) if j == 0 else (lane >= HEAD_DIM)


def _band_attn_kernel(sink_ref, diag_ref, q_ref, *refs, n_prev_blocks, n_back, shared_kv, has_sink, skew):
    nk = n_prev_blocks + 1
    k_refs = refs[:nk]
    v_refs = refs[nk:2 * nk]
    o_ref = refs[2 * nk]
    bias_ref = refs[2 * nk + 1]
    b = pl.program_id(0)
    i = pl.program_id(1)
    n_prev_rows = n_back * CHUNK
    kw = n_prev_rows + QB

    @pl.when((b == 0) & (i == 0))
    def _():
        n = diag_ref.shape[2]
        qc = lax.broadcasted_iota(jnp.int32, (QB, kw), 0) // CHUNK
        cb = lax.broadcasted_iota(jnp.int32, (QB, kw), 1) // CHUNK
        allowed = (cb >= qc) & (cb - n_back <= qc)
        for h in range(16):
            t = pltpu.roll(jnp.broadcast_to(diag_ref[h, 0:1, :], (QB, n)), 0, 1, stride=1, stride_axis=0)
            bias_ref[h] = jnp.where(allowed, t[:, :kw] * LOG2E, NEG)

    krow = lax.broadcasted_iota(jnp.int32, (kw, 1), 0)
    kmask = jnp.where(krow >= jnp.maximum(n_prev_rows - i * QB, 0), 0.0, NEG).astype(BF16)
    lane = lax.broadcasted_iota(jnp.int32, (1, LANES), 1)

    def kv_cols(h):
        c0 = (h // 8 if shared_kv else h // 2) * LANES
        return slice(c0, c0 + LANES)

    def scores(h):
        p, j = divmod(h, 2)
        q = q_ref[:, p * LANES:(p + 1) * LANES] * (HEAD_DIM ** -0.5 * LOG2E)
        k_all = jnp.concatenate([r[:, kv_cols(h)] for r in k_refs], axis=0)
        mask_lane = lane == (HEAD_DIM if j == 0 else 0)
        qj = jnp.where(mask_lane, jnp.ones_like(q), jnp.where(_head_lanes(j), q, jnp.zeros_like(q)))
        kj = jnp.where(mask_lane, kmask, k_all)
        return lax.dot_general(qj, kj, NT, preferred_element_type=F32)

    def softmax(h, s_all):
        sink = sink_ref[h] * LOG2E if has_sink else None
        ps, dens = [], []
        for r in range(QB // CHUNK):
            rows = slice(r * CHUNK, (r + 1) * CHUNK)
            lo = r * CHUNK // LANES * LANES
            hi = min(kw, -(-(r * CHUNK + n_prev_rows + CHUNK) // LANES) * LANES)
            s = s_all[rows, lo:hi] + bias_ref[h, rows, lo:hi]
            m = jnp.max(s, -1, keepdims=True)
            if has_sink:
                m = jnp.maximum(m, sink)
            e = jnp.exp2(s - m)
            den = jnp.sum(e, -1, keepdims=True)
            if has_sink:
                den = den + jnp.exp2(sink - m)
            parts = [e.astype(BF16)]
            if lo:
                parts.insert(0, jnp.zeros((CHUNK, lo), BF16))
            if hi < kw:
                parts.append(jnp.zeros((CHUNK, kw - hi), BF16))
            ps.append(jnp.concatenate(parts, 1))
            dens.append(den)
        return jnp.concatenate(ps, 0), jnp.concatenate(dens, 0)

    def weighted_values(h, p_all, den):
        v_all = jnp.concatenate([r[:, kv_cols(h)] for r in v_refs], axis=0)
        return jnp.dot(p_all, v_all, preferred_element_type=F32) / den

    lag = 16 if skew is None else skew
    s_q, p_q, outs = {}, {}, {}
    for t in range(16 + 2 * lag):
        if t < 16:
            s_q[t] = scores(t)
        if lag <= t < 16 + lag:
            p_q[t - lag] = softmax(t - lag, s_q.pop(t - lag))
        if t >= 2 * lag:
            h = t - 2 * lag
            outs[h] = weighted_values(h, *p_q.pop(h))
            if h % 2:
                p = h // 2
                o_ref[:, p * LANES:(p + 1) * LANES] = jnp.where(
                    _head_lanes(0), outs.pop(h - 1), outs.pop(h)).astype(o_ref.dtype)


def band_attention_prompt(proj, diag, sink, *, q_col, k_col, v_col, shared_kv, n_back, has_sink, skew):
    n_prev_rows = n_back * CHUNK
    nqb = SEQ // QB
    if n_prev_rows >= QB:
        n_prev_blocks, pb = n_prev_rows // QB, QB
    else:
        n_prev_blocks, pb = 1, n_prev_rows
    per = QB // pb
    kvw = 256 if shared_kv else 1024

    def prev_spec(col, back):
        return pl.BlockSpec(
            (pb, kvw), lambda b, i: (b * (SEQ // pb) + jnp.maximum(i * per - back, 0), col // kvw))

    def own_spec(col):
        return pl.BlockSpec((QB, kvw), lambda b, i: (b * nqb + i, col // kvw))

    k_specs = [prev_spec(k_col, n_prev_blocks - t) for t in range(n_prev_blocks)] + [own_spec(k_col)]
    v_specs = [prev_spec(v_col, n_prev_blocks - t) for t in range(n_prev_blocks)] + [own_spec(v_col)]
    kern = functools.partial(_band_attn_kernel, n_prev_blocks=n_prev_blocks, n_back=n_back,
                             shared_kv=shared_kv, has_sink=has_sink, skew=skew)
    n_in = 2 * (n_prev_blocks + 1)
    return pl.pallas_call(
        kern,
        grid=(BATCH, nqb),
        in_specs=[pl.BlockSpec(memory_space=pltpu.SMEM),
                  pl.BlockSpec(diag.shape, lambda b, i: (0, 0, 0)),
                  pl.BlockSpec((QB, 1024), lambda b, i: (b * nqb + i, q_col // 1024))]
                 + k_specs + v_specs,
        out_specs=pl.BlockSpec((QB, 1024), lambda b, i: (b * nqb + i, 0)),
        out_shape=jax.ShapeDtypeStruct((ROWS, 16 * HEAD_DIM), BF16),
        scratch_shapes=[pltpu.VMEM((16, QB, n_prev_rows + QB), F32)],
        compiler_params=_params(("arbitrary", "arbitrary")),
        name="band_attention_prompt",
    )(sink, diag, proj, *([proj] * n_in))


def _cached_attn_kernel(sink_ref, q_ref, kn_ref, vn_ref, kc_ref, vc_ref, diag_ref, bn_ref, _, o_ref, bc_ref,
                        *, shared_kv, has_sink):
    lc = kc_ref.shape[1]

    @pl.when(pl.program_id(0) == 0)
    def _():
        n = diag_ref.shape[2]
        for h in range(16):
            t = pltpu.roll(jnp.broadcast_to(diag_ref[h, 0:1, :], (DEC_SEQ, n)), 0, 1, stride=1, stride_axis=0)
            bc_ref[h] = t[:, :lc]

    kv_cols = [(p // 4 if shared_kv else p) * LANES for p in range(8)]
    scores = []
    for p in range(8):
        c0 = kv_cols[p]
        q = q_ref[:, p * LANES:(p + 1) * LANES] * (HEAD_DIM ** -0.5)
        kc = kc_ref[0, :, c0:c0 + LANES].astype(BF16)
        kn = kn_ref[:, c0:c0 + LANES]
        for j in range(2):
            h = 2 * p + j
            qj = jnp.where(_head_lanes(j), q, jnp.zeros_like(q))
            scores.append((lax.dot_general(qj, kc, NT, preferred_element_type=F32) + bc_ref[h],
                           lax.dot_general(qj, kn, NT, preferred_element_type=F32) + bn_ref[h]))
    probs = []
    for h, (s_c, s_n) in enumerate(scores):
        m = jnp.maximum(jnp.max(s_c, -1, keepdims=True), jnp.max(s_n, -1, keepdims=True))
        if has_sink:
            m = jnp.maximum(m, sink_ref[h])
        e_c, e_n = jnp.exp(s_c - m), jnp.exp(s_n - m)
        den = jnp.sum(e_c, -1, keepdims=True) + jnp.sum(e_n, -1, keepdims=True)
        if has_sink:
            den = den + jnp.exp(sink_ref[h] - m)
        probs.append((e_c.astype(BF16), e_n.astype(BF16), den))
    for p in range(8):
        c0 = kv_cols[p]
        vc = vc_ref[0, :, c0:c0 + LANES].astype(BF16)
        vn = vn_ref[:, c0:c0 + LANES]
        outs = []
        for j in range(2):
            e_c, e_n, den = probs[2 * p + j]
            outs.append((jnp.dot(e_c, vc, preferred_element_type=F32)
                         + jnp.dot(e_n, vn, preferred_element_type=F32)) / den)
        o_ref[:, p * LANES:(p + 1) * LANES] = jnp.where(_head_lanes(0), outs[0], outs[1]).astype(o_ref.dtype)


def cached_attention_sample(proj, k_cache, v_cache, layer, diag_c, bias_n, sink, o_all, *, q_col, k_col, v_col,
                            shared_kv, has_sink):
    lc, wc = k_cache.shape[2], k_cache.shape[3]
    r0 = P_ROWS // DEC_SEQ
    kern = functools.partial(_cached_attn_kernel, shared_kv=shared_kv, has_sink=has_sink)
    cache = pl.BlockSpec((None, 1, lc, wc), lambda s: (layer, s, 0, 0))
    return pl.pallas_call(
        kern,
        grid=(DEC_BATCH,),
        in_specs=[pl.BlockSpec(memory_space=pltpu.SMEM),
                  pl.BlockSpec((DEC_SEQ, 1024), lambda s: (r0 + s, q_col // 1024)),
                  pl.BlockSpec((DEC_SEQ, wc), lambda s: (r0 + s, k_col // wc)),
                  pl.BlockSpec((DEC_SEQ, wc), lambda s: (r0 + s, v_col // wc)),
                  cache, cache,
                  pl.BlockSpec(diag_c.shape, lambda s: (0, 0, 0)),
                  pl.BlockSpec((16, DEC_SEQ, DEC_SEQ), lambda s: (0, 0, 0)),
                  pl.BlockSpec(memory_space=pl.ANY)],
        out_specs=pl.BlockSpec((DEC_SEQ, 1024), lambda s: (r0 + s, 0)),
        out_shape=jax.ShapeDtypeStruct(o_all.shape, o_all.dtype),
        scratch_shapes=[pltpu.VMEM((16, DEC_SEQ, lc), F32)],
        input_output_aliases={8: 0},
        compiler_params=_params(("arbitrary",)),
        name="cached_attention_sample",
    )(sink, proj, proj, proj, k_cache, v_cache, diag_c, bias_n, o_all)


def _ret_kernel(q_ref, k_ref, v_ref, g_ref, cos_ref, sin_ref, s0_ref, *refs, blk_len):
    o_ref, sout_ref, st_ref, dec_ref = refs[-4:]
    blk = pl.program_id(1)
    log_gs = [math.log(1.0 - 2.0 ** (-5.0 - h)) for h in range(RET_HEADS)]

    @pl.when((pl.program_id(0) == 0) & (blk == 0))
    def _():
        ii = lax.broadcasted_iota(jnp.int32, (blk_len, blk_len), 0)
        jj = lax.broadcasted_iota(jnp.int32, (blk_len, blk_len), 1)
        diff = (ii - jj).astype(F32)
        for h in range(RET_HEADS):
            dec_ref[h] = jnp.where(diff >= 0, jnp.exp(log_gs[h] * jnp.maximum(diff, 0.0)), 0.0)

    @pl.when(blk == 0)
    def _():
        st_ref[...] = s0_ref[0]

    cos = cos_ref[...]
    sin = sin_ref[...]
    lane = lax.broadcasted_iota(jnp.int32, (1, LANES), 1)
    low_half = (lane % RET_DK) < (RET_DK // 2)

    def rope(x):
        x = x.astype(F32)
        swapped = jnp.where(low_half, pltpu.roll(x, LANES - RET_DK // 2, 1), pltpu.roll(x, RET_DK // 2, 1))
        return x * cos + swapped * sin

    row = lax.broadcasted_iota(jnp.int32, (blk_len, 1), 0).astype(F32)
    srow = lax.broadcasted_iota(jnp.int32, (LANES, 1), 0)

    for p in range(RET_HEADS // 2):
        qr = rope(q_ref[:, p * LANES:(p + 1) * LANES])
        kr = rope(k_ref[:, p * LANES:(p + 1) * LANES]) * (RET_DK ** -0.5)
        kb = kr.astype(BF16)
        state = st_ref[p]
        state_b = state.astype(BF16)
        upd = None
        for j in range(2):
            h = 2 * p + j
            log_g = log_gs[h]
            qj = jnp.where(_head_lanes(j), qr, 0.0).astype(BF16)
            vh = v_ref[:, h * RET_DV:(h + 1) * RET_DV]
            qk = lax.dot_general(qj, kb, NT, preferred_element_type=F32) * dec_ref[h]
            o = jnp.dot(qk.astype(BF16), vh, preferred_element_type=F32)
            o = o + jnp.dot(qj, state_b, preferred_element_type=F32) * jnp.exp(log_g * (row + 1.0))
            mu = jnp.mean(o, -1, keepdims=True)
            d = o - mu
            var = jnp.mean(d * d, -1, keepdims=True)
            gate = g_ref[:, h * RET_DV:(h + 1) * RET_DV].astype(F32)
            o_ref[:, h * RET_DV:(h + 1) * RET_DV] = (
                d * lax.rsqrt(var + RET_NORM_EPS) * (gate * jax.nn.sigmoid(gate))).astype(o_ref.dtype)
            kwj = jnp.where(_head_lanes(j), kr * jnp.exp(log_g * (blk_len - 1.0 - row)), 0.0).astype(BF16)
            u = lax.dot_general(kwj, vh, TN, preferred_element_type=F32)
            upd = u if upd is None else upd + u
        carry = jnp.where(srow < RET_DK, math.exp(log_gs[2 * p] * blk_len), math.exp(log_gs[2 * p + 1] * blk_len))
        st_ref[p] = carry * state + upd

    @pl.when(blk == pl.num_programs(1) - 1)
    def _():
        sout_ref[0] = st_ref[...]


def retention(proj, cos, sin, s0, layer, o_all=None, *, row0, n_seq, seq_len, blk_len, pos_per_blk):
    nb = seq_len // blk_len
    rb0 = row0 // blk_len

    def rows(col, width):
        return pl.BlockSpec((blk_len, width), lambda b, t: (rb0 + b * nb + t, col // width))

    tab = pl.BlockSpec((blk_len, LANES), lambda b, t: (t if pos_per_blk else 0, 0))
    st = pl.BlockSpec((1, 4, LANES, LANES), lambda b, t: (b, 0, 0, 0))
    st_in = pl.BlockSpec((None, 1, 4, LANES, LANES), lambda b, t: (layer, b, 0, 0, 0))
    in_specs = [rows(COL_QB, 512), rows(COL_KB, 512), rows(COL_VB, 1024), rows(COL_GR, 1024), tab, tab, st_in]
    args = [proj, proj, proj, proj, cos, sin, s0]
    aliases = {}
    if o_all is not None:
        in_specs.append(pl.BlockSpec(memory_space=pl.ANY))
        args.append(o_all)
        aliases = {len(args) - 1: 0}
    return pl.pallas_call(
        functools.partial(_ret_kernel, blk_len=blk_len),
        grid=(n_seq, nb),
        in_specs=in_specs,
        out_specs=[rows(0, 1024), st],
        out_shape=[jax.ShapeDtypeStruct((ROWS, 1024), BF16),
                   jax.ShapeDtypeStruct((n_seq, 4, LANES, LANES), F32)],
        scratch_shapes=[pltpu.VMEM((4, LANES, LANES), F32), pltpu.VMEM((RET_HEADS, blk_len, blk_len), F32)],
        input_output_aliases=aliases,
        compiler_params=_params(("arbitrary", "arbitrary")),
        name="retention",
    )(*args)


def _mem_attn_kernel(q_ref, k_ref, v_ref, *refs, heads):
    o_ref = refs[-1]
    cols = [slice(h * MEM_HD, (h + 1) * MEM_HD) for h in range(heads)]
    scores = [lax.dot_general(q_ref[:, c], k_ref[0, :, c].astype(BF16), NT, preferred_element_type=F32)
              * (MEM_HD ** -0.5) for c in cols]
    probs = []
    for s in scores:
        e = jnp.exp(s - jnp.max(s, -1, keepdims=True))
        probs.append((e.astype(BF16), jnp.sum(e, -1, keepdims=True)))
    for c, (e, den) in zip(cols, probs):
        o_ref[:, c] = (jnp.dot(e, v_ref[0, :, c].astype(BF16), preferred_element_type=F32) / den
                       ).astype(o_ref.dtype)


def mem_attention(qm, k_src, v_src, layer, o_all=None, *, row0, n_rows, tm, heads, rows_per_kv):
    rb0 = row0 // tm
    w = heads * MEM_HD
    kv = pl.BlockSpec((None, 1, MEM_LEN, w), lambda i, h: (layer, i * tm // rows_per_kv, 0, h))
    in_specs = [pl.BlockSpec((tm, w), lambda i, h: (rb0 + i, h)), kv, kv]
    args = [qm, k_src, v_src]
    aliases = {}
    if o_all is not None:
        in_specs.append(pl.BlockSpec(memory_space=pl.ANY))
        args.append(o_all)
        aliases = {3: 0}
    return pl.pallas_call(
        functools.partial(_mem_attn_kernel, heads=heads),
        grid=(n_rows // tm, MEM_HEADS // heads),
        in_specs=in_specs,
        out_specs=pl.BlockSpec((tm, w), lambda i, h: (rb0 + i, h)),
        out_shape=jax.ShapeDtypeStruct((ROWS, D_MODEL), BF16),
        input_output_aliases=aliases,
        compiler_params=_params(("parallel", "parallel")),
        name="mem_attention",
    )(*args)


def _gelu(x):
    return 0.5 * x * (1.0 + lax.erf(x * (2.0 ** -0.5)))


def _ffn_in_kernel(*refs, sample, seq_tiles, tail):
    h_ref, tg_ref, tv_ref, bg_ref, bv_ref = refs[-5:]
    if sample:
        a_ref, wfg_ref, wfv_ref, wg_ref, wv_ref, cbg_ref, cbv_ref, f1g_ref, f1v_ref, f2g_ref, f2v_ref = refs[:11]
    else:
        a_ref, ah_ref, wfg_ref, wfv_ref, wg_ref, wv_ref, cbg_ref, cbv_ref = refs[:8]
    tm = a_ref.shape[0]
    i = pl.program_id(1)

    @pl.when(i == 0)
    def _():
        bg_ref[...] = wfg_ref[...].astype(BF16)
        bv_ref[...] = wfv_ref[...].astype(BF16)

    a = a_ref[...]
    if sample:
        pos = lax.broadcasted_iota(jnp.int32, (tm, 1), 0) % DEC_SEQ
    else:
        top = lax.broadcasted_iota(jnp.int32, (8, 1), 0)
        keep = (i % seq_tiles != 0).astype(F32)

    def project(cols, b_ref):
        u = jnp.dot(a, b_ref[:, cols], preferred_element_type=F32)
        uh = None if sample else jnp.dot(ah_ref[...], b_ref[:, cols], preferred_element_type=F32)
        return u, uh

    def conv(cols, u, uh, w_ref, cb_ref, f1_ref, f2_ref, t_ref):
        t_ref[0, :, cols] = u[tm - tail:, :]
        r1 = pltpu.roll(u, 1, 0)
        r2 = pltpu.roll(u, 2, 0)
        if sample:
            u1 = jnp.where(pos < 1, f1_ref[:, cols], r1)
            u2 = jnp.where(pos < 2, f2_ref[:, cols], r2)
        else:
            n = uh.shape[0]
            prev1, prev2 = uh[n - 1:n, :] * keep, uh[n - 2:n - 1, :] * keep
            u1 = jnp.concatenate([jnp.where(top < 1, prev1, r1[:8]), r1[8:]], 0)
            top2 = jnp.where(top < 1, prev2, jnp.where(top < 2, prev1, r2[:8]))
            u2 = jnp.concatenate([top2, r2[8:]], 0)
        w = w_ref[:, cols]
        return w[0:1, :] * u2 + w[1:2, :] * u1 + w[2:3, :] * u + cb_ref[:, cols]

    sub = [slice(c0, c0 + MXU_COLS) for c0 in range(0, h_ref.shape[1], MXU_COLS)]
    prods = [(project(cols, bg_ref), project(cols, bv_ref)) for cols in sub]
    for cols, ((ug, uhg), (uv, uhv)) in zip(sub, prods):
        if sample:
            cg = conv(cols, ug, uhg, wg_ref, cbg_ref, f1g_ref, f2g_ref, tg_ref)
            cv = conv(cols, uv, uhv, wv_ref, cbv_ref, f1v_ref, f2v_ref, tv_ref)
        else:
            cg = conv(cols, ug, uhg, wg_ref, cbg_ref, None, None, tg_ref)
            cv = conv(cols, uv, uhv, wv_ref, cbv_ref, None, None, tv_ref)
        h_ref[:, cols] = (_gelu(cg) * cv).astype(h_ref.dtype)


def ffn_in(x, w, conv_w, conv_b, layer, h_all=None, *, row0, n_rows, tm, tn, tail, fix1=None, fix2=None):
    sample = fix1 is not None
    k = x.shape[1]
    nj = D_FF // tn
    ni = n_rows // tm
    rb0 = row0 // tm
    halo = 16
    a_spec = pl.BlockSpec((tm, k), lambda j, i: (rb0 + i, 0))
    bg = pl.BlockSpec((None, k, tn), lambda j, i: (layer, 0, j))
    bv = pl.BlockSpec((None, k, tn), lambda j, i: (layer, 0, nj + j))
    wg = pl.BlockSpec((None, CONV_W, tn), lambda j, i: (layer, 0, j))
    wv = pl.BlockSpec((None, CONV_W, tn), lambda j, i: (layer, 0, nj + j))
    cg = pl.BlockSpec((None, 1, tn), lambda j, i: (layer, 0, j))
    cv = pl.BlockSpec((None, 1, tn), lambda j, i: (layer, 0, nj + j))
    conv_b3 = conv_b.reshape(DEPTH, 1, 2 * D_FF)
    if sample:
        fg = pl.BlockSpec((tm, tn), lambda j, i: (i, j))
        fv = pl.BlockSpec((tm, tn), lambda j, i: (i, nj + j))
        in_specs = [a_spec, bg, bv, wg, wv, cg, cv, fg, fv, fg, fv]
        args = [x, w, w, conv_w, conv_w, conv_b3, conv_b3, fix1, fix1, fix2, fix2]
    else:
        ah = pl.BlockSpec((halo, k), lambda j, i: (jnp.maximum((rb0 + i) * (tm // halo) - 1, 0), 0))
        in_specs = [a_spec, ah, bg, bv, wg, wv, cg, cv]
        args = [x, x, w, w, conv_w, conv_w, conv_b3, conv_b3]
    aliases = {}
    if h_all is not None:
        in_specs.append(pl.BlockSpec(memory_space=pl.ANY))
        args.append(h_all)
        aliases = {len(args) - 1: 0}
    t_spec = pl.BlockSpec((1, tail, tn), lambda j, i: (i, 0, j))
    return pl.pallas_call(
        functools.partial(_ffn_in_kernel, sample=sample, seq_tiles=SEQ // tm if not sample else 1, tail=tail),
        grid=(nj, ni),
        in_specs=in_specs,
        out_specs=[pl.BlockSpec((tm, tn), lambda j, i: (rb0 + i, j)), t_spec, t_spec],
        out_shape=[jax.ShapeDtypeStruct((ROWS, D_FF), BF16),
                   jax.ShapeDtypeStruct((ni, tail, D_FF), F32),
                   jax.ShapeDtypeStruct((ni, tail, D_FF), F32)],
        scratch_shapes=[pltpu.VMEM((k, tn), BF16), pltpu.VMEM((k, tn), BF16)],
        input_output_aliases=aliases,
        compiler_params=_params(("parallel", "arbitrary")),
        name="ffn_in",
    )(*args)


def _t5_bucket(rel):
    nb = T5_BUCKETS // 2
    max_exact = nb // 2
    n = jnp.abs(rel)
    nf = jnp.maximum(n, 1).astype(F32)
    large = max_exact + (jnp.log(nf / max_exact) / math.log(T5_MAX_DIST / max_exact)
                         * (nb - max_exact)).astype(jnp.int32)
    large = jnp.minimum(large, nb - 1)
    return jnp.where(rel > 0, nb, 0) + jnp.where(n < max_exact, n, large)


def _t5_bias(table, rel):
    return jnp.transpose(table[_t5_bucket(rel)], (2, 0, 1)).astype(F32)


def _clipped_bias(table, rel):
    return table[:, jnp.clip(rel, -BAND_MAX_REL, BAND_MAX_REL) + BAND_MAX_REL].astype(F32)


def _toeplitz_diag(bias_fn, n_rows, n_cols, rel0, n):
    assert n_cols + n_rows - 1 <= n
    k = np.arange(n)
    rel = np.where(k < n_cols, k, k - n) + rel0
    return jnp.broadcast_to(bias_fn(jnp.asarray(rel)[None, :]), (16, 8, n))


def _band_diag(bias_fn, n_back):
    return _toeplitz_diag(bias_fn, QB, n_back * CHUNK + QB, -n_back * CHUNK, 1024)


def _cache_diag(bias_fn, cache_len):
    n = max(256, cache_len * 2)
    return _toeplitz_diag(bias_fn, DEC_SEQ, cache_len, -cache_len, n)


def _rope_tables(pos):
    half = RET_DK // 2
    inv = ROPE_BASE ** (-jnp.arange(half, dtype=F32) / half)
    ang = pos.astype(F32)[:, None] * inv[None, :]
    cos, sin = jnp.cos(ang), jnp.sin(ang)
    cos_t = jnp.concatenate([cos, cos, cos, cos], -1)
    sin_t = jnp.concatenate([-sin, sin, -sin, sin], -1)
    return cos_t, sin_t


def _dup_groups(t):
    g0, g1 = t[..., 0, :], t[..., 1, :]
    return jnp.concatenate([g0, g0, g1, g1], -1)


def _undup(t):
    return jnp.stack([t[:, 0:64], t[:, 128:192]], 1)


def kernel(x_prompt, x_sample, mem_prompt, cache_swa_k, cache_swa_v, state_ret, cache_band_k, cache_band_v, state_ffn_conv, cache_mem_k, cache_mem_v, w_in, t5_table, swa_sink, band_rel_table, w_br_a, w_br_b, w_br_c, w_mix_o, ln1_g, ln1_b, w_mq, w_mk, w_mv, w_mo, ln2_g, ln2_b, w_ffn_in, ffn_conv_w, ffn_conv_b, w_ffn_out, ln3_g, ln3_b):
    x, xb = pack_rows(x_prompt, x_sample)
    memb = mem_prompt.reshape(BATCH * MEM_LEN, D_MODEL).astype(BF16)

    t5 = functools.partial(_t5_bias, t5_table)
    diag_a = _band_diag(t5, SWA_BACK)
    qpos = PAST_LEN + jnp.arange(DEC_SEQ)
    la, lc = cache_swa_k.shape[2], cache_band_k.shape[2]
    rel_n = qpos[None, :] - qpos[:, None]
    diag_a_c, bias_a_n = _cache_diag(t5, la), t5(rel_n)
    cos_p, sin_p = _rope_tables(jnp.arange(SEQ))
    cos_s, sin_s = _rope_tables(qpos)
    zero_state = jnp.zeros((1, BATCH, 4, LANES, LANES), F32)
    no_sink = jnp.zeros((16,), F32)

    w_in_b = permute_w_in(w_in)
    w_br_a_b, w_br_b_b, w_br_c_b = w_br_a.astype(BF16), w_br_b.astype(BF16), w_br_c.astype(BF16)
    w_mix_o_b, w_mo_b = w_mix_o.astype(BF16), w_mo.astype(BF16)
    w_ffn_out_b = w_ffn_out.astype(BF16)
    swa_k2, swa_v2 = _dup_groups(cache_swa_k), _dup_groups(cache_swa_v)
    band_k2 = cache_band_k.reshape(DEPTH, DEC_BATCH, lc, BAND_HEADS * HEAD_DIM)
    band_v2 = cache_band_v.reshape(DEPTH, DEC_BATCH, lc, BAND_HEADS * HEAD_DIM)
    mem_k2 = cache_mem_k.reshape(DEPTH, DEC_BATCH, MEM_LEN, D_MODEL)
    mem_v2 = cache_mem_v.reshape(DEPTH, DEC_BATCH, MEM_LEN, D_MODEL)
    ret_s0 = state_ret.reshape(DEPTH, DEC_BATCH, 4, LANES, LANES)

    outs = {k: [] for k in ("p_ak", "p_av", "p_rs", "p_bk", "p_bv", "p_fc", "p_mk", "p_mv",
                            "s_ak", "s_av", "s_rs", "s_bk", "s_bv", "s_fc")}
    for l in range(DEPTH):
        proj = matmul(xb, w_in_b, l, tm=1056, tn=1536, out_dtype=BF16)
        clipped = functools.partial(_clipped_bias, band_rel_table[l])
        oa = band_attention_prompt(proj, diag_a, swa_sink[l], q_col=COL_QA, k_col=COL_KA2, v_col=COL_VA2,
                                   shared_kv=True, n_back=SWA_BACK, has_sink=True, skew=None)
        oc = band_attention_prompt(proj, _band_diag(clipped, BAND_BACK), no_sink, q_col=COL_QC, k_col=COL_KC,
                                   v_col=COL_VC, shared_kv=False, n_back=BAND_BACK, has_sink=False, skew=1)
        ob, rs_p = retention(proj, cos_p, sin_p, zero_state, 0, row0=0, n_seq=BATCH, seq_len=SEQ,
                             blk_len=RET_L, pos_per_blk=True)
        oa = cached_attention_sample(
            proj, swa_k2, swa_v2, l, diag_a_c, bias_a_n, swa_sink[l], oa,
            q_col=COL_QA, k_col=COL_KA2, v_col=COL_VA2, shared_kv=True, has_sink=True)
        oc = cached_attention_sample(
            proj, band_k2, band_v2, l, _cache_diag(clipped, lc), clipped(rel_n), no_sink, oc,
            q_col=COL_QC, k_col=COL_KC, v_col=COL_VC, shared_kv=False, has_sink=False)
        ob, rs_s = retention(proj, cos_s, sin_s, ret_s0, l, ob, row0=P_ROWS, n_seq=DEC_BATCH, seq_len=DEC_SEQ,
                             blk_len=DEC_SEQ, pos_per_blk=False)
        mix = gated_branch_sum(oa, ob, oc, w_br_a_b, w_br_b_b, w_br_c_b, proj, l)
        x, xb = matmul_residual_ln(mix, w_mix_o_b, l, x, ln1_g, ln1_b, tm=528)

        mk = matmul(memb, w_mk, l, tm=512, tn=1024, out_dtype=F32)
        mv = matmul(memb, w_mv, l, tm=512, tn=1024, out_dtype=F32)
        qm = matmul(xb, w_mq, l, tm=1056, tn=2048, out_dtype=BF16)
        om = mem_attention(qm, mk.reshape(1, BATCH, MEM_LEN, D_MODEL), mv.reshape(1, BATCH, MEM_LEN, D_MODEL), 0,
                           row0=0, n_rows=P_ROWS, tm=1024, heads=MEM_HEADS, rows_per_kv=SEQ)
        om = mem_attention(qm, mem_k2, mem_v2, l, om, row0=P_ROWS, n_rows=S_ROWS, tm=DEC_SEQ,
                           heads=MEM_HEADS, rows_per_kv=DEC_SEQ)
        x, xb = matmul_residual_ln(om, w_mo_b, l, x, ln2_g, ln2_b, tm=528)

        h, tg_p, tv_p = ffn_in(xb, w_ffn_in, ffn_conv_w, ffn_conv_b, l, row0=0, n_rows=P_ROWS,
                               tm=1024, tn=512, tail=8)
        st = state_ffn_conv[l]
        fix1 = jnp.pad(st[:, 1:2], ((0, 0), (0, DEC_SEQ - 1), (0, 0))).reshape(S_ROWS, 2 * D_FF)
        fix2 = jnp.pad(st, ((0, 0), (0, DEC_SEQ - 2), (0, 0))).reshape(S_ROWS, 2 * D_FF)
        h, tg_s, tv_s = ffn_in(xb, w_ffn_in, ffn_conv_w, ffn_conv_b, l, h, row0=P_ROWS, n_rows=S_ROWS,
                               tm=S_ROWS, tn=512, tail=S_ROWS, fix1=fix1, fix2=fix2)
        if l < DEPTH - 1:
            x, xb = matmul_residual_ln(h, w_ffn_out_b, l, x, ln3_g, ln3_b, tm=S_ROWS)
        else:
            y_prompt, y_sample = matmul_residual_ln(h, w_ffn_out_b, l, x, ln3_g, ln3_b, tm=S_ROWS,
                                                    split_rows=P_ROWS)

        sf = proj[P_ROWS:]
        la_p, lc_p = min(SWA_BACK * CHUNK, SEQ), min(BAND_BACK * CHUNK, SEQ)
        seq_tail = lambda n, c0, w: jnp.stack(
            [proj[(b + 1) * SEQ - n:(b + 1) * SEQ, c0:c0 + w] for b in range(BATCH)], 0)
        outs["p_ak"].append(_undup(seq_tail(la_p, COL_KA2, 256).reshape(BATCH * la_p, 256))
                            .reshape(BATCH, la_p, 2, 64).astype(F32))
        outs["p_av"].append(_undup(seq_tail(la_p, COL_VA2, 256).reshape(BATCH * la_p, 256))
                            .reshape(BATCH, la_p, 2, 64).astype(F32))
        outs["p_rs"].append(rs_p.reshape(BATCH, RET_HEADS, RET_DK, RET_DV))
        outs["p_bk"].append(seq_tail(lc_p, COL_KC, 1024).reshape(BATCH, lc_p, 16, 64).astype(F32))
        outs["p_bv"].append(seq_tail(lc_p, COL_VC, 1024).reshape(BATCH, lc_p, 16, 64).astype(F32))
        last = [(b + 1) * (SEQ // 1024) - 1 for b in range(BATCH)]
        outs["p_fc"].append(jnp.stack(
            [jnp.concatenate([tg_p[t, 6:8], tv_p[t, 6:8]], -1) for t in last], 0))
        outs["p_mk"].append(mk.reshape(BATCH, MEM_LEN, MEM_HEADS, MEM_HD))
        outs["p_mv"].append(mv.reshape(BATCH, MEM_LEN, MEM_HEADS, MEM_HD))
        outs["s_ak"].append(_undup(sf[:, COL_KA2:COL_KA2 + 256]).reshape(DEC_BATCH, DEC_SEQ, 2, 64).astype(F32))
        outs["s_av"].append(_undup(sf[:, COL_VA2:COL_VA2 + 256]).reshape(DEC_BATCH, DEC_SEQ, 2, 64).astype(F32))
        outs["s_rs"].append(rs_s.reshape(DEC_BATCH, RET_HEADS, RET_DK, RET_DV))
        outs["s_bk"].append(sf[:, COL_KC:COL_KC + 1024].reshape(DEC_BATCH, DEC_SEQ, 16, 64).astype(F32))
        outs["s_bv"].append(sf[:, COL_VC:COL_VC + 1024].reshape(DEC_BATCH, DEC_SEQ, 16, 64).astype(F32))
        u_s = jnp.concatenate([tg_s[0], tv_s[0]], -1).reshape(DEC_BATCH, DEC_SEQ, 2 * D_FF)
        outs["s_fc"].append(u_s[:, DEC_SEQ - 2:])

    st = lambda name: jnp.stack(outs[name], 0)
    return (y_prompt.reshape(BATCH, SEQ, D_MODEL), y_sample.reshape(DEC_BATCH, DEC_SEQ, D_MODEL),
            st("p_ak"), st("p_av"), st("p_rs"), st("p_bk"), st("p_bv"), st("p_fc"), st("p_mk"), st("p_mv"),
            st("s_ak"), st("s_av"), st("s_rs"), st("s_bk"), st("s_bv"), st("s_fc"))
```

```python
import functools
import math

import numpy as np
import jax
import jax.numpy as jnp
from jax import lax
from jax.experimental import pallas as pl
from jax.experimental.pallas import tpu as pltpu

F32 = jnp.float32
BF16 = jnp.bfloat16

D_MODEL = 2048
BATCH = 2
SEQ = 4096
DEPTH = 2
DEC_BATCH = 16
DEC_SEQ = 16
PAST_LEN = 2048
CHUNK = 64
HEAD_DIM = 64
SWA_BACK = 2
SWA_HEADS = 16
SWA_KV_HEADS = 2
T5_BUCKETS = 32
T5_MAX_DIST = 128
RET_HEADS = 8
RET_DK = 64
RET_DV = 128
ROPE_BASE = 10000.0
RET_NORM_EPS = 1e-5
BAND_BACK = 8
BAND_HEADS = 16
BAND_MAX_REL = 256
MEM_LEN = 256
MEM_HEADS = 4
MEM_HD = D_MODEL // MEM_HEADS
D_FF = 5632
CONV_W = 3
DN_ALPHA = (2 * DEPTH) ** 0.25
LN_EPS = 1e-5

P_ROWS = BATCH * SEQ
S_ROWS = DEC_BATCH * DEC_SEQ
ROWS = P_ROWS + S_ROWS

COL_QA = 0
COL_QC = 1024
COL_KC = 2048
COL_VC = 3072
COL_VB = 4096
COL_GR = 5120
COL_GA = 6144
COL_GB = 8192
COL_GC = 10240
COL_QB = 12288
COL_KB = 12800
COL_KA2 = 13312
COL_VA2 = 13568
PROJ_COLS = 13824

LANES = 128
MXU_COLS = 256
QB = 256
RET_L = 256
NEG = -1e30
LOG2E = math.log2(math.e)
VMEM_LIMIT = 48 * 1024 * 1024

NT = (((1,), (1,)), ((), ()))
TN = (((0,), (0,)), ((), ()))


def _params(sem, vmem=VMEM_LIMIT):
    return pltpu.CompilerParams(dimension_semantics=sem, vmem_limit_bytes=vmem)


def _pack_rows_kernel(p_ref, s_ref, of_ref, ob_ref, *, n_prompt_tiles):
    src = jnp.where(pl.program_id(0) < n_prompt_tiles, p_ref[...], s_ref[...])
    of_ref[...] = src
    ob_ref[...] = src.astype(BF16)


def pack_rows(x_prompt, x_sample):
    tm = S_ROWS
    npt = P_ROWS // tm
    return pl.pallas_call(
        functools.partial(_pack_rows_kernel, n_prompt_tiles=npt),
        grid=(npt + 1,),
        in_specs=[pl.BlockSpec((tm, D_MODEL), lambda i: (jnp.minimum(i, npt - 1), 0)),
                  pl.BlockSpec((tm, D_MODEL), lambda i: (0, 0))],
        out_specs=[pl.BlockSpec((tm, D_MODEL), lambda i: (i, 0)), pl.BlockSpec((tm, D_MODEL), lambda i: (i, 0))],
        out_shape=[jax.ShapeDtypeStruct((ROWS, D_MODEL), F32), jax.ShapeDtypeStruct((ROWS, D_MODEL), BF16)],
        compiler_params=_params(("parallel",)),
        name="pack_rows",
    )(x_prompt.reshape(P_ROWS, D_MODEL), x_sample.reshape(S_ROWS, D_MODEL))


def _w_in_permute_kernel(tbl_ref, src_ref, o_ref):
    mode = tbl_ref[1, pl.program_id(1)]
    t = src_ref[...]

    @pl.when(mode == 0)
    def _():
        o_ref[...] = t.astype(BF16)

    for m, half in ((1, t[:, :LANES]), (2, t[:, LANES:])):
        @pl.when(mode == m)
        def _():
            swapped = pltpu.roll(half, HEAD_DIM, 1)
            o_ref[...] = jnp.concatenate([jnp.where(_head_lanes(0), half, swapped),
                                          jnp.where(_head_lanes(0), swapped, half)], 1).astype(BF16)


def permute_w_in(w_in):
    blk = 256
    src = lambda off, width: list(range(off // blk, (off + width) // blk))
    order = (src(0, 1024) + src(4352, 1024) + src(5376, 1024) + src(6400, 1024) + src(2304, 1024)
             + src(3328, 1024) + src(7424, 2048) + src(9472, 2048) + src(11520, 2048) + src(1280, 512)
             + src(1792, 512))
    kv = 1024 // blk
    tbl = np.array([order + [kv, kv], [0] * len(order) + [1, 2]], np.int32)
    assert tbl.shape[1] * blk == PROJ_COLS
    d = w_in.shape[1]
    return pl.pallas_call(
        _w_in_permute_kernel,
        grid_spec=pltpu.PrefetchScalarGridSpec(
            num_scalar_prefetch=1,
            grid=(DEPTH, PROJ_COLS // blk),
            in_specs=[pl.BlockSpec((None, d, blk), lambda l, j, tbl: (l, 0, tbl[0, j]))],
            out_specs=pl.BlockSpec((None, d, blk), lambda l, j, tbl: (l, 0, j)),
        ),
        out_shape=jax.ShapeDtypeStruct((DEPTH, d, PROJ_COLS), BF16),
        compiler_params=_params(("parallel", "parallel")),
        name="permute_w_in",
    )(jnp.asarray(tbl), w_in)


def _mm_kernel(a_ref, b_ref, o_ref, *scratch):
    if scratch:
        bb_ref, = scratch

        @pl.when(pl.program_id(0) == 0)
        def _():
            bb_ref[...] = b_ref[...].astype(BF16)

        b = bb_ref[...]
    else:
        b = b_ref[...].astype(BF16)
    o_ref[...] = jnp.dot(a_ref[...], b, preferred_element_type=F32).astype(o_ref.dtype)


def matmul(a, b, layer, *, tm, tn, out_dtype):
    m, k = a.shape
    n = b.shape[2]
    ni, nj = m // tm, n // tn
    resident = b.dtype == F32 and ni > 1
    assert b.dtype == BF16 or ni == 1 or nj == 1
    b_mode = dict(pipeline_mode=pl.Buffered(1)) if resident else {}
    return pl.pallas_call(
        _mm_kernel,
        grid=(ni, nj),
        in_specs=[pl.BlockSpec((tm, k), lambda i, j: (i, 0)),
                  pl.BlockSpec((None, k, tn), lambda i, j: (layer, 0, j), **b_mode)],
        out_specs=pl.BlockSpec((tm, tn), lambda i, j: (i, j)),
        out_shape=jax.ShapeDtypeStruct((m, n), out_dtype),
        scratch_shapes=[pltpu.VMEM((k, tn), BF16)] if resident else [],
        compiler_params=_params(("arbitrary", "arbitrary") if resident else ("parallel", "parallel")),
        name="matmul",
    )(a, b)


def _gate_mm_kernel(oa_ref, ob_ref, oc_ref, wa_ref, wb_ref, wc_ref, ga_ref, gb_ref, gc_ref, o_ref):
    acc = None
    for o, w, g in ((oa_ref, wa_ref, ga_ref), (ob_ref, wb_ref, gb_ref), (oc_ref, wc_ref, gc_ref)):
        t = jnp.dot(o[...], w[...], preferred_element_type=F32) * jax.nn.sigmoid(g[...].astype(F32))
        acc = t if acc is None else acc + t
    o_ref[...] = acc.astype(o_ref.dtype)


def gated_branch_sum(oa, ob, oc, wa, wb, wc, proj, layer, *, tm=528):
    m, k = oa.shape
    n = wa.shape[2]
    o_spec = pl.BlockSpec((tm, k), lambda i: (i, 0))
    w_spec = pl.BlockSpec((None, k, n), lambda i: (layer, 0, 0), pipeline_mode=pl.Buffered(1))

    def g_spec(col):
        return pl.BlockSpec((tm, n), lambda i: (i, col // n))

    return pl.pallas_call(
        _gate_mm_kernel,
        grid=(m // tm,),
        in_specs=[o_spec, o_spec, o_spec, w_spec, w_spec, w_spec,
                  g_spec(COL_GA), g_spec(COL_GB), g_spec(COL_GC)],
        out_specs=pl.BlockSpec((tm, n), lambda i: (i, 0)),
        out_shape=jax.ShapeDtypeStruct((m, n), BF16),
        compiler_params=_params(("parallel",)),
        name="gated_branch_sum",
    )(oa, ob, oc, wa, wb, wc, proj, proj, proj)


def _mm_ln_kernel(a_ref, b_ref, r_ref, g_ref, bt_ref, o1_ref, o2_ref, acc_ref, *, split_tiles):
    i = pl.program_id(0)

    @pl.when(i == 0)
    def _():
        acc_ref[...] = jnp.zeros_like(acc_ref)

    y = DN_ALPHA * r_ref[...] + acc_ref[...]
    mu = jnp.mean(y, -1, keepdims=True)
    d = y - mu
    var = jnp.mean(d * d, -1, keepdims=True)
    out = d * lax.rsqrt(var + LN_EPS) * g_ref[...] + bt_ref[...]
    if split_tiles is None:
        o1_ref[...] = out
        o2_ref[...] = out.astype(BF16)
    else:
        o2_ref[...] = out

        @pl.when(i <= split_tiles)
        def _():
            o1_ref[...] = out
    acc_ref[...] = jnp.dot(a_ref[...], b_ref[...], preferred_element_type=F32)


def matmul_residual_ln(a, b, layer, resid, gain, bias, *, tm, split_rows=None):
    m, kk = a.shape
    n = b.shape[-1]
    nt = m // tm
    cur = lambda i: (jnp.minimum(i, nt - 1), 0)
    prev = lambda i: (jnp.maximum(i - 1, 0), 0)
    if b.ndim == 3:
        b_spec = pl.BlockSpec((None, kk, n), lambda i: (layer, 0, 0), pipeline_mode=pl.Buffered(1))
    else:
        b_spec = pl.BlockSpec((kk, n), lambda i: (0, 0), pipeline_mode=pl.Buffered(1))
    if split_rows is None:
        split_tiles = None
        out_specs = [pl.BlockSpec((tm, n), prev), pl.BlockSpec((tm, n), prev)]
        out_shape = [jax.ShapeDtypeStruct((m, n), F32), jax.ShapeDtypeStruct((m, n), BF16)]
    else:
        split_tiles = split_rows // tm
        assert split_tiles == nt - 1
        out_specs = [pl.BlockSpec((tm, n), lambda i: (jnp.clip(i - 1, 0, split_tiles - 1), 0)),
                     pl.BlockSpec((tm, n), lambda i: (0, 0))]
        out_shape = [jax.ShapeDtypeStruct((split_rows, n), F32), jax.ShapeDtypeStruct((tm, n), F32)]
    return pl.pallas_call(
        functools.partial(_mm_ln_kernel, split_tiles=split_tiles),
        grid=(nt + 1,),
        in_specs=[pl.BlockSpec((tm, kk), cur),
                  b_spec,
                  pl.BlockSpec((tm, n), prev),
                  pl.BlockSpec((None, 1, n), lambda i: (layer, 0, 0)),
                  pl.BlockSpec((None, 1, n), lambda i: (layer, 0, 0))],
        out_specs=out_specs,
        out_shape=out_shape,
        scratch_shapes=[pltpu.VMEM((tm, n), F32)],
        compiler_params=_params(("arbitrary",)),
        name="matmul_residual_ln",
    )(a, b, resid, gain.reshape(DEPTH, 1, n), bias.reshape(DEPTH, 1, n))


def _head_lanes(j):
    lane = lax.broadcasted_iota(jnp.int32, (1, LANES), 1)
    return (lane < HEAD_DIM) if j == 0 else (lane >= HEAD_DIM)


def _band_attn_kernel(sink_ref, diag_ref, q_ref, *refs, n_prev_blocks, n_back, shared_kv, has_sink, skew):
    nk = n_prev_blocks + 1
    k_refs = refs[:nk]
    v_refs = refs[nk:2 * nk]
    o_ref = refs[2 * nk]
    bias_ref = refs[2 * nk + 1]
    b = pl.program_id(0)
    i = pl.program_id(1)
    n_prev_rows = n_back * CHUNK
    kw = n_prev_rows + QB

    @pl.when((b == 0) & (i == 0))
    def _():
        n = diag_ref.shape[2]
        qc = lax.broadcasted_iota(jnp.int32, (QB, kw), 0) // CHUNK
        cb = lax.broadcasted_iota(jnp.int32, (QB, kw), 1) // CHUNK
        allowed = (cb >= qc) & (cb - n_back <= qc)
        for h in range(16):
            t = pltpu.roll(jnp.broadcast_to(diag_ref[h, 0:1, :], (QB, n)), 0, 1, stride=1, stride_axis=0)
            bias_ref[h] = jnp.where(allowed, t[:, :kw] * LOG2E, NEG)

    krow = lax.broadcasted_iota(jnp.int32, (kw, 1), 0)
    kmask = jnp.where(krow >= jnp.maximum(n_prev_rows - i * QB, 0), 0.0, NEG).astype(BF16)
    lane = lax.broadcasted_iota(jnp.int32, (1, LANES), 1)

    def kv_cols(h):
        c0 = (h // 8 if shared_kv else h // 2) * LANES
        return slice(c0, c0 + LANES)

    def scores(h):
        p, j = divmod(h, 2)
        q = q_ref[:, p * LANES:(p + 1) * LANES] * (HEAD_DIM ** -0.5 * LOG2E)
        k_all = jnp.concatenate([r[:, kv_cols(h)] for r in k_refs], axis=0)
        mask_lane = lane == (HEAD_DIM if j == 0 else 0)
        qj = jnp.where(mask_lane, jnp.ones_like(q), jnp.where(_head_lanes(j), q, jnp.zeros_like(q)))
        kj = jnp.where(mask_lane, kmask, k_all)
        return lax.dot_general(qj, kj, NT, preferred_element_type=F32)

    def softmax(h, s_all):
        sink = sink_ref[h] * LOG2E if has_sink else None
        ps, dens = [], []
        for r in range(QB // CHUNK):
            rows = slice(r * CHUNK, (r + 1) * CHUNK)
            lo = r * CHUNK // LANES * LANES
            hi = min(kw, -(-(r * CHUNK + n_prev_rows + CHUNK) // LANES) * LANES)
            s = s_all[rows, lo:hi] + bias_ref[h, rows, lo:hi]
            m = jnp.max(s, -1, keepdims=True)
            if has_sink:
                m = jnp.maximum(m, sink)
            e = jnp.exp2(s - m)
            den = jnp.sum(e, -1, keepdims=True)
            if has_sink:
                den = den + jnp.exp2(sink - m)
            parts = [e.astype(BF16)]
            if lo:
                parts.insert(0, jnp.zeros((CHUNK, lo), BF16))
            if hi < kw:
                parts.append(jnp.zeros((CHUNK, kw - hi), BF16))
            ps.append(jnp.concatenate(parts, 1))
            dens.append(den)
        return jnp.concatenate(ps, 0), jnp.concatenate(dens, 0)

    def weighted_values(h, p_all, den):
        v_all = jnp.concatenate([r[:, kv_cols(h)] for r in v_refs], axis=0)
        return jnp.dot(p_all, v_all, preferred_element_type=F32) / den

    lag = 16 if skew is None else skew
    s_q, p_q, outs = {}, {}, {}
    for t in range(16 + 2 * lag):
        if t < 16:
            s_q[t] = scores(t)
        if lag <= t < 16 + lag:
            p_q[t - lag] = softmax(t - lag, s_q.pop(t - lag))
        if t >= 2 * lag:
            h = t - 2 * lag
            outs[h] = weighted_values(h, *p_q.pop(h))
            if h % 2:
                p = h // 2
                o_ref[:, p * LANES:(p + 1) * LANES] = jnp.where(
                    _head_lanes(0), outs.pop(h - 1), outs.pop(h)).astype(o_ref.dtype)


def band_attention_prompt(proj, diag, sink, *, q_col, k_col, v_col, shared_kv, n_back, has_sink, skew):
    n_prev_rows = n_back * CHUNK
    nqb = SEQ // QB
    if n_prev_rows >= QB:
        n_prev_blocks, pb = n_prev_rows // QB, QB
    else:
        n_prev_blocks, pb = 1, n_prev_rows
    per = QB // pb
    kvw = 256 if shared_kv else 1024

    def prev_spec(col, back):
        return pl.BlockSpec(
            (pb, kvw), lambda b, i: (b * (SEQ // pb) + jnp.maximum(i * per - back, 0), col // kvw))

    def own_spec(col):
        return pl.BlockSpec((QB, kvw), lambda b, i: (b * nqb + i, col // kvw))

    k_specs = [prev_spec(k_col, n_prev_blocks - t) for t in range(n_prev_blocks)] + [own_spec(k_col)]
    v_specs = [prev_spec(v_col, n_prev_blocks - t) for t in range(n_prev_blocks)] + [own_spec(v_col)]
    kern = functools.partial(_band_attn_kernel, n_prev_blocks=n_prev_blocks, n_back=n_back,
                             shared_kv=shared_kv, has_sink=has_sink, skew=skew)
    n_in = 2 * (n_prev_blocks + 1)
    return pl.pallas_call(
        kern,
        grid=(BATCH, nqb),
        in_specs=[pl.BlockSpec(memory_space=pltpu.SMEM),
                  pl.BlockSpec(diag.shape, lambda b, i: (0, 0, 0)),
                  pl.BlockSpec((QB, 1024), lambda b, i: (b * nqb + i, q_col // 1024))]
                 + k_specs + v_specs,
        out_specs=pl.BlockSpec((QB, 1024), lambda b, i: (b * nqb + i, 0)),
        out_shape=jax.ShapeDtypeStruct((ROWS, 16 * HEAD_DIM), BF16),
        scratch_shapes=[pltpu.VMEM((16, QB, n_prev_rows + QB), F32)],
        compiler_params=_params(("arbitrary", "arbitrary")),
        name="band_attention_prompt",
    )(sink, diag, proj, *([proj] * n_in))


def _cached_attn_kernel(sink_ref, q_ref, kn_ref, vn_ref, kc_ref, vc_ref, diag_ref, bn_ref, _, o_ref, bc_ref,
                        *, shared_kv, has_sink):
    lc = kc_ref.shape[1]

    @pl.when(pl.program_id(0) == 0)
    def _():
        n = diag_ref.shape[2]
        for h in range(16):
            t = pltpu.roll(jnp.broadcast_to(diag_ref[h, 0:1, :], (DEC_SEQ, n)), 0, 1, stride=1, stride_axis=0)
            bc_ref[h] = t[:, :lc]

    kv_cols = [(p // 4 if shared_kv else p) * LANES for p in range(8)]
    scores = []
    for p in range(8):
        c0 = kv_cols[p]
        q = q_ref[:, p * LANES:(p + 1) * LANES] * (HEAD_DIM ** -0.5)
        kc = kc_ref[0, :, c0:c0 + LANES].astype(BF16)
        kn = kn_ref[:, c0:c0 + LANES]
        for j in range(2):
            h = 2 * p + j
            qj = jnp.where(_head_lanes(j), q, jnp.zeros_like(q))
            scores.append((lax.dot_general(qj, kc, NT, preferred_element_type=F32) + bc_ref[h],
                           lax.dot_general(qj, kn, NT, preferred_element_type=F32) + bn_ref[h]))
    probs = []
    for h, (s_c, s_n) in enumerate(scores):
        m = jnp.maximum(jnp.max(s_c, -1, keepdims=True), jnp.max(s_n, -1, keepdims=True))
        if has_sink:
            m = jnp.maximum(m, sink_ref[h])
        e_c, e_n = jnp.exp(s_c - m), jnp.exp(s_n - m)
        den = jnp.sum(e_c, -1, keepdims=True) + jnp.sum(e_n, -1, keepdims=True)
        if has_sink:
            den = den + jnp.exp(sink_ref[h] - m)
        probs.append((e_c.astype(BF16), e_n.astype(BF16), den))
    for p in range(8):
        c0 = kv_cols[p]
        vc = vc_ref[0, :, c0:c0 + LANES].astype(BF16)
        vn = vn_ref[:, c0:c0 + LANES]
        outs = []
        for j in range(2):
            e_c, e_n, den = probs[2 * p + j]
            outs.append((jnp.dot(e_c, vc, preferred_element_type=F32)
                         + jnp.dot(e_n, vn, preferred_element_type=F32)) / den)
        o_ref[:, p * LANES:(p + 1) * LANES] = jnp.where(_head_lanes(0), outs[0], outs[1]).astype(o_ref.dtype)


def cached_attention_sample(proj, k_cache, v_cache, layer, diag_c, bias_n, sink, o_all, *, q_col, k_col, v_col,
                            shared_kv, has_sink):
    lc, wc = k_cache.shape[2], k_cache.shape[3]
    r0 = P_ROWS // DEC_SEQ
    kern = functools.partial(_cached_attn_kernel, shared_kv=shared_kv, has_sink=has_sink)
    cache = pl.BlockSpec((None, 1, lc, wc), lambda s: (layer, s, 0, 0))
    return pl.pallas_call(
        kern,
        grid=(DEC_BATCH,),
        in_specs=[pl.BlockSpec(memory_space=pltpu.SMEM),
                  pl.BlockSpec((DEC_SEQ, 1024), lambda s: (r0 + s, q_col // 1024)),
                  pl.BlockSpec((DEC_SEQ, wc), lambda s: (r0 + s, k_col // wc)),
                  pl.BlockSpec((DEC_SEQ, wc), lambda s: (r0 + s, v_col // wc)),
                  cache, cache,
                  pl.BlockSpec(diag_c.shape, lambda s: (0, 0, 0)),
                  pl.BlockSpec((16, DEC_SEQ, DEC_SEQ), lambda s: (0, 0, 0)),
                  pl.BlockSpec(memory_space=pl.ANY)],
        out_specs=pl.BlockSpec((DEC_SEQ, 1024), lambda s: (r0 + s, 0)),
        out_shape=jax.ShapeDtypeStruct(o_all.shape, o_all.dtype),
        scratch_shapes=[pltpu.VMEM((16, DEC_SEQ, lc), F32)],
        input_output_aliases={8: 0},
        compiler_params=_params(("arbitrary",)),
        name="cached_attention_sample",
    )(sink, proj, proj, proj, k_cache, v_cache, diag_c, bias_n, o_all)


def _ret_kernel(q_ref, k_ref, v_ref, g_ref, cos_ref, sin_ref, s0_ref, *refs, blk_len):
    o_ref, sout_ref, st_ref, dec_ref, rdec_ref = refs[-5:]
    blk = pl.program_id(1)
    log_gs = [math.log(1.0 - 2.0 ** (-5.0 - h)) for h in range(RET_HEADS)]

    @pl.when((pl.program_id(0) == 0) & (blk == 0))
    def _():
        ii = lax.broadcasted_iota(jnp.int32, (blk_len, blk_len), 0)
        jj = lax.broadcasted_iota(jnp.int32, (blk_len, blk_len), 1)
        diff = (ii - jj).astype(F32)
        row = lax.broadcasted_iota(jnp.int32, (blk_len, LANES), 0).astype(F32)
        for h in range(RET_HEADS):
            dec_ref[h] = jnp.where(diff >= 0, jnp.exp(log_gs[h] * jnp.maximum(diff, 0.0)), 0.0)
            rdec_ref[h] = jnp.exp(log_gs[h] * (row + 1.0))
            rdec_ref[RET_HEADS + h] = jnp.exp(log_gs[h] * (blk_len - 1.0 - row))

    @pl.when(blk == 0)
    def _():
        st_ref[...] = s0_ref[0]

    cos = cos_ref[...]
    sin = sin_ref[...]
    lane = lax.broadcasted_iota(jnp.int32, (1, LANES), 1)
    low_half = (lane % RET_DK) < (RET_DK // 2)

    def rope(x):
        x = x.astype(F32)
        swapped = jnp.where(low_half, pltpu.roll(x, LANES - RET_DK // 2, 1), pltpu.roll(x, RET_DK // 2, 1))
        return x * cos + swapped * sin

    srow = lax.broadcasted_iota(jnp.int32, (LANES, 1), 0)

    for p in range(RET_HEADS // 2):
        qr = rope(q_ref[:, p * LANES:(p + 1) * LANES])
        kr = rope(k_ref[:, p * LANES:(p + 1) * LANES]) * (RET_DK ** -0.5)
        kb = kr.astype(BF16)
        state = st_ref[p]
        state_b = state.astype(BF16)
        upd = None
        for j in range(2):
            h = 2 * p + j
            qj = jnp.where(_head_lanes(j), qr, 0.0).astype(BF16)
            vh = v_ref[:, h * RET_DV:(h + 1) * RET_DV]
            qk = lax.dot_general(qj, kb, NT, preferred_element_type=F32) * dec_ref[h]
            o = jnp.dot(qk.astype(BF16), vh, preferred_element_type=F32)
            o = o + jnp.dot(qj, state_b, preferred_element_type=F32) * rdec_ref[h]
            mu = jnp.mean(o, -1, keepdims=True)
            d = o - mu
            var = jnp.mean(d * d, -1, keepdims=True)
            gate = g_ref[:, h * RET_DV:(h + 1) * RET_DV].astype(F32)
            o_ref[:, h * RET_DV:(h + 1) * RET_DV] = (
                d * lax.rsqrt(var + RET_NORM_EPS) * (gate * jax.nn.sigmoid(gate))).astype(o_ref.dtype)
            kwj = jnp.where(_head_lanes(j), kr * rdec_ref[RET_HEADS + h], 0.0).astype(BF16)
            u = lax.dot_general(kwj, vh, TN, preferred_element_type=F32)
            upd = u if upd is None else upd + u
        carry = jnp.where(srow < RET_DK, math.exp(log_gs[2 * p] * blk_len), math.exp(log_gs[2 * p + 1] * blk_len))
        st_ref[p] = carry * state + upd

    @pl.when(blk == pl.num_programs(1) - 1)
    def _():
        sout_ref[0] = st_ref[...]


def retention(proj, cos, sin, s0, layer, o_all=None, *, row0, n_seq, seq_len, blk_len, pos_per_blk):
    nb = seq_len // blk_len
    rb0 = row0 // blk_len

    def rows(col, width):
        return pl.BlockSpec((blk_len, width), lambda b, t: (rb0 + b * nb + t, col // width))

    tab = pl.BlockSpec((blk_len, LANES), lambda b, t: (t if pos_per_blk else 0, 0))
    st = pl.BlockSpec((1, 4, LANES, LANES), lambda b, t: (b, 0, 0, 0))
    st_in = pl.BlockSpec((None, 1, 4, LANES, LANES), lambda b, t: (layer, b, 0, 0, 0))
    in_specs = [rows(COL_QB, 512), rows(COL_KB, 512), rows(COL_VB, 1024), rows(COL_GR, 1024), tab, tab, st_in]
    args = [proj, proj, proj, proj, cos, sin, s0]
    aliases = {}
    if o_all is not None:
        in_specs.append(pl.BlockSpec(memory_space=pl.ANY))
        args.append(o_all)
        aliases = {len(args) - 1: 0}
    return pl.pallas_call(
        functools.partial(_ret_kernel, blk_len=blk_len),
        grid=(n_seq, nb),
        in_specs=in_specs,
        out_specs=[rows(0, 1024), st],
        out_shape=[jax.ShapeDtypeStruct((ROWS, 1024), BF16),
                   jax.ShapeDtypeStruct((n_seq, 4, LANES, LANES), F32)],
        scratch_shapes=[pltpu.VMEM((4, LANES, LANES), F32), pltpu.VMEM((RET_HEADS, blk_len, blk_len), F32),
                        pltpu.VMEM((2 * RET_HEADS, blk_len, LANES), F32)],
        input_output_aliases=aliases,
        compiler_params=_params(("arbitrary", "arbitrary")),
        name="retention",
    )(*args)


def _mem_attn_kernel(q_ref, k_ref, v_ref, *refs, heads):
    o_ref = refs[-1]
    cols = [slice(h * MEM_HD, (h + 1) * MEM_HD) for h in range(heads)]
    scores = [lax.dot_general(q_ref[:, c], k_ref[0, :, c].astype(BF16), NT, preferred_element_type=F32)
              * (MEM_HD ** -0.5) for c in cols]
    probs = []
    for s in scores:
        e = jnp.exp(s - jnp.max(s, -1, keepdims=True))
        probs.append((e.astype(BF16), jnp.sum(e, -1, keepdims=True)))
    for c, (e, den) in zip(cols, probs):
        o_ref[:, c] = (jnp.dot(e, v_ref[0, :, c].astype(BF16), preferred_element_type=F32) / den
                       ).astype(o_ref.dtype)


def mem_attention(qm, k_src, v_src, layer, o_all=None, *, row0, n_rows, tm, heads, rows_per_kv):
    rb0 = row0 // tm
    w = heads * MEM_HD
    kv = pl.BlockSpec((None, 1, MEM_LEN, w), lambda i, h: (layer, i * tm // rows_per_kv, 0, h))
    in_specs = [pl.BlockSpec((tm, w), lambda i, h: (rb0 + i, h)), kv, kv]
    args = [qm, k_src, v_src]
    aliases = {}
    if o_all is not None:
        in_specs.append(pl.BlockSpec(memory_space=pl.ANY))
        args.append(o_all)
        aliases = {3: 0}
    return pl.pallas_call(
        functools.partial(_mem_attn_kernel, heads=heads),
        grid=(n_rows // tm, MEM_HEADS // heads),
        in_specs=in_specs,
        out_specs=pl.BlockSpec((tm, w), lambda i, h: (rb0 + i, h)),
        out_shape=jax.ShapeDtypeStruct((ROWS, D_MODEL), BF16),
        input_output_aliases=aliases,
        compiler_params=_params(("parallel", "parallel")),
        name="mem_attention",
    )(*args)


def _gelu(x):
    return 0.5 * x * (1.0 + lax.erf(x * (2.0 ** -0.5)))


def _ffn_in_kernel(*refs, sample, seq_tiles, tail, side_cast):
    bg_ref, bv_ref = refs[-2:]
    if sample:
        a_ref, wfg_ref, wfv_ref, wg_ref, wv_ref, cbg_ref, cbv_ref, f1g_ref, f1v_ref, f2g_ref, f2v_ref = refs[:11]
    else:
        a_ref, ah_ref, wfg_ref, wfv_ref, wg_ref, wv_ref, cbg_ref, cbv_ref = refs[:8]
    if side_cast:
        h_ref, tg_ref, tv_ref, side_out_ref = refs[-6:-2]
        side_out_ref[...] = refs[8][...].astype(BF16)
    else:
        h_ref, tg_ref, tv_ref = refs[-5:-2]
    tm = a_ref.shape[0]
    i = pl.program_id(1)

    @pl.when(i == 0)
    def _():
        bg_ref[...] = wfg_ref[...].astype(BF16)
        bv_ref[...] = wfv_ref[...].astype(BF16)

    a = a_ref[...]
    if sample:
        pos = lax.broadcasted_iota(jnp.int32, (tm, 1), 0) % DEC_SEQ
    else:
        top = lax.broadcasted_iota(jnp.int32, (8, 1), 0)
        keep = (i % seq_tiles != 0).astype(F32)

    def project(cols, b_ref):
        u = jnp.dot(a, b_ref[:, cols], preferred_element_type=F32)
        uh = None if sample else jnp.dot(ah_ref[...], b_ref[:, cols], preferred_element_type=F32)
        return u, uh

    def conv(cols, u, uh, w_ref, cb_ref, f1_ref, f2_ref, t_ref):
        t_ref[0, :, cols] = u[tm - tail:, :]
        r1 = pltpu.roll(u, 1, 0)
        r2 = pltpu.roll(u, 2, 0)
        if sample:
            u1 = jnp.where(pos < 1, f1_ref[:, cols], r1)
            u2 = jnp.where(pos < 2, f2_ref[:, cols], r2)
        else:
            n = uh.shape[0]
            prev1, prev2 = uh[n - 1:n, :] * keep, uh[n - 2:n - 1, :] * keep
            u1 = jnp.concatenate([jnp.where(top < 1, prev1, r1[:8]), r1[8:]], 0)
            top2 = jnp.where(top < 1, prev2, jnp.where(top < 2, prev1, r2[:8]))
            u2 = jnp.concatenate([top2, r2[8:]], 0)
        w = w_ref[:, cols]
        return w[0:1, :] * u2 + w[1:2, :] * u1 + w[2:3, :] * u + cb_ref[:, cols]

    sub = [slice(c0, c0 + MXU_COLS) for c0 in range(0, h_ref.shape[1], MXU_COLS)]
    prods = [(project(cols, bg_ref), project(cols, bv_ref)) for cols in sub]
    for cols, ((ug, uhg), (uv, uhv)) in zip(sub, prods):
        if sample:
            cg = conv(cols, ug, uhg, wg_ref, cbg_ref, f1g_ref, f2g_ref, tg_ref)
            cv = conv(cols, uv, uhv, wv_ref, cbv_ref, f1v_ref, f2v_ref, tv_ref)
        else:
            cg = conv(cols, ug, uhg, wg_ref, cbg_ref, None, None, tg_ref)
            cv = conv(cols, uv, uhv, wv_ref, cbv_ref, None, None, tv_ref)
        h_ref[:, cols] = (_gelu(cg) * cv).astype(h_ref.dtype)


def ffn_in(x, w, conv_w, conv_b, layer, h_all=None, *, row0, n_rows, tm, tn, tail, fix1=None, fix2=None,
           side_cast=None):
    sample = fix1 is not None
    k = x.shape[1]
    nj = D_FF // tn
    ni = n_rows // tm
    rb0 = row0 // tm
    halo = 16
    a_spec = pl.BlockSpec((tm, k), lambda j, i: (rb0 + i, 0))
    bg = pl.BlockSpec((None, k, tn), lambda j, i: (layer, 0, j))
    bv = pl.BlockSpec((None, k, tn), lambda j, i: (layer, 0, nj + j))
    wg = pl.BlockSpec((None, CONV_W, tn), lambda j, i: (layer, 0, j))
    wv = pl.BlockSpec((None, CONV_W, tn), lambda j, i: (layer, 0, nj + j))
    cg = pl.BlockSpec((None, 1, tn), lambda j, i: (layer, 0, j))
    cv = pl.BlockSpec((None, 1, tn), lambda j, i: (layer, 0, nj + j))
    conv_b3 = conv_b.reshape(DEPTH, 1, 2 * D_FF)
    if sample:
        fg = pl.BlockSpec((tm, tn), lambda j, i: (i, j))
        fv = pl.BlockSpec((tm, tn), lambda j, i: (i, nj + j))
        in_specs = [a_spec, bg, bv, wg, wv, cg, cv, fg, fv, fg, fv]
        args = [x, w, w, conv_w, conv_w, conv_b3, conv_b3, fix1, fix1, fix2, fix2]
    else:
        ah = pl.BlockSpec((halo, k), lambda j, i: (jnp.maximum((rb0 + i) * (tm // halo) - 1, 0), 0))
        in_specs = [a_spec, ah, bg, bv, wg, wv, cg, cv]
        args = [x, x, w, w, conv_w, conv_w, conv_b3, conv_b3]
    t_spec = pl.BlockSpec((1, tail, tn), lambda j, i: (i, 0, j))
    out_specs = [pl.BlockSpec((tm, tn), lambda j, i: (rb0 + i, j)), t_spec, t_spec]
    out_shape = [jax.ShapeDtypeStruct((ROWS, D_FF), BF16),
                 jax.ShapeDtypeStruct((ni, tail, D_FF), F32),
                 jax.ShapeDtypeStruct((ni, tail, D_FF), F32)]
    if side_cast is not None:
        assert not sample
        rows, cols = side_cast.shape[1:]
        slab = rows // (nj * ni)
        assert slab * nj * ni == rows and slab % 16 == 0
        in_specs.append(pl.BlockSpec((None, slab, cols), lambda j, i: (layer, j * ni + i, 0)))
        args.append(side_cast)
        out_specs.append(pl.BlockSpec((slab, cols), lambda j, i: (j * ni + i, 0)))
        out_shape.append(jax.ShapeDtypeStruct((rows, cols), BF16))
    aliases = {}
    if h_all is not None:
        in_specs.append(pl.BlockSpec(memory_space=pl.ANY))
        args.append(h_all)
        aliases = {len(args) - 1: 0}
    return pl.pallas_call(
        functools.partial(_ffn_in_kernel, sample=sample, seq_tiles=SEQ // tm if not sample else 1, tail=tail,
                          side_cast=side_cast is not None),
        grid=(nj, ni),
        in_specs=in_specs,
        out_specs=out_specs,
        out_shape=out_shape,
        scratch_shapes=[pltpu.VMEM((k, tn), BF16), pltpu.VMEM((k, tn), BF16)],
        input_output_aliases=aliases,
        compiler_params=_params(("parallel", "arbitrary")),
        name="ffn_in",
    )(*args)


def _t5_bucket(rel):
    nb = T5_BUCKETS // 2
    max_exact = nb // 2
    n = jnp.abs(rel)
    nf = jnp.maximum(n, 1).astype(F32)
    large = max_exact + (jnp.log(nf / max_exact) / math.log(T5_MAX_DIST / max_exact)
                         * (nb - max_exact)).astype(jnp.int32)
    large = jnp.minimum(large, nb - 1)
    return jnp.where(rel > 0, nb, 0) + jnp.where(n < max_exact, n, large)


def _t5_bias(table, rel):
    return jnp.transpose(table[_t5_bucket(rel)], (2, 0, 1)).astype(F32)


def _clipped_bias(table, rel):
    return table[:, jnp.clip(rel, -BAND_MAX_REL, BAND_MAX_REL) + BAND_MAX_REL].astype(F32)


def _toeplitz_diag(bias_fn, n_rows, n_cols, rel0, n):
    assert n_cols + n_rows - 1 <= n
    k = np.arange(n)
    rel = np.where(k < n_cols, k, k - n) + rel0
    return jnp.broadcast_to(bias_fn(jnp.asarray(rel)[None, :]), (16, 8, n))


def _band_diag(bias_fn, n_back):
    return _toeplitz_diag(bias_fn, QB, n_back * CHUNK + QB, -n_back * CHUNK, 1024)


def _cache_diag(bias_fn, cache_len):
    n = max(256, cache_len * 2)
    return _toeplitz_diag(bias_fn, DEC_SEQ, cache_len, -cache_len, n)


def _rope_tables(pos):
    half = RET_DK // 2
    inv = ROPE_BASE ** (-jnp.arange(half, dtype=F32) / half)
    ang = pos.astype(F32)[:, None] * inv[None, :]
    cos, sin = jnp.cos(ang), jnp.sin(ang)
    cos_t = jnp.concatenate([cos, cos, cos, cos], -1)
    sin_t = jnp.concatenate([-sin, sin, -sin, sin], -1)
    return cos_t, sin_t


def _dup_groups(t):
    g0, g1 = t[..., 0, :], t[..., 1, :]
    return jnp.concatenate([g0, g0, g1, g1], -1)


def _undup(t):
    return jnp.stack([t[:, 0:64], t[:, 128:192]], 1)


def kernel(x_prompt, x_sample, mem_prompt, cache_swa_k, cache_swa_v, state_ret, cache_band_k, cache_band_v, state_ffn_conv, cache_mem_k, cache_mem_v, w_in, t5_table, swa_sink, band_rel_table, w_br_a, w_br_b, w_br_c, w_mix_o, ln1_g, ln1_b, w_mq, w_mk, w_mv, w_mo, ln2_g, ln2_b, w_ffn_in, ffn_conv_w, ffn_conv_b, w_ffn_out, ln3_g, ln3_b):
    x, xb = pack_rows(x_prompt, x_sample)
    memb = mem_prompt.reshape(BATCH * MEM_LEN, D_MODEL).astype(BF16)

    t5 = functools.partial(_t5_bias, t5_table)
    diag_a = _band_diag(t5, SWA_BACK)
    qpos = PAST_LEN + jnp.arange(DEC_SEQ)
    la, lc = cache_swa_k.shape[2], cache_band_k.shape[2]
    rel_n = qpos[None, :] - qpos[:, None]
    diag_a_c, bias_a_n = _cache_diag(t5, la), t5(rel_n)
    cos_p, sin_p = _rope_tables(jnp.arange(SEQ))
    cos_s, sin_s = _rope_tables(qpos)
    zero_state = jnp.zeros((1, BATCH, 4, LANES, LANES), F32)
    no_sink = jnp.zeros((16,), F32)

    w_in_b = permute_w_in(w_in)
    w_br_a_b, w_br_b_b, w_br_c_b = w_br_a.astype(BF16), w_br_b.astype(BF16), w_br_c.astype(BF16)
    w_mix_o_b, w_mo_b = w_mix_o.astype(BF16), w_mo.astype(BF16)
    swa_k2, swa_v2 = _dup_groups(cache_swa_k), _dup_groups(cache_swa_v)
    band_k2 = cache_band_k.reshape(DEPTH, DEC_BATCH, lc, BAND_HEADS * HEAD_DIM)
    band_v2 = cache_band_v.reshape(DEPTH, DEC_BATCH, lc, BAND_HEADS * HEAD_DIM)
    mem_k2 = cache_mem_k.reshape(DEPTH, DEC_BATCH, MEM_LEN, D_MODEL)
    mem_v2 = cache_mem_v.reshape(DEPTH, DEC_BATCH, MEM_LEN, D_MODEL)
    ret_s0 = state_ret.reshape(DEPTH, DEC_BATCH, 4, LANES, LANES)

    outs = {k: [] for k in ("p_ak", "p_av", "p_rs", "p_bk", "p_bv", "p_fc", "p_mk", "p_mv",
                            "s_ak", "s_av", "s_rs", "s_bk", "s_bv", "s_fc")}
    for l in range(DEPTH):
        proj = matmul(xb, w_in_b, l, tm=1056, tn=1536, out_dtype=BF16)
        clipped = functools.partial(_clipped_bias, band_rel_table[l])
        oa = band_attention_prompt(proj, diag_a, swa_sink[l], q_col=COL_QA, k_col=COL_KA2, v_col=COL_VA2,
                                   shared_kv=True, n_back=SWA_BACK, has_sink=True, skew=None)
        oc = band_attention_prompt(proj, _band_diag(clipped, BAND_BACK), no_sink, q_col=COL_QC, k_col=COL_KC,
                                   v_col=COL_VC, shared_kv=False, n_back=BAND_BACK, has_sink=False, skew=1)
        ob, rs_p = retention(proj, cos_p, sin_p, zero_state, 0, row0=0, n_seq=BATCH, seq_len=SEQ,
                             blk_len=RET_L, pos_per_blk=True)
        oa = cached_attention_sample(
            proj, swa_k2, swa_v2, l, diag_a_c, bias_a_n, swa_sink[l], oa,
            q_col=COL_QA, k_col=COL_KA2, v_col=COL_VA2, shared_kv=True, has_sink=True)
        oc = cached_attention_sample(
            proj, band_k2, band_v2, l, _cache_diag(clipped, lc), clipped(rel_n), no_sink, oc,
            q_col=COL_QC, k_col=COL_KC, v_col=COL_VC, shared_kv=False, has_sink=False)
        ob, rs_s = retention(proj, cos_s, sin_s, ret_s0, l, ob, row0=P_ROWS, n_seq=DEC_BATCH, seq_len=DEC_SEQ,
                             blk_len=DEC_SEQ, pos_per_blk=False)
        mix = gated_branch_sum(oa, ob, oc, w_br_a_b, w_br_b_b, w_br_c_b, proj, l)
        x, xb = matmul_residual_ln(mix, w_mix_o_b, l, x, ln1_g, ln1_b, tm=528)

        mk = matmul(memb, w_mk, l, tm=512, tn=1024, out_dtype=F32)
        mv = matmul(memb, w_mv, l, tm=512, tn=1024, out_dtype=F32)
        qm = matmul(xb, w_mq, l, tm=1056, tn=2048, out_dtype=BF16)
        om = mem_attention(qm, mk.reshape(1, BATCH, MEM_LEN, D_MODEL), mv.reshape(1, BATCH, MEM_LEN, D_MODEL), 0,
                           row0=0, n_rows=P_ROWS, tm=1024, heads=MEM_HEADS, rows_per_kv=SEQ)
        om = mem_attention(qm, mem_k2, mem_v2, l, om, row0=P_ROWS, n_rows=S_ROWS, tm=DEC_SEQ,
                           heads=MEM_HEADS, rows_per_kv=DEC_SEQ)
        x, xb = matmul_residual_ln(om, w_mo_b, l, x, ln2_g, ln2_b, tm=528)

        h, tg_p, tv_p, w_ffn_out_b = ffn_in(xb, w_ffn_in, ffn_conv_w, ffn_conv_b, l, row0=0, n_rows=P_ROWS,
                                            tm=1024, tn=512, tail=8, side_cast=w_ffn_out)
        st = state_ffn_conv[l]
        fix1 = jnp.pad(st[:, 1:2], ((0, 0), (0, DEC_SEQ - 1), (0, 0))).reshape(S_ROWS, 2 * D_FF)
        fix2 = jnp.pad(st, ((0, 0), (0, DEC_SEQ - 2), (0, 0))).reshape(S_ROWS, 2 * D_FF)
        h, tg_s, tv_s = ffn_in(xb, w_ffn_in, ffn_conv_w, ffn_conv_b, l, h, row0=P_ROWS, n_rows=S_ROWS,
                               tm=S_ROWS, tn=512, tail=S_ROWS, fix1=fix1, fix2=fix2)
        if l < DEPTH - 1:
            x, xb = matmul_residual_ln(h, w_ffn_out_b, l, x, ln3_g, ln3_b, tm=S_ROWS)
        else:
            y_prompt, y_sample = matmul_residual_ln(h, w_ffn_out_b, l, x, ln3_g, ln3_b, tm=S_ROWS,
                                                    split_rows=P_ROWS)

        sf = proj[P_ROWS:]
        la_p, lc_p = min(SWA_BACK * CHUNK, SEQ), min(BAND_BACK * CHUNK, SEQ)
        seq_tail = lambda n, c0, w: jnp.stack(
            [proj[(b + 1) * SEQ - n:(b + 1) * SEQ, c0:c0 + w] for b in range(BATCH)], 0)
        outs["p_ak"].append(_undup(seq_tail(la_p, COL_KA2, 256).reshape(BATCH * la_p, 256))
                            .reshape(BATCH, la_p, 2, 64).astype(F32))
        outs["p_av"].append(_undup(seq_tail(la_p, COL_VA2, 256).reshape(BATCH * la_p, 256))
                            .reshape(BATCH, la_p, 2, 64).astype(F32))
        outs["p_rs"].append(rs_p.reshape(BATCH, RET_HEADS, RET_DK, RET_DV))
        outs["p_bk"].append(seq_tail(lc_p, COL_KC, 1024).reshape(BATCH, lc_p, 16, 64).astype(F32))
        outs["p_bv"].append(seq_tail(lc_p, COL_VC, 1024).reshape(BATCH, lc_p, 16, 64).astype(F32))
        last = [(b + 1) * (SEQ // 1024) - 1 for b in range(BATCH)]
        outs["p_fc"].append(jnp.stack(
            [jnp.concatenate([tg_p[t, 6:8], tv_p[t, 6:8]], -1) for t in last], 0))
        outs["p_mk"].append(mk.reshape(BATCH, MEM_LEN, MEM_HEADS, MEM_HD))
        outs["p_mv"].append(mv.reshape(BATCH, MEM_LEN, MEM_HEADS, MEM_HD))
        outs["s_ak"].append(_undup(sf[:, COL_KA2:COL_KA2 + 256]).reshape(DEC_BATCH, DEC_SEQ, 2, 64).astype(F32))
        outs["s_av"].append(_undup(sf[:, COL_VA2:COL_VA2 + 256]).reshape(DEC_BATCH, DEC_SEQ, 2, 64).astype(F32))
        outs["s_rs"].append(rs_s.reshape(DEC_BATCH, RET_HEADS, RET_DK, RET_DV))
        outs["s_bk"].append(sf[:, COL_KC:COL_KC + 1024].reshape(DEC_BATCH, DEC_SEQ, 16, 64).astype(F32))
        outs["s_bv"].append(sf[:, COL_VC:COL_VC + 1024].reshape(DEC_BATCH, DEC_SEQ, 16, 64).astype(F32))
        u_s = jnp.concatenate([tg_s[0], tv_s[0]], -1).reshape(DEC_BATCH, DEC_SEQ, 2 * D_FF)
        outs["s_fc"].append(u_s[:, DEC_SEQ - 2:])

    st = lambda name: jnp.stack(outs[name], 0)
    return (y_prompt.reshape(BATCH, SEQ, D_MODEL), y_sample.reshape(DEC_BATCH, DEC_SEQ, D_MODEL),
            st("p_ak"), st("p_av"), st("p_rs"), st("p_bk"), st("p_bv"), st("p_fc"), st("p_mk"), st("p_mv"),
            st("s_ak"), st("s_av"), st("s_rs"), st("s_bk"), st("s_bv"), st("s_fc"))
```

```python
import functools
import math

import numpy as np
import jax
import jax.numpy as jnp
from jax import lax
from jax.experimental import pallas as pl
from jax.experimental.pallas import tpu as pltpu

F32 = jnp.float32
BF16 = jnp.bfloat16

D_MODEL = 2048
BATCH = 2
SEQ = 4096
DEPTH = 2
DEC_BATCH = 16
DEC_SEQ = 16
PAST_LEN = 2048
CHUNK = 64
HEAD_DIM = 64
SWA_BACK = 2
SWA_HEADS = 16
SWA_KV_HEADS = 2
T5_BUCKETS = 32
T5_MAX_DIST = 128
RET_HEADS = 8
RET_DK = 64
RET_DV = 128
ROPE_BASE = 10000.0
RET_NORM_EPS = 1e-5
BAND_BACK = 8
BAND_HEADS = 16
BAND_MAX_REL = 256
MEM_LEN = 256
MEM_HEADS = 4
MEM_HD = D_MODEL // MEM_HEADS
D_FF = 5632
CONV_W = 3
DN_ALPHA = (2 * DEPTH) ** 0.25
LN_EPS = 1e-5

P_ROWS = BATCH * SEQ
S_ROWS = DEC_BATCH * DEC_SEQ
ROWS = P_ROWS + S_ROWS

COL_QA = 0
COL_QC = 1024
COL_KC = 2048
COL_VC = 3072
COL_VB = 4096
COL_GR = 5120
COL_GA = 6144
COL_GB = 8192
COL_GC = 10240
COL_QB = 12288
COL_KB = 12800
COL_KA2 = 13312
COL_VA2 = 13568
PROJ_COLS = 13824

LANES = 128
MXU_COLS = 256
QB = 256
RET_L = 256
NEG = -1e30
LOG2E = math.log2(math.e)
VMEM_LIMIT = 48 * 1024 * 1024

NT = (((1,), (1,)), ((), ()))
TN = (((0,), (0,)), ((), ()))


def _params(sem, vmem=VMEM_LIMIT):
    return pltpu.CompilerParams(dimension_semantics=sem, vmem_limit_bytes=vmem)


def _pack_rows_kernel(p_ref, s_ref, of_ref, ob_ref, *, n_prompt_tiles):
    src = jnp.where(pl.program_id(0) < n_prompt_tiles, p_ref[...], s_ref[...])
    of_ref[...] = src
    ob_ref[...] = src.astype(BF16)


def pack_rows(x_prompt, x_sample):
    tm = S_ROWS
    npt = P_ROWS // tm
    return pl.pallas_call(
        functools.partial(_pack_rows_kernel, n_prompt_tiles=npt),
        grid=(npt + 1,),
        in_specs=[pl.BlockSpec((tm, D_MODEL), lambda i: (jnp.minimum(i, npt - 1), 0)),
                  pl.BlockSpec((tm, D_MODEL), lambda i: (0, 0))],
        out_specs=[pl.BlockSpec((tm, D_MODEL), lambda i: (i, 0)), pl.BlockSpec((tm, D_MODEL), lambda i: (i, 0))],
        out_shape=[jax.ShapeDtypeStruct((ROWS, D_MODEL), F32), jax.ShapeDtypeStruct((ROWS, D_MODEL), BF16)],
        compiler_params=_params(("parallel",)),
        name="pack_rows",
    )(x_prompt.reshape(P_ROWS, D_MODEL), x_sample.reshape(S_ROWS, D_MODEL))


def _w_in_permute_kernel(tbl_ref, src_ref, o_ref):
    mode = tbl_ref[1, pl.program_id(1)]
    t = src_ref[...]

    @pl.when(mode == 0)
    def _():
        o_ref[...] = t.astype(BF16)

    for m, half in ((1, t[:, :LANES]), (2, t[:, LANES:])):
        @pl.when(mode == m)
        def _():
            swapped = pltpu.roll(half, HEAD_DIM, 1)
            o_ref[...] = jnp.concatenate([jnp.where(_head_lanes(0), half, swapped),
                                          jnp.where(_head_lanes(0), swapped, half)], 1).astype(BF16)


def permute_w_in(w_in):
    blk = 256
    src = lambda off, width: list(range(off // blk, (off + width) // blk))
    order = (src(0, 1024) + src(4352, 1024) + src(5376, 1024) + src(6400, 1024) + src(2304, 1024)
             + src(3328, 1024) + src(7424, 2048) + src(9472, 2048) + src(11520, 2048) + src(1280, 512)
             + src(1792, 512))
    kv = 1024 // blk
    tbl = np.array([order + [kv, kv], [0] * len(order) + [1, 2]], np.int32)
    assert tbl.shape[1] * blk == PROJ_COLS
    d = w_in.shape[1]
    return pl.pallas_call(
        _w_in_permute_kernel,
        grid_spec=pltpu.PrefetchScalarGridSpec(
            num_scalar_prefetch=1,
            grid=(DEPTH, PROJ_COLS // blk),
            in_specs=[pl.BlockSpec((None, d, blk), lambda l, j, tbl: (l, 0, tbl[0, j]))],
            out_specs=pl.BlockSpec((None, d, blk), lambda l, j, tbl: (l, 0, j)),
        ),
        out_shape=jax.ShapeDtypeStruct((DEPTH, d, PROJ_COLS), BF16),
        compiler_params=_params(("parallel", "parallel")),
        name="permute_w_in",
    )(jnp.asarray(tbl), w_in)


def _mm_kernel(a_ref, b_ref, o_ref, *scratch):
    if scratch:
        bb_ref, = scratch

        @pl.when(pl.program_id(0) == 0)
        def _():
            bb_ref[...] = b_ref[...].astype(BF16)

        b = bb_ref[...]
    else:
        b = b_ref[...].astype(BF16)
    o_ref[...] = jnp.dot(a_ref[...], b, preferred_element_type=F32).astype(o_ref.dtype)


def matmul(a, b, layer, *, tm, tn, out_dtype):
    m, k = a.shape
    n = b.shape[2]
    ni, nj = m // tm, n // tn
    resident = b.dtype == F32 and ni > 1
    assert b.dtype == BF16 or ni == 1 or nj == 1
    b_mode = dict(pipeline_mode=pl.Buffered(1)) if resident else {}
    return pl.pallas_call(
        _mm_kernel,
        grid=(ni, nj),
        in_specs=[pl.BlockSpec((tm, k), lambda i, j: (i, 0)),
                  pl.BlockSpec((None, k, tn), lambda i, j: (layer, 0, j), **b_mode)],
        out_specs=pl.BlockSpec((tm, tn), lambda i, j: (i, j)),
        out_shape=jax.ShapeDtypeStruct((m, n), out_dtype),
        scratch_shapes=[pltpu.VMEM((k, tn), BF16)] if resident else [],
        compiler_params=_params(("arbitrary", "arbitrary") if resident else ("parallel", "parallel")),
        name="matmul",
    )(a, b)


def _gate_mm_kernel(oa_ref, ob_ref, oc_ref, wa_ref, wb_ref, wc_ref, ga_ref, gb_ref, gc_ref, o_ref):
    acc = None
    for o, w, g in ((oa_ref, wa_ref, ga_ref), (ob_ref, wb_ref, gb_ref), (oc_ref, wc_ref, gc_ref)):
        t = jnp.dot(o[...], w[...], preferred_element_type=F32) * jax.nn.sigmoid(g[...].astype(F32))
        acc = t if acc is None else acc + t
    o_ref[...] = acc.astype(o_ref.dtype)


def gated_branch_sum(oa, ob, oc, wa, wb, wc, proj, layer, *, tm=528):
    m, k = oa.shape
    n = wa.shape[2]
    o_spec = pl.BlockSpec((tm, k), lambda i: (i, 0))
    w_spec = pl.BlockSpec((None, k, n), lambda i: (layer, 0, 0), pipeline_mode=pl.Buffered(1))

    def g_spec(col):
        return pl.BlockSpec((tm, n), lambda i: (i, col // n))

    return pl.pallas_call(
        _gate_mm_kernel,
        grid=(m // tm,),
        in_specs=[o_spec, o_spec, o_spec, w_spec, w_spec, w_spec,
                  g_spec(COL_GA), g_spec(COL_GB), g_spec(COL_GC)],
        out_specs=pl.BlockSpec((tm, n), lambda i: (i, 0)),
        out_shape=jax.ShapeDtypeStruct((m, n), BF16),
        compiler_params=_params(("parallel",)),
        name="gated_branch_sum",
    )(oa, ob, oc, wa, wb, wc, proj, proj, proj)


def _mm_ln_kernel(a_ref, b_ref, r_ref, g_ref, bt_ref, o1_ref, o2_ref, acc_ref, *, split_tiles):
    i = pl.program_id(0)

    @pl.when(i == 0)
    def _():
        acc_ref[...] = jnp.zeros_like(acc_ref)

    y = DN_ALPHA * r_ref[...] + acc_ref[...]
    mu = jnp.mean(y, -1, keepdims=True)
    d = y - mu
    var = jnp.mean(d * d, -1, keepdims=True)
    out = d * lax.rsqrt(var + LN_EPS) * g_ref[...] + bt_ref[...]
    if split_tiles is None:
        o1_ref[...] = out
        o2_ref[...] = out.astype(BF16)
    else:
        o2_ref[...] = out

        @pl.when(i <= split_tiles)
        def _():
            o1_ref[...] = out
    acc_ref[...] = jnp.dot(a_ref[...], b_ref[...], preferred_element_type=F32)


def matmul_residual_ln(a, b, layer, resid, gain, bias, *, tm, split_rows=None):
    m, kk = a.shape
    n = b.shape[-1]
    nt = m // tm
    cur = lambda i: (jnp.minimum(i, nt - 1), 0)
    prev = lambda i: (jnp.maximum(i - 1, 0), 0)
    if b.ndim == 3:
        b_spec = pl.BlockSpec((None, kk, n), lambda i: (layer, 0, 0), pipeline_mode=pl.Buffered(1))
    else:
        b_spec = pl.BlockSpec((kk, n), lambda i: (0, 0), pipeline_mode=pl.Buffered(1))
    if split_rows is None:
        split_tiles = None
        out_specs = [pl.BlockSpec((tm, n), prev), pl.BlockSpec((tm, n), prev)]
        out_shape = [jax.ShapeDtypeStruct((m, n), F32), jax.ShapeDtypeStruct((m, n), BF16)]
    else:
        split_tiles = split_rows // tm
        assert split_tiles == nt - 1
        out_specs = [pl.BlockSpec((tm, n), lambda i: (jnp.clip(i - 1, 0, split_tiles - 1), 0)),
                     pl.BlockSpec((tm, n), lambda i: (0, 0))]
        out_shape = [jax.ShapeDtypeStruct((split_rows, n), F32), jax.ShapeDtypeStruct((tm, n), F32)]
    return pl.pallas_call(
        functools.partial(_mm_ln_kernel, split_tiles=split_tiles),
        grid=(nt + 1,),
        in_specs=[pl.BlockSpec((tm, kk), cur),
                  b_spec,
                  pl.BlockSpec((tm, n), prev),
                  pl.BlockSpec((None, 1, n), lambda i: (layer, 0, 0)),
                  pl.BlockSpec((None, 1, n), lambda i: (layer, 0, 0))],
        out_specs=out_specs,
        out_shape=out_shape,
        scratch_shapes=[pltpu.VMEM((tm, n), F32)],
        compiler_params=_params(("arbitrary",)),
        name="matmul_residual_ln",
    )(a, b, resid, gain.reshape(DEPTH, 1, n), bias.reshape(DEPTH, 1, n))


def _head_lanes(j):
    lane = lax.broadcasted_iota(jnp.int32, (1, LANES), 1)
    return (lane < HEAD_DIM) if j == 0 else (lane >= HEAD_DIM)


def _band_attn_kernel(sink_ref, diag_ref, q_ref, *refs, n_prev_blocks, n_back, shared_kv, has_sink, skew):
    nk = n_prev_blocks + 1
    k_refs = refs[:nk]
    v_refs = refs[nk:2 * nk]
    o_ref = refs[2 * nk]
    bias_ref = refs[2 * nk + 1]
    b = pl.program_id(0)
    i = pl.program_id(1)
    n_prev_rows = n_back * CHUNK
    kw = n_prev_rows + QB

    @pl.when((b == 0) & (i == 0))
    def _():
        n = diag_ref.shape[2]
        cb = lax.broadcasted_iota(jnp.int32, (kw, QB), 0) // CHUNK
        qc = lax.broadcasted_iota(jnp.int32, (kw, QB), 1) // CHUNK
        allowed = (cb >= qc) & (cb - n_back <= qc)
        for h in range(16):
            t = pltpu.roll(jnp.broadcast_to(diag_ref[h, 0:1, :], (kw, n)), 0, 1, stride=1, stride_axis=0)
            bias_ref[h] = jnp.where(allowed, t[:, :QB] * LOG2E, NEG)

    krow = lax.broadcasted_iota(jnp.int32, (kw, 1), 0)
    kmask = jnp.where(krow >= jnp.maximum(n_prev_rows - i * QB, 0), 0.0, NEG).astype(BF16)
    lane = lax.broadcasted_iota(jnp.int32, (1, LANES), 1)

    def kv_cols(h):
        c0 = (h // 8 if shared_kv else h // 2) * LANES
        return slice(c0, c0 + LANES)

    def scores(h):
        p, j = divmod(h, 2)
        q = q_ref[:, p * LANES:(p + 1) * LANES] * (HEAD_DIM ** -0.5 * LOG2E)
        k_all = jnp.concatenate([r[:, kv_cols(h)] for r in k_refs], axis=0)
        mask_lane = lane == (HEAD_DIM if j == 0 else 0)
        qj = jnp.where(mask_lane, jnp.ones_like(q), jnp.where(_head_lanes(j), q, jnp.zeros_like(q)))
        kj = jnp.where(mask_lane, kmask, k_all)
        return lax.dot_general(kj, qj, NT, preferred_element_type=F32)

    def softmax(h, s_all):
        sink = sink_ref[h] * LOG2E if has_sink else None
        ps, dens = [], []
        for c in range(QB // LANES):
            lanes = slice(c * LANES, (c + 1) * LANES)
            lo, hi = c * LANES, c * LANES + n_prev_rows + LANES
            s = s_all[lo:hi, lanes] + bias_ref[h, lo:hi, lanes]
            m = jnp.max(s, 0, keepdims=True)
            if has_sink:
                m = jnp.maximum(m, sink)
            e = jnp.exp2(s - m)
            den = jnp.sum(e, 0, keepdims=True)
            if has_sink:
                den = den + jnp.exp2(sink - m)
            parts = [e.astype(BF16)]
            if lo:
                parts.insert(0, jnp.zeros((lo, LANES), BF16))
            if hi < kw:
                parts.append(jnp.zeros((kw - hi, LANES), BF16))
            ps.append(jnp.concatenate(parts, 0))
            dens.append(den)
        return jnp.concatenate(ps, 1), jnp.concatenate(dens, 1)

    def weighted_values(h, p_all, den):
        v_all = jnp.concatenate([r[:, kv_cols(h)] for r in v_refs], axis=0)
        return lax.dot_general(v_all, p_all, TN, preferred_element_type=F32) / den

    lag = 16 if skew is None else skew
    s_q, p_q, outs = {}, {}, {}
    for t in range(16 + 2 * lag):
        if t < 16:
            s_q[t] = scores(t)
        if lag <= t < 16 + lag:
            p_q[t - lag] = softmax(t - lag, s_q.pop(t - lag))
        if t >= 2 * lag:
            h = t - 2 * lag
            outs[h] = weighted_values(h, *p_q.pop(h))
            if h % 2:
                p = h // 2
                pair = jnp.where(lax.broadcasted_iota(jnp.int32, (LANES, 1), 0) < HEAD_DIM,
                                 outs.pop(h - 1), outs.pop(h))
                o_ref[:, p * LANES:(p + 1) * LANES] = pair.T.astype(o_ref.dtype)


def band_attention_prompt(proj, diag, sink, *, q_col, k_col, v_col, shared_kv, n_back, has_sink, skew):
    n_prev_rows = n_back * CHUNK
    nqb = SEQ // QB
    if n_prev_rows >= QB:
        n_prev_blocks, pb = n_prev_rows // QB, QB
    else:
        n_prev_blocks, pb = 1, n_prev_rows
    per = QB // pb
    kvw = 256 if shared_kv else 1024

    def prev_spec(col, back):
        return pl.BlockSpec(
            (pb, kvw), lambda b, i: (b * (SEQ // pb) + jnp.maximum(i * per - back, 0), col // kvw))

    def own_spec(col):
        return pl.BlockSpec((QB, kvw), lambda b, i: (b * nqb + i, col // kvw))

    k_specs = [prev_spec(k_col, n_prev_blocks - t) for t in range(n_prev_blocks)] + [own_spec(k_col)]
    v_specs = [prev_spec(v_col, n_prev_blocks - t) for t in range(n_prev_blocks)] + [own_spec(v_col)]
    kern = functools.partial(_band_attn_kernel, n_prev_blocks=n_prev_blocks, n_back=n_back,
                             shared_kv=shared_kv, has_sink=has_sink, skew=skew)
    n_in = 2 * (n_prev_blocks + 1)
    return pl.pallas_call(
        kern,
        grid=(BATCH, nqb),
        in_specs=[pl.BlockSpec(memory_space=pltpu.SMEM),
                  pl.BlockSpec(diag.shape, lambda b, i: (0, 0, 0)),
                  pl.BlockSpec((QB, 1024), lambda b, i: (b * nqb + i, q_col // 1024))]
                 + k_specs + v_specs,
        out_specs=pl.BlockSpec((QB, 1024), lambda b, i: (b * nqb + i, 0)),
        out_shape=jax.ShapeDtypeStruct((ROWS, 16 * HEAD_DIM), BF16),
        scratch_shapes=[pltpu.VMEM((16, n_prev_rows + QB, QB), F32)],
        compiler_params=_params(("arbitrary", "arbitrary")),
        name="band_attention_prompt",
    )(sink, diag, proj, *([proj] * n_in))


def _cached_attn_kernel(sink_ref, q_ref, kn_ref, vn_ref, kc_ref, vc_ref, diag_ref, bn_ref, _, o_ref, bc_ref,
                        *, shared_kv, has_sink):
    lc = kc_ref.shape[1]

    @pl.when(pl.program_id(0) == 0)
    def _():
        n = diag_ref.shape[2]
        for h in range(16):
            t = pltpu.roll(jnp.broadcast_to(diag_ref[h, 0:1, :], (DEC_SEQ, n)), 0, 1, stride=1, stride_axis=0)
            bc_ref[h] = t[:, :lc]

    kv_cols = [(p // 4 if shared_kv else p) * LANES for p in range(8)]
    scores = []
    for p in range(8):
        c0 = kv_cols[p]
        q = q_ref[:, p * LANES:(p + 1) * LANES] * (HEAD_DIM ** -0.5)
        kc = kc_ref[0, :, c0:c0 + LANES].astype(BF16)
        kn = kn_ref[:, c0:c0 + LANES]
        for j in range(2):
            h = 2 * p + j
            qj = jnp.where(_head_lanes(j), q, jnp.zeros_like(q))
            scores.append((lax.dot_general(qj, kc, NT, preferred_element_type=F32) + bc_ref[h],
                           lax.dot_general(qj, kn, NT, preferred_element_type=F32) + bn_ref[h]))
    probs = []
    for h, (s_c, s_n) in enumerate(scores):
        m = jnp.maximum(jnp.max(s_c, -1, keepdims=True), jnp.max(s_n, -1, keepdims=True))
        if has_sink:
            m = jnp.maximum(m, sink_ref[h])
        e_c, e_n = jnp.exp(s_c - m), jnp.exp(s_n - m)
        den = jnp.sum(e_c, -1, keepdims=True) + jnp.sum(e_n, -1, keepdims=True)
        if has_sink:
            den = den + jnp.exp(sink_ref[h] - m)
        probs.append((e_c.astype(BF16), e_n.astype(BF16), den))
    for p in range(8):
        c0 = kv_cols[p]
        vc = vc_ref[0, :, c0:c0 + LANES].astype(BF16)
        vn = vn_ref[:, c0:c0 + LANES]
        outs = []
        for j in range(2):
            e_c, e_n, den = probs[2 * p + j]
            outs.append((jnp.dot(e_c, vc, preferred_element_type=F32)
                         + jnp.dot(e_n, vn, preferred_element_type=F32)) / den)
        o_ref[:, p * LANES:(p + 1) * LANES] = jnp.where(_head_lanes(0), outs[0], outs[1]).astype(o_ref.dtype)


def cached_attention_sample(proj, k_cache, v_cache, layer, diag_c, bias_n, sink, o_all, *, q_col, k_col, v_col,
                            shared_kv, has_sink):
    lc, wc = k_cache.shape[2], k_cache.shape[3]
    r0 = P_ROWS // DEC_SEQ
    kern = functools.partial(_cached_attn_kernel, shared_kv=shared_kv, has_sink=has_sink)
    cache = pl.BlockSpec((None, 1, lc, wc), lambda s: (layer, s, 0, 0))
    return pl.pallas_call(
        kern,
        grid=(DEC_BATCH,),
        in_specs=[pl.BlockSpec(memory_space=pltpu.SMEM),
                  pl.BlockSpec((DEC_SEQ, 1024), lambda s: (r0 + s, q_col // 1024)),
                  pl.BlockSpec((DEC_SEQ, wc), lambda s: (r0 + s, k_col // wc)),
                  pl.BlockSpec((DEC_SEQ, wc), lambda s: (r0 + s, v_col // wc)),
                  cache, cache,
                  pl.BlockSpec(diag_c.shape, lambda s: (0, 0, 0)),
                  pl.BlockSpec((16, DEC_SEQ, DEC_SEQ), lambda s: (0, 0, 0)),
                  pl.BlockSpec(memory_space=pl.ANY)],
        out_specs=pl.BlockSpec((DEC_SEQ, 1024), lambda s: (r0 + s, 0)),
        out_shape=jax.ShapeDtypeStruct(o_all.shape, o_all.dtype),
        scratch_shapes=[pltpu.VMEM((16, DEC_SEQ, lc), F32)],
        input_output_aliases={8: 0},
        compiler_params=_params(("arbitrary",)),
        name="cached_attention_sample",
    )(sink, proj, proj, proj, k_cache, v_cache, diag_c, bias_n, o_all)


def _ret_kernel(q_ref, k_ref, v_ref, g_ref, cos_ref, sin_ref, s0_ref, *refs, blk_len):
    o_ref, sout_ref, st_ref, dec_ref, rdec_ref = refs[-5:]
    blk = pl.program_id(1)
    log_gs = [math.log(1.0 - 2.0 ** (-5.0 - h)) for h in range(RET_HEADS)]

    @pl.when((pl.program_id(0) == 0) & (blk == 0))
    def _():
        ii = lax.broadcasted_iota(jnp.int32, (blk_len, blk_len), 0)
        jj = lax.broadcasted_iota(jnp.int32, (blk_len, blk_len), 1)
        diff = (ii - jj).astype(F32)
        row = lax.broadcasted_iota(jnp.int32, (blk_len, LANES), 0).astype(F32)
        for h in range(RET_HEADS):
            dec_ref[h] = jnp.where(diff >= 0, jnp.exp(log_gs[h] * jnp.maximum(diff, 0.0)), 0.0)
            rdec_ref[h] = jnp.exp(log_gs[h] * (row + 1.0))
            rdec_ref[RET_HEADS + h] = jnp.exp(log_gs[h] * (blk_len - 1.0 - row))

    @pl.when(blk == 0)
    def _():
        st_ref[...] = s0_ref[0]

    cos = cos_ref[...]
    sin = sin_ref[...]
    lane = lax.broadcasted_iota(jnp.int32, (1, LANES), 1)
    low_half = (lane % RET_DK) < (RET_DK // 2)

    def rope(x):
        x = x.astype(F32)
        swapped = jnp.where(low_half, pltpu.roll(x, LANES - RET_DK // 2, 1), pltpu.roll(x, RET_DK // 2, 1))
        return x * cos + swapped * sin

    srow = lax.broadcasted_iota(jnp.int32, (LANES, 1), 0)

    for p in range(RET_HEADS // 2):
        qr = rope(q_ref[:, p * LANES:(p + 1) * LANES])
        kr = rope(k_ref[:, p * LANES:(p + 1) * LANES]) * (RET_DK ** -0.5)
        kb = kr.astype(BF16)
        state = st_ref[p]
        state_b = state.astype(BF16)
        upd = None
        for j in range(2):
            h = 2 * p + j
            qj = jnp.where(_head_lanes(j), qr, 0.0).astype(BF16)
            vh = v_ref[:, h * RET_DV:(h + 1) * RET_DV]
            qk = lax.dot_general(qj, kb, NT, preferred_element_type=F32) * dec_ref[h]
            o = jnp.dot(qk.astype(BF16), vh, preferred_element_type=F32)
            o = o + jnp.dot(qj, state_b, preferred_element_type=F32) * rdec_ref[h]
            mu = jnp.mean(o, -1, keepdims=True)
            d = o - mu
            var = jnp.mean(d * d, -1, keepdims=True)
            gate = g_ref[:, h * RET_DV:(h + 1) * RET_DV].astype(F32)
            o_ref[:, h * RET_DV:(h + 1) * RET_DV] = (
                d * lax.rsqrt(var + RET_NORM_EPS) * (gate * jax.nn.sigmoid(gate))).astype(o_ref.dtype)
            kwj = jnp.where(_head_lanes(j), kr * rdec_ref[RET_HEADS + h], 0.0).astype(BF16)
            u = lax.dot_general(kwj, vh, TN, preferred_element_type=F32)
            upd = u if upd is None else upd + u
        carry = jnp.where(srow < RET_DK, math.exp(log_gs[2 * p] * blk_len), math.exp(log_gs[2 * p + 1] * blk_len))
        st_ref[p] = carry * state + upd

    @pl.when(blk == pl.num_programs(1) - 1)
    def _():
        sout_ref[0] = st_ref[...]


def retention(proj, cos, sin, s0, layer, o_all=None, *, row0, n_seq, seq_len, blk_len, pos_per_blk):
    nb = seq_len // blk_len
    rb0 = row0 // blk_len

    def rows(col, width):
        return pl.BlockSpec((blk_len, width), lambda b, t: (rb0 + b * nb + t, col // width))

    tab = pl.BlockSpec((blk_len, LANES), lambda b, t: (t if pos_per_blk else 0, 0))
    st = pl.BlockSpec((1, 4, LANES, LANES), lambda b, t: (b, 0, 0, 0))
    st_in = pl.BlockSpec((None, 1, 4, LANES, LANES), lambda b, t: (layer, b, 0, 0, 0))
    in_specs = [rows(COL_QB, 512), rows(COL_KB, 512), rows(COL_VB, 1024), rows(COL_GR, 1024), tab, tab, st_in]
    args = [proj, proj, proj, proj, cos, sin, s0]
    aliases = {}
    if o_all is not None:
        in_specs.append(pl.BlockSpec(memory_space=pl.ANY))
        args.append(o_all)
        aliases = {len(args) - 1: 0}
    return pl.pallas_call(
        functools.partial(_ret_kernel, blk_len=blk_len),
        grid=(n_seq, nb),
        in_specs=in_specs,
        out_specs=[rows(0, 1024), st],
        out_shape=[jax.ShapeDtypeStruct((ROWS, 1024), BF16),
                   jax.ShapeDtypeStruct((n_seq, 4, LANES, LANES), F32)],
        scratch_shapes=[pltpu.VMEM((4, LANES, LANES), F32), pltpu.VMEM((RET_HEADS, blk_len, blk_len), F32),
                        pltpu.VMEM((2 * RET_HEADS, blk_len, LANES), F32)],
        input_output_aliases=aliases,
        compiler_params=_params(("arbitrary", "arbitrary")),
        name="retention",
    )(*args)


def _mem_attn_kernel(q_ref, k_ref, v_ref, *refs, heads):
    o_ref = refs[-1]
    cols = [slice(h * MEM_HD, (h + 1) * MEM_HD) for h in range(heads)]
    scores = [lax.dot_general(q_ref[:, c], k_ref[0, :, c].astype(BF16), NT, preferred_element_type=F32)
              * (MEM_HD ** -0.5) for c in cols]
    probs = []
    for s in scores:
        e = jnp.exp(s - jnp.max(s, -1, keepdims=True))
        probs.append((e.astype(BF16), jnp.sum(e, -1, keepdims=True)))
    for c, (e, den) in zip(cols, probs):
        o_ref[:, c] = (jnp.dot(e, v_ref[0, :, c].astype(BF16), preferred_element_type=F32) / den
                       ).astype(o_ref.dtype)


def mem_attention(qm, k_src, v_src, layer, o_all=None, *, row0, n_rows, tm, heads, rows_per_kv):
    rb0 = row0 // tm
    w = heads * MEM_HD
    kv = pl.BlockSpec((None, 1, MEM_LEN, w), lambda i, h: (layer, i * tm // rows_per_kv, 0, h))
    in_specs = [pl.BlockSpec((tm, w), lambda i, h: (rb0 + i, h)), kv, kv]
    args = [qm, k_src, v_src]
    aliases = {}
    if o_all is not None:
        in_specs.append(pl.BlockSpec(memory_space=pl.ANY))
        args.append(o_all)
        aliases = {3: 0}
    return pl.pallas_call(
        functools.partial(_mem_attn_kernel, heads=heads),
        grid=(n_rows // tm, MEM_HEADS // heads),
        in_specs=in_specs,
        out_specs=pl.BlockSpec((tm, w), lambda i, h: (rb0 + i, h)),
        out_shape=jax.ShapeDtypeStruct((ROWS, D_MODEL), BF16),
        input_output_aliases=aliases,
        compiler_params=_params(("parallel", "parallel")),
        name="mem_attention",
    )(*args)


def _gelu(x):
    return 0.5 * x * (1.0 + lax.erf(x * (2.0 ** -0.5)))


def _ffn_in_kernel(*refs, sample, seq_tiles, tail, side_cast):
    bg_ref, bv_ref = refs[-2:]
    if sample:
        a_ref, wfg_ref, wfv_ref, wg_ref, wv_ref, cbg_ref, cbv_ref, f1g_ref, f1v_ref, f2g_ref, f2v_ref = refs[:11]
    else:
        a_ref, ah_ref, wfg_ref, wfv_ref, wg_ref, wv_ref, cbg_ref, cbv_ref = refs[:8]
    if side_cast:
        h_ref, tg_ref, tv_ref, side_out_ref = refs[-6:-2]
        side_out_ref[...] = refs[8][...].astype(BF16)
    else:
        h_ref, tg_ref, tv_ref = refs[-5:-2]
    tm = a_ref.shape[0]
    i = pl.program_id(1)

    @pl.when(i == 0)
    def _():
        bg_ref[...] = wfg_ref[...].astype(BF16)
        bv_ref[...] = wfv_ref[...].astype(BF16)

    a = a_ref[...]
    if sample:
        pos = lax.broadcasted_iota(jnp.int32, (tm, 1), 0) % DEC_SEQ
    else:
        top = lax.broadcasted_iota(jnp.int32, (8, 1), 0)
        keep = (i % seq_tiles != 0).astype(F32)

    def project(cols, b_ref):
        u = jnp.dot(a, b_ref[:, cols], preferred_element_type=F32)
        uh = None if sample else jnp.dot(ah_ref[...], b_ref[:, cols], preferred_element_type=F32)
        return u, uh

    def conv(cols, u, uh, w_ref, cb_ref, f1_ref, f2_ref, t_ref):
        t_ref[0, :, cols] = u[tm - tail:, :]
        r1 = pltpu.roll(u, 1, 0)
        r2 = pltpu.roll(u, 2, 0)
        if sample:
            u1 = jnp.where(pos < 1, f1_ref[:, cols], r1)
            u2 = jnp.where(pos < 2, f2_ref[:, cols], r2)
        else:
            n = uh.shape[0]
            prev1, prev2 = uh[n - 1:n, :] * keep, uh[n - 2:n - 1, :] * keep
            u1 = jnp.concatenate([jnp.where(top < 1, prev1, r1[:8]), r1[8:]], 0)
            top2 = jnp.where(top < 1, prev2, jnp.where(top < 2, prev1, r2[:8]))
            u2 = jnp.concatenate([top2, r2[8:]], 0)
        w = w_ref[:, cols]
        return w[0:1, :] * u2 + w[1:2, :] * u1 + w[2:3, :] * u + cb_ref[:, cols]

    sub = [slice(c0, c0 + MXU_COLS) for c0 in range(0, h_ref.shape[1], MXU_COLS)]
    prods = [(project(cols, bg_ref), project(cols, bv_ref)) for cols in sub]
    for cols, ((ug, uhg), (uv, uhv)) in zip(sub, prods):
        if sample:
            cg = conv(cols, ug, uhg, wg_ref, cbg_ref, f1g_ref, f2g_ref, tg_ref)
            cv = conv(cols, uv, uhv, wv_ref, cbv_ref, f1v_ref, f2v_ref, tv_ref)
        else:
            cg = conv(cols, ug, uhg, wg_ref, cbg_ref, None, None, tg_ref)
            cv = conv(cols, uv, uhv, wv_ref, cbv_ref, None, None, tv_ref)
        h_ref[:, cols] = (_gelu(cg) * cv).astype(h_ref.dtype)


def ffn_in(x, w, conv_w, conv_b, layer, h_all=None, *, row0, n_rows, tm, tn, tail, fix1=None, fix2=None,
           side_cast=None):
    sample = fix1 is not None
    k = x.shape[1]
    nj = D_FF // tn
    ni = n_rows // tm
    rb0 = row0 // tm
    halo = 16
    a_spec = pl.BlockSpec((tm, k), lambda j, i: (rb0 + i, 0))
    bg = pl.BlockSpec((None, k, tn), lambda j, i: (layer, 0, j))
    bv = pl.BlockSpec((None, k, tn), lambda j, i: (layer, 0, nj + j))
    wg = pl.BlockSpec((None, CONV_W, tn), lambda j, i: (layer, 0, j))
    wv = pl.BlockSpec((None, CONV_W, tn), lambda j, i: (layer, 0, nj + j))
    cg = pl.BlockSpec((None, 1, tn), lambda j, i: (layer, 0, j))
    cv = pl.BlockSpec((None, 1, tn), lambda j, i: (layer, 0, nj + j))
    conv_b3 = conv_b.reshape(DEPTH, 1, 2 * D_FF)
    if sample:
        fg = pl.BlockSpec((tm, tn), lambda j, i: (i, j))
        fv = pl.BlockSpec((tm, tn), lambda j, i: (i, nj + j))
        in_specs = [a_spec, bg, bv, wg, wv, cg, cv, fg, fv, fg, fv]
        args = [x, w, w, conv_w, conv_w, conv_b3, conv_b3, fix1, fix1, fix2, fix2]
    else:
        ah = pl.BlockSpec((halo, k), lambda j, i: (jnp.maximum((rb0 + i) * (tm // halo) - 1, 0), 0))
        in_specs = [a_spec, ah, bg, bv, wg, wv, cg, cv]
        args = [x, x, w, w, conv_w, conv_w, conv_b3, conv_b3]
    t_spec = pl.BlockSpec((1, tail, tn), lambda j, i: (i, 0, j))
    out_specs = [pl.BlockSpec((tm, tn), lambda j, i: (rb0 + i, j)), t_spec, t_spec]
    out_shape = [jax.ShapeDtypeStruct((ROWS, D_FF), BF16),
                 jax.ShapeDtypeStruct((ni, tail, D_FF), F32),
                 jax.ShapeDtypeStruct((ni, tail, D_FF), F32)]
    if side_cast is not None:
        assert not sample
        rows, cols = side_cast.shape[1:]
        slab = rows // (nj * ni)
        assert slab * nj * ni == rows and slab % 16 == 0
        in_specs.append(pl.BlockSpec((None, slab, cols), lambda j, i: (layer, j * ni + i, 0)))
        args.append(side_cast)
        out_specs.append(pl.BlockSpec((slab, cols), lambda j, i: (j * ni + i, 0)))
        out_shape.append(jax.ShapeDtypeStruct((rows, cols), BF16))
    aliases = {}
    if h_all is not None:
        in_specs.append(pl.BlockSpec(memory_space=pl.ANY))
        args.append(h_all)
        aliases = {len(args) - 1: 0}
    return pl.pallas_call(
        functools.partial(_ffn_in_kernel, sample=sample, seq_tiles=SEQ // tm if not sample else 1, tail=tail,
                          side_cast=side_cast is not None),
        grid=(nj, ni),
        in_specs=in_specs,
        out_specs=out_specs,
        out_shape=out_shape,
        scratch_shapes=[pltpu.VMEM((k, tn), BF16), pltpu.VMEM((k, tn), BF16)],
        input_output_aliases=aliases,
        compiler_params=_params(("parallel", "arbitrary")),
        name="ffn_in",
    )(*args)


def _t5_bucket(rel):
    nb = T5_BUCKETS // 2
    max_exact = nb // 2
    n = jnp.abs(rel)
    nf = jnp.maximum(n, 1).astype(F32)
    large = max_exact + (jnp.log(nf / max_exact) / math.log(T5_MAX_DIST / max_exact)
                         * (nb - max_exact)).astype(jnp.int32)
    large = jnp.minimum(large, nb - 1)
    return jnp.where(rel > 0, nb, 0) + jnp.where(n < max_exact, n, large)


def _t5_bias(table, rel):
    return jnp.transpose(table[_t5_bucket(rel)], (2, 0, 1)).astype(F32)


def _clipped_bias(table, rel):
    return table[:, jnp.clip(rel, -BAND_MAX_REL, BAND_MAX_REL) + BAND_MAX_REL].astype(F32)


def _toeplitz_diag(bias_fn, n_rows, n_cols, rel0, n):
    assert n_cols + n_rows - 1 <= n
    k = np.arange(n)
    rel = np.where(k < n_cols, k, k - n) + rel0
    return jnp.broadcast_to(bias_fn(jnp.asarray(rel)[None, :]), (16, 8, n))


def _band_diag(bias_fn, n_back):
    n_prev = n_back * CHUNK
    return _toeplitz_diag(lambda d: bias_fn(-d), n_prev + QB, QB, n_prev, 1024)


def _cache_diag(bias_fn, cache_len):
    n = max(256, cache_len * 2)
    return _toeplitz_diag(bias_fn, DEC_SEQ, cache_len, -cache_len, n)


def _rope_tables(pos):
    half = RET_DK // 2
    inv = ROPE_BASE ** (-jnp.arange(half, dtype=F32) / half)
    ang = pos.astype(F32)[:, None] * inv[None, :]
    cos, sin = jnp.cos(ang), jnp.sin(ang)
    cos_t = jnp.concatenate([cos, cos, cos, cos], -1)
    sin_t = jnp.concatenate([-sin, sin, -sin, sin], -1)
    return cos_t, sin_t


def _dup_groups(t):
    g0, g1 = t[..., 0, :], t[..., 1, :]
    return jnp.concatenate([g0, g0, g1, g1], -1)


def _undup(t):
    return jnp.stack([t[:, 0:64], t[:, 128:192]], 1)


def kernel(x_prompt, x_sample, mem_prompt, cache_swa_k, cache_swa_v, state_ret, cache_band_k, cache_band_v, state_ffn_conv, cache_mem_k, cache_mem_v, w_in, t5_table, swa_sink, band_rel_table, w_br_a, w_br_b, w_br_c, w_mix_o, ln1_g, ln1_b, w_mq, w_mk, w_mv, w_mo, ln2_g, ln2_b, w_ffn_in, ffn_conv_w, ffn_conv_b, w_ffn_out, ln3_g, ln3_b):
    x, xb = pack_rows(x_prompt, x_sample)
    memb = mem_prompt.reshape(BATCH * MEM_LEN, D_MODEL).astype(BF16)

    t5 = functools.partial(_t5_bias, t5_table)
    diag_a = _band_diag(t5, SWA_BACK)
    qpos = PAST_LEN + jnp.arange(DEC_SEQ)
    la, lc = cache_swa_k.shape[2], cache_band_k.shape[2]
    rel_n = qpos[None, :] - qpos[:, None]
    diag_a_c, bias_a_n = _cache_diag(t5, la), t5(rel_n)
    cos_p, sin_p = _rope_tables(jnp.arange(SEQ))
    cos_s, sin_s = _rope_tables(qpos)
    zero_state = jnp.zeros((1, BATCH, 4, LANES, LANES), F32)
    no_sink = jnp.zeros((16,), F32)

    w_in_b = permute_w_in(w_in)
    w_br_a_b, w_br_b_b, w_br_c_b = w_br_a.astype(BF16), w_br_b.astype(BF16), w_br_c.astype(BF16)
    w_mix_o_b, w_mo_b = w_mix_o.astype(BF16), w_mo.astype(BF16)
    swa_k2, swa_v2 = _dup_groups(cache_swa_k), _dup_groups(cache_swa_v)
    band_k2 = cache_band_k.reshape(DEPTH, DEC_BATCH, lc, BAND_HEADS * HEAD_DIM)
    band_v2 = cache_band_v.reshape(DEPTH, DEC_BATCH, lc, BAND_HEADS * HEAD_DIM)
    mem_k2 = cache_mem_k.reshape(DEPTH, DEC_BATCH, MEM_LEN, D_MODEL)
    mem_v2 = cache_mem_v.reshape(DEPTH, DEC_BATCH, MEM_LEN, D_MODEL)
    ret_s0 = state_ret.reshape(DEPTH, DEC_BATCH, 4, LANES, LANES)

    outs = {k: [] for k in ("p_ak", "p_av", "p_rs", "p_bk", "p_bv", "p_fc", "p_mk", "p_mv",
                            "s_ak", "s_av", "s_rs", "s_bk", "s_bv", "s_fc")}
    for l in range(DEPTH):
        proj = matmul(xb, w_in_b, l, tm=1056, tn=1536, out_dtype=BF16)
        clipped = functools.partial(_clipped_bias, band_rel_table[l])
        oa = band_attention_prompt(proj, diag_a, swa_sink[l], q_col=COL_QA, k_col=COL_KA2, v_col=COL_VA2,
                                   shared_kv=True, n_back=SWA_BACK, has_sink=True, skew=None)
        oc = band_attention_prompt(proj, _band_diag(clipped, BAND_BACK), no_sink, q_col=COL_QC, k_col=COL_KC,
                                   v_col=COL_VC, shared_kv=False, n_back=BAND_BACK, has_sink=False, skew=1)
        ob, rs_p = retention(proj, cos_p, sin_p, zero_state, 0, row0=0, n_seq=BATCH, seq_len=SEQ,
                             blk_len=RET_L, pos_per_blk=True)
        oa = cached_attention_sample(
            proj, swa_k2, swa_v2, l, diag_a_c, bias_a_n, swa_sink[l], oa,
            q_col=COL_QA, k_col=COL_KA2, v_col=COL_VA2, shared_kv=True, has_sink=True)
        oc = cached_attention_sample(
            proj, band_k2, band_v2, l, _cache_diag(clipped, lc), clipped(rel_n), no_sink, oc,
            q_col=COL_QC, k_col=COL_KC, v_col=COL_VC, shared_kv=False, has_sink=False)
        ob, rs_s = retention(proj, cos_s, sin_s, ret_s0, l, ob, row0=P_ROWS, n_seq=DEC_BATCH, seq_len=DEC_SEQ,
                             blk_len=DEC_SEQ, pos_per_blk=False)
        mix = gated_branch_sum(oa, ob, oc, w_br_a_b, w_br_b_b, w_br_c_b, proj, l)
        x, xb = matmul_residual_ln(mix, w_mix_o_b, l, x, ln1_g, ln1_b, tm=528)

        mk = matmul(memb, w_mk, l, tm=512, tn=1024, out_dtype=F32)
        mv = matmul(memb, w_mv, l, tm=512, tn=1024, out_dtype=F32)
        qm = matmul(xb, w_mq, l, tm=1056, tn=2048, out_dtype=BF16)
        om = mem_attention(qm, mk.reshape(1, BATCH, MEM_LEN, D_MODEL), mv.reshape(1, BATCH, MEM_LEN, D_MODEL), 0,
                           row0=0, n_rows=P_ROWS, tm=1024, heads=MEM_HEADS, rows_per_kv=SEQ)
        om = mem_attention(qm, mem_k2, mem_v2, l, om, row0=P_ROWS, n_rows=S_ROWS, tm=DEC_SEQ,
                           heads=MEM_HEADS, rows_per_kv=DEC_SEQ)
        x, xb = matmul_residual_ln(om, w_mo_b, l, x, ln2_g, ln2_b, tm=528)

        h, tg_p, tv_p, w_ffn_out_b = ffn_in(xb, w_ffn_in, ffn_conv_w, ffn_conv_b, l, row0=0, n_rows=P_ROWS,
                                            tm=1024, tn=512, tail=8, side_cast=w_ffn_out)
        st = state_ffn_conv[l]
        fix1 = jnp.pad(st[:, 1:2], ((0, 0), (0, DEC_SEQ - 1), (0, 0))).reshape(S_ROWS, 2 * D_FF)
        fix2 = jnp.pad(st, ((0, 0), (0, DEC_SEQ - 2), (0, 0))).reshape(S_ROWS, 2 * D_FF)
        h, tg_s, tv_s = ffn_in(xb, w_ffn_in, ffn_conv_w, ffn_conv_b, l, h, row0=P_ROWS, n_rows=S_ROWS,
                               tm=S_ROWS, tn=512, tail=S_ROWS, fix1=fix1, fix2=fix2)
        if l < DEPTH - 1:
            x, xb = matmul_residual_ln(h, w_ffn_out_b, l, x, ln3_g, ln3_b, tm=S_ROWS)
        else:
            y_prompt, y_sample = matmul_residual_ln(h, w_ffn_out_b, l, x, ln3_g, ln3_b, tm=S_ROWS,
                                                    split_rows=P_ROWS)

        sf = proj[P_ROWS:]
        la_p, lc_p = min(SWA_BACK * CHUNK, SEQ), min(BAND_BACK * CHUNK, SEQ)
        seq_tail = lambda n, c0, w: jnp.stack(
            [proj[(b + 1) * SEQ - n:(b + 1) * SEQ, c0:c0 + w] for b in range(BATCH)], 0)
        outs["p_ak"].append(_undup(seq_tail(la_p, COL_KA2, 256).reshape(BATCH * la_p, 256))
                            .reshape(BATCH, la_p, 2, 64).astype(F32))
        outs["p_av"].append(_undup(seq_tail(la_p, COL_VA2, 256).reshape(BATCH * la_p, 256))
                            .reshape(BATCH, la_p, 2, 64).astype(F32))
        outs["p_rs"].append(rs_p.reshape(BATCH, RET_HEADS, RET_DK, RET_DV))
        outs["p_bk"].append(seq_tail(lc_p, COL_KC, 1024).reshape(BATCH, lc_p, 16, 64).astype(F32))
        outs["p_bv"].append(seq_tail(lc_p, COL_VC, 1024).reshape(BATCH, lc_p, 16, 64).astype(F32))
        last = [(b + 1) * (SEQ // 1024) - 1 for b in range(BATCH)]
        outs["p_fc"].append(jnp.stack(
            [jnp.concatenate([tg_p[t, 6:8], tv_p[t, 6:8]], -1) for t in last], 0))
        outs["p_mk"].append(mk.reshape(BATCH, MEM_LEN, MEM_HEADS, MEM_HD))
        outs["p_mv"].append(mv.reshape(BATCH, MEM_LEN, MEM_HEADS, MEM_HD))
        outs["s_ak"].append(_undup(sf[:, COL_KA2:COL_KA2 + 256]).reshape(DEC_BATCH, DEC_SEQ, 2, 64).astype(F32))
        outs["s_av"].append(_undup(sf[:, COL_VA2:COL_VA2 + 256]).reshape(DEC_BATCH, DEC_SEQ, 2, 64).astype(F32))
        outs["s_rs"].append(rs_s.reshape(DEC_BATCH, RET_HEADS, RET_DK, RET_DV))
        outs["s_bk"].append(sf[:, COL_KC:COL_KC + 1024].reshape(DEC_BATCH, DEC_SEQ, 16, 64).astype(F32))
        outs["s_bv"].append(sf[:, COL_VC:COL_VC + 1024].reshape(DEC_BATCH, DEC_SEQ, 16, 64).astype(F32))
        u_s = jnp.concatenate([tg_s[0], tv_s[0]], -1).reshape(DEC_BATCH, DEC_SEQ, 2 * D_FF)
        outs["s_fc"].append(u_s[:, DEC_SEQ - 2:])

    st = lambda name: jnp.stack(outs[name], 0)
    return (y_prompt.reshape(BATCH, SEQ, D_MODEL), y_sample.reshape(DEC_BATCH, DEC_SEQ, D_MODEL),
            st("p_ak"), st("p_av"), st("p_rs"), st("p_bk"), st("p_bv"), st("p_fc"), st("p_mk"), st("p_mv"),
            st("s_ak"), st("s_av"), st("s_rs"), st("s_bk"), st("s_bv"), st("s_fc"))
```

```python
import functools
import math

import numpy as np
import jax
import jax.numpy as jnp
from jax import lax
from jax.experimental import pallas as pl
from jax.experimental.pallas import tpu as pltpu

F32 = jnp.float32
BF16 = jnp.bfloat16

D_MODEL = 2048
BATCH = 2
SEQ = 4096
DEPTH = 2
DEC_BATCH = 16
DEC_SEQ = 16
PAST_LEN = 2048
CHUNK = 64
HEAD_DIM = 64
SWA_BACK = 2
SWA_HEADS = 16
SWA_KV_HEADS = 2
T5_BUCKETS = 32
T5_MAX_DIST = 128
RET_HEADS = 8
RET_DK = 64
RET_DV = 128
ROPE_BASE = 10000.0
RET_NORM_EPS = 1e-5
BAND_BACK = 8
BAND_HEADS = 16
BAND_MAX_REL = 256
MEM_LEN = 256
MEM_HEADS = 4
MEM_HD = D_MODEL // MEM_HEADS
D_FF = 5632
CONV_W = 3
DN_ALPHA = (2 * DEPTH) ** 0.25
LN_EPS = 1e-5

P_ROWS = BATCH * SEQ
S_ROWS = DEC_BATCH * DEC_SEQ
ROWS = P_ROWS + S_ROWS

COL_QA = 0
COL_QC = 1024
COL_KC = 2048
COL_VC = 3072
COL_VB = 4096
COL_GR = 5120
COL_GA = 6144
COL_GB = 8192
COL_GC = 10240
COL_QB = 12288
COL_KB = 12800
COL_KA2 = 13312
COL_VA2 = 13568
PROJ_COLS = 13824

LANES = 128
MXU_COLS = 256
QB = 256
RET_L = 256
NEG = -1e30
LOG2E = math.log2(math.e)
VMEM_LIMIT = 48 * 1024 * 1024

NT = (((1,), (1,)), ((), ()))
TN = (((0,), (0,)), ((), ()))


def _params(sem, vmem=VMEM_LIMIT):
    return pltpu.CompilerParams(dimension_semantics=sem, vmem_limit_bytes=vmem)


def _pack_rows_kernel(p_ref, s_ref, of_ref, ob_ref, *, n_prompt_tiles):
    src = jnp.where(pl.program_id(0) < n_prompt_tiles, p_ref[...], s_ref[...])
    of_ref[...] = src
    ob_ref[...] = src.astype(BF16)


def pack_rows(x_prompt, x_sample):
    tm = S_ROWS
    npt = P_ROWS // tm
    return pl.pallas_call(
        functools.partial(_pack_rows_kernel, n_prompt_tiles=npt),
        grid=(npt + 1,),
        in_specs=[pl.BlockSpec((tm, D_MODEL), lambda i: (jnp.minimum(i, npt - 1), 0)),
                  pl.BlockSpec((tm, D_MODEL), lambda i: (0, 0))],
        out_specs=[pl.BlockSpec((tm, D_MODEL), lambda i: (i, 0)), pl.BlockSpec((tm, D_MODEL), lambda i: (i, 0))],
        out_shape=[jax.ShapeDtypeStruct((ROWS, D_MODEL), F32), jax.ShapeDtypeStruct((ROWS, D_MODEL), BF16)],
        compiler_params=_params(("parallel",)),
        name="pack_rows",
    )(x_prompt.reshape(P_ROWS, D_MODEL), x_sample.reshape(S_ROWS, D_MODEL))


_W_IN_MOVES = ((0, COL_QA, 1024), (4352, COL_QC, 1024), (5376, COL_KC, 1024), (6400, COL_VC, 1024),
               (2304, COL_VB, 1024), (3328, COL_GR, 1024), (7424, COL_GA, 2048), (9472, COL_GB, 2048),
               (11520, COL_GC, 2048), (1280, COL_QB, 512), (1792, COL_KB, 512))
_W_IN_DUPS = ((1024, COL_KA2), (1152, COL_VA2))


def _w_in_permute_kernel(src_ref, o_ref):
    for src, dst, width in _W_IN_MOVES:
        o_ref[:, dst:dst + width] = src_ref[:, src:src + width].astype(BF16)
    for src, dst in _W_IN_DUPS:
        pair = src_ref[:, src:src + LANES]
        swapped = pltpu.roll(pair, HEAD_DIM, 1)
        o_ref[:, dst:dst + LANES] = jnp.where(_head_lanes(0), pair, swapped).astype(BF16)
        o_ref[:, dst + LANES:dst + 2 * LANES] = jnp.where(_head_lanes(0), swapped, pair).astype(BF16)


def permute_w_in(w_in, *, rows=128):
    depth, d, in_cols = w_in.shape
    assert sum(w for _, _, w in _W_IN_MOVES) + 2 * LANES == in_cols
    return pl.pallas_call(
        _w_in_permute_kernel,
        grid=(depth, d // rows),
        in_specs=[pl.BlockSpec((None, rows, in_cols), lambda l, r: (l, r, 0))],
        out_specs=pl.BlockSpec((None, rows, PROJ_COLS), lambda l, r: (l, r, 0)),
        out_shape=jax.ShapeDtypeStruct((depth, d, PROJ_COLS), BF16),
        compiler_params=_params(("parallel", "parallel")),
        name="permute_w_in",
    )(w_in)


def _mm_kernel(a_ref, b_ref, o_ref, *scratch):
    if scratch:
        bb_ref, = scratch

        @pl.when(pl.program_id(0) == 0)
        def _():
            bb_ref[...] = b_ref[...].astype(BF16)

        b = bb_ref[...]
    else:
        b = b_ref[...].astype(BF16)
    o_ref[...] = jnp.dot(a_ref[...], b, preferred_element_type=F32).astype(o_ref.dtype)


def matmul(a, b, layer, *, tm, tn, out_dtype):
    m, k = a.shape
    n = b.shape[2]
    ni, nj = m // tm, n // tn
    resident = b.dtype == F32 and ni > 1
    assert b.dtype == BF16 or ni == 1 or nj == 1
    b_mode = dict(pipeline_mode=pl.Buffered(1)) if resident else {}
    return pl.pallas_call(
        _mm_kernel,
        grid=(ni, nj),
        in_specs=[pl.BlockSpec((tm, k), lambda i, j: (i, 0)),
                  pl.BlockSpec((None, k, tn), lambda i, j: (layer, 0, j), **b_mode)],
        out_specs=pl.BlockSpec((tm, tn), lambda i, j: (i, j)),
        out_shape=jax.ShapeDtypeStruct((m, n), out_dtype),
        scratch_shapes=[pltpu.VMEM((k, tn), BF16)] if resident else [],
        compiler_params=_params(("arbitrary", "arbitrary") if resident else ("parallel", "parallel")),
        name="matmul",
    )(a, b)


def _gate_mm_kernel(oa_ref, ob_ref, oc_ref, wa_ref, wb_ref, wc_ref, ga_ref, gb_ref, gc_ref, o_ref):
    acc = None
    for o, w, g in ((oa_ref, wa_ref, ga_ref), (ob_ref, wb_ref, gb_ref), (oc_ref, wc_ref, gc_ref)):
        t = jnp.dot(o[...], w[...], preferred_element_type=F32) * jax.nn.sigmoid(g[...].astype(F32))
        acc = t if acc is None else acc + t
    o_ref[...] = acc.astype(o_ref.dtype)


def gated_branch_sum(oa, ob, oc, wa, wb, wc, proj, layer, *, tm=528):
    m, k = oa.shape
    n = wa.shape[2]
    o_spec = pl.BlockSpec((tm, k), lambda i: (i, 0))
    w_spec = pl.BlockSpec((None, k, n), lambda i: (layer, 0, 0), pipeline_mode=pl.Buffered(1))

    def g_spec(col):
        return pl.BlockSpec((tm, n), lambda i: (i, col // n))

    return pl.pallas_call(
        _gate_mm_kernel,
        grid=(m // tm,),
        in_specs=[o_spec, o_spec, o_spec, w_spec, w_spec, w_spec,
                  g_spec(COL_GA), g_spec(COL_GB), g_spec(COL_GC)],
        out_specs=pl.BlockSpec((tm, n), lambda i: (i, 0)),
        out_shape=jax.ShapeDtypeStruct((m, n), BF16),
        compiler_params=_params(("parallel",)),
        name="gated_branch_sum",
    )(oa, ob, oc, wa, wb, wc, proj, proj, proj)


def _mm_ln_kernel(a_ref, b_ref, r_ref, g_ref, bt_ref, o1_ref, o2_ref, acc_ref, *, split_tiles):
    i = pl.program_id(0)

    @pl.when(i == 0)
    def _():
        acc_ref[...] = jnp.zeros_like(acc_ref)

    y = DN_ALPHA * r_ref[...] + acc_ref[...]
    mu = jnp.mean(y, -1, keepdims=True)
    d = y - mu
    var = jnp.mean(d * d, -1, keepdims=True)
    out = d * lax.rsqrt(var + LN_EPS) * g_ref[...] + bt_ref[...]
    if split_tiles is None:
        o1_ref[...] = out
        o2_ref[...] = out.astype(BF16)
    else:
        o2_ref[...] = out

        @pl.when(i <= split_tiles)
        def _():
            o1_ref[...] = out
    acc_ref[...] = jnp.dot(a_ref[...], b_ref[...], preferred_element_type=F32)


def matmul_residual_ln(a, b, layer, resid, gain, bias, *, tm, split_rows=None):
    m, kk = a.shape
    n = b.shape[-1]
    nt = m // tm
    cur = lambda i: (jnp.minimum(i, nt - 1), 0)
    prev = lambda i: (jnp.maximum(i - 1, 0), 0)
    if b.ndim == 3:
        b_spec = pl.BlockSpec((None, kk, n), lambda i: (layer, 0, 0), pipeline_mode=pl.Buffered(1))
    else:
        b_spec = pl.BlockSpec((kk, n), lambda i: (0, 0), pipeline_mode=pl.Buffered(1))
    if split_rows is None:
        split_tiles = None
        out_specs = [pl.BlockSpec((tm, n), prev), pl.BlockSpec((tm, n), prev)]
        out_shape = [jax.ShapeDtypeStruct((m, n), F32), jax.ShapeDtypeStruct((m, n), BF16)]
    else:
        split_tiles = split_rows // tm
        assert split_tiles == nt - 1
        out_specs = [pl.BlockSpec((tm, n), lambda i: (jnp.clip(i - 1, 0, split_tiles - 1), 0)),
                     pl.BlockSpec((tm, n), lambda i: (0, 0))]
        out_shape = [jax.ShapeDtypeStruct((split_rows, n), F32), jax.ShapeDtypeStruct((tm, n), F32)]
    return pl.pallas_call(
        functools.partial(_mm_ln_kernel, split_tiles=split_tiles),
        grid=(nt + 1,),
        in_specs=[pl.BlockSpec((tm, kk), cur),
                  b_spec,
                  pl.BlockSpec((tm, n), prev),
                  pl.BlockSpec((None, 1, n), lambda i: (layer, 0, 0)),
                  pl.BlockSpec((None, 1, n), lambda i: (layer, 0, 0))],
        out_specs=out_specs,
        out_shape=out_shape,
        scratch_shapes=[pltpu.VMEM((tm, n), F32)],
        compiler_params=_params(("arbitrary",)),
        name="matmul_residual_ln",
    )(a, b, resid, gain.reshape(DEPTH, 1, n), bias.reshape(DEPTH, 1, n))


def _head_lanes(j):
    lane = lax.broadcasted_iota(jnp.int32, (1, LANES), 1)
    return (lane < HEAD_DIM) if j == 0 else (lane >= HEAD_DIM)


def _band_attn_kernel(sink_ref, diag_ref, q_ref, *refs, n_prev_blocks, n_back, shared_kv, has_sink, skew):
    nk = n_prev_blocks + 1
    k_refs = refs[:nk]
    v_refs = refs[nk:2 * nk]
    o_ref = refs[2 * nk]
    bias_ref = refs[2 * nk + 1]
    b = pl.program_id(0)
    i = pl.program_id(1)
    n_prev_rows = n_back * CHUNK
    kw = n_prev_rows + QB

    @pl.when((b == 0) & (i == 0))
    def _():
        n = diag_ref.shape[2]
        cb = lax.broadcasted_iota(jnp.int32, (kw, QB), 0) // CHUNK
        qc = lax.broadcasted_iota(jnp.int32, (kw, QB), 1) // CHUNK
        allowed = (cb >= qc) & (cb - n_back <= qc)
        for h in range(16):
            t = pltpu.roll(jnp.broadcast_to(diag_ref[h, 0:1, :], (kw, n)), 0, 1, stride=1, stride_axis=0)
            bias_ref[h] = jnp.where(allowed, t[:, :QB] * LOG2E, NEG)

    krow = lax.broadcasted_iota(jnp.int32, (kw, 1), 0)
    kmask = jnp.where(krow >= jnp.maximum(n_prev_rows - i * QB, 0), 0.0, NEG).astype(BF16)
    lane = lax.broadcasted_iota(jnp.int32, (1, LANES), 1)

    def kv_cols(h):
        c0 = (h // 8 if shared_kv else h // 2) * LANES
        return slice(c0, c0 + LANES)

    def scores(h):
        p, j = divmod(h, 2)
        q = q_ref[:, p * LANES:(p + 1) * LANES] * (HEAD_DIM ** -0.5 * LOG2E)
        k_all = jnp.concatenate([r[:, kv_cols(h)] for r in k_refs], axis=0)
        mask_lane = lane == (HEAD_DIM if j == 0 else 0)
        qj = jnp.where(mask_lane, jnp.ones_like(q), jnp.where(_head_lanes(j), q, jnp.zeros_like(q)))
        kj = jnp.where(mask_lane, kmask, k_all)
        return lax.dot_general(kj, qj, NT, preferred_element_type=F32)

    def softmax(h, s_all):
        sink = sink_ref[h] * LOG2E if has_sink else None
        ps, dens = [], []
        for c in range(QB // LANES):
            lanes = slice(c * LANES, (c + 1) * LANES)
            lo, hi = c * LANES, c * LANES + n_prev_rows + LANES
            s = s_all[lo:hi, lanes] + bias_ref[h, lo:hi, lanes]
            m = jnp.max(s, 0, keepdims=True)
            if has_sink:
                m = jnp.maximum(m, sink)
            e = jnp.exp2(s - m)
            den = jnp.sum(e, 0, keepdims=True)
            if has_sink:
                den = den + jnp.exp2(sink - m)
            parts = [e.astype(BF16)]
            if lo:
                parts.insert(0, jnp.zeros((lo, LANES), BF16))
            if hi < kw:
                parts.append(jnp.zeros((kw - hi, LANES), BF16))
            ps.append(jnp.concatenate(parts, 0))
            dens.append(den)
        return jnp.concatenate(ps, 1), jnp.concatenate(dens, 1)

    def weighted_values(h, p_all, den):
        v_all = jnp.concatenate([r[:, kv_cols(h)] for r in v_refs], axis=0)
        return lax.dot_general(v_all, p_all, TN, preferred_element_type=F32) / den

    lag = 16 if skew is None else skew
    s_q, p_q, outs = {}, {}, {}
    for t in range(16 + 2 * lag):
        if t < 16:
            s_q[t] = scores(t)
        if lag <= t < 16 + lag:
            p_q[t - lag] = softmax(t - lag, s_q.pop(t - lag))
        if t >= 2 * lag:
            h = t - 2 * lag
            outs[h] = weighted_values(h, *p_q.pop(h))
            if h % 2:
                p = h // 2
                pair = jnp.where(lax.broadcasted_iota(jnp.int32, (LANES, 1), 0) < HEAD_DIM,
                                 outs.pop(h - 1), outs.pop(h))
                o_ref[:, p * LANES:(p + 1) * LANES] = pair.T.astype(o_ref.dtype)


def band_attention_prompt(proj, diag, sink, *, q_col, k_col, v_col, shared_kv, n_back, has_sink, skew):
    n_prev_rows = n_back * CHUNK
    nqb = SEQ // QB
    if n_prev_rows >= QB:
        n_prev_blocks, pb = n_prev_rows // QB, QB
    else:
        n_prev_blocks, pb = 1, n_prev_rows
    per = QB // pb
    kvw = 256 if shared_kv else 1024

    def prev_spec(col, back):
        return pl.BlockSpec(
            (pb, kvw), lambda b, i: (b * (SEQ // pb) + jnp.maximum(i * per - back, 0), col // kvw))

    def own_spec(col):
        return pl.BlockSpec((QB, kvw), lambda b, i: (b * nqb + i, col // kvw))

    k_specs = [prev_spec(k_col, n_prev_blocks - t) for t in range(n_prev_blocks)] + [own_spec(k_col)]
    v_specs = [prev_spec(v_col, n_prev_blocks - t) for t in range(n_prev_blocks)] + [own_spec(v_col)]
    kern = functools.partial(_band_attn_kernel, n_prev_blocks=n_prev_blocks, n_back=n_back,
                             shared_kv=shared_kv, has_sink=has_sink, skew=skew)
    n_in = 2 * (n_prev_blocks + 1)
    return pl.pallas_call(
        kern,
        grid=(BATCH, nqb),
        in_specs=[pl.BlockSpec(memory_space=pltpu.SMEM),
                  pl.BlockSpec(diag.shape, lambda b, i: (0, 0, 0)),
                  pl.BlockSpec((QB, 1024), lambda b, i: (b * nqb + i, q_col // 1024))]
                 + k_specs + v_specs,
        out_specs=pl.BlockSpec((QB, 1024), lambda b, i: (b * nqb + i, 0)),
        out_shape=jax.ShapeDtypeStruct((ROWS, 16 * HEAD_DIM), BF16),
        scratch_shapes=[pltpu.VMEM((16, n_prev_rows + QB, QB), F32)],
        compiler_params=_params(("arbitrary", "arbitrary")),
        name="band_attention_prompt",
    )(sink, diag, proj, *([proj] * n_in))


def _cached_attn_kernel(sink_ref, q_ref, kn_ref, vn_ref, kc_ref, vc_ref, diag_ref, bn_ref, _, o_ref, bc_ref,
                        *, shared_kv, has_sink):
    lc = kc_ref.shape[1]

    @pl.when(pl.program_id(0) == 0)
    def _():
        n = diag_ref.shape[2]
        for h in range(16):
            t = pltpu.roll(jnp.broadcast_to(diag_ref[h, 0:1, :], (DEC_SEQ, n)), 0, 1, stride=1, stride_axis=0)
            bc_ref[h] = t[:, :lc]

    kv_cols = [(p // 4 if shared_kv else p) * LANES for p in range(8)]
    scores = []
    for p in range(8):
        c0 = kv_cols[p]
        q = q_ref[:, p * LANES:(p + 1) * LANES] * (HEAD_DIM ** -0.5)
        kc = kc_ref[0, :, c0:c0 + LANES].astype(BF16)
        kn = kn_ref[:, c0:c0 + LANES]
        for j in range(2):
            h = 2 * p + j
            qj = jnp.where(_head_lanes(j), q, jnp.zeros_like(q))
            scores.append((lax.dot_general(qj, kc, NT, preferred_element_type=F32) + bc_ref[h],
                           lax.dot_general(qj, kn, NT, preferred_element_type=F32) + bn_ref[h]))
    probs = []
    for h, (s_c, s_n) in enumerate(scores):
        m = jnp.maximum(jnp.max(s_c, -1, keepdims=True), jnp.max(s_n, -1, keepdims=True))
        if has_sink:
            m = jnp.maximum(m, sink_ref[h])
        e_c, e_n = jnp.exp(s_c - m), jnp.exp(s_n - m)
        den = jnp.sum(e_c, -1, keepdims=True) + jnp.sum(e_n, -1, keepdims=True)
        if has_sink:
            den = den + jnp.exp(sink_ref[h] - m)
        probs.append((e_c.astype(BF16), e_n.astype(BF16), den))
    for p in range(8):
        c0 = kv_cols[p]
        vc = vc_ref[0, :, c0:c0 + LANES].astype(BF16)
        vn = vn_ref[:, c0:c0 + LANES]
        outs = []
        for j in range(2):
            e_c, e_n, den = probs[2 * p + j]
            outs.append((jnp.dot(e_c, vc, preferred_element_type=F32)
                         + jnp.dot(e_n, vn, preferred_element_type=F32)) / den)
        o_ref[:, p * LANES:(p + 1) * LANES] = jnp.where(_head_lanes(0), outs[0], outs[1]).astype(o_ref.dtype)


def cached_attention_sample(proj, k_cache, v_cache, layer, diag_c, bias_n, sink, o_all, *, q_col, k_col, v_col,
                            shared_kv, has_sink):
    lc, wc = k_cache.shape[2], k_cache.shape[3]
    r0 = P_ROWS // DEC_SEQ
    kern = functools.partial(_cached_attn_kernel, shared_kv=shared_kv, has_sink=has_sink)
    cache = pl.BlockSpec((None, 1, lc, wc), lambda s: (layer, s, 0, 0))
    return pl.pallas_call(
        kern,
        grid=(DEC_BATCH,),
        in_specs=[pl.BlockSpec(memory_space=pltpu.SMEM),
                  pl.BlockSpec((DEC_SEQ, 1024), lambda s: (r0 + s, q_col // 1024)),
                  pl.BlockSpec((DEC_SEQ, wc), lambda s: (r0 + s, k_col // wc)),
                  pl.BlockSpec((DEC_SEQ, wc), lambda s: (r0 + s, v_col // wc)),
                  cache, cache,
                  pl.BlockSpec(diag_c.shape, lambda s: (0, 0, 0)),
                  pl.BlockSpec((16, DEC_SEQ, DEC_SEQ), lambda s: (0, 0, 0)),
                  pl.BlockSpec(memory_space=pl.ANY)],
        out_specs=pl.BlockSpec((DEC_SEQ, 1024), lambda s: (r0 + s, 0)),
        out_shape=jax.ShapeDtypeStruct(o_all.shape, o_all.dtype),
        scratch_shapes=[pltpu.VMEM((16, DEC_SEQ, lc), F32)],
        input_output_aliases={8: 0},
        compiler_params=_params(("arbitrary",)),
        name="cached_attention_sample",
    )(sink, proj, proj, proj, k_cache, v_cache, diag_c, bias_n, o_all)


def _ret_kernel(q_ref, k_ref, v_ref, g_ref, cos_ref, sin_ref, s0_ref, *refs, blk_len):
    o_ref, sout_ref, st_ref, dec_ref, rdec_ref = refs[-5:]
    blk = pl.program_id(1)
    log_gs = [math.log(1.0 - 2.0 ** (-5.0 - h)) for h in range(RET_HEADS)]

    @pl.when((pl.program_id(0) == 0) & (blk == 0))
    def _():
        ii = lax.broadcasted_iota(jnp.int32, (blk_len, blk_len), 0)
        jj = lax.broadcasted_iota(jnp.int32, (blk_len, blk_len), 1)
        diff = (ii - jj).astype(F32)
        row = lax.broadcasted_iota(jnp.int32, (blk_len, LANES), 0).astype(F32)
        for h in range(RET_HEADS):
            dec_ref[h] = jnp.where(diff >= 0, jnp.exp(log_gs[h] * jnp.maximum(diff, 0.0)), 0.0)
            rdec_ref[h] = jnp.exp(log_gs[h] * (row + 1.0))
            rdec_ref[RET_HEADS + h] = jnp.exp(log_gs[h] * (blk_len - 1.0 - row))

    @pl.when(blk == 0)
    def _():
        st_ref[...] = s0_ref[0]

    cos = cos_ref[...]
    sin = sin_ref[...]
    lane = lax.broadcasted_iota(jnp.int32, (1, LANES), 1)
    low_half = (lane % RET_DK) < (RET_DK // 2)

    def rope(x):
        x = x.astype(F32)
        swapped = jnp.where(low_half, pltpu.roll(x, LANES - RET_DK // 2, 1), pltpu.roll(x, RET_DK // 2, 1))
        return x * cos + swapped * sin

    srow = lax.broadcasted_iota(jnp.int32, (LANES, 1), 0)

    for p in range(RET_HEADS // 2):
        qr = rope(q_ref[:, p * LANES:(p + 1) * LANES])
        kr = rope(k_ref[:, p * LANES:(p + 1) * LANES]) * (RET_DK ** -0.5)
        kb = kr.astype(BF16)
        state = st_ref[p]
        state_b = state.astype(BF16)
        upd = None
        for j in range(2):
            h = 2 * p + j
            qj = jnp.where(_head_lanes(j), qr, 0.0).astype(BF16)
            vh = v_ref[:, h * RET_DV:(h + 1) * RET_DV]
            qk = lax.dot_general(qj, kb, NT, preferred_element_type=F32) * dec_ref[h]
            o = jnp.dot(qk.astype(BF16), vh, preferred_element_type=F32)
            o = o + jnp.dot(qj, state_b, preferred_element_type=F32) * rdec_ref[h]
            mu = jnp.mean(o, -1, keepdims=True)
            d = o - mu
            var = jnp.mean(d * d, -1, keepdims=True)
            gate = g_ref[:, h * RET_DV:(h + 1) * RET_DV].astype(F32)
            o_ref[:, h * RET_DV:(h + 1) * RET_DV] = (
                d * lax.rsqrt(var + RET_NORM_EPS) * (gate * jax.nn.sigmoid(gate))).astype(o_ref.dtype)
            kwj = jnp.where(_head_lanes(j), kr * rdec_ref[RET_HEADS + h], 0.0).astype(BF16)
            u = lax.dot_general(kwj, vh, TN, preferred_element_type=F32)
            upd = u if upd is None else upd + u
        carry = jnp.where(srow < RET_DK, math.exp(log_gs[2 * p] * blk_len), math.exp(log_gs[2 * p + 1] * blk_len))
        st_ref[p] = carry * state + upd

    @pl.when(blk == pl.num_programs(1) - 1)
    def _():
        sout_ref[0] = st_ref[...]


def retention(proj, cos, sin, s0, layer, o_all=None, *, row0, n_seq, seq_len, blk_len, pos_per_blk):
    nb = seq_len // blk_len
    rb0 = row0 // blk_len

    def rows(col, width):
        return pl.BlockSpec((blk_len, width), lambda b, t: (rb0 + b * nb + t, col // width))

    tab = pl.BlockSpec((blk_len, LANES), lambda b, t: (t if pos_per_blk else 0, 0))
    st = pl.BlockSpec((1, 4, LANES, LANES), lambda b, t: (b, 0, 0, 0))
    st_in = pl.BlockSpec((None, 1, 4, LANES, LANES), lambda b, t: (layer, b, 0, 0, 0))
    in_specs = [rows(COL_QB, 512), rows(COL_KB, 512), rows(COL_VB, 1024), rows(COL_GR, 1024), tab, tab, st_in]
    args = [proj, proj, proj, proj, cos, sin, s0]
    aliases = {}
    if o_all is not None:
        in_specs.append(pl.BlockSpec(memory_space=pl.ANY))
        args.append(o_all)
        aliases = {len(args) - 1: 0}
    return pl.pallas_call(
        functools.partial(_ret_kernel, blk_len=blk_len),
        grid=(n_seq, nb),
        in_specs=in_specs,
        out_specs=[rows(0, 1024), st],
        out_shape=[jax.ShapeDtypeStruct((ROWS, 1024), BF16),
                   jax.ShapeDtypeStruct((n_seq, 4, LANES, LANES), F32)],
        scratch_shapes=[pltpu.VMEM((4, LANES, LANES), F32), pltpu.VMEM((RET_HEADS, blk_len, blk_len), F32),
                        pltpu.VMEM((2 * RET_HEADS, blk_len, LANES), F32)],
        input_output_aliases=aliases,
        compiler_params=_params(("arbitrary", "arbitrary")),
        name="retention",
    )(*args)


def _mem_attn_kernel(q_ref, k_ref, v_ref, *refs, heads):
    o_ref = refs[-1]
    cols = [slice(h * MEM_HD, (h + 1) * MEM_HD) for h in range(heads)]
    scores = [lax.dot_general(q_ref[:, c], k_ref[0, :, c].astype(BF16), NT, preferred_element_type=F32)
              * (MEM_HD ** -0.5) for c in cols]
    probs = []
    for s in scores:
        e = jnp.exp(s - jnp.max(s, -1, keepdims=True))
        probs.append((e.astype(BF16), jnp.sum(e, -1, keepdims=True)))
    for c, (e, den) in zip(cols, probs):
        o_ref[:, c] = (jnp.dot(e, v_ref[0, :, c].astype(BF16), preferred_element_type=F32) / den
                       ).astype(o_ref.dtype)


def mem_attention(qm, k_src, v_src, layer, o_all=None, *, row0, n_rows, tm, heads, rows_per_kv):
    rb0 = row0 // tm
    w = heads * MEM_HD
    kv = pl.BlockSpec((None, 1, MEM_LEN, w), lambda i, h: (layer, i * tm // rows_per_kv, 0, h))
    in_specs = [pl.BlockSpec((tm, w), lambda i, h: (rb0 + i, h)), kv, kv]
    args = [qm, k_src, v_src]
    aliases = {}
    if o_all is not None:
        in_specs.append(pl.BlockSpec(memory_space=pl.ANY))
        args.append(o_all)
        aliases = {3: 0}
    return pl.pallas_call(
        functools.partial(_mem_attn_kernel, heads=heads),
        grid=(n_rows // tm, MEM_HEADS // heads),
        in_specs=in_specs,
        out_specs=pl.BlockSpec((tm, w), lambda i, h: (rb0 + i, h)),
        out_shape=jax.ShapeDtypeStruct((ROWS, D_MODEL), BF16),
        input_output_aliases=aliases,
        compiler_params=_params(("parallel", "parallel")),
        name="mem_attention",
    )(*args)


def _gelu(x):
    return 0.5 * x * (1.0 + lax.erf(x * (2.0 ** -0.5)))


def _ffn_in_kernel(*refs, sample, seq_tiles, tail):
    i = pl.program_id(1)
    if sample:
        a_ref, bg_ref, bv_ref, wg_ref, wv_ref, cbg_ref, cbv_ref, f1g_ref, f1v_ref, f2g_ref, f2v_ref = refs[:11]
        h_ref, tg_ref, tv_ref = refs[-3:]
    else:
        a_ref, ah_ref, wfg_ref, wfv_ref, wg_ref, wv_ref, cbg_ref, cbv_ref, side_ref = refs[:9]
        h_ref, tg_ref, tv_ref, side_out_ref, bg_ref, bv_ref = refs[-6:]
        side_out_ref[...] = side_ref[...].astype(BF16)

        @pl.when(i == 0)
        def _():
            bg_ref[...] = wfg_ref[...].astype(BF16)
            bv_ref[...] = wfv_ref[...].astype(BF16)

    tm = a_ref.shape[0]
    a = a_ref[...]
    if sample:
        pos = lax.broadcasted_iota(jnp.int32, (tm, 1), 0) % DEC_SEQ
    else:
        top = lax.broadcasted_iota(jnp.int32, (8, 1), 0)
        keep = (i % seq_tiles != 0).astype(F32)

    def project(cols, b_ref):
        u = jnp.dot(a, b_ref[:, cols], preferred_element_type=F32)
        uh = None if sample else jnp.dot(ah_ref[...], b_ref[:, cols], preferred_element_type=F32)
        return u, uh

    def conv(cols, u, uh, w_ref, cb_ref, f1_ref, f2_ref, t_ref):
        t_ref[0, :, cols] = u[tm - tail:, :]
        r1 = pltpu.roll(u, 1, 0)
        r2 = pltpu.roll(u, 2, 0)
        if sample:
            u1 = jnp.where(pos < 1, f1_ref[:, cols], r1)
            u2 = jnp.where(pos < 2, f2_ref[:, cols], r2)
        else:
            n = uh.shape[0]
            prev1, prev2 = uh[n - 1:n, :] * keep, uh[n - 2:n - 1, :] * keep
            u1 = jnp.concatenate([jnp.where(top < 1, prev1, r1[:8]), r1[8:]], 0)
            top2 = jnp.where(top < 1, prev2, jnp.where(top < 2, prev1, r2[:8]))
            u2 = jnp.concatenate([top2, r2[8:]], 0)
        w = w_ref[:, cols]
        return w[0:1, :] * u2 + w[1:2, :] * u1 + w[2:3, :] * u + cb_ref[:, cols]

    sub = [slice(c0, c0 + MXU_COLS) for c0 in range(0, h_ref.shape[1], MXU_COLS)]
    prods = [(project(cols, bg_ref), project(cols, bv_ref)) for cols in sub]
    for cols, ((ug, uhg), (uv, uhv)) in zip(sub, prods):
        if sample:
            cg = conv(cols, ug, uhg, wg_ref, cbg_ref, f1g_ref, f2g_ref, tg_ref)
            cv = conv(cols, uv, uhv, wv_ref, cbv_ref, f1v_ref, f2v_ref, tv_ref)
        else:
            cg = conv(cols, ug, uhg, wg_ref, cbg_ref, None, None, tg_ref)
            cv = conv(cols, uv, uhv, wv_ref, cbv_ref, None, None, tv_ref)
        h_ref[:, cols] = (_gelu(cg) * cv).astype(h_ref.dtype)


def ffn_in(x, w, conv_w, conv_b, layer, *, row0, n_rows, tm, tn, tail, side_cast=None, sample_state=None):
    sample = sample_state is not None
    k = x.shape[1]
    nj = D_FF // tn
    ni = n_rows // tm
    rb0 = row0 // tm
    halo = 16
    a_spec = pl.BlockSpec((tm, k), lambda j, i: (rb0 + i, 0))
    wg = pl.BlockSpec((None, CONV_W, tn), lambda j, i: (layer, 0, j))
    wv = pl.BlockSpec((None, CONV_W, tn), lambda j, i: (layer, 0, nj + j))
    cg = pl.BlockSpec((None, 1, tn), lambda j, i: (layer, 0, j))
    cv = pl.BlockSpec((None, 1, tn), lambda j, i: (layer, 0, nj + j))
    conv_b3 = conv_b.reshape(DEPTH, 1, 2 * D_FF)
    w_half = pl.BlockSpec((k, tn), lambda j, i: (0, j))
    t_spec = pl.BlockSpec((1, tail, tn), lambda j, i: (i, 0, j))
    out_specs = [pl.BlockSpec((tm, tn), lambda j, i: (rb0 + i, j)), t_spec, t_spec]
    out_shape = [jax.ShapeDtypeStruct((ROWS, D_FF), BF16),
                 jax.ShapeDtypeStruct((ni, tail, D_FF), F32),
                 jax.ShapeDtypeStruct((ni, tail, D_FF), F32)]
    aliases = {}
    if sample:
        fix1, fix2, h_all = sample_state
        fg = pl.BlockSpec((tm, tn), lambda j, i: (i, j))
        fv = pl.BlockSpec((tm, tn), lambda j, i: (i, nj + j))
        in_specs = [a_spec, w_half, w_half, wg, wv, cg, cv, fg, fv, fg, fv, pl.BlockSpec(memory_space=pl.ANY)]
        args = [x, w[0], w[1], conv_w, conv_w, conv_b3, conv_b3, fix1, fix1, fix2, fix2, h_all]
        aliases = {len(args) - 1: 0}
    else:
        rows, cols = side_cast.shape[1:]
        slab = rows // (nj * ni)
        assert slab * nj * ni == rows and slab % 16 == 0
        ah = pl.BlockSpec((halo, k), lambda j, i: (jnp.maximum((rb0 + i) * (tm // halo) - 1, 0), 0))
        in_specs = [a_spec, ah,
                    pl.BlockSpec((None, k, tn), lambda j, i: (layer, 0, j)),
                    pl.BlockSpec((None, k, tn), lambda j, i: (layer, 0, nj + j)),
                    wg, wv, cg, cv,
                    pl.BlockSpec((None, slab, cols), lambda j, i: (layer, j * ni + i, 0))]
        args = [x, x, w, w, conv_w, conv_w, conv_b3, conv_b3, side_cast]
        out_specs += [pl.BlockSpec((slab, cols), lambda j, i: (j * ni + i, 0)), w_half, w_half]
        out_shape += [jax.ShapeDtypeStruct((rows, cols), BF16),
                      jax.ShapeDtypeStruct((k, D_FF), BF16), jax.ShapeDtypeStruct((k, D_FF), BF16)]
    return pl.pallas_call(
        functools.partial(_ffn_in_kernel, sample=sample, seq_tiles=SEQ // tm if not sample else 1, tail=tail),
        grid=(nj, ni),
        in_specs=in_specs,
        out_specs=out_specs,
        out_shape=out_shape,
        input_output_aliases=aliases,
        compiler_params=_params(("parallel", "arbitrary")),
        name="ffn_in",
    )(*args)


def _t5_bucket(rel):
    nb = T5_BUCKETS // 2
    max_exact = nb // 2
    n = jnp.abs(rel)
    nf = jnp.maximum(n, 1).astype(F32)
    large = max_exact + (jnp.log(nf / max_exact) / math.log(T5_MAX_DIST / max_exact)
                         * (nb - max_exact)).astype(jnp.int32)
    large = jnp.minimum(large, nb - 1)
    return jnp.where(rel > 0, nb, 0) + jnp.where(n < max_exact, n, large)


def _t5_bias(table, rel):
    return jnp.transpose(table[_t5_bucket(rel)], (2, 0, 1)).astype(F32)


def _clipped_bias(table, rel):
    return table[:, jnp.clip(rel, -BAND_MAX_REL, BAND_MAX_REL) + BAND_MAX_REL].astype(F32)


def _toeplitz_diag(bias_fn, n_rows, n_cols, rel0, n):
    assert n_cols + n_rows - 1 <= n
    k = np.arange(n)
    rel = np.where(k < n_cols, k, k - n) + rel0
    return jnp.broadcast_to(bias_fn(jnp.asarray(rel)[None, :]), (16, 8, n))


def _band_diag(bias_fn, n_back):
    n_prev = n_back * CHUNK
    return _toeplitz_diag(lambda d: bias_fn(-d), n_prev + QB, QB, n_prev, 1024)


def _cache_diag(bias_fn, cache_len):
    n = max(256, cache_len * 2)
    return _toeplitz_diag(bias_fn, DEC_SEQ, cache_len, -cache_len, n)


def _rope_tables(pos):
    half = RET_DK // 2
    inv = ROPE_BASE ** (-jnp.arange(half, dtype=F32) / half)
    ang = pos.astype(F32)[:, None] * inv[None, :]
    cos, sin = jnp.cos(ang), jnp.sin(ang)
    cos_t = jnp.concatenate([cos, cos, cos, cos], -1)
    sin_t = jnp.concatenate([-sin, sin, -sin, sin], -1)
    return cos_t, sin_t


def _dup_groups(t):
    g0, g1 = t[..., 0, :], t[..., 1, :]
    return jnp.concatenate([g0, g0, g1, g1], -1)


def _undup(t):
    return jnp.stack([t[:, 0:64], t[:, 128:192]], 1)


def kernel(x_prompt, x_sample, mem_prompt, cache_swa_k, cache_swa_v, state_ret, cache_band_k, cache_band_v, state_ffn_conv, cache_mem_k, cache_mem_v, w_in, t5_table, swa_sink, band_rel_table, w_br_a, w_br_b, w_br_c, w_mix_o, ln1_g, ln1_b, w_mq, w_mk, w_mv, w_mo, ln2_g, ln2_b, w_ffn_in, ffn_conv_w, ffn_conv_b, w_ffn_out, ln3_g, ln3_b):
    x, xb = pack_rows(x_prompt, x_sample)
    memb = mem_prompt.reshape(BATCH * MEM_LEN, D_MODEL).astype(BF16)

    t5 = functools.partial(_t5_bias, t5_table)
    diag_a = _band_diag(t5, SWA_BACK)
    qpos = PAST_LEN + jnp.arange(DEC_SEQ)
    la, lc = cache_swa_k.shape[2], cache_band_k.shape[2]
    rel_n = qpos[None, :] - qpos[:, None]
    diag_a_c, bias_a_n = _cache_diag(t5, la), t5(rel_n)
    cos_p, sin_p = _rope_tables(jnp.arange(SEQ))
    cos_s, sin_s = _rope_tables(qpos)
    zero_state = jnp.zeros((1, BATCH, 4, LANES, LANES), F32)
    no_sink = jnp.zeros((16,), F32)

    w_in_b = permute_w_in(w_in)
    w_br_a_b, w_br_b_b, w_br_c_b = w_br_a.astype(BF16), w_br_b.astype(BF16), w_br_c.astype(BF16)
    w_mix_o_b, w_mo_b = w_mix_o.astype(BF16), w_mo.astype(BF16)
    swa_k2, swa_v2 = _dup_groups(cache_swa_k), _dup_groups(cache_swa_v)
    band_k2 = cache_band_k.reshape(DEPTH, DEC_BATCH, lc, BAND_HEADS * HEAD_DIM)
    band_v2 = cache_band_v.reshape(DEPTH, DEC_BATCH, lc, BAND_HEADS * HEAD_DIM)
    mem_k2 = cache_mem_k.reshape(DEPTH, DEC_BATCH, MEM_LEN, D_MODEL)
    mem_v2 = cache_mem_v.reshape(DEPTH, DEC_BATCH, MEM_LEN, D_MODEL)
    ret_s0 = state_ret.reshape(DEPTH, DEC_BATCH, 4, LANES, LANES)

    outs = {k: [] for k in ("p_ak", "p_av", "p_rs", "p_bk", "p_bv", "p_fc", "p_mk", "p_mv",
                            "s_ak", "s_av", "s_rs", "s_bk", "s_bv", "s_fc")}
    for l in range(DEPTH):
        proj = matmul(xb, w_in_b, l, tm=1056, tn=1536, out_dtype=BF16)
        clipped = functools.partial(_clipped_bias, band_rel_table[l])
        oa = band_attention_prompt(proj, diag_a, swa_sink[l], q_col=COL_QA, k_col=COL_KA2, v_col=COL_VA2,
                                   shared_kv=True, n_back=SWA_BACK, has_sink=True, skew=None)
        oc = band_attention_prompt(proj, _band_diag(clipped, BAND_BACK), no_sink, q_col=COL_QC, k_col=COL_KC,
                                   v_col=COL_VC, shared_kv=False, n_back=BAND_BACK, has_sink=False, skew=1)
        ob, rs_p = retention(proj, cos_p, sin_p, zero_state, 0, row0=0, n_seq=BATCH, seq_len=SEQ,
                             blk_len=RET_L, pos_per_blk=True)
        oa = cached_attention_sample(
            proj, swa_k2, swa_v2, l, diag_a_c, bias_a_n, swa_sink[l], oa,
            q_col=COL_QA, k_col=COL_KA2, v_col=COL_VA2, shared_kv=True, has_sink=True)
        oc = cached_attention_sample(
            proj, band_k2, band_v2, l, _cache_diag(clipped, lc), clipped(rel_n), no_sink, oc,
            q_col=COL_QC, k_col=COL_KC, v_col=COL_VC, shared_kv=False, has_sink=False)
        ob, rs_s = retention(proj, cos_s, sin_s, ret_s0, l, ob, row0=P_ROWS, n_seq=DEC_BATCH, seq_len=DEC_SEQ,
                             blk_len=DEC_SEQ, pos_per_blk=False)
        mix = gated_branch_sum(oa, ob, oc, w_br_a_b, w_br_b_b, w_br_c_b, proj, l)
        x, xb = matmul_residual_ln(mix, w_mix_o_b, l, x, ln1_g, ln1_b, tm=528)

        mk = matmul(memb, w_mk, l, tm=512, tn=1024, out_dtype=F32)
        mv = matmul(memb, w_mv, l, tm=512, tn=1024, out_dtype=F32)
        qm = matmul(xb, w_mq, l, tm=1056, tn=2048, out_dtype=BF16)
        om = mem_attention(qm, mk.reshape(1, BATCH, MEM_LEN, D_MODEL), mv.reshape(1, BATCH, MEM_LEN, D_MODEL), 0,
                           row0=0, n_rows=P_ROWS, tm=1024, heads=MEM_HEADS, rows_per_kv=SEQ)
        om = mem_attention(qm, mem_k2, mem_v2, l, om, row0=P_ROWS, n_rows=S_ROWS, tm=DEC_SEQ,
                           heads=MEM_HEADS, rows_per_kv=DEC_SEQ)
        x, xb = matmul_residual_ln(om, w_mo_b, l, x, ln2_g, ln2_b, tm=528)

        h, tg_p, tv_p, w_ffn_out_b, wfg_b, wfv_b = ffn_in(
            xb, w_ffn_in, ffn_conv_w, ffn_conv_b, l, row0=0, n_rows=P_ROWS, tm=1024, tn=512, tail=8,
            side_cast=w_ffn_out)
        st = state_ffn_conv[l]
        fix1 = jnp.pad(st[:, 1:2], ((0, 0), (0, DEC_SEQ - 1), (0, 0))).reshape(S_ROWS, 2 * D_FF)
        fix2 = jnp.pad(st, ((0, 0), (0, DEC_SEQ - 2), (0, 0))).reshape(S_ROWS, 2 * D_FF)
        h, tg_s, tv_s = ffn_in(xb, (wfg_b, wfv_b), ffn_conv_w, ffn_conv_b, l, row0=P_ROWS, n_rows=S_ROWS,
                               tm=S_ROWS, tn=512, tail=S_ROWS, sample_state=(fix1, fix2, h))
        if l < DEPTH - 1:
            x, xb = matmul_residual_ln(h, w_ffn_out_b, l, x, ln3_g, ln3_b, tm=S_ROWS)
        else:
            y_prompt, y_sample = matmul_residual_ln(h, w_ffn_out_b, l, x, ln3_g, ln3_b, tm=S_ROWS,
                                                    split_rows=P_ROWS)

        sf = proj[P_ROWS:]
        la_p, lc_p = min(SWA_BACK * CHUNK, SEQ), min(BAND_BACK * CHUNK, SEQ)
        seq_tail = lambda n, c0, w: jnp.stack(
            [proj[(b + 1) * SEQ - n:(b + 1) * SEQ, c0:c0 + w] for b in range(BATCH)], 0)
        outs["p_ak"].append(_undup(seq_tail(la_p, COL_KA2, 256).reshape(BATCH * la_p, 256))
                            .reshape(BATCH, la_p, 2, 64).astype(F32))
        outs["p_av"].append(_undup(seq_tail(la_p, COL_VA2, 256).reshape(BATCH * la_p, 256))
                            .reshape(BATCH, la_p, 2, 64).astype(F32))
        outs["p_rs"].append(rs_p.reshape(BATCH, RET_HEADS, RET_DK, RET_DV))
        outs["p_bk"].append(seq_tail(lc_p, COL_KC, 1024).reshape(BATCH, lc_p, 16, 64).astype(F32))
        outs["p_bv"].append(seq_tail(lc_p, COL_VC, 1024).reshape(BATCH, lc_p, 16, 64).astype(F32))
        last = [(b + 1) * (SEQ // 1024) - 1 for b in range(BATCH)]
        outs["p_fc"].append(jnp.stack(
            [jnp.concatenate([tg_p[t, 6:8], tv_p[t, 6:8]], -1) for t in last], 0))
        outs["p_mk"].append(mk.reshape(BATCH, MEM_LEN, MEM_HEADS, MEM_HD))
        outs["p_mv"].append(mv.reshape(BATCH, MEM_LEN, MEM_HEADS, MEM_HD))
        outs["s_ak"].append(_undup(sf[:, COL_KA2:COL_KA2 + 256]).reshape(DEC_BATCH, DEC_SEQ, 2, 64).astype(F32))
        outs["s_av"].append(_undup(sf[:, COL_VA2:COL_VA2 + 256]).reshape(DEC_BATCH, DEC_SEQ, 2, 64).astype(F32))
        outs["s_rs"].append(rs_s.reshape(DEC_BATCH, RET_HEADS, RET_DK, RET_DV))
        outs["s_bk"].append(sf[:, COL_KC:COL_KC + 1024].reshape(DEC_BATCH, DEC_SEQ, 16, 64).astype(F32))
        outs["s_bv"].append(sf[:, COL_VC:COL_VC + 1024].reshape(DEC_BATCH, DEC_SEQ, 16, 64).astype(F32))
        u_s = jnp.concatenate([tg_s[0], tv_s[0]], -1).reshape(DEC_BATCH, DEC_SEQ, 2 * D_FF)
        outs["s_fc"].append(u_s[:, DEC_SEQ - 2:])

    st = lambda name: jnp.stack(outs[name], 0)
    return (y_prompt.reshape(BATCH, SEQ, D_MODEL), y_sample.reshape(DEC_BATCH, DEC_SEQ, D_MODEL),
            st("p_ak"), st("p_av"), st("p_rs"), st("p_bk"), st("p_bv"), st("p_fc"), st("p_mk"), st("p_mv"),
            st("s_ak"), st("s_av"), st("s_rs"), st("s_bk"), st("s_bv"), st("s_fc"))
```

```python
import functools
import math

import numpy as np
import jax
import jax.numpy as jnp
from jax import lax
from jax.experimental import pallas as pl
from jax.experimental.pallas import tpu as pltpu

F32 = jnp.float32
BF16 = jnp.bfloat16

D_MODEL = 2048
BATCH = 2
SEQ = 4096
DEPTH = 2
DEC_BATCH = 16
DEC_SEQ = 16
PAST_LEN = 2048
CHUNK = 64
HEAD_DIM = 64
SWA_BACK = 2
SWA_HEADS = 16
SWA_KV_HEADS = 2
N_HEADS = SWA_HEADS
ATT_W = N_HEADS * HEAD_DIM
T5_BUCKETS = 32
T5_MAX_DIST = 128
RET_HEADS = 8
RET_DK = 64
RET_DV = 128
ROPE_BASE = 10000.0
RET_NORM_EPS = 1e-5
BAND_BACK = 8
BAND_HEADS = 16
BAND_MAX_REL = 256
MEM_LEN = 256
MEM_HEADS = 4
MEM_HD = D_MODEL // MEM_HEADS
D_FF = 5632
CONV_W = 3
DN_ALPHA = (2 * DEPTH) ** 0.25
LN_EPS = 1e-5

P_ROWS = BATCH * SEQ
S_ROWS = DEC_BATCH * DEC_SEQ
ROWS = P_ROWS + S_ROWS

COL_QA = 0
COL_QC = 1024
COL_KC = 2048
COL_VC = 3072
COL_VB = 4096
COL_GR = 5120
COL_GA = 6144
COL_GB = 8192
COL_GC = 10240
COL_QB = 12288
COL_KB = 12800
COL_KA2 = 13312
COL_VA2 = 13568
PROJ_COLS = 13824

LANES = 128
BF16_ROWS = 16
MXU_COLS = 256
QB = 256
RET_L = 256
NEG = -1e30
LOG2E = math.log2(math.e)
VMEM_LIMIT = 48 * 1024 * 1024

NT = (((1,), (1,)), ((), ()))
TN = (((0,), (0,)), ((), ()))


def _params(sem, vmem=VMEM_LIMIT):
    return pltpu.CompilerParams(dimension_semantics=sem, vmem_limit_bytes=vmem)


def _pack_rows_kernel(p_ref, s_ref, of_ref, ob_ref, *, n_prompt_tiles):
    src = jnp.where(pl.program_id(0) < n_prompt_tiles, p_ref[...], s_ref[...])
    of_ref[...] = src
    ob_ref[...] = src.astype(BF16)


def pack_rows(x_prompt, x_sample):
    tm = S_ROWS
    npt = P_ROWS // tm
    return pl.pallas_call(
        functools.partial(_pack_rows_kernel, n_prompt_tiles=npt),
        grid=(npt + 1,),
        in_specs=[pl.BlockSpec((tm, D_MODEL), lambda i: (jnp.minimum(i, npt - 1), 0)),
                  pl.BlockSpec((tm, D_MODEL), lambda i: (0, 0))],
        out_specs=[pl.BlockSpec((tm, D_MODEL), lambda i: (i, 0)), pl.BlockSpec((tm, D_MODEL), lambda i: (i, 0))],
        out_shape=[jax.ShapeDtypeStruct((ROWS, D_MODEL), F32), jax.ShapeDtypeStruct((ROWS, D_MODEL), BF16)],
        compiler_params=_params(("parallel",)),
        name="pack_rows",
    )(x_prompt.reshape(P_ROWS, D_MODEL), x_sample.reshape(S_ROWS, D_MODEL))


_W_IN_MOVES = ((0, COL_QA, 1024), (4352, COL_QC, 1024), (5376, COL_KC, 1024), (6400, COL_VC, 1024),
               (2304, COL_VB, 1024), (3328, COL_GR, 1024), (7424, COL_GA, 2048), (9472, COL_GB, 2048),
               (11520, COL_GC, 2048), (1280, COL_QB, 512), (1792, COL_KB, 512))
_W_IN_DUPS = ((1024, COL_KA2), (1152, COL_VA2))


def _w_in_permute_kernel(src_ref, o_ref):
    for src, dst, width in _W_IN_MOVES:
        o_ref[:, dst:dst + width] = src_ref[:, src:src + width].astype(BF16)
    for src, dst in _W_IN_DUPS:
        pair = src_ref[:, src:src + LANES]
        swapped = pltpu.roll(pair, HEAD_DIM, 1)
        o_ref[:, dst:dst + LANES] = jnp.where(_head_lanes(0), pair, swapped).astype(BF16)
        o_ref[:, dst + LANES:dst + 2 * LANES] = jnp.where(_head_lanes(0), swapped, pair).astype(BF16)


def permute_w_in(w_in, *, rows=128):
    depth, d, in_cols = w_in.shape
    assert sum(w for _, _, w in _W_IN_MOVES) + 2 * LANES == in_cols
    return pl.pallas_call(
        _w_in_permute_kernel,
        grid=(depth, d // rows),
        in_specs=[pl.BlockSpec((None, rows, in_cols), lambda l, r: (l, r, 0))],
        out_specs=pl.BlockSpec((None, rows, PROJ_COLS), lambda l, r: (l, r, 0)),
        out_shape=jax.ShapeDtypeStruct((depth, d, PROJ_COLS), BF16),
        compiler_params=_params(("parallel", "parallel")),
        name="permute_w_in",
    )(w_in)


def _mm_kernel(a_ref, b_ref, o_ref, *scratch):
    if scratch:
        bb_ref, = scratch

        @pl.when(pl.program_id(0) == 0)
        def _():
            bb_ref[...] = b_ref[...].astype(BF16)

        b = bb_ref[...]
    else:
        b = b_ref[...].astype(BF16)
    o_ref[...] = jnp.dot(a_ref[...], b, preferred_element_type=F32).astype(o_ref.dtype)


def matmul(a, b, layer, *, tm, tn, out_dtype):
    m, k = a.shape
    n = b.shape[2]
    ni, nj = m // tm, n // tn
    resident = b.dtype == F32 and ni > 1
    assert b.dtype == BF16 or ni == 1 or nj == 1
    b_mode = dict(pipeline_mode=pl.Buffered(1)) if resident else {}
    return pl.pallas_call(
        _mm_kernel,
        grid=(ni, nj),
        in_specs=[pl.BlockSpec((tm, k), lambda i, j: (i, 0)),
                  pl.BlockSpec((None, k, tn), lambda i, j: (layer, 0, j), **b_mode)],
        out_specs=pl.BlockSpec((tm, tn), lambda i, j: (i, j)),
        out_shape=jax.ShapeDtypeStruct((m, n), out_dtype),
        scratch_shapes=[pltpu.VMEM((k, tn), BF16)] if resident else [],
        compiler_params=_params(("arbitrary", "arbitrary") if resident else ("parallel", "parallel")),
        name="matmul",
    )(a, b)


def _gate_mm_kernel(oa_ref, ob_ref, oc_ref, wa_ref, wb_ref, wc_ref, ga_ref, gb_ref, gc_ref, o_ref):
    acc = None
    for o, w, g in ((oa_ref, wa_ref, ga_ref), (ob_ref, wb_ref, gb_ref), (oc_ref, wc_ref, gc_ref)):
        t = jnp.dot(o[...], w[...], preferred_element_type=F32) * jax.nn.sigmoid(g[...].astype(F32))
        acc = t if acc is None else acc + t
    o_ref[...] = acc.astype(o_ref.dtype)


def gated_branch_sum(oa, ob, oc, wa, wb, wc, proj, layer, *, tm=528):
    m, k = oa.shape
    n = wa.shape[2]
    o_spec = pl.BlockSpec((tm, k), lambda i: (i, 0))
    w_spec = pl.BlockSpec((None, k, n), lambda i: (layer, 0, 0), pipeline_mode=pl.Buffered(1))

    def g_spec(col):
        return pl.BlockSpec((tm, n), lambda i: (i, col // n))

    return pl.pallas_call(
        _gate_mm_kernel,
        grid=(m // tm,),
        in_specs=[o_spec, o_spec, o_spec, w_spec, w_spec, w_spec,
                  g_spec(COL_GA), g_spec(COL_GB), g_spec(COL_GC)],
        out_specs=pl.BlockSpec((tm, n), lambda i: (i, 0)),
        out_shape=jax.ShapeDtypeStruct((m, n), BF16),
        compiler_params=_params(("parallel",)),
        name="gated_branch_sum",
    )(oa, ob, oc, wa, wb, wc, proj, proj, proj)


def _mm_ln_kernel(a_ref, b_ref, r_ref, g_ref, bt_ref, o1_ref, o2_ref, acc_ref, *, split):
    @pl.when(pl.program_id(0) == 0)
    def _():
        acc_ref[...] = jnp.zeros_like(acc_ref)

    y = DN_ALPHA * r_ref[...] + acc_ref[...]
    mu = jnp.mean(y, -1, keepdims=True)
    d = y - mu
    var = jnp.mean(d * d, -1, keepdims=True)
    out = d * lax.rsqrt(var + LN_EPS) * g_ref[...] + bt_ref[...]
    o1_ref[...] = out
    o2_ref[...] = out if split else out.astype(BF16)
    acc_ref[...] = jnp.dot(a_ref[...], b_ref[...], preferred_element_type=F32)


def matmul_residual_ln(a, b, layer, resid, gain, bias, *, tm, split_rows=None):
    m, kk = a.shape
    n = b.shape[-1]
    nt = m // tm
    if b.ndim == 3:
        b_spec = pl.BlockSpec((None, kk, n), lambda i: (layer, 0, 0), pipeline_mode=pl.Buffered(1))
    else:
        b_spec = pl.BlockSpec((kk, n), lambda i: (0, 0), pipeline_mode=pl.Buffered(1))
    if split_rows is None:
        cur = lambda i: (jnp.minimum(i, nt - 1), 0)
        prev = lambda i: (jnp.maximum(i - 1, 0), 0)
        out_specs = [pl.BlockSpec((tm, n), prev), pl.BlockSpec((tm, n), prev)]
        out_shape = [jax.ShapeDtypeStruct((m, n), F32), jax.ShapeDtypeStruct((m, n), BF16)]
    else:
        assert split_rows == (nt - 1) * tm
        cur = lambda i: (jnp.where(i == 0, nt - 1, jnp.minimum(i - 1, nt - 2)), 0)
        prev = lambda i: (jnp.where(i <= 1, nt - 1, i - 2), 0)
        out_specs = [pl.BlockSpec((tm, n), lambda i: (jnp.clip(i - 2, 0, nt - 2), 0)),
                     pl.BlockSpec((tm, n), lambda i: (jnp.where(i <= 1, 0, 1), 0))]
        out_shape = [jax.ShapeDtypeStruct((split_rows, n), F32), jax.ShapeDtypeStruct((2 * tm, n), F32)]
    return pl.pallas_call(
        functools.partial(_mm_ln_kernel, split=split_rows is not None),
        grid=(nt + 1,),
        in_specs=[pl.BlockSpec((tm, kk), cur),
                  b_spec,
                  pl.BlockSpec((tm, n), prev),
                  pl.BlockSpec((None, 1, n), lambda i: (layer, 0, 0)),
                  pl.BlockSpec((None, 1, n), lambda i: (layer, 0, 0))],
        out_specs=out_specs,
        out_shape=out_shape,
        scratch_shapes=[pltpu.VMEM((tm, n), F32)],
        compiler_params=_params(("arbitrary",)),
        name="matmul_residual_ln",
    )(a, b, resid, gain.reshape(DEPTH, 1, n), bias.reshape(DEPTH, 1, n))


def _head_lanes(j):
    lane = lax.broadcasted_iota(jnp.int32, (1, LANES), 1)
    return (lane < HEAD_DIM) if j == 0 else (lane >= HEAD_DIM)


def _band_attn_kernel(sink_ref, diag_ref, q_ref, *refs, n_prev_blocks, n_back, shared_kv, has_sink, skew):
    nk = n_prev_blocks + 1
    k_refs = refs[:nk]
    v_refs = refs[nk:2 * nk]
    o_ref = refs[2 * nk]
    bias_ref = refs[2 * nk + 1]
    b = pl.program_id(0)
    i = pl.program_id(1)
    n_prev_rows = n_back * CHUNK
    kw = n_prev_rows + QB

    @pl.when((b == 0) & (i == 0))
    def _():
        n = diag_ref.shape[2]
        cb = lax.broadcasted_iota(jnp.int32, (kw, QB), 0) // CHUNK
        qc = lax.broadcasted_iota(jnp.int32, (kw, QB), 1) // CHUNK
        allowed = (cb >= qc) & (cb - n_back <= qc)
        for h in range(N_HEADS):
            t = pltpu.roll(jnp.broadcast_to(diag_ref[h, 0:1, :], (kw, n)), 0, 1, stride=1, stride_axis=0)
            bias_ref[h] = jnp.where(allowed, t[:, :QB] * LOG2E, NEG)

    krow = lax.broadcasted_iota(jnp.int32, (kw, 1), 0)
    kmask = jnp.where(krow >= jnp.maximum(n_prev_rows - i * QB, 0), 0.0, NEG).astype(BF16)
    lane = lax.broadcasted_iota(jnp.int32, (1, LANES), 1)

    def kv_cols(h):
        c0 = (h // 8 if shared_kv else h // 2) * LANES
        return slice(c0, c0 + LANES)

    def scores(h):
        p, j = divmod(h, 2)
        q = q_ref[:, p * LANES:(p + 1) * LANES] * (HEAD_DIM ** -0.5 * LOG2E)
        k_all = jnp.concatenate([r[:, kv_cols(h)] for r in k_refs], axis=0)
        mask_lane = lane == (HEAD_DIM if j == 0 else 0)
        qj = jnp.where(mask_lane, jnp.ones_like(q), jnp.where(_head_lanes(j), q, jnp.zeros_like(q)))
        kj = jnp.where(mask_lane, kmask, k_all)
        return lax.dot_general(kj, qj, NT, preferred_element_type=F32)

    def softmax(h, s_all):
        sink = sink_ref[h] * LOG2E if has_sink else None
        ps, dens = [], []
        for c in range(QB // LANES):
            lanes = slice(c * LANES, (c + 1) * LANES)
            lo, hi = c * LANES, c * LANES + n_prev_rows + LANES
            s = s_all[lo:hi, lanes] + bias_ref[h, lo:hi, lanes]
            m = jnp.max(s, 0, keepdims=True)
            if has_sink:
                m = jnp.maximum(m, sink)
            e = jnp.exp2(s - m)
            den = jnp.sum(e, 0, keepdims=True)
            if has_sink:
                den = den + jnp.exp2(sink - m)
            parts = [e.astype(BF16)]
            if lo:
                parts.insert(0, jnp.zeros((lo, LANES), BF16))
            if hi < kw:
                parts.append(jnp.zeros((kw - hi, LANES), BF16))
            ps.append(jnp.concatenate(parts, 0))
            dens.append(den)
        return jnp.concatenate(ps, 1), jnp.concatenate(dens, 1)

    def weighted_values(h, p_all, den):
        v_all = jnp.concatenate([r[:, kv_cols(h)] for r in v_refs], axis=0)
        return lax.dot_general(v_all, p_all, TN, preferred_element_type=F32) / den

    lag = N_HEADS if skew is None else skew
    s_q, p_q, outs = {}, {}, {}
    for t in range(N_HEADS + 2 * lag):
        if t < N_HEADS:
            s_q[t] = scores(t)
        if lag <= t < N_HEADS + lag:
            p_q[t - lag] = softmax(t - lag, s_q.pop(t - lag))
        if t >= 2 * lag:
            h = t - 2 * lag
            outs[h] = weighted_values(h, *p_q.pop(h))
            if h % 2:
                p = h // 2
                pair = jnp.where(lax.broadcasted_iota(jnp.int32, (LANES, 1), 0) < HEAD_DIM,
                                 outs.pop(h - 1), outs.pop(h))
                o_ref[:, p * LANES:(p + 1) * LANES] = pair.T.astype(o_ref.dtype)


def band_attention_prompt(proj, diag, sink, *, q_col, k_col, v_col, shared_kv, n_back, has_sink, skew):
    n_prev_rows = n_back * CHUNK
    nqb = SEQ // QB
    if n_prev_rows >= QB:
        n_prev_blocks, pb = n_prev_rows // QB, QB
    else:
        n_prev_blocks, pb = 1, n_prev_rows
    per = QB // pb
    kvw = 2 * SWA_KV_HEADS * HEAD_DIM if shared_kv else ATT_W

    def prev_spec(col, back):
        return pl.BlockSpec(
            (pb, kvw), lambda b, i: (b * (SEQ // pb) + jnp.maximum(i * per - back, 0), col // kvw))

    def own_spec(col):
        return pl.BlockSpec((QB, kvw), lambda b, i: (b * nqb + i, col // kvw))

    k_specs = [prev_spec(k_col, n_prev_blocks - t) for t in range(n_prev_blocks)] + [own_spec(k_col)]
    v_specs = [prev_spec(v_col, n_prev_blocks - t) for t in range(n_prev_blocks)] + [own_spec(v_col)]
    kern = functools.partial(_band_attn_kernel, n_prev_blocks=n_prev_blocks, n_back=n_back,
                             shared_kv=shared_kv, has_sink=has_sink, skew=skew)
    n_in = 2 * (n_prev_blocks + 1)
    return pl.pallas_call(
        kern,
        grid=(BATCH, nqb),
        in_specs=[pl.BlockSpec(memory_space=pltpu.SMEM),
                  pl.BlockSpec(diag.shape, lambda b, i: (0, 0, 0)),
                  pl.BlockSpec((QB, ATT_W), lambda b, i: (b * nqb + i, q_col // ATT_W))]
                 + k_specs + v_specs,
        out_specs=pl.BlockSpec((QB, ATT_W), lambda b, i: (b * nqb + i, 0)),
        out_shape=jax.ShapeDtypeStruct((ROWS, ATT_W), BF16),
        scratch_shapes=[pltpu.VMEM((N_HEADS, n_prev_rows + QB, QB), F32)],
        compiler_params=_params(("arbitrary", "arbitrary")),
        name="band_attention_prompt",
    )(sink, diag, proj, *([proj] * n_in))


def _cached_attn_kernel(sink_ref, q_ref, kn_ref, vn_ref, kc_ref, vc_ref, diag_ref, bn_ref, _, o_ref, bc_ref,
                        *, shared_kv, has_sink):
    lc = kc_ref.shape[1]

    @pl.when(pl.program_id(0) == 0)
    def _():
        n = diag_ref.shape[2]
        for h in range(N_HEADS):
            t = pltpu.roll(jnp.broadcast_to(diag_ref[h, 0:1, :], (DEC_SEQ, n)), 0, 1, stride=1, stride_axis=0)
            bc_ref[h] = t[:, :lc]

    kv_cols = [(p // 4 if shared_kv else p) * LANES for p in range(8)]
    scores = []
    for p in range(8):
        c0 = kv_cols[p]
        q = q_ref[:, p * LANES:(p + 1) * LANES] * (HEAD_DIM ** -0.5)
        kc = kc_ref[0, :, c0:c0 + LANES].astype(BF16)
        kn = kn_ref[:, c0:c0 + LANES]
        for j in range(2):
            h = 2 * p + j
            qj = jnp.where(_head_lanes(j), q, jnp.zeros_like(q))
            scores.append((lax.dot_general(qj, kc, NT, preferred_element_type=F32) + bc_ref[h],
                           lax.dot_general(qj, kn, NT, preferred_element_type=F32) + bn_ref[h]))
    probs = []
    for h, (s_c, s_n) in enumerate(scores):
        m = jnp.maximum(jnp.max(s_c, -1, keepdims=True), jnp.max(s_n, -1, keepdims=True))
        if has_sink:
            m = jnp.maximum(m, sink_ref[h])
        e_c, e_n = jnp.exp(s_c - m), jnp.exp(s_n - m)
        den = jnp.sum(e_c, -1, keepdims=True) + jnp.sum(e_n, -1, keepdims=True)
        if has_sink:
            den = den + jnp.exp(sink_ref[h] - m)
        probs.append((e_c.astype(BF16), e_n.astype(BF16), den))
    for p in range(8):
        c0 = kv_cols[p]
        vc = vc_ref[0, :, c0:c0 + LANES].astype(BF16)
        vn = vn_ref[:, c0:c0 + LANES]
        outs = []
        for j in range(2):
            e_c, e_n, den = probs[2 * p + j]
            outs.append((jnp.dot(e_c, vc, preferred_element_type=F32)
                         + jnp.dot(e_n, vn, preferred_element_type=F32)) / den)
        o_ref[:, p * LANES:(p + 1) * LANES] = jnp.where(_head_lanes(0), outs[0], outs[1]).astype(o_ref.dtype)


def cached_attention_sample(proj, k_cache, v_cache, layer, diag_c, bias_n, sink, o_all, *, q_col, k_col, v_col,
                            shared_kv, has_sink):
    lc, wc = k_cache.shape[2], k_cache.shape[3]
    r0 = P_ROWS // DEC_SEQ
    kern = functools.partial(_cached_attn_kernel, shared_kv=shared_kv, has_sink=has_sink)
    cache = pl.BlockSpec((None, 1, lc, wc), lambda s: (layer, s, 0, 0))
    return pl.pallas_call(
        kern,
        grid=(DEC_BATCH,),
        in_specs=[pl.BlockSpec(memory_space=pltpu.SMEM),
                  pl.BlockSpec((DEC_SEQ, ATT_W), lambda s: (r0 + s, q_col // ATT_W)),
                  pl.BlockSpec((DEC_SEQ, wc), lambda s: (r0 + s, k_col // wc)),
                  pl.BlockSpec((DEC_SEQ, wc), lambda s: (r0 + s, v_col // wc)),
                  cache, cache,
                  pl.BlockSpec(diag_c.shape, lambda s: (0, 0, 0)),
                  pl.BlockSpec((N_HEADS, DEC_SEQ, DEC_SEQ), lambda s: (0, 0, 0)),
                  pl.BlockSpec(memory_space=pl.ANY)],
        out_specs=pl.BlockSpec((DEC_SEQ, ATT_W), lambda s: (r0 + s, 0)),
        out_shape=jax.ShapeDtypeStruct(o_all.shape, o_all.dtype),
        scratch_shapes=[pltpu.VMEM((N_HEADS, DEC_SEQ, lc), F32)],
        input_output_aliases={8: 0},
        compiler_params=_params(("arbitrary",)),
        name="cached_attention_sample",
    )(sink, proj, proj, proj, k_cache, v_cache, diag_c, bias_n, o_all)


def _ret_kernel(q_ref, k_ref, v_ref, g_ref, cos_ref, sin_ref, s0_ref, *refs, blk_len):
    o_ref, sout_ref, st_ref, dec_ref, rdec_ref = refs[-5:]
    blk = pl.program_id(1)
    log_gs = [math.log(1.0 - 2.0 ** (-5.0 - h)) for h in range(RET_HEADS)]

    @pl.when((pl.program_id(0) == 0) & (blk == 0))
    def _():
        ii = lax.broadcasted_iota(jnp.int32, (blk_len, blk_len), 0)
        jj = lax.broadcasted_iota(jnp.int32, (blk_len, blk_len), 1)
        diff = (ii - jj).astype(F32)
        row = lax.broadcasted_iota(jnp.int32, (blk_len, LANES), 0).astype(F32)
        for h in range(RET_HEADS):
            dec_ref[h] = jnp.where(diff >= 0, jnp.exp(log_gs[h] * jnp.maximum(diff, 0.0)), 0.0)
            rdec_ref[h] = jnp.exp(log_gs[h] * (row + 1.0))
            rdec_ref[RET_HEADS + h] = jnp.exp(log_gs[h] * (blk_len - 1.0 - row))

    @pl.when(blk == 0)
    def _():
        st_ref[...] = s0_ref[0]

    cos = cos_ref[...]
    sin = sin_ref[...]
    lane = lax.broadcasted_iota(jnp.int32, (1, LANES), 1)
    low_half = (lane % RET_DK) < (RET_DK // 2)

    def rope(x):
        x = x.astype(F32)
        swapped = jnp.where(low_half, pltpu.roll(x, LANES - RET_DK // 2, 1), pltpu.roll(x, RET_DK // 2, 1))
        return x * cos + swapped * sin

    srow = lax.broadcasted_iota(jnp.int32, (LANES, 1), 0)

    for p in range(RET_HEADS // 2):
        qr = rope(q_ref[:, p * LANES:(p + 1) * LANES])
        kr = rope(k_ref[:, p * LANES:(p + 1) * LANES]) * (RET_DK ** -0.5)
        kb = kr.astype(BF16)
        state = st_ref[p]
        state_b = state.astype(BF16)
        upd = None
        for j in range(2):
            h = 2 * p + j
            qj = jnp.where(_head_lanes(j), qr, 0.0).astype(BF16)
            vh = v_ref[:, h * RET_DV:(h + 1) * RET_DV]
            qk = lax.dot_general(qj, kb, NT, preferred_element_type=F32) * dec_ref[h]
            o = jnp.dot(qk.astype(BF16), vh, preferred_element_type=F32)
            o = o + jnp.dot(qj, state_b, preferred_element_type=F32) * rdec_ref[h]
            mu = jnp.mean(o, -1, keepdims=True)
            d = o - mu
            var = jnp.mean(d * d, -1, keepdims=True)
            gate = g_ref[:, h * RET_DV:(h + 1) * RET_DV].astype(F32)
            o_ref[:, h * RET_DV:(h + 1) * RET_DV] = (
                d * lax.rsqrt(var + RET_NORM_EPS) * (gate * jax.nn.sigmoid(gate))).astype(o_ref.dtype)
            kwj = jnp.where(_head_lanes(j), kr * rdec_ref[RET_HEADS + h], 0.0).astype(BF16)
            u = lax.dot_general(kwj, vh, TN, preferred_element_type=F32)
            upd = u if upd is None else upd + u
        carry = jnp.where(srow < RET_DK, math.exp(log_gs[2 * p] * blk_len), math.exp(log_gs[2 * p + 1] * blk_len))
        st_ref[p] = carry * state + upd

    @pl.when(blk == pl.num_programs(1) - 1)
    def _():
        sout_ref[0] = st_ref[...]


def retention(proj, cos, sin, s0, layer, o_all=None, *, row0, n_seq, seq_len, blk_len, pos_per_blk):
    nb = seq_len // blk_len
    rb0 = row0 // blk_len

    def rows(col, width):
        return pl.BlockSpec((blk_len, width), lambda b, t: (rb0 + b * nb + t, col // width))

    tab = pl.BlockSpec((blk_len, LANES), lambda b, t: (t if pos_per_blk else 0, 0))
    st = pl.BlockSpec((1, 4, LANES, LANES), lambda b, t: (b, 0, 0, 0))
    st_in = pl.BlockSpec((None, 1, 4, LANES, LANES), lambda b, t: (layer, b, 0, 0, 0))
    in_specs = [rows(COL_QB, 512), rows(COL_KB, 512), rows(COL_VB, 1024), rows(COL_GR, 1024), tab, tab, st_in]
    args = [proj, proj, proj, proj, cos, sin, s0]
    aliases = {}
    if o_all is not None:
        in_specs.append(pl.BlockSpec(memory_space=pl.ANY))
        args.append(o_all)
        aliases = {len(args) - 1: 0}
    return pl.pallas_call(
        functools.partial(_ret_kernel, blk_len=blk_len),
        grid=(n_seq, nb),
        in_specs=in_specs,
        out_specs=[rows(0, 1024), st],
        out_shape=[jax.ShapeDtypeStruct((ROWS, 1024), BF16),
                   jax.ShapeDtypeStruct((n_seq, 4, LANES, LANES), F32)],
        scratch_shapes=[pltpu.VMEM((4, LANES, LANES), F32), pltpu.VMEM((RET_HEADS, blk_len, blk_len), F32),
                        pltpu.VMEM((2 * RET_HEADS, blk_len, LANES), F32)],
        input_output_aliases=aliases,
        compiler_params=_params(("arbitrary", "arbitrary")),
        name="retention",
    )(*args)


def _mem_attn_kernel(q_ref, k_ref, v_ref, *refs, heads):
    o_ref = refs[-1]
    cols = [slice(h * MEM_HD, (h + 1) * MEM_HD) for h in range(heads)]
    scores = [lax.dot_general(q_ref[:, c], k_ref[0, :, c].astype(BF16), NT, preferred_element_type=F32)
              * (MEM_HD ** -0.5) for c in cols]
    probs = []
    for s in scores:
        e = jnp.exp(s - jnp.max(s, -1, keepdims=True))
        probs.append((e.astype(BF16), jnp.sum(e, -1, keepdims=True)))
    for c, (e, den) in zip(cols, probs):
        o_ref[:, c] = (jnp.dot(e, v_ref[0, :, c].astype(BF16), preferred_element_type=F32) / den
                       ).astype(o_ref.dtype)


def mem_attention(qm, k_src, v_src, layer, o_all=None, *, row0, n_rows, tm, heads, rows_per_kv):
    rb0 = row0 // tm
    w = heads * MEM_HD
    kv = pl.BlockSpec((None, 1, MEM_LEN, w), lambda i, h: (layer, i * tm // rows_per_kv, 0, h))
    in_specs = [pl.BlockSpec((tm, w), lambda i, h: (rb0 + i, h)), kv, kv]
    args = [qm, k_src, v_src]
    aliases = {}
    if o_all is not None:
        in_specs.append(pl.BlockSpec(memory_space=pl.ANY))
        args.append(o_all)
        aliases = {3: 0}
    return pl.pallas_call(
        functools.partial(_mem_attn_kernel, heads=heads),
        grid=(n_rows // tm, MEM_HEADS // heads),
        in_specs=in_specs,
        out_specs=pl.BlockSpec((tm, w), lambda i, h: (rb0 + i, h)),
        out_shape=jax.ShapeDtypeStruct((ROWS, D_MODEL), BF16),
        input_output_aliases=aliases,
        compiler_params=_params(("parallel", "parallel")),
        name="mem_attention",
    )(*args)


def _gelu(x):
    return 0.5 * x * (1.0 + lax.erf(x * (2.0 ** -0.5)))


def _ffn_in_kernel(*refs, sample, seq_tiles, tail):
    i = pl.program_id(1)
    if sample:
        a_ref, bg_ref, bv_ref, wg_ref, wv_ref, cbg_ref, cbv_ref, f1g_ref, f1v_ref, f2g_ref, f2v_ref = refs[:11]
        h_ref, tg_ref, tv_ref = refs[-3:]
    else:
        a_ref, ah_ref, wfg_ref, wfv_ref, wg_ref, wv_ref, cbg_ref, cbv_ref, side_ref = refs[:9]
        h_ref, tg_ref, tv_ref, side_out_ref, bg_ref, bv_ref = refs[-6:]
        side_out_ref[...] = side_ref[...].astype(BF16)

        @pl.when(i == 0)
        def _():
            bg_ref[...] = wfg_ref[...].astype(BF16)
            bv_ref[...] = wfv_ref[...].astype(BF16)

    tm = a_ref.shape[0]
    a = a_ref[...]
    if sample:
        pos = lax.broadcasted_iota(jnp.int32, (tm, 1), 0) % DEC_SEQ
    else:
        top = lax.broadcasted_iota(jnp.int32, (8, 1), 0)
        keep = (i % seq_tiles != 0).astype(F32)

    def project(cols, b_ref):
        u = jnp.dot(a, b_ref[:, cols], preferred_element_type=F32)
        uh = None if sample else jnp.dot(ah_ref[...], b_ref[:, cols], preferred_element_type=F32)
        return u, uh

    def conv(cols, u, uh, w_ref, cb_ref, f1_ref, f2_ref, t_ref):
        t_ref[0, :, cols] = u[tm - tail:, :]
        r1 = pltpu.roll(u, 1, 0)
        r2 = pltpu.roll(u, 2, 0)
        if sample:
            u1 = jnp.where(pos < 1, f1_ref[:, cols], r1)
            u2 = jnp.where(pos < 2, f2_ref[:, cols], r2)
        else:
            n = uh.shape[0]
            prev1, prev2 = uh[n - 1:n, :] * keep, uh[n - 2:n - 1, :] * keep
            u1 = jnp.concatenate([jnp.where(top < 1, prev1, r1[:8]), r1[8:]], 0)
            top2 = jnp.where(top < 1, prev2, jnp.where(top < 2, prev1, r2[:8]))
            u2 = jnp.concatenate([top2, r2[8:]], 0)
        w = w_ref[:, cols]
        return w[0:1, :] * u2 + w[1:2, :] * u1 + w[2:3, :] * u + cb_ref[:, cols]

    sub = [slice(c0, c0 + MXU_COLS) for c0 in range(0, h_ref.shape[1], MXU_COLS)]
    prods = [(project(cols, bg_ref), project(cols, bv_ref)) for cols in sub]
    for cols, ((ug, uhg), (uv, uhv)) in zip(sub, prods):
        if sample:
            cg = conv(cols, ug, uhg, wg_ref, cbg_ref, f1g_ref, f2g_ref, tg_ref)
            cv = conv(cols, uv, uhv, wv_ref, cbv_ref, f1v_ref, f2v_ref, tv_ref)
        else:
            cg = conv(cols, ug, uhg, wg_ref, cbg_ref, None, None, tg_ref)
            cv = conv(cols, uv, uhv, wv_ref, cbv_ref, None, None, tv_ref)
        h_ref[:, cols] = (_gelu(cg) * cv).astype(h_ref.dtype)


def ffn_in(x, w, conv_w, conv_b, layer, *, row0, n_rows, tm, tn, tail, side_cast=None, sample_state=None):
    sample = sample_state is not None
    k = x.shape[1]
    nj = D_FF // tn
    ni = n_rows // tm
    rb0 = row0 // tm
    halo = BF16_ROWS
    a_spec = pl.BlockSpec((tm, k), lambda j, i: (rb0 + i, 0))
    wg = pl.BlockSpec((None, CONV_W, tn), lambda j, i: (layer, 0, j))
    wv = pl.BlockSpec((None, CONV_W, tn), lambda j, i: (layer, 0, nj + j))
    cg = pl.BlockSpec((None, 1, tn), lambda j, i: (layer, 0, j))
    cv = pl.BlockSpec((None, 1, tn), lambda j, i: (layer, 0, nj + j))
    conv_b3 = conv_b.reshape(DEPTH, 1, 2 * D_FF)
    w_half = pl.BlockSpec((k, tn), lambda j, i: (0, j))
    t_spec = pl.BlockSpec((1, tail, tn), lambda j, i: (i, 0, j))
    out_specs = [pl.BlockSpec((tm, tn), lambda j, i: (rb0 + i, j)), t_spec, t_spec]
    out_shape = [jax.ShapeDtypeStruct((ROWS, D_FF), BF16),
                 jax.ShapeDtypeStruct((ni, tail, D_FF), F32),
                 jax.ShapeDtypeStruct((ni, tail, D_FF), F32)]
    aliases = {}
    if sample:
        fix1, fix2, h_all = sample_state
        fg = pl.BlockSpec((tm, tn), lambda j, i: (i, j))
        fv = pl.BlockSpec((tm, tn), lambda j, i: (i, nj + j))
        in_specs = [a_spec, w_half, w_half, wg, wv, cg, cv, fg, fv, fg, fv, pl.BlockSpec(memory_space=pl.ANY)]
        args = [x, w[0], w[1], conv_w, conv_w, conv_b3, conv_b3, fix1, fix1, fix2, fix2, h_all]
        aliases = {len(args) - 1: 0}
    else:
        rows, cols = side_cast.shape[1:]
        slab = rows // (nj * ni)
        assert slab * nj * ni == rows and slab % BF16_ROWS == 0
        ah = pl.BlockSpec((halo, k), lambda j, i: (jnp.maximum((rb0 + i) * (tm // halo) - 1, 0), 0))
        in_specs = [a_spec, ah,
                    pl.BlockSpec((None, k, tn), lambda j, i: (layer, 0, j)),
                    pl.BlockSpec((None, k, tn), lambda j, i: (layer, 0, nj + j)),
                    wg, wv, cg, cv,
                    pl.BlockSpec((None, slab, cols), lambda j, i: (layer, j * ni + i, 0))]
        args = [x, x, w, w, conv_w, conv_w, conv_b3, conv_b3, side_cast]
        out_specs += [pl.BlockSpec((slab, cols), lambda j, i: (j * ni + i, 0)), w_half, w_half]
        out_shape += [jax.ShapeDtypeStruct((rows, cols), BF16),
                      jax.ShapeDtypeStruct((k, D_FF), BF16), jax.ShapeDtypeStruct((k, D_FF), BF16)]
    return pl.pallas_call(
        functools.partial(_ffn_in_kernel, sample=sample, seq_tiles=SEQ // tm if not sample else 1, tail=tail),
        grid=(nj, ni),
        in_specs=in_specs,
        out_specs=out_specs,
        out_shape=out_shape,
        input_output_aliases=aliases,
        compiler_params=_params(("parallel", "arbitrary")),
        name="ffn_in",
    )(*args)


def _t5_bucket(rel):
    nb = T5_BUCKETS // 2
    max_exact = nb // 2
    n = jnp.abs(rel)
    nf = jnp.maximum(n, 1).astype(F32)
    large = max_exact + (jnp.log(nf / max_exact) / math.log(T5_MAX_DIST / max_exact)
                         * (nb - max_exact)).astype(jnp.int32)
    large = jnp.minimum(large, nb - 1)
    return jnp.where(rel > 0, nb, 0) + jnp.where(n < max_exact, n, large)


def _t5_bias(table, rel):
    return jnp.transpose(table[_t5_bucket(rel)], (2, 0, 1)).astype(F32)


def _clipped_bias(table, rel):
    return table[:, jnp.clip(rel, -BAND_MAX_REL, BAND_MAX_REL) + BAND_MAX_REL].astype(F32)


def _toeplitz_diag(bias_fn, n_rows, n_cols, rel0, n):
    assert n_cols + n_rows - 1 <= n
    k = np.arange(n)
    rel = np.where(k < n_cols, k, k - n) + rel0
    return jnp.broadcast_to(bias_fn(jnp.asarray(rel)[None, :]), (N_HEADS, 8, n))


def _band_diag(bias_fn, n_back):
    n_prev = n_back * CHUNK
    return _toeplitz_diag(lambda d: bias_fn(-d), n_prev + QB, QB, n_prev, 1024)


def _cache_diag(bias_fn, cache_len):
    n = max(256, cache_len * 2)
    return _toeplitz_diag(bias_fn, DEC_SEQ, cache_len, -cache_len, n)


def _rope_tables(pos):
    half = RET_DK // 2
    inv = ROPE_BASE ** (-jnp.arange(half, dtype=F32) / half)
    ang = pos.astype(F32)[:, None] * inv[None, :]
    cos, sin = jnp.cos(ang), jnp.sin(ang)
    cos_t = jnp.concatenate([cos, cos, cos, cos], -1)
    sin_t = jnp.concatenate([-sin, sin, -sin, sin], -1)
    return cos_t, sin_t


def _dup_groups(t):
    g0, g1 = t[..., 0, :], t[..., 1, :]
    return jnp.concatenate([g0, g0, g1, g1], -1)


def _undup(t):
    return jnp.stack([t[:, 0:64], t[:, 128:192]], 1)


def kernel(x_prompt, x_sample, mem_prompt, cache_swa_k, cache_swa_v, state_ret, cache_band_k, cache_band_v, state_ffn_conv, cache_mem_k, cache_mem_v, w_in, t5_table, swa_sink, band_rel_table, w_br_a, w_br_b, w_br_c, w_mix_o, ln1_g, ln1_b, w_mq, w_mk, w_mv, w_mo, ln2_g, ln2_b, w_ffn_in, ffn_conv_w, ffn_conv_b, w_ffn_out, ln3_g, ln3_b):
    x, xb = pack_rows(x_prompt, x_sample)
    memb = mem_prompt.reshape(BATCH * MEM_LEN, D_MODEL).astype(BF16)

    t5 = functools.partial(_t5_bias, t5_table)
    diag_a = _band_diag(t5, SWA_BACK)
    qpos = PAST_LEN + jnp.arange(DEC_SEQ)
    la, lc = cache_swa_k.shape[2], cache_band_k.shape[2]
    rel_n = qpos[None, :] - qpos[:, None]
    diag_a_c, bias_a_n = _cache_diag(t5, la), t5(rel_n)
    cos_p, sin_p = _rope_tables(jnp.arange(SEQ))
    cos_s, sin_s = _rope_tables(qpos)
    zero_state = jnp.zeros((1, BATCH, 4, LANES, LANES), F32)
    no_sink = jnp.zeros((N_HEADS,), F32)

    w_in_b = permute_w_in(w_in)
    w_br_a_b, w_br_b_b, w_br_c_b = w_br_a.astype(BF16), w_br_b.astype(BF16), w_br_c.astype(BF16)
    w_mix_o_b, w_mo_b = w_mix_o.astype(BF16), w_mo.astype(BF16)
    swa_k2, swa_v2 = _dup_groups(cache_swa_k), _dup_groups(cache_swa_v)
    band_k2 = cache_band_k.reshape(DEPTH, DEC_BATCH, lc, BAND_HEADS * HEAD_DIM)
    band_v2 = cache_band_v.reshape(DEPTH, DEC_BATCH, lc, BAND_HEADS * HEAD_DIM)
    mem_k2 = cache_mem_k.reshape(DEPTH, DEC_BATCH, MEM_LEN, D_MODEL)
    mem_v2 = cache_mem_v.reshape(DEPTH, DEC_BATCH, MEM_LEN, D_MODEL)
    ret_s0 = state_ret.reshape(DEPTH, DEC_BATCH, 4, LANES, LANES)

    outs = {k: [] for k in ("p_ak", "p_av", "p_rs", "p_bk", "p_bv", "p_fc", "p_mk", "p_mv",
                            "s_ak", "s_av", "s_rs", "s_bk", "s_bv", "s_fc")}
    for l in range(DEPTH):
        proj = matmul(xb, w_in_b, l, tm=1056, tn=1536, out_dtype=BF16)
        clipped = functools.partial(_clipped_bias, band_rel_table[l])
        oa = band_attention_prompt(proj, diag_a, swa_sink[l], q_col=COL_QA, k_col=COL_KA2, v_col=COL_VA2,
                                   shared_kv=True, n_back=SWA_BACK, has_sink=True, skew=None)
        oc = band_attention_prompt(proj, _band_diag(clipped, BAND_BACK), no_sink, q_col=COL_QC, k_col=COL_KC,
                                   v_col=COL_VC, shared_kv=False, n_back=BAND_BACK, has_sink=False, skew=1)
        ob, rs_p = retention(proj, cos_p, sin_p, zero_state, 0, row0=0, n_seq=BATCH, seq_len=SEQ,
                             blk_len=RET_L, pos_per_blk=True)
        oa = cached_attention_sample(
            proj, swa_k2, swa_v2, l, diag_a_c, bias_a_n, swa_sink[l], oa,
            q_col=COL_QA, k_col=COL_KA2, v_col=COL_VA2, shared_kv=True, has_sink=True)
        oc = cached_attention_sample(
            proj, band_k2, band_v2, l, _cache_diag(clipped, lc), clipped(rel_n), no_sink, oc,
            q_col=COL_QC, k_col=COL_KC, v_col=COL_VC, shared_kv=False, has_sink=False)
        ob, rs_s = retention(proj, cos_s, sin_s, ret_s0, l, ob, row0=P_ROWS, n_seq=DEC_BATCH, seq_len=DEC_SEQ,
                             blk_len=DEC_SEQ, pos_per_blk=False)
        mix = gated_branch_sum(oa, ob, oc, w_br_a_b, w_br_b_b, w_br_c_b, proj, l)
        x, xb = matmul_residual_ln(mix, w_mix_o_b, l, x, ln1_g, ln1_b, tm=528)

        mk = matmul(memb, w_mk, l, tm=512, tn=1024, out_dtype=F32)
        mv = matmul(memb, w_mv, l, tm=512, tn=1024, out_dtype=F32)
        qm = matmul(xb, w_mq, l, tm=1056, tn=2048, out_dtype=BF16)
        om = mem_attention(qm, mk.reshape(1, BATCH, MEM_LEN, D_MODEL), mv.reshape(1, BATCH, MEM_LEN, D_MODEL), 0,
                           row0=0, n_rows=P_ROWS, tm=1024, heads=MEM_HEADS, rows_per_kv=SEQ)
        om = mem_attention(qm, mem_k2, mem_v2, l, om, row0=P_ROWS, n_rows=S_ROWS, tm=DEC_SEQ,
                           heads=MEM_HEADS, rows_per_kv=DEC_SEQ)
        x, xb = matmul_residual_ln(om, w_mo_b, l, x, ln2_g, ln2_b, tm=528)

        h, tg_p, tv_p, w_ffn_out_b, wfg_b, wfv_b = ffn_in(
            xb, w_ffn_in, ffn_conv_w, ffn_conv_b, l, row0=0, n_rows=P_ROWS, tm=1024, tn=512, tail=8,
            side_cast=w_ffn_out)
        st = state_ffn_conv[l]
        fix1 = jnp.pad(st[:, 1:2], ((0, 0), (0, DEC_SEQ - 1), (0, 0))).reshape(S_ROWS, 2 * D_FF)
        fix2 = jnp.pad(st, ((0, 0), (0, DEC_SEQ - 2), (0, 0))).reshape(S_ROWS, 2 * D_FF)
        h, tg_s, tv_s = ffn_in(xb, (wfg_b, wfv_b), ffn_conv_w, ffn_conv_b, l, row0=P_ROWS, n_rows=S_ROWS,
                               tm=S_ROWS, tn=512, tail=S_ROWS, sample_state=(fix1, fix2, h))
        if l < DEPTH - 1:
            x, xb = matmul_residual_ln(h, w_ffn_out_b, l, x, ln3_g, ln3_b, tm=S_ROWS)
        else:
            y_prompt, y_sample = matmul_residual_ln(h, w_ffn_out_b, l, x, ln3_g, ln3_b, tm=S_ROWS,
                                                    split_rows=P_ROWS)

        sf = proj[P_ROWS:]
        la_p, lc_p = min(SWA_BACK * CHUNK, SEQ), min(BAND_BACK * CHUNK, SEQ)
        seq_tail = lambda n, c0, w: jnp.stack(
            [proj[(b + 1) * SEQ - n:(b + 1) * SEQ, c0:c0 + w] for b in range(BATCH)], 0)
        outs["p_ak"].append(_undup(seq_tail(la_p, COL_KA2, 256).reshape(BATCH * la_p, 256))
                            .reshape(BATCH, la_p, 2, 64).astype(F32))
        outs["p_av"].append(_undup(seq_tail(la_p, COL_VA2, 256).reshape(BATCH * la_p, 256))
                            .reshape(BATCH, la_p, 2, 64).astype(F32))
        outs["p_rs"].append(rs_p.reshape(BATCH, RET_HEADS, RET_DK, RET_DV))
        outs["p_bk"].append(seq_tail(lc_p, COL_KC, 1024).reshape(BATCH, lc_p, 16, 64).astype(F32))
        outs["p_bv"].append(seq_tail(lc_p, COL_VC, 1024).reshape(BATCH, lc_p, 16, 64).astype(F32))
        last = [(b + 1) * (SEQ // 1024) - 1 for b in range(BATCH)]
        outs["p_fc"].append(jnp.stack(
            [jnp.concatenate([tg_p[t, 6:8], tv_p[t, 6:8]], -1) for t in last], 0))
        outs["p_mk"].append(mk.reshape(BATCH, MEM_LEN, MEM_HEADS, MEM_HD))
        outs["p_mv"].append(mv.reshape(BATCH, MEM_LEN, MEM_HEADS, MEM_HD))
        outs["s_ak"].append(_undup(sf[:, COL_KA2:COL_KA2 + 256]).reshape(DEC_BATCH, DEC_SEQ, 2, 64).astype(F32))
        outs["s_av"].append(_undup(sf[:, COL_VA2:COL_VA2 + 256]).reshape(DEC_BATCH, DEC_SEQ, 2, 64).astype(F32))
        outs["s_rs"].append(rs_s.reshape(DEC_BATCH, RET_HEADS, RET_DK, RET_DV))
        outs["s_bk"].append(sf[:, COL_KC:COL_KC + 1024].reshape(DEC_BATCH, DEC_SEQ, 16, 64).astype(F32))
        outs["s_bv"].append(sf[:, COL_VC:COL_VC + 1024].reshape(DEC_BATCH, DEC_SEQ, 16, 64).astype(F32))
        u_s = jnp.concatenate([tg_s[0], tv_s[0]], -1).reshape(DEC_BATCH, DEC_SEQ, 2 * D_FF)
        outs["s_fc"].append(u_s[:, DEC_SEQ - 2:])

    st = lambda name: jnp.stack(outs[name], 0)
    return (y_prompt.reshape(BATCH, SEQ, D_MODEL), y_sample[:S_ROWS].reshape(DEC_BATCH, DEC_SEQ, D_MODEL),
            st("p_ak"), st("p_av"), st("p_rs"), st("p_bk"), st("p_bv"), st("p_fc"), st("p_mk"), st("p_mv"),
            st("s_ak"), st("s_av"), st("s_rs"), st("s_bk"), st("s_bv"), st("s_fc"))
```

```python
import functools
import math

import numpy as np
import jax
import jax.numpy as jnp
from jax import lax
from jax.experimental import pallas as pl
from jax.experimental.pallas import tpu as pltpu

F32 = jnp.float32
BF16 = jnp.bfloat16

D_MODEL = 2048
BATCH = 2
SEQ = 4096
DEPTH = 2
DEC_BATCH = 16
DEC_SEQ = 16
PAST_LEN = 2048
CHUNK = 64
HEAD_DIM = 64
SWA_BACK = 2
SWA_HEADS = 16
SWA_KV_HEADS = 2
N_HEADS = SWA_HEADS
ATT_W = N_HEADS * HEAD_DIM
T5_BUCKETS = 32
T5_MAX_DIST = 128
RET_HEADS = 8
RET_DK = 64
RET_DV = 128
ROPE_BASE = 10000.0
RET_NORM_EPS = 1e-5
BAND_BACK = 8
BAND_HEADS = 16
BAND_MAX_REL = 256
MEM_LEN = 256
MEM_HEADS = 4
MEM_HD = D_MODEL // MEM_HEADS
D_FF = 5632
CONV_W = 3
DN_ALPHA = (2 * DEPTH) ** 0.25
LN_EPS = 1e-5

P_ROWS = BATCH * SEQ
S_ROWS = DEC_BATCH * DEC_SEQ
ROWS = P_ROWS + S_ROWS

COL_QA = 0
COL_QC = 1024
COL_KC = 2048
COL_VC = 3072
COL_VB = 4096
COL_GR = 5120
COL_GA = 6144
COL_GB = 8192
COL_GC = 10240
COL_QB = 12288
COL_KB = 12800
COL_KA2 = 13312
COL_VA2 = 13568
PROJ_COLS = 13824

LANES = 128
BF16_ROWS = 16
MXU_COLS = 256
QB = 256
RET_L = 256
NEG = -1e30
LOG2E = math.log2(math.e)
VMEM_LIMIT = 48 * 1024 * 1024

NT = (((1,), (1,)), ((), ()))
TN = (((0,), (0,)), ((), ()))


def _params(sem, vmem=VMEM_LIMIT):
    return pltpu.CompilerParams(dimension_semantics=sem, vmem_limit_bytes=vmem)


def _pack_rows_kernel(p_ref, s_ref, of_ref, ob_ref, *, n_prompt_tiles):
    src = jnp.where(pl.program_id(0) < n_prompt_tiles, p_ref[...], s_ref[...])
    of_ref[...] = src
    ob_ref[...] = src.astype(BF16)


def pack_rows(x_prompt, x_sample):
    tm = S_ROWS
    npt = P_ROWS // tm
    return pl.pallas_call(
        functools.partial(_pack_rows_kernel, n_prompt_tiles=npt),
        grid=(npt + 1,),
        in_specs=[pl.BlockSpec((tm, D_MODEL), lambda i: (jnp.minimum(i, npt - 1), 0)),
                  pl.BlockSpec((tm, D_MODEL), lambda i: (0, 0))],
        out_specs=[pl.BlockSpec((tm, D_MODEL), lambda i: (i, 0)), pl.BlockSpec((tm, D_MODEL), lambda i: (i, 0))],
        out_shape=[jax.ShapeDtypeStruct((ROWS, D_MODEL), F32), jax.ShapeDtypeStruct((ROWS, D_MODEL), BF16)],
        compiler_params=_params(("parallel",)),
        name="pack_rows",
    )(x_prompt.reshape(P_ROWS, D_MODEL), x_sample.reshape(S_ROWS, D_MODEL))


_W_IN_MOVES = ((0, COL_QA, 1024), (4352, COL_QC, 1024), (5376, COL_KC, 1024), (6400, COL_VC, 1024),
               (2304, COL_VB, 1024), (3328, COL_GR, 1024), (7424, COL_GA, 2048), (9472, COL_GB, 2048),
               (11520, COL_GC, 2048), (1280, COL_QB, 512), (1792, COL_KB, 512))
_W_IN_DUPS = ((1024, COL_KA2), (1152, COL_VA2))


def _w_in_permute_kernel(src_ref, o_ref):
    for src, dst, width in _W_IN_MOVES:
        o_ref[:, dst:dst + width] = src_ref[:, src:src + width].astype(BF16)
    for src, dst in _W_IN_DUPS:
        pair = src_ref[:, src:src + LANES]
        swapped = pltpu.roll(pair, HEAD_DIM, 1)
        o_ref[:, dst:dst + LANES] = jnp.where(_head_lanes(0), pair, swapped).astype(BF16)
        o_ref[:, dst + LANES:dst + 2 * LANES] = jnp.where(_head_lanes(0), swapped, pair).astype(BF16)


def permute_w_in(w_in, *, rows=128):
    depth, d, in_cols = w_in.shape
    assert sum(w for _, _, w in _W_IN_MOVES) + 2 * LANES == in_cols
    return pl.pallas_call(
        _w_in_permute_kernel,
        grid=(depth, d // rows),
        in_specs=[pl.BlockSpec((None, rows, in_cols), lambda l, r: (l, r, 0))],
        out_specs=pl.BlockSpec((None, rows, PROJ_COLS), lambda l, r: (l, r, 0)),
        out_shape=jax.ShapeDtypeStruct((depth, d, PROJ_COLS), BF16),
        compiler_params=_params(("parallel", "parallel")),
        name="permute_w_in",
    )(w_in)


def _mm_kernel(a_ref, b_ref, *refs, n_side, resident):
    side_in, o_ref, side_out = refs[:n_side], refs[n_side], refs[n_side + 1:2 * n_side + 1]
    for src, dst in zip(side_in, side_out):
        dst[...] = src[...].astype(BF16)
    if resident:
        bb_ref = refs[-1]

        @pl.when(pl.program_id(0) == 0)
        def _():
            bb_ref[...] = b_ref[...].astype(BF16)

        b = bb_ref[...]
    else:
        b = b_ref[...].astype(BF16)
    o_ref[...] = jnp.dot(a_ref[...], b, preferred_element_type=F32).astype(o_ref.dtype)


def matmul(a, b, layer, *, tm, tn, out_dtype, side_casts=()):
    m, k = a.shape
    n = b.shape[2]
    ni, nj = m // tm, n // tn
    resident = b.dtype == F32 and ni > 1
    assert b.dtype == BF16 or ni == 1 or nj == 1
    b_mode = dict(pipeline_mode=pl.Buffered(1)) if resident else {}
    in_specs = [pl.BlockSpec((tm, k), lambda i, j: (i, 0)),
                pl.BlockSpec((None, k, tn), lambda i, j: (layer, 0, j), **b_mode)]
    out_specs = [pl.BlockSpec((tm, tn), lambda i, j: (i, j))]
    out_shape = [jax.ShapeDtypeStruct((m, n), out_dtype)]
    n_slabs = ni * (nj - 1)
    slab_of = lambda i, j: i * (nj - 1) + jnp.minimum(j, nj - 2)
    for w in side_casts:
        rows, cols = w.shape[1:]
        slab = rows // n_slabs
        assert slab * n_slabs == rows and slab % BF16_ROWS == 0
        in_specs.append(pl.BlockSpec((None, slab, cols), lambda i, j: (layer, slab_of(i, j), 0)))
        out_specs.append(pl.BlockSpec((slab, cols), lambda i, j: (slab_of(i, j), 0)))
        out_shape.append(jax.ShapeDtypeStruct((rows, cols), BF16))
    sem = ("arbitrary", "arbitrary") if resident or side_casts else ("parallel", "parallel")
    outs = pl.pallas_call(
        functools.partial(_mm_kernel, n_side=len(side_casts), resident=resident),
        grid=(ni, nj),
        in_specs=in_specs,
        out_specs=out_specs,
        out_shape=out_shape,
        scratch_shapes=[pltpu.VMEM((k, tn), BF16)] if resident else [],
        compiler_params=_params(sem),
        name="matmul",
    )(a, b, *side_casts)
    return outs if side_casts else outs[0]


def _gate_mm_kernel(oa_ref, ob_ref, oc_ref, wa_ref, wb_ref, wc_ref, ga_ref, gb_ref, gc_ref, o_ref):
    acc = None
    for o, w, g in ((oa_ref, wa_ref, ga_ref), (ob_ref, wb_ref, gb_ref), (oc_ref, wc_ref, gc_ref)):
        t = jnp.dot(o[...], w[...], preferred_element_type=F32) * jax.nn.sigmoid(g[...].astype(F32))
        acc = t if acc is None else acc + t
    o_ref[...] = acc.astype(o_ref.dtype)


def gated_branch_sum(oa, ob, oc, wa, wb, wc, proj, *, tm=528):
    m, k = oa.shape
    n = wa.shape[1]
    o_spec = pl.BlockSpec((tm, k), lambda i: (i, 0))
    w_spec = pl.BlockSpec((k, n), lambda i: (0, 0), pipeline_mode=pl.Buffered(1))

    def g_spec(col):
        return pl.BlockSpec((tm, n), lambda i: (i, col // n))

    return pl.pallas_call(
        _gate_mm_kernel,
        grid=(m // tm,),
        in_specs=[o_spec, o_spec, o_spec, w_spec, w_spec, w_spec,
                  g_spec(COL_GA), g_spec(COL_GB), g_spec(COL_GC)],
        out_specs=pl.BlockSpec((tm, n), lambda i: (i, 0)),
        out_shape=jax.ShapeDtypeStruct((m, n), BF16),
        compiler_params=_params(("parallel",)),
        name="gated_branch_sum",
    )(oa, ob, oc, wa, wb, wc, proj, proj, proj)


def _mm_ln_kernel(a_ref, b_ref, r_ref, g_ref, bt_ref, o1_ref, o2_ref, acc_ref, *, split):
    @pl.when(pl.program_id(0) == 0)
    def _():
        acc_ref[...] = jnp.zeros_like(acc_ref)

    y = DN_ALPHA * r_ref[...] + acc_ref[...]
    mu = jnp.mean(y, -1, keepdims=True)
    d = y - mu
    var = jnp.mean(d * d, -1, keepdims=True)
    out = d * lax.rsqrt(var + LN_EPS) * g_ref[...] + bt_ref[...]
    o1_ref[...] = out
    o2_ref[...] = out if split else out.astype(BF16)
    acc_ref[...] = jnp.dot(a_ref[...], b_ref[...], preferred_element_type=F32)


def matmul_residual_ln(a, b, layer, resid, gain, bias, *, tm, split_rows=None):
    m, kk = a.shape
    n = b.shape[-1]
    nt = m // tm
    if b.ndim == 3:
        b_spec = pl.BlockSpec((None, kk, n), lambda i: (layer, 0, 0), pipeline_mode=pl.Buffered(1))
    else:
        b_spec = pl.BlockSpec((kk, n), lambda i: (0, 0), pipeline_mode=pl.Buffered(1))
    if split_rows is None:
        cur = lambda i: (jnp.minimum(i, nt - 1), 0)
        prev = lambda i: (jnp.maximum(i - 1, 0), 0)
        out_specs = [pl.BlockSpec((tm, n), prev), pl.BlockSpec((tm, n), prev)]
        out_shape = [jax.ShapeDtypeStruct((m, n), F32), jax.ShapeDtypeStruct((m, n), BF16)]
    else:
        assert split_rows == (nt - 1) * tm
        cur = lambda i: (jnp.where(i == 0, nt - 1, jnp.minimum(i - 1, nt - 2)), 0)
        prev = lambda i: (jnp.where(i <= 1, nt - 1, i - 2), 0)
        out_specs = [pl.BlockSpec((tm, n), lambda i: (jnp.clip(i - 2, 0, nt - 2), 0)),
                     pl.BlockSpec((tm, n), lambda i: (jnp.where(i <= 1, 0, 1), 0))]
        out_shape = [jax.ShapeDtypeStruct((split_rows, n), F32), jax.ShapeDtypeStruct((2 * tm, n), F32)]
    return pl.pallas_call(
        functools.partial(_mm_ln_kernel, split=split_rows is not None),
        grid=(nt + 1,),
        in_specs=[pl.BlockSpec((tm, kk), cur),
                  b_spec,
                  pl.BlockSpec((tm, n), prev),
                  pl.BlockSpec((None, 1, n), lambda i: (layer, 0, 0)),
                  pl.BlockSpec((None, 1, n), lambda i: (layer, 0, 0))],
        out_specs=out_specs,
        out_shape=out_shape,
        scratch_shapes=[pltpu.VMEM((tm, n), F32)],
        compiler_params=_params(("arbitrary",)),
        name="matmul_residual_ln",
    )(a, b, resid, gain.reshape(DEPTH, 1, n), bias.reshape(DEPTH, 1, n))


def _head_lanes(j):
    lane = lax.broadcasted_iota(jnp.int32, (1, LANES), 1)
    return (lane < HEAD_DIM) if j == 0 else (lane >= HEAD_DIM)


def _band_attn_kernel(sink_ref, diag_ref, q_ref, *refs, n_prev_blocks, n_back, shared_kv, has_sink, skew):
    nk = n_prev_blocks + 1
    k_refs = refs[:nk]
    v_refs = refs[nk:2 * nk]
    o_ref = refs[2 * nk]
    bias_ref = refs[2 * nk + 1]
    b = pl.program_id(0)
    i = pl.program_id(1)
    n_prev_rows = n_back * CHUNK
    kw = n_prev_rows + QB

    @pl.when((b == 0) & (i == 0))
    def _():
        n = diag_ref.shape[2]
        cb = lax.broadcasted_iota(jnp.int32, (kw, QB), 0) // CHUNK
        qc = lax.broadcasted_iota(jnp.int32, (kw, QB), 1) // CHUNK
        allowed = (cb >= qc) & (cb - n_back <= qc)
        for h in range(N_HEADS):
            t = pltpu.roll(jnp.broadcast_to(diag_ref[h, 0:1, :], (kw, n)), 0, 1, stride=1, stride_axis=0)
            bias_ref[h] = jnp.where(allowed, t[:, :QB] * LOG2E, NEG)

    krow = lax.broadcasted_iota(jnp.int32, (kw, 1), 0)
    kmask = jnp.where(krow >= jnp.maximum(n_prev_rows - i * QB, 0), 0.0, NEG).astype(BF16)
    lane = lax.broadcasted_iota(jnp.int32, (1, LANES), 1)

    def kv_cols(h):
        c0 = (h // 8 if shared_kv else h // 2) * LANES
        return slice(c0, c0 + LANES)

    def scores(h):
        p, j = divmod(h, 2)
        q = q_ref[:, p * LANES:(p + 1) * LANES] * (HEAD_DIM ** -0.5 * LOG2E)
        k_all = jnp.concatenate([r[:, kv_cols(h)] for r in k_refs], axis=0)
        mask_lane = lane == (HEAD_DIM if j == 0 else 0)
        qj = jnp.where(mask_lane, jnp.ones_like(q), jnp.where(_head_lanes(j), q, jnp.zeros_like(q)))
        kj = jnp.where(mask_lane, kmask, k_all)
        return lax.dot_general(kj, qj, NT, preferred_element_type=F32)

    def softmax(h, s_all):
        sink = sink_ref[h] * LOG2E if has_sink else None
        ps, dens = [], []
        for c in range(QB // LANES):
            lanes = slice(c * LANES, (c + 1) * LANES)
            lo, hi = c * LANES, c * LANES + n_prev_rows + LANES
            s = s_all[lo:hi, lanes] + bias_ref[h, lo:hi, lanes]
            m = jnp.max(s, 0, keepdims=True)
            if has_sink:
                m = jnp.maximum(m, sink)
            e = jnp.exp2(s - m)
            den = jnp.sum(e, 0, keepdims=True)
            if has_sink:
                den = den + jnp.exp2(sink - m)
            parts = [e.astype(BF16)]
            if lo:
                parts.insert(0, jnp.zeros((lo, LANES), BF16))
            if hi < kw:
                parts.append(jnp.zeros((kw - hi, LANES), BF16))
            ps.append(jnp.concatenate(parts, 0))
            dens.append(den)
        return jnp.concatenate(ps, 1), jnp.concatenate(dens, 1)

    def weighted_values(h, p_all, den):
        v_all = jnp.concatenate([r[:, kv_cols(h)] for r in v_refs], axis=0)
        return lax.dot_general(v_all, p_all, TN, preferred_element_type=F32) / den

    lag = N_HEADS if skew is None else skew
    s_q, p_q, outs = {}, {}, {}
    for t in range(N_HEADS + 2 * lag):
        if t < N_HEADS:
            s_q[t] = scores(t)
        if lag <= t < N_HEADS + lag:
            p_q[t - lag] = softmax(t - lag, s_q.pop(t - lag))
        if t >= 2 * lag:
            h = t - 2 * lag
            outs[h] = weighted_values(h, *p_q.pop(h))
            if h % 2:
                p = h // 2
                pair = jnp.where(lax.broadcasted_iota(jnp.int32, (LANES, 1), 0) < HEAD_DIM,
                                 outs.pop(h - 1), outs.pop(h))
                o_ref[:, p * LANES:(p + 1) * LANES] = pair.T.astype(o_ref.dtype)


def band_attention_prompt(proj, diag, sink, *, q_col, k_col, v_col, shared_kv, n_back, has_sink, skew):
    n_prev_rows = n_back * CHUNK
    nqb = SEQ // QB
    if n_prev_rows >= QB:
        n_prev_blocks, pb = n_prev_rows // QB, QB
    else:
        n_prev_blocks, pb = 1, n_prev_rows
    per = QB // pb
    kvw = 2 * SWA_KV_HEADS * HEAD_DIM if shared_kv else ATT_W

    def prev_spec(col, back):
        return pl.BlockSpec(
            (pb, kvw), lambda b, i: (b * (SEQ // pb) + jnp.maximum(i * per - back, 0), col // kvw))

    def own_spec(col):
        return pl.BlockSpec((QB, kvw), lambda b, i: (b * nqb + i, col // kvw))

    k_specs = [prev_spec(k_col, n_prev_blocks - t) for t in range(n_prev_blocks)] + [own_spec(k_col)]
    v_specs = [prev_spec(v_col, n_prev_blocks - t) for t in range(n_prev_blocks)] + [own_spec(v_col)]
    kern = functools.partial(_band_attn_kernel, n_prev_blocks=n_prev_blocks, n_back=n_back,
                             shared_kv=shared_kv, has_sink=has_sink, skew=skew)
    n_in = 2 * (n_prev_blocks + 1)
    return pl.pallas_call(
        kern,
        grid=(BATCH, nqb),
        in_specs=[pl.BlockSpec(memory_space=pltpu.SMEM),
                  pl.BlockSpec(diag.shape, lambda b, i: (0, 0, 0)),
                  pl.BlockSpec((QB, ATT_W), lambda b, i: (b * nqb + i, q_col // ATT_W))]
                 + k_specs + v_specs,
        out_specs=pl.BlockSpec((QB, ATT_W), lambda b, i: (b * nqb + i, 0)),
        out_shape=jax.ShapeDtypeStruct((ROWS, ATT_W), BF16),
        scratch_shapes=[pltpu.VMEM((N_HEADS, n_prev_rows + QB, QB), F32)],
        compiler_params=_params(("arbitrary", "arbitrary")),
        name="band_attention_prompt",
    )(sink, diag, proj, *([proj] * n_in))


def _cached_attn_kernel(sink_ref, q_ref, kn_ref, vn_ref, kc_ref, vc_ref, diag_ref, bn_ref, _, o_ref, bc_ref,
                        *, shared_kv, has_sink):
    lc = kc_ref.shape[1]

    @pl.when(pl.program_id(0) == 0)
    def _():
        n = diag_ref.shape[2]
        for h in range(N_HEADS):
            t = pltpu.roll(jnp.broadcast_to(diag_ref[h, 0:1, :], (DEC_SEQ, n)), 0, 1, stride=1, stride_axis=0)
            bc_ref[h] = t[:, :lc]

    kv_cols = [(p // 4 if shared_kv else p) * LANES for p in range(8)]
    scores = []
    for p in range(8):
        c0 = kv_cols[p]
        q = q_ref[:, p * LANES:(p + 1) * LANES] * (HEAD_DIM ** -0.5)
        kc = kc_ref[0, :, c0:c0 + LANES].astype(BF16)
        kn = kn_ref[:, c0:c0 + LANES]
        for j in range(2):
            h = 2 * p + j
            qj = jnp.where(_head_lanes(j), q, jnp.zeros_like(q))
            scores.append((lax.dot_general(qj, kc, NT, preferred_element_type=F32) + bc_ref[h],
                           lax.dot_general(qj, kn, NT, preferred_element_type=F32) + bn_ref[h]))
    probs = []
    for h, (s_c, s_n) in enumerate(scores):
        m = jnp.maximum(jnp.max(s_c, -1, keepdims=True), jnp.max(s_n, -1, keepdims=True))
        if has_sink:
            m = jnp.maximum(m, sink_ref[h])
        e_c, e_n = jnp.exp(s_c - m), jnp.exp(s_n - m)
        den = jnp.sum(e_c, -1, keepdims=True) + jnp.sum(e_n, -1, keepdims=True)
        if has_sink:
            den = den + jnp.exp(sink_ref[h] - m)
        probs.append((e_c.astype(BF16), e_n.astype(BF16), den))
    for p in range(8):
        c0 = kv_cols[p]
        vc = vc_ref[0, :, c0:c0 + LANES].astype(BF16)
        vn = vn_ref[:, c0:c0 + LANES]
        outs = []
        for j in range(2):
            e_c, e_n, den = probs[2 * p + j]
            outs.append((jnp.dot(e_c, vc, preferred_element_type=F32)
                         + jnp.dot(e_n, vn, preferred_element_type=F32)) / den)
        o_ref[:, p * LANES:(p + 1) * LANES] = jnp.where(_head_lanes(0), outs[0], outs[1]).astype(o_ref.dtype)


def cached_attention_sample(proj, k_cache, v_cache, layer, diag_c, bias_n, sink, o_all, *, q_col, k_col, v_col,
                            shared_kv, has_sink):
    lc, wc = k_cache.shape[2], k_cache.shape[3]
    r0 = P_ROWS // DEC_SEQ
    kern = functools.partial(_cached_attn_kernel, shared_kv=shared_kv, has_sink=has_sink)
    cache = pl.BlockSpec((None, 1, lc, wc), lambda s: (layer, s, 0, 0))
    return pl.pallas_call(
        kern,
        grid=(DEC_BATCH,),
        in_specs=[pl.BlockSpec(memory_space=pltpu.SMEM),
                  pl.BlockSpec((DEC_SEQ, ATT_W), lambda s: (r0 + s, q_col // ATT_W)),
                  pl.BlockSpec((DEC_SEQ, wc), lambda s: (r0 + s, k_col // wc)),
                  pl.BlockSpec((DEC_SEQ, wc), lambda s: (r0 + s, v_col // wc)),
                  cache, cache,
                  pl.BlockSpec(diag_c.shape, lambda s: (0, 0, 0)),
                  pl.BlockSpec((N_HEADS, DEC_SEQ, DEC_SEQ), lambda s: (0, 0, 0)),
                  pl.BlockSpec(memory_space=pl.ANY)],
        out_specs=pl.BlockSpec((DEC_SEQ, ATT_W), lambda s: (r0 + s, 0)),
        out_shape=jax.ShapeDtypeStruct(o_all.shape, o_all.dtype),
        scratch_shapes=[pltpu.VMEM((N_HEADS, DEC_SEQ, lc), F32)],
        input_output_aliases={8: 0},
        compiler_params=_params(("arbitrary",)),
        name="cached_attention_sample",
    )(sink, proj, proj, proj, k_cache, v_cache, diag_c, bias_n, o_all)


def _ret_kernel(q_ref, k_ref, v_ref, g_ref, cos_ref, sin_ref, s0_ref, *refs, blk_len):
    o_ref, sout_ref, st_ref, dec_ref, rdec_ref = refs[-5:]
    blk = pl.program_id(1)
    log_gs = [math.log(1.0 - 2.0 ** (-5.0 - h)) for h in range(RET_HEADS)]

    @pl.when((pl.program_id(0) == 0) & (blk == 0))
    def _():
        ii = lax.broadcasted_iota(jnp.int32, (blk_len, blk_len), 0)
        jj = lax.broadcasted_iota(jnp.int32, (blk_len, blk_len), 1)
        diff = (ii - jj).astype(F32)
        row = lax.broadcasted_iota(jnp.int32, (blk_len, LANES), 0).astype(F32)
        for h in range(RET_HEADS):
            dec_ref[h] = jnp.where(diff >= 0, jnp.exp(log_gs[h] * jnp.maximum(diff, 0.0)), 0.0)
            rdec_ref[h] = jnp.exp(log_gs[h] * (row + 1.0))
            rdec_ref[RET_HEADS + h] = jnp.exp(log_gs[h] * (blk_len - 1.0 - row))

    @pl.when(blk == 0)
    def _():
        st_ref[...] = s0_ref[0]

    cos = cos_ref[...]
    sin = sin_ref[...]
    lane = lax.broadcasted_iota(jnp.int32, (1, LANES), 1)
    low_half = (lane % RET_DK) < (RET_DK // 2)

    def rope(x):
        x = x.astype(F32)
        swapped = jnp.where(low_half, pltpu.roll(x, LANES - RET_DK // 2, 1), pltpu.roll(x, RET_DK // 2, 1))
        return x * cos + swapped * sin

    srow = lax.broadcasted_iota(jnp.int32, (LANES, 1), 0)

    for p in range(RET_HEADS // 2):
        qr = rope(q_ref[:, p * LANES:(p + 1) * LANES])
        kr = rope(k_ref[:, p * LANES:(p + 1) * LANES]) * (RET_DK ** -0.5)
        kb = kr.astype(BF16)
        state = st_ref[p]
        state_b = state.astype(BF16)
        upd = None
        for j in range(2):
            h = 2 * p + j
            qj = jnp.where(_head_lanes(j), qr, 0.0).astype(BF16)
            vh = v_ref[:, h * RET_DV:(h + 1) * RET_DV]
            qk = lax.dot_general(qj, kb, NT, preferred_element_type=F32) * dec_ref[h]
            o = jnp.dot(qk.astype(BF16), vh, preferred_element_type=F32)
            o = o + jnp.dot(qj, state_b, preferred_element_type=F32) * rdec_ref[h]
            mu = jnp.mean(o, -1, keepdims=True)
            d = o - mu
            var = jnp.mean(d * d, -1, keepdims=True)
            gate = g_ref[:, h * RET_DV:(h + 1) * RET_DV].astype(F32)
            o_ref[:, h * RET_DV:(h + 1) * RET_DV] = (
                d * lax.rsqrt(var + RET_NORM_EPS) * (gate * jax.nn.sigmoid(gate))).astype(o_ref.dtype)
            kwj = jnp.where(_head_lanes(j), kr * rdec_ref[RET_HEADS + h], 0.0).astype(BF16)
            u = lax.dot_general(kwj, vh, TN, preferred_element_type=F32)
            upd = u if upd is None else upd + u
        carry = jnp.where(srow < RET_DK, math.exp(log_gs[2 * p] * blk_len), math.exp(log_gs[2 * p + 1] * blk_len))
        st_ref[p] = carry * state + upd

    @pl.when(blk == pl.num_programs(1) - 1)
    def _():
        sout_ref[0] = st_ref[...]


def retention(proj, cos, sin, s0, layer, o_all=None, *, row0, n_seq, seq_len, blk_len, pos_per_blk):
    nb = seq_len // blk_len
    rb0 = row0 // blk_len

    def rows(col, width):
        return pl.BlockSpec((blk_len, width), lambda b, t: (rb0 + b * nb + t, col // width))

    tab = pl.BlockSpec((blk_len, LANES), lambda b, t: (t if pos_per_blk else 0, 0))
    st = pl.BlockSpec((1, 4, LANES, LANES), lambda b, t: (b, 0, 0, 0))
    st_in = pl.BlockSpec((None, 1, 4, LANES, LANES), lambda b, t: (layer, b, 0, 0, 0))
    in_specs = [rows(COL_QB, 512), rows(COL_KB, 512), rows(COL_VB, 1024), rows(COL_GR, 1024), tab, tab, st_in]
    args = [proj, proj, proj, proj, cos, sin, s0]
    aliases = {}
    if o_all is not None:
        in_specs.append(pl.BlockSpec(memory_space=pl.ANY))
        args.append(o_all)
        aliases = {len(args) - 1: 0}
    return pl.pallas_call(
        functools.partial(_ret_kernel, blk_len=blk_len),
        grid=(n_seq, nb),
        in_specs=in_specs,
        out_specs=[rows(0, 1024), st],
        out_shape=[jax.ShapeDtypeStruct((ROWS, 1024), BF16),
                   jax.ShapeDtypeStruct((n_seq, 4, LANES, LANES), F32)],
        scratch_shapes=[pltpu.VMEM((4, LANES, LANES), F32), pltpu.VMEM((RET_HEADS, blk_len, blk_len), F32),
                        pltpu.VMEM((2 * RET_HEADS, blk_len, LANES), F32)],
        input_output_aliases=aliases,
        compiler_params=_params(("arbitrary", "arbitrary")),
        name="retention",
    )(*args)


def _mem_attn_kernel(q_ref, k_ref, v_ref, *refs, heads):
    o_ref = refs[-1]
    cols = [slice(h * MEM_HD, (h + 1) * MEM_HD) for h in range(heads)]
    scores = [lax.dot_general(q_ref[:, c], k_ref[0, :, c].astype(BF16), NT, preferred_element_type=F32)
              * (MEM_HD ** -0.5) for c in cols]
    probs = []
    for s in scores:
        e = jnp.exp(s - jnp.max(s, -1, keepdims=True))
        probs.append((e.astype(BF16), jnp.sum(e, -1, keepdims=True)))
    for c, (e, den) in zip(cols, probs):
        o_ref[:, c] = (jnp.dot(e, v_ref[0, :, c].astype(BF16), preferred_element_type=F32) / den
                       ).astype(o_ref.dtype)


def mem_attention(qm, k_src, v_src, layer, o_all=None, *, row0, n_rows, tm, heads, rows_per_kv):
    rb0 = row0 // tm
    w = heads * MEM_HD
    kv = pl.BlockSpec((None, 1, MEM_LEN, w), lambda i, h: (layer, i * tm // rows_per_kv, 0, h))
    in_specs = [pl.BlockSpec((tm, w), lambda i, h: (rb0 + i, h)), kv, kv]
    args = [qm, k_src, v_src]
    aliases = {}
    if o_all is not None:
        in_specs.append(pl.BlockSpec(memory_space=pl.ANY))
        args.append(o_all)
        aliases = {3: 0}
    return pl.pallas_call(
        functools.partial(_mem_attn_kernel, heads=heads),
        grid=(n_rows // tm, MEM_HEADS // heads),
        in_specs=in_specs,
        out_specs=pl.BlockSpec((tm, w), lambda i, h: (rb0 + i, h)),
        out_shape=jax.ShapeDtypeStruct((ROWS, D_MODEL), BF16),
        input_output_aliases=aliases,
        compiler_params=_params(("parallel", "parallel")),
        name="mem_attention",
    )(*args)


def _gelu(x):
    return 0.5 * x * (1.0 + lax.erf(x * (2.0 ** -0.5)))


def _ffn_in_kernel(*refs, sample, seq_tiles, tail):
    i = pl.program_id(1)
    if sample:
        a_ref, bg_ref, bv_ref, wg_ref, wv_ref, cbg_ref, cbv_ref, f1g_ref, f1v_ref, f2g_ref, f2v_ref = refs[:11]
        h_ref, tg_ref, tv_ref = refs[-3:]
    else:
        a_ref, ah_ref, wfg_ref, wfv_ref, wg_ref, wv_ref, cbg_ref, cbv_ref, side_ref = refs[:9]
        h_ref, tg_ref, tv_ref, side_out_ref, bg_ref, bv_ref = refs[-6:]
        side_out_ref[...] = side_ref[...].astype(BF16)

        @pl.when(i == 0)
        def _():
            bg_ref[...] = wfg_ref[...].astype(BF16)
            bv_ref[...] = wfv_ref[...].astype(BF16)

    tm = a_ref.shape[0]
    a = a_ref[...]
    if sample:
        pos = lax.broadcasted_iota(jnp.int32, (tm, 1), 0) % DEC_SEQ
    else:
        top = lax.broadcasted_iota(jnp.int32, (8, 1), 0)
        keep = (i % seq_tiles != 0).astype(F32)

    def project(cols, b_ref):
        u = jnp.dot(a, b_ref[:, cols], preferred_element_type=F32)
        uh = None if sample else jnp.dot(ah_ref[...], b_ref[:, cols], preferred_element_type=F32)
        return u, uh

    def conv(cols, u, uh, w_ref, cb_ref, f1_ref, f2_ref, t_ref):
        t_ref[0, :, cols] = u[tm - tail:, :]
        r1 = pltpu.roll(u, 1, 0)
        r2 = pltpu.roll(u, 2, 0)
        if sample:
            u1 = jnp.where(pos < 1, f1_ref[:, cols], r1)
            u2 = jnp.where(pos < 2, f2_ref[:, cols], r2)
        else:
            n = uh.shape[0]
            prev1, prev2 = uh[n - 1:n, :] * keep, uh[n - 2:n - 1, :] * keep
            u1 = jnp.concatenate([jnp.where(top < 1, prev1, r1[:8]), r1[8:]], 0)
            top2 = jnp.where(top < 1, prev2, jnp.where(top < 2, prev1, r2[:8]))
            u2 = jnp.concatenate([top2, r2[8:]], 0)
        w = w_ref[:, cols]
        return w[0:1, :] * u2 + w[1:2, :] * u1 + w[2:3, :] * u + cb_ref[:, cols]

    sub = [slice(c0, c0 + MXU_COLS) for c0 in range(0, h_ref.shape[1], MXU_COLS)]
    prods = [(project(cols, bg_ref), project(cols, bv_ref)) for cols in sub]
    for cols, ((ug, uhg), (uv, uhv)) in zip(sub, prods):
        if sample:
            cg = conv(cols, ug, uhg, wg_ref, cbg_ref, f1g_ref, f2g_ref, tg_ref)
            cv = conv(cols, uv, uhv, wv_ref, cbv_ref, f1v_ref, f2v_ref, tv_ref)
        else:
            cg = conv(cols, ug, uhg, wg_ref, cbg_ref, None, None, tg_ref)
            cv = conv(cols, uv, uhv, wv_ref, cbv_ref, None, None, tv_ref)
        h_ref[:, cols] = (_gelu(cg) * cv).astype(h_ref.dtype)


def ffn_in(x, w, conv_w, conv_b, layer, *, row0, n_rows, tm, tn, tail, side_cast=None, sample_state=None):
    sample = sample_state is not None
    k = x.shape[1]
    nj = D_FF // tn
    ni = n_rows // tm
    rb0 = row0 // tm
    halo = BF16_ROWS
    a_spec = pl.BlockSpec((tm, k), lambda j, i: (rb0 + i, 0))
    wg = pl.BlockSpec((None, CONV_W, tn), lambda j, i: (layer, 0, j))
    wv = pl.BlockSpec((None, CONV_W, tn), lambda j, i: (layer, 0, nj + j))
    cg = pl.BlockSpec((None, 1, tn), lambda j, i: (layer, 0, j))
    cv = pl.BlockSpec((None, 1, tn), lambda j, i: (layer, 0, nj + j))
    conv_b3 = conv_b.reshape(DEPTH, 1, 2 * D_FF)
    w_half = pl.BlockSpec((k, tn), lambda j, i: (0, j))
    t_spec = pl.BlockSpec((1, tail, tn), lambda j, i: (i, 0, j))
    out_specs = [pl.BlockSpec((tm, tn), lambda j, i: (rb0 + i, j)), t_spec, t_spec]
    out_shape = [jax.ShapeDtypeStruct((ROWS, D_FF), BF16),
                 jax.ShapeDtypeStruct((ni, tail, D_FF), F32),
                 jax.ShapeDtypeStruct((ni, tail, D_FF), F32)]
    aliases = {}
    if sample:
        fix1, fix2, h_all = sample_state
        fg = pl.BlockSpec((tm, tn), lambda j, i: (i, j))
        fv = pl.BlockSpec((tm, tn), lambda j, i: (i, nj + j))
        in_specs = [a_spec, w_half, w_half, wg, wv, cg, cv, fg, fv, fg, fv, pl.BlockSpec(memory_space=pl.ANY)]
        args = [x, w[0], w[1], conv_w, conv_w, conv_b3, conv_b3, fix1, fix1, fix2, fix2, h_all]
        aliases = {len(args) - 1: 0}
    else:
        rows, cols = side_cast.shape[1:]
        slab = rows // (nj * ni)
        assert slab * nj * ni == rows and slab % BF16_ROWS == 0
        ah = pl.BlockSpec((halo, k), lambda j, i: (jnp.maximum((rb0 + i) * (tm // halo) - 1, 0), 0))
        in_specs = [a_spec, ah,
                    pl.BlockSpec((None, k, tn), lambda j, i: (layer, 0, j)),
                    pl.BlockSpec((None, k, tn), lambda j, i: (layer, 0, nj + j)),
                    wg, wv, cg, cv,
                    pl.BlockSpec((None, slab, cols), lambda j, i: (layer, j * ni + i, 0))]
        args = [x, x, w, w, conv_w, conv_w, conv_b3, conv_b3, side_cast]
        out_specs += [pl.BlockSpec((slab, cols), lambda j, i: (j * ni + i, 0)), w_half, w_half]
        out_shape += [jax.ShapeDtypeStruct((rows, cols), BF16),
                      jax.ShapeDtypeStruct((k, D_FF), BF16), jax.ShapeDtypeStruct((k, D_FF), BF16)]
    return pl.pallas_call(
        functools.partial(_ffn_in_kernel, sample=sample, seq_tiles=SEQ // tm if not sample else 1, tail=tail),
        grid=(nj, ni),
        in_specs=in_specs,
        out_specs=out_specs,
        out_shape=out_shape,
        input_output_aliases=aliases,
        compiler_params=_params(("parallel", "arbitrary")),
        name="ffn_in",
    )(*args)


def _t5_bucket(rel):
    nb = T5_BUCKETS // 2
    max_exact = nb // 2
    n = jnp.abs(rel)
    nf = jnp.maximum(n, 1).astype(F32)
    large = max_exact + (jnp.log(nf / max_exact) / math.log(T5_MAX_DIST / max_exact)
                         * (nb - max_exact)).astype(jnp.int32)
    large = jnp.minimum(large, nb - 1)
    return jnp.where(rel > 0, nb, 0) + jnp.where(n < max_exact, n, large)


def _t5_bias(table, rel):
    return jnp.transpose(table[_t5_bucket(rel)], (2, 0, 1)).astype(F32)


def _clipped_bias(table, rel):
    return table[:, jnp.clip(rel, -BAND_MAX_REL, BAND_MAX_REL) + BAND_MAX_REL].astype(F32)


def _toeplitz_diag(bias_fn, n_rows, n_cols, rel0, n):
    assert n_cols + n_rows - 1 <= n
    k = np.arange(n)
    rel = np.where(k < n_cols, k, k - n) + rel0
    return jnp.broadcast_to(bias_fn(jnp.asarray(rel)[None, :]), (N_HEADS, 8, n))


def _band_diag(bias_fn, n_back):
    n_prev = n_back * CHUNK
    return _toeplitz_diag(lambda d: bias_fn(-d), n_prev + QB, QB, n_prev, 1024)


def _cache_diag(bias_fn, cache_len):
    n = max(256, cache_len * 2)
    return _toeplitz_diag(bias_fn, DEC_SEQ, cache_len, -cache_len, n)


def _rope_tables(pos):
    half = RET_DK // 2
    inv = ROPE_BASE ** (-jnp.arange(half, dtype=F32) / half)
    ang = pos.astype(F32)[:, None] * inv[None, :]
    cos, sin = jnp.cos(ang), jnp.sin(ang)
    cos_t = jnp.concatenate([cos, cos, cos, cos], -1)
    sin_t = jnp.concatenate([-sin, sin, -sin, sin], -1)
    return cos_t, sin_t


def _dup_groups(t):
    g0, g1 = t[..., 0, :], t[..., 1, :]
    return jnp.concatenate([g0, g0, g1, g1], -1)


def _undup(t):
    return jnp.stack([t[:, 0:64], t[:, 128:192]], 1)


def kernel(x_prompt, x_sample, mem_prompt, cache_swa_k, cache_swa_v, state_ret, cache_band_k, cache_band_v, state_ffn_conv, cache_mem_k, cache_mem_v, w_in, t5_table, swa_sink, band_rel_table, w_br_a, w_br_b, w_br_c, w_mix_o, ln1_g, ln1_b, w_mq, w_mk, w_mv, w_mo, ln2_g, ln2_b, w_ffn_in, ffn_conv_w, ffn_conv_b, w_ffn_out, ln3_g, ln3_b):
    x, xb = pack_rows(x_prompt, x_sample)
    memb = mem_prompt.reshape(BATCH * MEM_LEN, D_MODEL).astype(BF16)

    t5 = functools.partial(_t5_bias, t5_table)
    diag_a = _band_diag(t5, SWA_BACK)
    qpos = PAST_LEN + jnp.arange(DEC_SEQ)
    la, lc = cache_swa_k.shape[2], cache_band_k.shape[2]
    rel_n = qpos[None, :] - qpos[:, None]
    diag_a_c, bias_a_n = _cache_diag(t5, la), t5(rel_n)
    cos_p, sin_p = _rope_tables(jnp.arange(SEQ))
    cos_s, sin_s = _rope_tables(qpos)
    zero_state = jnp.zeros((1, BATCH, 4, LANES, LANES), F32)
    no_sink = jnp.zeros((N_HEADS,), F32)

    w_in_b = permute_w_in(w_in)
    swa_k2, swa_v2 = _dup_groups(cache_swa_k), _dup_groups(cache_swa_v)
    band_k2 = cache_band_k.reshape(DEPTH, DEC_BATCH, lc, BAND_HEADS * HEAD_DIM)
    band_v2 = cache_band_v.reshape(DEPTH, DEC_BATCH, lc, BAND_HEADS * HEAD_DIM)
    mem_k2 = cache_mem_k.reshape(DEPTH, DEC_BATCH, MEM_LEN, D_MODEL)
    mem_v2 = cache_mem_v.reshape(DEPTH, DEC_BATCH, MEM_LEN, D_MODEL)
    ret_s0 = state_ret.reshape(DEPTH, DEC_BATCH, 4, LANES, LANES)

    outs = {k: [] for k in ("p_ak", "p_av", "p_rs", "p_bk", "p_bv", "p_fc", "p_mk", "p_mv",
                            "s_ak", "s_av", "s_rs", "s_bk", "s_bv", "s_fc")}
    for l in range(DEPTH):
        proj, w_br_a_b, w_br_b_b, w_br_c_b, w_mix_o_b, w_mo_b = matmul(
            xb, w_in_b, l, tm=1056, tn=1536, out_dtype=BF16, side_casts=(w_br_a, w_br_b, w_br_c, w_mix_o, w_mo))
        clipped = functools.partial(_clipped_bias, band_rel_table[l])
        oa = band_attention_prompt(proj, diag_a, swa_sink[l], q_col=COL_QA, k_col=COL_KA2, v_col=COL_VA2,
                                   shared_kv=True, n_back=SWA_BACK, has_sink=True, skew=None)
        oc = band_attention_prompt(proj, _band_diag(clipped, BAND_BACK), no_sink, q_col=COL_QC, k_col=COL_KC,
                                   v_col=COL_VC, shared_kv=False, n_back=BAND_BACK, has_sink=False, skew=1)
        ob, rs_p = retention(proj, cos_p, sin_p, zero_state, 0, row0=0, n_seq=BATCH, seq_len=SEQ,
                             blk_len=RET_L, pos_per_blk=True)
        oa = cached_attention_sample(
            proj, swa_k2, swa_v2, l, diag_a_c, bias_a_n, swa_sink[l], oa,
            q_col=COL_QA, k_col=COL_KA2, v_col=COL_VA2, shared_kv=True, has_sink=True)
        oc = cached_attention_sample(
            proj, band_k2, band_v2, l, _cache_diag(clipped, lc), clipped(rel_n), no_sink, oc,
            q_col=COL_QC, k_col=COL_KC, v_col=COL_VC, shared_kv=False, has_sink=False)
        ob, rs_s = retention(proj, cos_s, sin_s, ret_s0, l, ob, row0=P_ROWS, n_seq=DEC_BATCH, seq_len=DEC_SEQ,
                             blk_len=DEC_SEQ, pos_per_blk=False)
        mix = gated_branch_sum(oa, ob, oc, w_br_a_b, w_br_b_b, w_br_c_b, proj)
        x, xb = matmul_residual_ln(mix, w_mix_o_b, l, x, ln1_g, ln1_b, tm=528)

        mk = matmul(memb, w_mk, l, tm=512, tn=1024, out_dtype=F32)
        mv = matmul(memb, w_mv, l, tm=512, tn=1024, out_dtype=F32)
        qm = matmul(xb, w_mq, l, tm=1056, tn=2048, out_dtype=BF16)
        om = mem_attention(qm, mk.reshape(1, BATCH, MEM_LEN, D_MODEL), mv.reshape(1, BATCH, MEM_LEN, D_MODEL), 0,
                           row0=0, n_rows=P_ROWS, tm=1024, heads=MEM_HEADS, rows_per_kv=SEQ)
        om = mem_attention(qm, mem_k2, mem_v2, l, om, row0=P_ROWS, n_rows=S_ROWS, tm=DEC_SEQ,
                           heads=MEM_HEADS, rows_per_kv=DEC_SEQ)
        x, xb = matmul_residual_ln(om, w_mo_b, l, x, ln2_g, ln2_b, tm=528)

        h, tg_p, tv_p, w_ffn_out_b, wfg_b, wfv_b = ffn_in(
            xb, w_ffn_in, ffn_conv_w, ffn_conv_b, l, row0=0, n_rows=P_ROWS, tm=1024, tn=512, tail=8,
            side_cast=w_ffn_out)
        st = state_ffn_conv[l]
        fix1 = jnp.pad(st[:, 1:2], ((0, 0), (0, DEC_SEQ - 1), (0, 0))).reshape(S_ROWS, 2 * D_FF)
        fix2 = jnp.pad(st, ((0, 0), (0, DEC_SEQ - 2), (0, 0))).reshape(S_ROWS, 2 * D_FF)
        h, tg_s, tv_s = ffn_in(xb, (wfg_b, wfv_b), ffn_conv_w, ffn_conv_b, l, row0=P_ROWS, n_rows=S_ROWS,
                               tm=S_ROWS, tn=512, tail=S_ROWS, sample_state=(fix1, fix2, h))
        if l < DEPTH - 1:
            x, xb = matmul_residual_ln(h, w_ffn_out_b, l, x, ln3_g, ln3_b, tm=S_ROWS)
        else:
            y_prompt, y_sample = matmul_residual_ln(h, w_ffn_out_b, l, x, ln3_g, ln3_b, tm=S_ROWS,
                                                    split_rows=P_ROWS)

        sf = proj[P_ROWS:]
        la_p, lc_p = min(SWA_BACK * CHUNK, SEQ), min(BAND_BACK * CHUNK, SEQ)
        seq_tail = lambda n, c0, w: jnp.stack(
            [proj[(b + 1) * SEQ - n:(b + 1) * SEQ, c0:c0 + w] for b in range(BATCH)], 0)
        outs["p_ak"].append(_undup(seq_tail(la_p, COL_KA2, 256).reshape(BATCH * la_p, 256))
                            .reshape(BATCH, la_p, 2, 64).astype(F32))
        outs["p_av"].append(_undup(seq_tail(la_p, COL_VA2, 256).reshape(BATCH * la_p, 256))
                            .reshape(BATCH, la_p, 2, 64).astype(F32))
        outs["p_rs"].append(rs_p.reshape(BATCH, RET_HEADS, RET_DK, RET_DV))
        outs["p_bk"].append(seq_tail(lc_p, COL_KC, 1024).reshape(BATCH, lc_p, 16, 64).astype(F32))
        outs["p_bv"].append(seq_tail(lc_p, COL_VC, 1024).reshape(BATCH, lc_p, 16, 64).astype(F32))
        last = [(b + 1) * (SEQ // 1024) - 1 for b in range(BATCH)]
        outs["p_fc"].append(jnp.stack(
            [jnp.concatenate([tg_p[t, 6:8], tv_p[t, 6:8]], -1) for t in last], 0))
        outs["p_mk"].append(mk.reshape(BATCH, MEM_LEN, MEM_HEADS, MEM_HD))
        outs["p_mv"].append(mv.reshape(BATCH, MEM_LEN, MEM_HEADS, MEM_HD))
        outs["s_ak"].append(_undup(sf[:, COL_KA2:COL_KA2 + 256]).reshape(DEC_BATCH, DEC_SEQ, 2, 64).astype(F32))
        outs["s_av"].append(_undup(sf[:, COL_VA2:COL_VA2 + 256]).reshape(DEC_BATCH, DEC_SEQ, 2, 64).astype(F32))
        outs["s_rs"].append(rs_s.reshape(DEC_BATCH, RET_HEADS, RET_DK, RET_DV))
        outs["s_bk"].append(sf[:, COL_KC:COL_KC + 1024].reshape(DEC_BATCH, DEC_SEQ, 16, 64).astype(F32))
        outs["s_bv"].append(sf[:, COL_VC:COL_VC + 1024].reshape(DEC_BATCH, DEC_SEQ, 16, 64).astype(F32))
        u_s = jnp.concatenate([tg_s[0], tv_s[0]], -1).reshape(DEC_BATCH, DEC_SEQ, 2 * D_FF)
        outs["s_fc"].append(u_s[:, DEC_SEQ - 2:])

    st = lambda name: jnp.stack(outs[name], 0)
    return (y_prompt.reshape(BATCH, SEQ, D_MODEL), y_sample[:S_ROWS].reshape(DEC_BATCH, DEC_SEQ, D_MODEL),
            st("p_ak"), st("p_av"), st("p_rs"), st("p_bk"), st("p_bv"), st("p_fc"), st("p_mk"), st("p_mv"),
            st("s_ak"), st("s_av"), st("s_rs"), st("s_bk"), st("s_bv"), st("s_fc"))
```

```python
import functools
import math

import numpy as np
import jax
import jax.numpy as jnp
from jax import lax
from jax.experimental import pallas as pl
from jax.experimental.pallas import tpu as pltpu

F32 = jnp.float32
BF16 = jnp.bfloat16

D_MODEL = 2048
BATCH = 2
SEQ = 4096
DEPTH = 2
DEC_BATCH = 16
DEC_SEQ = 16
PAST_LEN = 2048
CHUNK = 64
HEAD_DIM = 64
SWA_BACK = 2
SWA_HEADS = 16
SWA_KV_HEADS = 2
N_HEADS = SWA_HEADS
ATT_W = N_HEADS * HEAD_DIM
T5_BUCKETS = 32
T5_MAX_DIST = 128
RET_HEADS = 8
RET_DK = 64
RET_DV = 128
ROPE_BASE = 10000.0
RET_NORM_EPS = 1e-5
BAND_BACK = 8
BAND_HEADS = 16
BAND_MAX_REL = 256
MEM_LEN = 256
MEM_HEADS = 4
MEM_HD = D_MODEL // MEM_HEADS
D_FF = 5632
CONV_W = 3
DN_ALPHA = (2 * DEPTH) ** 0.25
LN_EPS = 1e-5

P_ROWS = BATCH * SEQ
S_ROWS = DEC_BATCH * DEC_SEQ
ROWS = P_ROWS + S_ROWS

COL_QA = 0
COL_QC = 1024
COL_KC = 2048
COL_VC = 3072
COL_VB = 4096
COL_GR = 5120
COL_GA = 6144
COL_GB = 8192
COL_GC = 10240
COL_QB = 12288
COL_KB = 12800
COL_KA2 = 13312
COL_VA2 = 13568
PROJ_COLS = 13824

LANES = 128
BF16_ROWS = 16
MXU_COLS = 256
QB = 256
RET_L = 256
NEG = -1e30
LOG2E = math.log2(math.e)
VMEM_LIMIT = 48 * 1024 * 1024

NT = (((1,), (1,)), ((), ()))
TN = (((0,), (0,)), ((), ()))


def _params(sem, vmem=VMEM_LIMIT):
    return pltpu.CompilerParams(dimension_semantics=sem, vmem_limit_bytes=vmem)


def _pack_rows_kernel(p_ref, s_ref, of_ref, ob_ref, *, n_prompt_tiles):
    src = jnp.where(pl.program_id(0) < n_prompt_tiles, p_ref[...], s_ref[...])
    of_ref[...] = src
    ob_ref[...] = src.astype(BF16)


def pack_rows(x_prompt, x_sample):
    tm = S_ROWS
    npt = P_ROWS // tm
    return pl.pallas_call(
        functools.partial(_pack_rows_kernel, n_prompt_tiles=npt),
        grid=(npt + 1,),
        in_specs=[pl.BlockSpec((tm, D_MODEL), lambda i: (jnp.minimum(i, npt - 1), 0)),
                  pl.BlockSpec((tm, D_MODEL), lambda i: (0, 0))],
        out_specs=[pl.BlockSpec((tm, D_MODEL), lambda i: (i, 0)), pl.BlockSpec((tm, D_MODEL), lambda i: (i, 0))],
        out_shape=[jax.ShapeDtypeStruct((ROWS, D_MODEL), F32), jax.ShapeDtypeStruct((ROWS, D_MODEL), BF16)],
        compiler_params=_params(("parallel",)),
        name="pack_rows",
    )(x_prompt.reshape(P_ROWS, D_MODEL), x_sample.reshape(S_ROWS, D_MODEL))


_W_IN_MOVES = ((0, COL_QA, 1024), (4352, COL_QC, 1024), (5376, COL_KC, 1024), (6400, COL_VC, 1024),
               (2304, COL_VB, 1024), (3328, COL_GR, 1024), (7424, COL_GA, 2048), (9472, COL_GB, 2048),
               (11520, COL_GC, 2048), (1280, COL_QB, 512), (1792, COL_KB, 512))
_W_IN_DUPS = ((1024, COL_KA2), (1152, COL_VA2))


def _w_in_permute_kernel(src_ref, o_ref):
    for src, dst, width in _W_IN_MOVES:
        o_ref[:, dst:dst + width] = src_ref[:, src:src + width].astype(BF16)
    for src, dst in _W_IN_DUPS:
        pair = src_ref[:, src:src + LANES]
        swapped = pltpu.roll(pair, HEAD_DIM, 1)
        o_ref[:, dst:dst + LANES] = jnp.where(_head_lanes(0), pair, swapped).astype(BF16)
        o_ref[:, dst + LANES:dst + 2 * LANES] = jnp.where(_head_lanes(0), swapped, pair).astype(BF16)


def permute_w_in(w_in, *, rows=128):
    depth, d, in_cols = w_in.shape
    assert sum(w for _, _, w in _W_IN_MOVES) + 2 * LANES == in_cols
    return pl.pallas_call(
        _w_in_permute_kernel,
        grid=(depth, d // rows),
        in_specs=[pl.BlockSpec((None, rows, in_cols), lambda l, r: (l, r, 0))],
        out_specs=pl.BlockSpec((None, rows, PROJ_COLS), lambda l, r: (l, r, 0)),
        out_shape=jax.ShapeDtypeStruct((depth, d, PROJ_COLS), BF16),
        compiler_params=_params(("parallel", "parallel")),
        name="permute_w_in",
    )(w_in)


def _mm_kernel(a_ref, b_ref, *refs, n_side, resident):
    side_in, o_ref, side_out = refs[:n_side], refs[n_side], refs[n_side + 1:2 * n_side + 1]
    for src, dst in zip(side_in, side_out):
        dst[...] = src[...].astype(BF16)
    if resident:
        bb_ref = refs[-1]

        @pl.when(pl.program_id(0) == 0)
        def _():
            bb_ref[...] = b_ref[...].astype(BF16)

        b = bb_ref[...]
    else:
        b = b_ref[...].astype(BF16)
    o_ref[...] = jnp.dot(a_ref[...], b, preferred_element_type=F32).astype(o_ref.dtype)


def matmul(a, b, layer, *, tm, tn, out_dtype, side_casts=()):
    m, k = a.shape
    n = b.shape[2]
    ni, nj = m // tm, n // tn
    resident = b.dtype == F32 and ni > 1
    assert b.dtype == BF16 or ni == 1 or nj == 1
    b_mode = dict(pipeline_mode=pl.Buffered(1)) if resident else {}
    in_specs = [pl.BlockSpec((tm, k), lambda i, j: (i, 0)),
                pl.BlockSpec((None, k, tn), lambda i, j: (layer, 0, j), **b_mode)]
    out_specs = [pl.BlockSpec((tm, tn), lambda i, j: (i, j))]
    out_shape = [jax.ShapeDtypeStruct((m, n), out_dtype)]
    n_slabs = ni * (nj - 1)
    slab_of = lambda i, j: i * (nj - 1) + jnp.minimum(j, nj - 2)
    for w in side_casts:
        rows, cols = w.shape[1:]
        slab = rows // n_slabs
        assert slab * n_slabs == rows and slab % BF16_ROWS == 0
        in_specs.append(pl.BlockSpec((None, slab, cols), lambda i, j: (layer, slab_of(i, j), 0)))
        out_specs.append(pl.BlockSpec((slab, cols), lambda i, j: (slab_of(i, j), 0)))
        out_shape.append(jax.ShapeDtypeStruct((rows, cols), BF16))
    sem = ("arbitrary", "arbitrary") if resident or side_casts else ("parallel", "parallel")
    outs = pl.pallas_call(
        functools.partial(_mm_kernel, n_side=len(side_casts), resident=resident),
        grid=(ni, nj),
        in_specs=in_specs,
        out_specs=out_specs,
        out_shape=out_shape,
        scratch_shapes=[pltpu.VMEM((k, tn), BF16)] if resident else [],
        compiler_params=_params(sem),
        name="matmul",
    )(a, b, *side_casts)
    return outs if side_casts else outs[0]


def _gate_mm_kernel(oa_ref, ob_ref, oc_ref, wa_ref, wb_ref, wc_ref, ga_ref, gb_ref, gc_ref, o_ref):
    acc = None
    for o, w, g in ((oa_ref, wa_ref, ga_ref), (ob_ref, wb_ref, gb_ref), (oc_ref, wc_ref, gc_ref)):
        t = jnp.dot(o[...], w[...], preferred_element_type=F32) * jax.nn.sigmoid(g[...].astype(F32))
        acc = t if acc is None else acc + t
    o_ref[...] = acc.astype(o_ref.dtype)


def gated_branch_sum(oa, ob, oc, wa, wb, wc, proj, *, tm=528):
    m, k = oa.shape
    n = wa.shape[1]
    o_spec = pl.BlockSpec((tm, k), lambda i: (i, 0))
    w_spec = pl.BlockSpec((k, n), lambda i: (0, 0), pipeline_mode=pl.Buffered(1))

    def g_spec(col):
        return pl.BlockSpec((tm, n), lambda i: (i, col // n))

    return pl.pallas_call(
        _gate_mm_kernel,
        grid=(m // tm,),
        in_specs=[o_spec, o_spec, o_spec, w_spec, w_spec, w_spec,
                  g_spec(COL_GA), g_spec(COL_GB), g_spec(COL_GC)],
        out_specs=pl.BlockSpec((tm, n), lambda i: (i, 0)),
        out_shape=jax.ShapeDtypeStruct((m, n), BF16),
        compiler_params=_params(("parallel",)),
        name="gated_branch_sum",
    )(oa, ob, oc, wa, wb, wc, proj, proj, proj)


def _mm_ln_kernel(a_ref, b_ref, r_ref, g_ref, bt_ref, o1_ref, o2_ref, acc_ref, *, split):
    @pl.when(pl.program_id(0) == 0)
    def _():
        acc_ref[...] = jnp.zeros_like(acc_ref)

    y = DN_ALPHA * r_ref[...] + acc_ref[...]
    mu = jnp.mean(y, -1, keepdims=True)
    d = y - mu
    var = jnp.mean(d * d, -1, keepdims=True)
    out = d * lax.rsqrt(var + LN_EPS) * g_ref[...] + bt_ref[...]
    o1_ref[...] = out
    o2_ref[...] = out if split else out.astype(BF16)
    acc_ref[...] = jnp.dot(a_ref[...], b_ref[...], preferred_element_type=F32)


def matmul_residual_ln(a, b, layer, resid, gain, bias, *, tm, split_rows=None):
    m, kk = a.shape
    n = b.shape[-1]
    nt = m // tm
    if b.ndim == 3:
        b_spec = pl.BlockSpec((None, kk, n), lambda i: (layer, 0, 0), pipeline_mode=pl.Buffered(1))
    else:
        b_spec = pl.BlockSpec((kk, n), lambda i: (0, 0), pipeline_mode=pl.Buffered(1))
    if split_rows is None:
        cur = lambda i: (jnp.minimum(i, nt - 1), 0)
        prev = lambda i: (jnp.maximum(i - 1, 0), 0)
        out_specs = [pl.BlockSpec((tm, n), prev), pl.BlockSpec((tm, n), prev)]
        out_shape = [jax.ShapeDtypeStruct((m, n), F32), jax.ShapeDtypeStruct((m, n), BF16)]
    else:
        assert split_rows == (nt - 1) * tm
        cur = lambda i: (jnp.where(i == 0, nt - 1, jnp.minimum(i - 1, nt - 2)), 0)
        prev = lambda i: (jnp.where(i <= 1, nt - 1, i - 2), 0)
        out_specs = [pl.BlockSpec((tm, n), lambda i: (jnp.clip(i - 2, 0, nt - 2), 0)),
                     pl.BlockSpec((tm, n), lambda i: (jnp.where(i <= 1, 0, 1), 0))]
        out_shape = [jax.ShapeDtypeStruct((split_rows, n), F32), jax.ShapeDtypeStruct((2 * tm, n), F32)]
    return pl.pallas_call(
        functools.partial(_mm_ln_kernel, split=split_rows is not None),
        grid=(nt + 1,),
        in_specs=[pl.BlockSpec((tm, kk), cur),
                  b_spec,
                  pl.BlockSpec((tm, n), prev),
                  pl.BlockSpec((None, 1, n), lambda i: (layer, 0, 0)),
                  pl.BlockSpec((None, 1, n), lambda i: (layer, 0, 0))],
        out_specs=out_specs,
        out_shape=out_shape,
        scratch_shapes=[pltpu.VMEM((tm, n), F32)],
        compiler_params=_params(("arbitrary",)),
        name="matmul_residual_ln",
    )(a, b, resid, gain.reshape(DEPTH, 1, n), bias.reshape(DEPTH, 1, n))


def _head_lanes(j):
    lane = lax.broadcasted_iota(jnp.int32, (1, LANES), 1)
    return (lane < HEAD_DIM) if j == 0 else (lane >= HEAD_DIM)


def _band_attn_kernel(sink_ref, diag_ref, q_ref, *refs, n_prev_blocks, n_back, shared_kv, has_sink, skew):
    nk = n_prev_blocks + 1
    k_refs = refs[:nk]
    v_refs = refs[nk:2 * nk]
    o_ref = refs[2 * nk]
    bias_ref = refs[2 * nk + 1]
    b = pl.program_id(0)
    i = pl.program_id(1)
    n_prev_rows = n_back * CHUNK
    kw = n_prev_rows + QB

    @pl.when((b == 0) & (i == 0))
    def _():
        n = diag_ref.shape[2]
        cb = lax.broadcasted_iota(jnp.int32, (kw, QB), 0) // CHUNK
        qc = lax.broadcasted_iota(jnp.int32, (kw, QB), 1) // CHUNK
        allowed = (cb >= qc) & (cb - n_back <= qc)
        for h in range(N_HEADS):
            t = pltpu.roll(jnp.broadcast_to(diag_ref[h, 0:1, :], (kw, n)), 0, 1, stride=1, stride_axis=0)
            bias_ref[h] = jnp.where(allowed, t[:, :QB] * LOG2E, NEG)

    krow = lax.broadcasted_iota(jnp.int32, (kw, 1), 0)
    kmask = jnp.where(krow >= jnp.maximum(n_prev_rows - i * QB, 0), 0.0, NEG).astype(BF16)
    lane = lax.broadcasted_iota(jnp.int32, (1, LANES), 1)

    def kv_cols(h):
        c0 = (h // 8 if shared_kv else h // 2) * LANES
        return slice(c0, c0 + LANES)

    def scores(h):
        p, j = divmod(h, 2)
        q = q_ref[:, p * LANES:(p + 1) * LANES] * (HEAD_DIM ** -0.5 * LOG2E)
        k_all = jnp.concatenate([r[:, kv_cols(h)] for r in k_refs], axis=0)
        mask_lane = lane == (HEAD_DIM if j == 0 else 0)
        qj = jnp.where(mask_lane, jnp.ones_like(q), jnp.where(_head_lanes(j), q, jnp.zeros_like(q)))
        kj = jnp.where(mask_lane, kmask, k_all)
        return lax.dot_general(kj, qj, NT, preferred_element_type=F32)

    def softmax(h, s_all):
        sink = sink_ref[h] * LOG2E if has_sink else None
        ps, dens = [], []
        for c in range(QB // LANES):
            lanes = slice(c * LANES, (c + 1) * LANES)
            lo, hi = c * LANES, c * LANES + n_prev_rows + LANES
            s = s_all[lo:hi, lanes] + bias_ref[h, lo:hi, lanes]
            m = jnp.max(s, 0, keepdims=True)
            if has_sink:
                m = jnp.maximum(m, sink)
            e = jnp.exp2(s - m)
            den = jnp.sum(e, 0, keepdims=True)
            if has_sink:
                den = den + jnp.exp2(sink - m)
            parts = [e.astype(BF16)]
            if lo:
                parts.insert(0, jnp.zeros((lo, LANES), BF16))
            if hi < kw:
                parts.append(jnp.zeros((kw - hi, LANES), BF16))
            ps.append(jnp.concatenate(parts, 0))
            dens.append(den)
        return jnp.concatenate(ps, 1), jnp.concatenate(dens, 1)

    def weighted_values(h, p_all, den):
        v_all = jnp.concatenate([r[:, kv_cols(h)] for r in v_refs], axis=0)
        return lax.dot_general(v_all, p_all, TN, preferred_element_type=F32) / den

    lag = N_HEADS if skew is None else skew
    s_q, p_q, outs = {}, {}, {}
    for t in range(N_HEADS + 2 * lag):
        if t < N_HEADS:
            s_q[t] = scores(t)
        if lag <= t < N_HEADS + lag:
            p_q[t - lag] = softmax(t - lag, s_q.pop(t - lag))
        if t >= 2 * lag:
            h = t - 2 * lag
            outs[h] = weighted_values(h, *p_q.pop(h))
            if h % 2:
                p = h // 2
                pair = jnp.where(lax.broadcasted_iota(jnp.int32, (LANES, 1), 0) < HEAD_DIM,
                                 outs.pop(h - 1), outs.pop(h))
                o_ref[:, p * LANES:(p + 1) * LANES] = pair.T.astype(o_ref.dtype)


def band_attention_prompt(proj, diag, sink, *, q_col, k_col, v_col, shared_kv, n_back, has_sink, skew):
    n_prev_rows = n_back * CHUNK
    nqb = SEQ // QB
    if n_prev_rows >= QB:
        n_prev_blocks, pb = n_prev_rows // QB, QB
    else:
        n_prev_blocks, pb = 1, n_prev_rows
    per = QB // pb
    kvw = 2 * SWA_KV_HEADS * HEAD_DIM if shared_kv else ATT_W

    def prev_spec(col, back):
        return pl.BlockSpec(
            (pb, kvw), lambda b, i: (b * (SEQ // pb) + jnp.maximum(i * per - back, 0), col // kvw))

    def own_spec(col):
        return pl.BlockSpec((QB, kvw), lambda b, i: (b * nqb + i, col // kvw))

    k_specs = [prev_spec(k_col, n_prev_blocks - t) for t in range(n_prev_blocks)] + [own_spec(k_col)]
    v_specs = [prev_spec(v_col, n_prev_blocks - t) for t in range(n_prev_blocks)] + [own_spec(v_col)]
    kern = functools.partial(_band_attn_kernel, n_prev_blocks=n_prev_blocks, n_back=n_back,
                             shared_kv=shared_kv, has_sink=has_sink, skew=skew)
    n_in = 2 * (n_prev_blocks + 1)
    return pl.pallas_call(
        kern,
        grid=(BATCH, nqb),
        in_specs=[pl.BlockSpec(memory_space=pltpu.SMEM),
                  pl.BlockSpec(diag.shape, lambda b, i: (0, 0, 0)),
                  pl.BlockSpec((QB, ATT_W), lambda b, i: (b * nqb + i, q_col // ATT_W))]
                 + k_specs + v_specs,
        out_specs=pl.BlockSpec((QB, ATT_W), lambda b, i: (b * nqb + i, 0)),
        out_shape=jax.ShapeDtypeStruct((ROWS, ATT_W), BF16),
        scratch_shapes=[pltpu.VMEM((N_HEADS, n_prev_rows + QB, QB), F32)],
        compiler_params=_params(("arbitrary", "arbitrary")),
        name="band_attention_prompt",
    )(sink, diag, proj, *([proj] * n_in))


def _cached_attn_kernel(sink_ref, q_ref, kn_ref, vn_ref, kc_ref, vc_ref, diag_ref, bn_ref, _, o_ref, bc_ref,
                        *, shared_kv, has_sink):
    lc = kc_ref.shape[1]

    @pl.when(pl.program_id(0) == 0)
    def _():
        n = diag_ref.shape[2]
        for h in range(N_HEADS):
            t = pltpu.roll(jnp.broadcast_to(diag_ref[h, 0:1, :], (DEC_SEQ, n)), 0, 1, stride=1, stride_axis=0)
            bc_ref[h] = t[:, :lc]

    kv_cols = [(p // 4 if shared_kv else p) * LANES for p in range(8)]
    scores = []
    for p in range(8):
        c0 = kv_cols[p]
        q = q_ref[:, p * LANES:(p + 1) * LANES] * (HEAD_DIM ** -0.5)
        kc = kc_ref[0, :, c0:c0 + LANES].astype(BF16)
        kn = kn_ref[:, c0:c0 + LANES]
        for j in range(2):
            h = 2 * p + j
            qj = jnp.where(_head_lanes(j), q, jnp.zeros_like(q))
            scores.append((lax.dot_general(qj, kc, NT, preferred_element_type=F32) + bc_ref[h],
                           lax.dot_general(qj, kn, NT, preferred_element_type=F32) + bn_ref[h]))
    probs = []
    for h, (s_c, s_n) in enumerate(scores):
        m = jnp.maximum(jnp.max(s_c, -1, keepdims=True), jnp.max(s_n, -1, keepdims=True))
        if has_sink:
            m = jnp.maximum(m, sink_ref[h])
        e_c, e_n = jnp.exp(s_c - m), jnp.exp(s_n - m)
        den = jnp.sum(e_c, -1, keepdims=True) + jnp.sum(e_n, -1, keepdims=True)
        if has_sink:
            den = den + jnp.exp(sink_ref[h] - m)
        probs.append((e_c.astype(BF16), e_n.astype(BF16), den))
    for p in range(8):
        c0 = kv_cols[p]
        vc = vc_ref[0, :, c0:c0 + LANES].astype(BF16)
        vn = vn_ref[:, c0:c0 + LANES]
        outs = []
        for j in range(2):
            e_c, e_n, den = probs[2 * p + j]
            outs.append((jnp.dot(e_c, vc, preferred_element_type=F32)
                         + jnp.dot(e_n, vn, preferred_element_type=F32)) / den)
        o_ref[:, p * LANES:(p + 1) * LANES] = jnp.where(_head_lanes(0), outs[0], outs[1]).astype(o_ref.dtype)


def cached_attention_sample(proj, k_cache, v_cache, layer, diag_c, bias_n, sink, o_all, *, q_col, k_col, v_col,
                            shared_kv, has_sink):
    lc, wc = k_cache.shape[2], k_cache.shape[3]
    r0 = P_ROWS // DEC_SEQ
    kern = functools.partial(_cached_attn_kernel, shared_kv=shared_kv, has_sink=has_sink)
    cache = pl.BlockSpec((None, 1, lc, wc), lambda s: (layer, s, 0, 0))
    return pl.pallas_call(
        kern,
        grid=(DEC_BATCH,),
        in_specs=[pl.BlockSpec(memory_space=pltpu.SMEM),
                  pl.BlockSpec((DEC_SEQ, ATT_W), lambda s: (r0 + s, q_col // ATT_W)),
                  pl.BlockSpec((DEC_SEQ, wc), lambda s: (r0 + s, k_col // wc)),
                  pl.BlockSpec((DEC_SEQ, wc), lambda s: (r0 + s, v_col // wc)),
                  cache, cache,
                  pl.BlockSpec(diag_c.shape, lambda s: (0, 0, 0)),
                  pl.BlockSpec((N_HEADS, DEC_SEQ, DEC_SEQ), lambda s: (0, 0, 0)),
                  pl.BlockSpec(memory_space=pl.ANY)],
        out_specs=pl.BlockSpec((DEC_SEQ, ATT_W), lambda s: (r0 + s, 0)),
        out_shape=jax.ShapeDtypeStruct(o_all.shape, o_all.dtype),
        scratch_shapes=[pltpu.VMEM((N_HEADS, DEC_SEQ, lc), F32)],
        input_output_aliases={8: 0},
        compiler_params=_params(("arbitrary",)),
        name="cached_attention_sample",
    )(sink, proj, proj, proj, k_cache, v_cache, diag_c, bias_n, o_all)


def _ret_kernel(q_ref, k_ref, v_ref, g_ref, cos_ref, sin_ref, s0_ref, *refs, blk_len):
    o_ref, sout_ref, st_ref, dec_ref, rdec_ref = refs[-5:]
    blk = pl.program_id(1)
    log_gs = [math.log(1.0 - 2.0 ** (-5.0 - h)) for h in range(RET_HEADS)]

    @pl.when((pl.program_id(0) == 0) & (blk == 0))
    def _():
        ii = lax.broadcasted_iota(jnp.int32, (blk_len, blk_len), 0)
        jj = lax.broadcasted_iota(jnp.int32, (blk_len, blk_len), 1)
        diff = (ii - jj).astype(F32)
        row = lax.broadcasted_iota(jnp.int32, (blk_len, LANES), 0).astype(F32)
        for h in range(RET_HEADS):
            dec_ref[h] = jnp.where(diff >= 0, jnp.exp(log_gs[h] * jnp.maximum(diff, 0.0)), 0.0)
            rdec_ref[h] = jnp.exp(log_gs[h] * (row + 1.0))
            rdec_ref[RET_HEADS + h] = jnp.exp(log_gs[h] * (blk_len - 1.0 - row))

    @pl.when(blk == 0)
    def _():
        st_ref[...] = s0_ref[0]

    cos = cos_ref[...]
    sin = sin_ref[...]
    lane = lax.broadcasted_iota(jnp.int32, (1, LANES), 1)
    low_half = (lane % RET_DK) < (RET_DK // 2)

    def rope(x):
        x = x.astype(F32)
        swapped = jnp.where(low_half, pltpu.roll(x, LANES - RET_DK // 2, 1), pltpu.roll(x, RET_DK // 2, 1))
        return x * cos + swapped * sin

    srow = lax.broadcasted_iota(jnp.int32, (LANES, 1), 0)

    value = lambda h: v_ref[:, h * RET_DV:(h + 1) * RET_DV]
    pairs = []
    for p in range(RET_HEADS // 2):
        qr = rope(q_ref[:, p * LANES:(p + 1) * LANES])
        kr = rope(k_ref[:, p * LANES:(p + 1) * LANES]) * (RET_DK ** -0.5)
        pairs.append((qr, kr, kr.astype(BF16), st_ref[p]))
    heads = []
    for h in range(RET_HEADS):
        qr, kr, kb, state = pairs[h // 2]
        qj = jnp.where(_head_lanes(h % 2), qr, 0.0).astype(BF16)
        qk = lax.dot_general(qj, kb, NT, preferred_element_type=F32) * dec_ref[h]
        cross = jnp.dot(qj, state.astype(BF16), preferred_element_type=F32) * rdec_ref[h]
        kwj = jnp.where(_head_lanes(h % 2), kr * rdec_ref[RET_HEADS + h], 0.0).astype(BF16)
        heads.append((qk.astype(BF16), cross, kwj))
    updates = []
    for h, (qk, cross, kwj) in enumerate(heads):
        o = jnp.dot(qk, value(h), preferred_element_type=F32) + cross
        mu = jnp.mean(o, -1, keepdims=True)
        d = o - mu
        var = jnp.mean(d * d, -1, keepdims=True)
        gate = g_ref[:, h * RET_DV:(h + 1) * RET_DV].astype(F32)
        o_ref[:, h * RET_DV:(h + 1) * RET_DV] = (
            d * lax.rsqrt(var + RET_NORM_EPS) * (gate * jax.nn.sigmoid(gate))).astype(o_ref.dtype)
        updates.append(lax.dot_general(kwj, value(h), TN, preferred_element_type=F32))
    for p in range(RET_HEADS // 2):
        carry = jnp.where(srow < RET_DK, math.exp(log_gs[2 * p] * blk_len), math.exp(log_gs[2 * p + 1] * blk_len))
        st_ref[p] = carry * pairs[p][3] + updates[2 * p] + updates[2 * p + 1]

    @pl.when(blk == pl.num_programs(1) - 1)
    def _():
        sout_ref[0] = st_ref[...]


def retention(proj, cos, sin, s0, layer, o_all=None, *, row0, n_seq, seq_len, blk_len, pos_per_blk):
    nb = seq_len // blk_len
    rb0 = row0 // blk_len

    def rows(col, width):
        return pl.BlockSpec((blk_len, width), lambda b, t: (rb0 + b * nb + t, col // width))

    tab = pl.BlockSpec((blk_len, LANES), lambda b, t: (t if pos_per_blk else 0, 0))
    st = pl.BlockSpec((1, 4, LANES, LANES), lambda b, t: (b, 0, 0, 0))
    st_in = pl.BlockSpec((None, 1, 4, LANES, LANES), lambda b, t: (layer, b, 0, 0, 0))
    in_specs = [rows(COL_QB, 512), rows(COL_KB, 512), rows(COL_VB, 1024), rows(COL_GR, 1024), tab, tab, st_in]
    args = [proj, proj, proj, proj, cos, sin, s0]
    aliases = {}
    if o_all is not None:
        in_specs.append(pl.BlockSpec(memory_space=pl.ANY))
        args.append(o_all)
        aliases = {len(args) - 1: 0}
    return pl.pallas_call(
        functools.partial(_ret_kernel, blk_len=blk_len),
        grid=(n_seq, nb),
        in_specs=in_specs,
        out_specs=[rows(0, 1024), st],
        out_shape=[jax.ShapeDtypeStruct((ROWS, 1024), BF16),
                   jax.ShapeDtypeStruct((n_seq, 4, LANES, LANES), F32)],
        scratch_shapes=[pltpu.VMEM((4, LANES, LANES), F32), pltpu.VMEM((RET_HEADS, blk_len, blk_len), F32),
                        pltpu.VMEM((2 * RET_HEADS, blk_len, LANES), F32)],
        input_output_aliases=aliases,
        compiler_params=_params(("arbitrary", "arbitrary")),
        name="retention",
    )(*args)


def _mem_attn_kernel(q_ref, k_ref, v_ref, *refs, heads):
    o_ref = refs[-1]
    cols = [slice(h * MEM_HD, (h + 1) * MEM_HD) for h in range(heads)]
    scores = [lax.dot_general(q_ref[:, c], k_ref[0, :, c].astype(BF16), NT, preferred_element_type=F32)
              * (MEM_HD ** -0.5) for c in cols]
    probs = []
    for s in scores:
        e = jnp.exp(s - jnp.max(s, -1, keepdims=True))
        probs.append((e.astype(BF16), jnp.sum(e, -1, keepdims=True)))
    for c, (e, den) in zip(cols, probs):
        o_ref[:, c] = (jnp.dot(e, v_ref[0, :, c].astype(BF16), preferred_element_type=F32) / den
                       ).astype(o_ref.dtype)


def mem_attention(qm, k_src, v_src, layer, o_all=None, *, row0, n_rows, tm, heads, rows_per_kv):
    rb0 = row0 // tm
    w = heads * MEM_HD
    kv = pl.BlockSpec((None, 1, MEM_LEN, w), lambda i, h: (layer, i * tm // rows_per_kv, 0, h))
    in_specs = [pl.BlockSpec((tm, w), lambda i, h: (rb0 + i, h)), kv, kv]
    args = [qm, k_src, v_src]
    aliases = {}
    if o_all is not None:
        in_specs.append(pl.BlockSpec(memory_space=pl.ANY))
        args.append(o_all)
        aliases = {3: 0}
    return pl.pallas_call(
        functools.partial(_mem_attn_kernel, heads=heads),
        grid=(n_rows // tm, MEM_HEADS // heads),
        in_specs=in_specs,
        out_specs=pl.BlockSpec((tm, w), lambda i, h: (rb0 + i, h)),
        out_shape=jax.ShapeDtypeStruct((ROWS, D_MODEL), BF16),
        input_output_aliases=aliases,
        compiler_params=_params(("parallel", "parallel")),
        name="mem_attention",
    )(*args)


def _gelu(x):
    return 0.5 * x * (1.0 + lax.erf(x * (2.0 ** -0.5)))


def _ffn_in_kernel(*refs, sample, seq_tiles, tail):
    i = pl.program_id(1)
    if sample:
        a_ref, bg_ref, bv_ref, wg_ref, wv_ref, cbg_ref, cbv_ref, f1g_ref, f1v_ref, f2g_ref, f2v_ref = refs[:11]
        h_ref, tg_ref, tv_ref = refs[-3:]
    else:
        a_ref, ah_ref, wfg_ref, wfv_ref, wg_ref, wv_ref, cbg_ref, cbv_ref, side_ref = refs[:9]
        h_ref, tg_ref, tv_ref, side_out_ref, bg_ref, bv_ref = refs[-6:]
        side_out_ref[...] = side_ref[...].astype(BF16)

        @pl.when(i == 0)
        def _():
            bg_ref[...] = wfg_ref[...].astype(BF16)
            bv_ref[...] = wfv_ref[...].astype(BF16)

    tm = a_ref.shape[0]
    a = a_ref[...]
    if sample:
        pos = lax.broadcasted_iota(jnp.int32, (tm, 1), 0) % DEC_SEQ
    else:
        top = lax.broadcasted_iota(jnp.int32, (8, 1), 0)
        keep = (i % seq_tiles != 0).astype(F32)

    def project(cols, b_ref):
        u = jnp.dot(a, b_ref[:, cols], preferred_element_type=F32)
        uh = None if sample else jnp.dot(ah_ref[...], b_ref[:, cols], preferred_element_type=F32)
        return u, uh

    def conv(cols, u, uh, w_ref, cb_ref, f1_ref, f2_ref, t_ref):
        t_ref[0, :, cols] = u[tm - tail:, :]
        r1 = pltpu.roll(u, 1, 0)
        r2 = pltpu.roll(u, 2, 0)
        if sample:
            u1 = jnp.where(pos < 1, f1_ref[:, cols], r1)
            u2 = jnp.where(pos < 2, f2_ref[:, cols], r2)
        else:
            n = uh.shape[0]
            prev1, prev2 = uh[n - 1:n, :] * keep, uh[n - 2:n - 1, :] * keep
            u1 = jnp.concatenate([jnp.where(top < 1, prev1, r1[:8]), r1[8:]], 0)
            top2 = jnp.where(top < 1, prev2, jnp.where(top < 2, prev1, r2[:8]))
            u2 = jnp.concatenate([top2, r2[8:]], 0)
        w = w_ref[:, cols]
        return w[0:1, :] * u2 + w[1:2, :] * u1 + w[2:3, :] * u + cb_ref[:, cols]

    sub = [slice(c0, c0 + MXU_COLS) for c0 in range(0, h_ref.shape[1], MXU_COLS)]
    prods = [(project(cols, bg_ref), project(cols, bv_ref)) for cols in sub]
    for cols, ((ug, uhg), (uv, uhv)) in zip(sub, prods):
        if sample:
            cg = conv(cols, ug, uhg, wg_ref, cbg_ref, f1g_ref, f2g_ref, tg_ref)
            cv = conv(cols, uv, uhv, wv_ref, cbv_ref, f1v_ref, f2v_ref, tv_ref)
        else:
            cg = conv(cols, ug, uhg, wg_ref, cbg_ref, None, None, tg_ref)
            cv = conv(cols, uv, uhv, wv_ref, cbv_ref, None, None, tv_ref)
        h_ref[:, cols] = (_gelu(cg) * cv).astype(h_ref.dtype)


def ffn_in(x, w, conv_w, conv_b, layer, *, row0, n_rows, tm, tn, tail, side_cast=None, sample_state=None):
    sample = sample_state is not None
    k = x.shape[1]
    nj = D_FF // tn
    ni = n_rows // tm
    rb0 = row0 // tm
    halo = BF16_ROWS
    a_spec = pl.BlockSpec((tm, k), lambda j, i: (rb0 + i, 0))
    wg = pl.BlockSpec((None, CONV_W, tn), lambda j, i: (layer, 0, j))
    wv = pl.BlockSpec((None, CONV_W, tn), lambda j, i: (layer, 0, nj + j))
    cg = pl.BlockSpec((None, 1, tn), lambda j, i: (layer, 0, j))
    cv = pl.BlockSpec((None, 1, tn), lambda j, i: (layer, 0, nj + j))
    conv_b3 = conv_b.reshape(DEPTH, 1, 2 * D_FF)
    w_half = pl.BlockSpec((k, tn), lambda j, i: (0, j))
    t_spec = pl.BlockSpec((1, tail, tn), lambda j, i: (i, 0, j))
    out_specs = [pl.BlockSpec((tm, tn), lambda j, i: (rb0 + i, j)), t_spec, t_spec]
    out_shape = [jax.ShapeDtypeStruct((ROWS, D_FF), BF16),
                 jax.ShapeDtypeStruct((ni, tail, D_FF), F32),
                 jax.ShapeDtypeStruct((ni, tail, D_FF), F32)]
    aliases = {}
    if sample:
        fix1, fix2, h_all = sample_state
        fg = pl.BlockSpec((tm, tn), lambda j, i: (i, j))
        fv = pl.BlockSpec((tm, tn), lambda j, i: (i, nj + j))
        in_specs = [a_spec, w_half, w_half, wg, wv, cg, cv, fg, fv, fg, fv, pl.BlockSpec(memory_space=pl.ANY)]
        args = [x, w[0], w[1], conv_w, conv_w, conv_b3, conv_b3, fix1, fix1, fix2, fix2, h_all]
        aliases = {len(args) - 1: 0}
    else:
        rows, cols = side_cast.shape[1:]
        slab = rows // (nj * ni)
        assert slab * nj * ni == rows and slab % BF16_ROWS == 0
        ah = pl.BlockSpec((halo, k), lambda j, i: (jnp.maximum((rb0 + i) * (tm // halo) - 1, 0), 0))
        in_specs = [a_spec, ah,
                    pl.BlockSpec((None, k, tn), lambda j, i: (layer, 0, j)),
                    pl.BlockSpec((None, k, tn), lambda j, i: (layer, 0, nj + j)),
                    wg, wv, cg, cv,
                    pl.BlockSpec((None, slab, cols), lambda j, i: (layer, j * ni + i, 0))]
        args = [x, x, w, w, conv_w, conv_w, conv_b3, conv_b3, side_cast]
        out_specs += [pl.BlockSpec((slab, cols), lambda j, i: (j * ni + i, 0)), w_half, w_half]
        out_shape += [jax.ShapeDtypeStruct((rows, cols), BF16),
                      jax.ShapeDtypeStruct((k, D_FF), BF16), jax.ShapeDtypeStruct((k, D_FF), BF16)]
    return pl.pallas_call(
        functools.partial(_ffn_in_kernel, sample=sample, seq_tiles=SEQ // tm if not sample else 1, tail=tail),
        grid=(nj, ni),
        in_specs=in_specs,
        out_specs=out_specs,
        out_shape=out_shape,
        input_output_aliases=aliases,
        compiler_params=_params(("parallel", "arbitrary")),
        name="ffn_in",
    )(*args)


def _t5_bucket(rel):
    nb = T5_BUCKETS // 2
    max_exact = nb // 2
    n = jnp.abs(rel)
    nf = jnp.maximum(n, 1).astype(F32)
    large = max_exact + (jnp.log(nf / max_exact) / math.log(T5_MAX_DIST / max_exact)
                         * (nb - max_exact)).astype(jnp.int32)
    large = jnp.minimum(large, nb - 1)
    return jnp.where(rel > 0, nb, 0) + jnp.where(n < max_exact, n, large)


def _t5_bias(table, rel):
    return jnp.transpose(table[_t5_bucket(rel)], (2, 0, 1)).astype(F32)


def _clipped_bias(table, rel):
    return table[:, jnp.clip(rel, -BAND_MAX_REL, BAND_MAX_REL) + BAND_MAX_REL].astype(F32)


def _toeplitz_diag(bias_fn, n_rows, n_cols, rel0, n):
    assert n_cols + n_rows - 1 <= n
    k = np.arange(n)
    rel = np.where(k < n_cols, k, k - n) + rel0
    return jnp.broadcast_to(bias_fn(jnp.asarray(rel)[None, :]), (N_HEADS, 8, n))


def _band_diag(bias_fn, n_back):
    n_prev = n_back * CHUNK
    return _toeplitz_diag(lambda d: bias_fn(-d), n_prev + QB, QB, n_prev, 1024)


def _cache_diag(bias_fn, cache_len):
    n = max(256, cache_len * 2)
    return _toeplitz_diag(bias_fn, DEC_SEQ, cache_len, -cache_len, n)


def _rope_tables(pos):
    half = RET_DK // 2
    inv = ROPE_BASE ** (-jnp.arange(half, dtype=F32) / half)
    ang = pos.astype(F32)[:, None] * inv[None, :]
    cos, sin = jnp.cos(ang), jnp.sin(ang)
    cos_t = jnp.concatenate([cos, cos, cos, cos], -1)
    sin_t = jnp.concatenate([-sin, sin, -sin, sin], -1)
    return cos_t, sin_t


def _dup_groups(t):
    g0, g1 = t[..., 0, :], t[..., 1, :]
    return jnp.concatenate([g0, g0, g1, g1], -1)


def _undup(t):
    return jnp.stack([t[:, 0:64], t[:, 128:192]], 1)


def kernel(x_prompt, x_sample, mem_prompt, cache_swa_k, cache_swa_v, state_ret, cache_band_k, cache_band_v, state_ffn_conv, cache_mem_k, cache_mem_v, w_in, t5_table, swa_sink, band_rel_table, w_br_a, w_br_b, w_br_c, w_mix_o, ln1_g, ln1_b, w_mq, w_mk, w_mv, w_mo, ln2_g, ln2_b, w_ffn_in, ffn_conv_w, ffn_conv_b, w_ffn_out, ln3_g, ln3_b):
    x, xb = pack_rows(x_prompt, x_sample)
    memb = mem_prompt.reshape(BATCH * MEM_LEN, D_MODEL).astype(BF16)

    t5 = functools.partial(_t5_bias, t5_table)
    diag_a = _band_diag(t5, SWA_BACK)
    qpos = PAST_LEN + jnp.arange(DEC_SEQ)
    la, lc = cache_swa_k.shape[2], cache_band_k.shape[2]
    rel_n = qpos[None, :] - qpos[:, None]
    diag_a_c, bias_a_n = _cache_diag(t5, la), t5(rel_n)
    cos_p, sin_p = _rope_tables(jnp.arange(SEQ))
    cos_s, sin_s = _rope_tables(qpos)
    zero_state = jnp.zeros((1, BATCH, 4, LANES, LANES), F32)
    no_sink = jnp.zeros((N_HEADS,), F32)

    w_in_b = permute_w_in(w_in)
    swa_k2, swa_v2 = _dup_groups(cache_swa_k), _dup_groups(cache_swa_v)
    band_k2 = cache_band_k.reshape(DEPTH, DEC_BATCH, lc, BAND_HEADS * HEAD_DIM)
    band_v2 = cache_band_v.reshape(DEPTH, DEC_BATCH, lc, BAND_HEADS * HEAD_DIM)
    mem_k2 = cache_mem_k.reshape(DEPTH, DEC_BATCH, MEM_LEN, D_MODEL)
    mem_v2 = cache_mem_v.reshape(DEPTH, DEC_BATCH, MEM_LEN, D_MODEL)
    ret_s0 = state_ret.reshape(DEPTH, DEC_BATCH, 4, LANES, LANES)

    outs = {k: [] for k in ("p_ak", "p_av", "p_rs", "p_bk", "p_bv", "p_fc", "p_mk", "p_mv",
                            "s_ak", "s_av", "s_rs", "s_bk", "s_bv", "s_fc")}
    for l in range(DEPTH):
        proj, w_br_a_b, w_br_b_b, w_br_c_b, w_mix_o_b, w_mo_b = matmul(
            xb, w_in_b, l, tm=1056, tn=1536, out_dtype=BF16, side_casts=(w_br_a, w_br_b, w_br_c, w_mix_o, w_mo))
        clipped = functools.partial(_clipped_bias, band_rel_table[l])
        oa = band_attention_prompt(proj, diag_a, swa_sink[l], q_col=COL_QA, k_col=COL_KA2, v_col=COL_VA2,
                                   shared_kv=True, n_back=SWA_BACK, has_sink=True, skew=None)
        oc = band_attention_prompt(proj, _band_diag(clipped, BAND_BACK), no_sink, q_col=COL_QC, k_col=COL_KC,
                                   v_col=COL_VC, shared_kv=False, n_back=BAND_BACK, has_sink=False, skew=1)
        ob, rs_p = retention(proj, cos_p, sin_p, zero_state, 0, row0=0, n_seq=BATCH, seq_len=SEQ,
                             blk_len=RET_L, pos_per_blk=True)
        oa = cached_attention_sample(
            proj, swa_k2, swa_v2, l, diag_a_c, bias_a_n, swa_sink[l], oa,
            q_col=COL_QA, k_col=COL_KA2, v_col=COL_VA2, shared_kv=True, has_sink=True)
        oc = cached_attention_sample(
            proj, band_k2, band_v2, l, _cache_diag(clipped, lc), clipped(rel_n), no_sink, oc,
            q_col=COL_QC, k_col=COL_KC, v_col=COL_VC, shared_kv=False, has_sink=False)
        ob, rs_s = retention(proj, cos_s, sin_s, ret_s0, l, ob, row0=P_ROWS, n_seq=DEC_BATCH, seq_len=DEC_SEQ,
                             blk_len=DEC_SEQ, pos_per_blk=False)
        mix = gated_branch_sum(oa, ob, oc, w_br_a_b, w_br_b_b, w_br_c_b, proj)
        x, xb = matmul_residual_ln(mix, w_mix_o_b, l, x, ln1_g, ln1_b, tm=528)

        mk = matmul(memb, w_mk, l, tm=512, tn=1024, out_dtype=F32)
        mv = matmul(memb, w_mv, l, tm=512, tn=1024, out_dtype=F32)
        qm = matmul(xb, w_mq, l, tm=1056, tn=2048, out_dtype=BF16)
        om = mem_attention(qm, mk.reshape(1, BATCH, MEM_LEN, D_MODEL), mv.reshape(1, BATCH, MEM_LEN, D_MODEL), 0,
                           row0=0, n_rows=P_ROWS, tm=1024, heads=MEM_HEADS, rows_per_kv=SEQ)
        om = mem_attention(qm, mem_k2, mem_v2, l, om, row0=P_ROWS, n_rows=S_ROWS, tm=DEC_SEQ,
                           heads=MEM_HEADS, rows_per_kv=DEC_SEQ)
        x, xb = matmul_residual_ln(om, w_mo_b, l, x, ln2_g, ln2_b, tm=528)

        h, tg_p, tv_p, w_ffn_out_b, wfg_b, wfv_b = ffn_in(
            xb, w_ffn_in, ffn_conv_w, ffn_conv_b, l, row0=0, n_rows=P_ROWS, tm=1024, tn=512, tail=8,
            side_cast=w_ffn_out)
        st = state_ffn_conv[l]
        fix1 = jnp.pad(st[:, 1:2], ((0, 0), (0, DEC_SEQ - 1), (0, 0))).reshape(S_ROWS, 2 * D_FF)
        fix2 = jnp.pad(st, ((0, 0), (0, DEC_SEQ - 2), (0, 0))).reshape(S_ROWS, 2 * D_FF)
        h, tg_s, tv_s = ffn_in(xb, (wfg_b, wfv_b), ffn_conv_w, ffn_conv_b, l, row0=P_ROWS, n_rows=S_ROWS,
                               tm=S_ROWS, tn=512, tail=S_ROWS, sample_state=(fix1, fix2, h))
        if l < DEPTH - 1:
            x, xb = matmul_residual_ln(h, w_ffn_out_b, l, x, ln3_g, ln3_b, tm=S_ROWS)
        else:
            y_prompt, y_sample = matmul_residual_ln(h, w_ffn_out_b, l, x, ln3_g, ln3_b, tm=S_ROWS,
                                                    split_rows=P_ROWS)

        sf = proj[P_ROWS:]
        la_p, lc_p = min(SWA_BACK * CHUNK, SEQ), min(BAND_BACK * CHUNK, SEQ)
        seq_tail = lambda n, c0, w: jnp.stack(
            [proj[(b + 1) * SEQ - n:(b + 1) * SEQ, c0:c0 + w] for b in range(BATCH)], 0)
        outs["p_ak"].append(_undup(seq_tail(la_p, COL_KA2, 256).reshape(BATCH * la_p, 256))
                            .reshape(BATCH, la_p, 2, 64).astype(F32))
        outs["p_av"].append(_undup(seq_tail(la_p, COL_VA2, 256).reshape(BATCH * la_p, 256))
                            .reshape(BATCH, la_p, 2, 64).astype(F32))
        outs["p_rs"].append(rs_p.reshape(BATCH, RET_HEADS, RET_DK, RET_DV))
        outs["p_bk"].append(seq_tail(lc_p, COL_KC, 1024).reshape(BATCH, lc_p, 16, 64).astype(F32))
        outs["p_bv"].append(seq_tail(lc_p, COL_VC, 1024).reshape(BATCH, lc_p, 16, 64).astype(F32))
        last = [(b + 1) * (SEQ // 1024) - 1 for b in range(BATCH)]
        outs["p_fc"].append(jnp.stack(
            [jnp.concatenate([tg_p[t, 6:8], tv_p[t, 6:8]], -1) for t in last], 0))
        outs["p_mk"].append(mk.reshape(BATCH, MEM_LEN, MEM_HEADS, MEM_HD))
        outs["p_mv"].append(mv.reshape(BATCH, MEM_LEN, MEM_HEADS, MEM_HD))
        outs["s_ak"].append(_undup(sf[:, COL_KA2:COL_KA2 + 256]).reshape(DEC_BATCH, DEC_SEQ, 2, 64).astype(F32))
        outs["s_av"].append(_undup(sf[:, COL_VA2:COL_VA2 + 256]).reshape(DEC_BATCH, DEC_SEQ, 2, 64).astype(F32))
        outs["s_rs"].append(rs_s.reshape(DEC_BATCH, RET_HEADS, RET_DK, RET_DV))
        outs["s_bk"].append(sf[:, COL_KC:COL_KC + 1024].reshape(DEC_BATCH, DEC_SEQ, 16, 64).astype(F32))
        outs["s_bv"].append(sf[:, COL_VC:COL_VC + 1024].reshape(DEC_BATCH, DEC_SEQ, 16, 64).astype(F32))
        u_s = jnp.concatenate([tg_s[0], tv_s[0]], -1).reshape(DEC_BATCH, DEC_SEQ, 2 * D_FF)
        outs["s_fc"].append(u_s[:, DEC_SEQ - 2:])

    st = lambda name: jnp.stack(outs[name], 0)
    return (y_prompt.reshape(BATCH, SEQ, D_MODEL), y_sample[:S_ROWS].reshape(DEC_BATCH, DEC_SEQ, D_MODEL),
            st("p_ak"), st("p_av"), st("p_rs"), st("p_bk"), st("p_bv"), st("p_fc"), st("p_mk"), st("p_mv"),
            st("s_ak"), st("s_av"), st("s_rs"), st("s_bk"), st("s_bv"), st("s_fc"))
```

```python
import functools
import math

import numpy as np
import jax
import jax.numpy as jnp
from jax import lax
from jax.experimental import pallas as pl
from jax.experimental.pallas import tpu as pltpu

F32 = jnp.float32
BF16 = jnp.bfloat16

D_MODEL = 2048
BATCH = 2
SEQ = 4096
DEPTH = 2
DEC_BATCH = 16
DEC_SEQ = 16
PAST_LEN = 2048
CHUNK = 64
HEAD_DIM = 64
SWA_BACK = 2
SWA_HEADS = 16
SWA_KV_HEADS = 2
N_HEADS = SWA_HEADS
ATT_W = N_HEADS * HEAD_DIM
T5_BUCKETS = 32
T5_MAX_DIST = 128
RET_HEADS = 8
RET_DK = 64
RET_DV = 128
ROPE_BASE = 10000.0
RET_NORM_EPS = 1e-5
BAND_BACK = 8
BAND_HEADS = 16
BAND_MAX_REL = 256
MEM_LEN = 256
MEM_HEADS = 4
MEM_HD = D_MODEL // MEM_HEADS
D_FF = 5632
CONV_W = 3
DN_ALPHA = (2 * DEPTH) ** 0.25
LN_EPS = 1e-5

P_ROWS = BATCH * SEQ
S_ROWS = DEC_BATCH * DEC_SEQ
ROWS = P_ROWS + S_ROWS

COL_QA = 0
COL_QC = 1024
COL_KC = 2048
COL_VC = 3072
COL_VB = 4096
COL_GR = 5120
COL_GA = 6144
COL_GB = 8192
COL_GC = 10240
COL_QB = 12288
COL_KB = 12800
COL_KA2 = 13312
COL_VA2 = 13568
PROJ_COLS = 13824

LANES = 128
BF16_ROWS = 16
MXU_COLS = 256
QB = 256
RET_L = 256
NEG = -1e30
LOG2E = math.log2(math.e)
VMEM_LIMIT = 48 * 1024 * 1024

NT = (((1,), (1,)), ((), ()))
TN = (((0,), (0,)), ((), ()))


def _params(sem, vmem=VMEM_LIMIT):
    return pltpu.CompilerParams(dimension_semantics=sem, vmem_limit_bytes=vmem)


def _pack_rows_kernel(p_ref, s_ref, of_ref, ob_ref, *, n_prompt_tiles):
    src = jnp.where(pl.program_id(0) < n_prompt_tiles, p_ref[...], s_ref[...])
    of_ref[...] = src
    ob_ref[...] = src.astype(BF16)


def pack_rows(x_prompt, x_sample):
    tm = S_ROWS
    npt = P_ROWS // tm
    return pl.pallas_call(
        functools.partial(_pack_rows_kernel, n_prompt_tiles=npt),
        grid=(npt + 1,),
        in_specs=[pl.BlockSpec((tm, D_MODEL), lambda i: (jnp.minimum(i, npt - 1), 0)),
                  pl.BlockSpec((tm, D_MODEL), lambda i: (0, 0))],
        out_specs=[pl.BlockSpec((tm, D_MODEL), lambda i: (i, 0)), pl.BlockSpec((tm, D_MODEL), lambda i: (i, 0))],
        out_shape=[jax.ShapeDtypeStruct((ROWS, D_MODEL), F32), jax.ShapeDtypeStruct((ROWS, D_MODEL), BF16)],
        compiler_params=_params(("parallel",)),
        name="pack_rows",
    )(x_prompt.reshape(P_ROWS, D_MODEL), x_sample.reshape(S_ROWS, D_MODEL))


_W_IN_MOVES = ((0, COL_QA, 1024), (4352, COL_QC, 1024), (5376, COL_KC, 1024), (6400, COL_VC, 1024),
               (2304, COL_VB, 1024), (3328, COL_GR, 1024), (7424, COL_GA, 2048), (9472, COL_GB, 2048),
               (11520, COL_GC, 2048), (1280, COL_QB, 512), (1792, COL_KB, 512))
_W_IN_DUPS = ((1024, COL_KA2), (1152, COL_VA2))


def _w_in_permute_kernel(src_ref, o_ref):
    for src, dst, width in _W_IN_MOVES:
        o_ref[:, dst:dst + width] = src_ref[:, src:src + width].astype(BF16)
    for src, dst in _W_IN_DUPS:
        pair = src_ref[:, src:src + LANES]
        swapped = pltpu.roll(pair, HEAD_DIM, 1)
        o_ref[:, dst:dst + LANES] = jnp.where(_head_lanes(0), pair, swapped).astype(BF16)
        o_ref[:, dst + LANES:dst + 2 * LANES] = jnp.where(_head_lanes(0), swapped, pair).astype(BF16)


def permute_w_in(w_in, *, rows=128):
    depth, d, in_cols = w_in.shape
    assert sum(w for _, _, w in _W_IN_MOVES) + 2 * LANES == in_cols
    return pl.pallas_call(
        _w_in_permute_kernel,
        grid=(depth, d // rows),
        in_specs=[pl.BlockSpec((None, rows, in_cols), lambda l, r: (l, r, 0))],
        out_specs=pl.BlockSpec((None, rows, PROJ_COLS), lambda l, r: (l, r, 0)),
        out_shape=jax.ShapeDtypeStruct((depth, d, PROJ_COLS), BF16),
        compiler_params=_params(("parallel", "parallel")),
        name="permute_w_in",
    )(w_in)


def _mm_kernel(a_ref, b_ref, *refs, n_side, resident):
    side_in, o_ref, side_out = refs[:n_side], refs[n_side], refs[n_side + 1:2 * n_side + 1]
    for src, dst in zip(side_in, side_out):
        dst[...] = src[...].astype(BF16)
    if resident:
        bb_ref = refs[-1]

        @pl.when(pl.program_id(0) == 0)
        def _():
            bb_ref[...] = b_ref[...].astype(BF16)

        b = bb_ref[...]
    else:
        b = b_ref[...].astype(BF16)
    o_ref[...] = jnp.dot(a_ref[...], b, preferred_element_type=F32).astype(o_ref.dtype)


def matmul(a, b, layer, *, tm, tn, out_dtype, side_casts=()):
    m, k = a.shape
    n = b.shape[2]
    ni, nj = m // tm, n // tn
    resident = b.dtype == F32 and ni > 1
    assert b.dtype == BF16 or ni == 1 or nj == 1
    b_mode = dict(pipeline_mode=pl.Buffered(1)) if resident else {}
    in_specs = [pl.BlockSpec((tm, k), lambda i, j: (i, 0)),
                pl.BlockSpec((None, k, tn), lambda i, j: (layer, 0, j), **b_mode)]
    out_specs = [pl.BlockSpec((tm, tn), lambda i, j: (i, j))]
    out_shape = [jax.ShapeDtypeStruct((m, n), out_dtype)]
    n_slabs = ni * (nj - 1)
    slab_of = lambda i, j: i * (nj - 1) + jnp.minimum(j, nj - 2)
    for w in side_casts:
        rows, cols = w.shape[1:]
        slab = rows // n_slabs
        assert slab * n_slabs == rows and slab % BF16_ROWS == 0
        in_specs.append(pl.BlockSpec((None, slab, cols), lambda i, j: (layer, slab_of(i, j), 0)))
        out_specs.append(pl.BlockSpec((slab, cols), lambda i, j: (slab_of(i, j), 0)))
        out_shape.append(jax.ShapeDtypeStruct((rows, cols), BF16))
    sem = ("arbitrary", "arbitrary") if resident or side_casts else ("parallel", "parallel")
    outs = pl.pallas_call(
        functools.partial(_mm_kernel, n_side=len(side_casts), resident=resident),
        grid=(ni, nj),
        in_specs=in_specs,
        out_specs=out_specs,
        out_shape=out_shape,
        scratch_shapes=[pltpu.VMEM((k, tn), BF16)] if resident else [],
        compiler_params=_params(sem),
        name="matmul",
    )(a, b, *side_casts)
    return outs if side_casts else outs[0]


def _gate_mm_kernel(oa_ref, ob_ref, oc_ref, wa_ref, wb_ref, wc_ref, ga_ref, gb_ref, gc_ref, o_ref):
    acc = None
    for o, w, g in ((oa_ref, wa_ref, ga_ref), (ob_ref, wb_ref, gb_ref), (oc_ref, wc_ref, gc_ref)):
        t = jnp.dot(o[...], w[...], preferred_element_type=F32) * jax.nn.sigmoid(g[...].astype(F32))
        acc = t if acc is None else acc + t
    o_ref[...] = acc.astype(o_ref.dtype)


def gated_branch_sum(oa, ob, oc, wa, wb, wc, proj, *, tm=528):
    m, k = oa.shape
    n = wa.shape[1]
    o_spec = pl.BlockSpec((tm, k), lambda i: (i, 0))
    w_spec = pl.BlockSpec((k, n), lambda i: (0, 0), pipeline_mode=pl.Buffered(1))

    def g_spec(col):
        return pl.BlockSpec((tm, n), lambda i: (i, col // n))

    return pl.pallas_call(
        _gate_mm_kernel,
        grid=(m // tm,),
        in_specs=[o_spec, o_spec, o_spec, w_spec, w_spec, w_spec,
                  g_spec(COL_GA), g_spec(COL_GB), g_spec(COL_GC)],
        out_specs=pl.BlockSpec((tm, n), lambda i: (i, 0)),
        out_shape=jax.ShapeDtypeStruct((m, n), BF16),
        compiler_params=_params(("parallel",)),
        name="gated_branch_sum",
    )(oa, ob, oc, wa, wb, wc, proj, proj, proj)


def _mm_ln_kernel(a_ref, b_ref, r_ref, g_ref, bt_ref, o1_ref, o2_ref, acc_ref, *, split):
    @pl.when(pl.program_id(0) == 0)
    def _():
        acc_ref[...] = jnp.zeros_like(acc_ref)

    y = DN_ALPHA * r_ref[...] + acc_ref[...]
    mu = jnp.mean(y, -1, keepdims=True)
    d = y - mu
    var = jnp.mean(d * d, -1, keepdims=True)
    out = d * lax.rsqrt(var + LN_EPS) * g_ref[...] + bt_ref[...]
    o1_ref[...] = out
    o2_ref[...] = out if split else out.astype(BF16)
    acc_ref[...] = jnp.dot(a_ref[...], b_ref[...], preferred_element_type=F32)


def matmul_residual_ln(a, b, layer, resid, gain, bias, *, tm, split_rows=None):
    m, kk = a.shape
    n = b.shape[-1]
    nt = m // tm
    if b.ndim == 3:
        b_spec = pl.BlockSpec((None, kk, n), lambda i: (layer, 0, 0), pipeline_mode=pl.Buffered(1))
    else:
        b_spec = pl.BlockSpec((kk, n), lambda i: (0, 0), pipeline_mode=pl.Buffered(1))
    if split_rows is None:
        cur = lambda i: (jnp.minimum(i, nt - 1), 0)
        prev = lambda i: (jnp.maximum(i - 1, 0), 0)
        out_specs = [pl.BlockSpec((tm, n), prev), pl.BlockSpec((tm, n), prev)]
        out_shape = [jax.ShapeDtypeStruct((m, n), F32), jax.ShapeDtypeStruct((m, n), BF16)]
    else:
        assert split_rows == (nt - 1) * tm
        cur = lambda i: (jnp.where(i == 0, nt - 1, jnp.minimum(i - 1, nt - 2)), 0)
        prev = lambda i: (jnp.where(i <= 1, nt - 1, i - 2), 0)
        out_specs = [pl.BlockSpec((tm, n), lambda i: (jnp.clip(i - 2, 0, nt - 2), 0)),
                     pl.BlockSpec((tm, n), lambda i: (jnp.where(i <= 1, 0, 1), 0))]
        out_shape = [jax.ShapeDtypeStruct((split_rows, n), F32), jax.ShapeDtypeStruct((2 * tm, n), F32)]
    return pl.pallas_call(
        functools.partial(_mm_ln_kernel, split=split_rows is not None),
        grid=(nt + 1,),
        in_specs=[pl.BlockSpec((tm, kk), cur),
                  b_spec,
                  pl.BlockSpec((tm, n), prev),
                  pl.BlockSpec((None, 1, n), lambda i: (layer, 0, 0)),
                  pl.BlockSpec((None, 1, n), lambda i: (layer, 0, 0))],
        out_specs=out_specs,
        out_shape=out_shape,
        scratch_shapes=[pltpu.VMEM((tm, n), F32)],
        compiler_params=_params(("arbitrary",)),
        name="matmul_residual_ln",
    )(a, b, resid, gain.reshape(DEPTH, 1, n), bias.reshape(DEPTH, 1, n))


def _head_lanes(j):
    lane = lax.broadcasted_iota(jnp.int32, (1, LANES), 1)
    return (lane < HEAD_DIM) if j == 0 else (lane >= HEAD_DIM)


def _band_attn_kernel(sink_ref, diag_ref, q_ref, *refs, n_prev_blocks, n_back, shared_kv, has_sink, skew):
    nk = n_prev_blocks + 1
    k_refs = refs[:nk]
    v_refs = refs[nk:2 * nk]
    o_ref = refs[2 * nk]
    bias_ref = refs[2 * nk + 1]
    b = pl.program_id(0)
    i = pl.program_id(1)
    n_prev_rows = n_back * CHUNK
    kw = n_prev_rows + QB

    @pl.when((b == 0) & (i == 0))
    def _():
        n = diag_ref.shape[2]
        cb = lax.broadcasted_iota(jnp.int32, (kw, QB), 0) // CHUNK
        qc = lax.broadcasted_iota(jnp.int32, (kw, QB), 1) // CHUNK
        allowed = (cb >= qc) & (cb - n_back <= qc)
        for h in range(N_HEADS):
            t = pltpu.roll(jnp.broadcast_to(diag_ref[h, 0:1, :], (kw, n)), 0, 1, stride=1, stride_axis=0)
            bias_ref[h] = jnp.where(allowed, t[:, :QB] * LOG2E, NEG)

    krow = lax.broadcasted_iota(jnp.int32, (kw, 1), 0)
    kmask = jnp.where(krow >= jnp.maximum(n_prev_rows - i * QB, 0), 0.0, NEG).astype(BF16)
    lane = lax.broadcasted_iota(jnp.int32, (1, LANES), 1)

    def kv_cols(h):
        c0 = (h // 8 if shared_kv else h // 2) * LANES
        return slice(c0, c0 + LANES)

    def scores(h):
        p, j = divmod(h, 2)
        q = q_ref[:, p * LANES:(p + 1) * LANES] * (HEAD_DIM ** -0.5 * LOG2E)
        k_all = jnp.concatenate([r[:, kv_cols(h)] for r in k_refs], axis=0)
        mask_lane = lane == (HEAD_DIM if j == 0 else 0)
        qj = jnp.where(mask_lane, jnp.ones_like(q), jnp.where(_head_lanes(j), q, jnp.zeros_like(q)))
        kj = jnp.where(mask_lane, kmask, k_all)
        return lax.dot_general(kj, qj, NT, preferred_element_type=F32)

    def softmax(h, s_all):
        sink = sink_ref[h] * LOG2E if has_sink else None
        ps, dens = [], []
        for c in range(QB // LANES):
            lanes = slice(c * LANES, (c + 1) * LANES)
            lo, hi = c * LANES, c * LANES + n_prev_rows + LANES
            s = s_all[lo:hi, lanes] + bias_ref[h, lo:hi, lanes]
            m = jnp.max(s, 0, keepdims=True)
            if has_sink:
                m = jnp.maximum(m, sink)
            e = jnp.exp2(s - m)
            den = jnp.sum(e, 0, keepdims=True)
            if has_sink:
                den = den + jnp.exp2(sink - m)
            parts = [e.astype(BF16)]
            if lo:
                parts.insert(0, jnp.zeros((lo, LANES), BF16))
            if hi < kw:
                parts.append(jnp.zeros((kw - hi, LANES), BF16))
            ps.append(jnp.concatenate(parts, 0))
            dens.append(den)
        return jnp.concatenate(ps, 1), jnp.concatenate(dens, 1)

    def weighted_values(h, p_all, den):
        v_all = jnp.concatenate([r[:, kv_cols(h)] for r in v_refs], axis=0)
        return lax.dot_general(v_all, p_all, TN, preferred_element_type=F32) / den

    lag = N_HEADS if skew is None else skew
    s_q, p_q, outs = {}, {}, {}
    for t in range(N_HEADS + 2 * lag):
        if t < N_HEADS:
            s_q[t] = scores(t)
        if lag <= t < N_HEADS + lag:
            p_q[t - lag] = softmax(t - lag, s_q.pop(t - lag))
        if t >= 2 * lag:
            h = t - 2 * lag
            outs[h] = weighted_values(h, *p_q.pop(h))
            if h % 2:
                p = h // 2
                pair = jnp.where(lax.broadcasted_iota(jnp.int32, (LANES, 1), 0) < HEAD_DIM,
                                 outs.pop(h - 1), outs.pop(h))
                o_ref[:, p * LANES:(p + 1) * LANES] = pair.T.astype(o_ref.dtype)


def band_attention_prompt(proj, diag, sink, *, q_col, k_col, v_col, shared_kv, n_back, has_sink, skew):
    n_prev_rows = n_back * CHUNK
    nqb = SEQ // QB
    if n_prev_rows >= QB:
        n_prev_blocks, pb = n_prev_rows // QB, QB
    else:
        n_prev_blocks, pb = 1, n_prev_rows
    per = QB // pb
    kvw = 2 * SWA_KV_HEADS * HEAD_DIM if shared_kv else ATT_W

    def prev_spec(col, back):
        return pl.BlockSpec(
            (pb, kvw), lambda b, i: (b * (SEQ // pb) + jnp.maximum(i * per - back, 0), col // kvw))

    def own_spec(col):
        return pl.BlockSpec((QB, kvw), lambda b, i: (b * nqb + i, col // kvw))

    k_specs = [prev_spec(k_col, n_prev_blocks - t) for t in range(n_prev_blocks)] + [own_spec(k_col)]
    v_specs = [prev_spec(v_col, n_prev_blocks - t) for t in range(n_prev_blocks)] + [own_spec(v_col)]
    kern = functools.partial(_band_attn_kernel, n_prev_blocks=n_prev_blocks, n_back=n_back,
                             shared_kv=shared_kv, has_sink=has_sink, skew=skew)
    n_in = 2 * (n_prev_blocks + 1)
    return pl.pallas_call(
        kern,
        grid=(BATCH, nqb),
        in_specs=[pl.BlockSpec(memory_space=pltpu.SMEM),
                  pl.BlockSpec(diag.shape, lambda b, i: (0, 0, 0)),
                  pl.BlockSpec((QB, ATT_W), lambda b, i: (b * nqb + i, q_col // ATT_W))]
                 + k_specs + v_specs,
        out_specs=pl.BlockSpec((QB, ATT_W), lambda b, i: (b * nqb + i, 0)),
        out_shape=jax.ShapeDtypeStruct((ROWS, ATT_W), BF16),
        scratch_shapes=[pltpu.VMEM((N_HEADS, n_prev_rows + QB, QB), F32)],
        compiler_params=_params(("arbitrary", "arbitrary")),
        name="band_attention_prompt",
    )(sink, diag, proj, *([proj] * n_in))


def _cached_attn_kernel(sink_ref, q_ref, kn_ref, vn_ref, kc_ref, vc_ref, diag_ref, bn_ref, _, o_ref, bc_ref,
                        *, shared_kv, has_sink):
    lc = kc_ref.shape[1]

    @pl.when(pl.program_id(0) == 0)
    def _():
        n = diag_ref.shape[2]
        for h in range(N_HEADS):
            t = pltpu.roll(jnp.broadcast_to(diag_ref[h, 0:1, :], (DEC_SEQ, n)), 0, 1, stride=1, stride_axis=0)
            bc_ref[h] = t[:, :lc]

    kv_cols = [(p // 4 if shared_kv else p) * LANES for p in range(8)]
    scores = []
    for p in range(8):
        c0 = kv_cols[p]
        q = q_ref[:, p * LANES:(p + 1) * LANES] * (HEAD_DIM ** -0.5)
        kc = kc_ref[0, :, c0:c0 + LANES].astype(BF16)
        kn = kn_ref[:, c0:c0 + LANES]
        for j in range(2):
            h = 2 * p + j
            qj = jnp.where(_head_lanes(j), q, jnp.zeros_like(q))
            scores.append((lax.dot_general(qj, kc, NT, preferred_element_type=F32) + bc_ref[h],
                           lax.dot_general(qj, kn, NT, preferred_element_type=F32) + bn_ref[h]))
    probs = []
    for h, (s_c, s_n) in enumerate(scores):
        m = jnp.maximum(jnp.max(s_c, -1, keepdims=True), jnp.max(s_n, -1, keepdims=True))
        if has_sink:
            m = jnp.maximum(m, sink_ref[h])
        e_c, e_n = jnp.exp(s_c - m), jnp.exp(s_n - m)
        den = jnp.sum(e_c, -1, keepdims=True) + jnp.sum(e_n, -1, keepdims=True)
        if has_sink:
            den = den + jnp.exp(sink_ref[h] - m)
        probs.append((e_c.astype(BF16), e_n.astype(BF16), den))
    for p in range(8):
        c0 = kv_cols[p]
        vc = vc_ref[0, :, c0:c0 + LANES].astype(BF16)
        vn = vn_ref[:, c0:c0 + LANES]
        outs = []
        for j in range(2):
            e_c, e_n, den = probs[2 * p + j]
            outs.append((jnp.dot(e_c, vc, preferred_element_type=F32)
                         + jnp.dot(e_n, vn, preferred_element_type=F32)) / den)
        o_ref[:, p * LANES:(p + 1) * LANES] = jnp.where(_head_lanes(0), outs[0], outs[1]).astype(o_ref.dtype)


def cached_attention_sample(proj, k_cache, v_cache, layer, diag_c, bias_n, sink, o_all, *, q_col, k_col, v_col,
                            shared_kv, has_sink):
    lc, wc = k_cache.shape[2], k_cache.shape[3]
    r0 = P_ROWS // DEC_SEQ
    kern = functools.partial(_cached_attn_kernel, shared_kv=shared_kv, has_sink=has_sink)
    cache = pl.BlockSpec((None, 1, lc, wc), lambda s: (layer, s, 0, 0))
    return pl.pallas_call(
        kern,
        grid=(DEC_BATCH,),
        in_specs=[pl.BlockSpec(memory_space=pltpu.SMEM),
                  pl.BlockSpec((DEC_SEQ, ATT_W), lambda s: (r0 + s, q_col // ATT_W)),
                  pl.BlockSpec((DEC_SEQ, wc), lambda s: (r0 + s, k_col // wc)),
                  pl.BlockSpec((DEC_SEQ, wc), lambda s: (r0 + s, v_col // wc)),
                  cache, cache,
                  pl.BlockSpec(diag_c.shape, lambda s: (0, 0, 0)),
                  pl.BlockSpec((N_HEADS, DEC_SEQ, DEC_SEQ), lambda s: (0, 0, 0)),
                  pl.BlockSpec(memory_space=pl.ANY)],
        out_specs=pl.BlockSpec((DEC_SEQ, ATT_W), lambda s: (r0 + s, 0)),
        out_shape=jax.ShapeDtypeStruct(o_all.shape, o_all.dtype),
        scratch_shapes=[pltpu.VMEM((N_HEADS, DEC_SEQ, lc), F32)],
        input_output_aliases={8: 0},
        compiler_params=_params(("arbitrary",)),
        name="cached_attention_sample",
    )(sink, proj, proj, proj, k_cache, v_cache, diag_c, bias_n, o_all)


def _ret_kernel(q_ref, k_ref, v_ref, g_ref, cos_ref, sin_ref, s0_ref, *refs, blk_len):
    o_ref, sout_ref, st_ref, dec_ref, rdec_ref = refs[-5:]
    blk = pl.program_id(1)
    log_gs = [math.log(1.0 - 2.0 ** (-5.0 - h)) for h in range(RET_HEADS)]

    @pl.when((pl.program_id(0) == 0) & (blk == 0))
    def _():
        ii = lax.broadcasted_iota(jnp.int32, (blk_len, blk_len), 0)
        jj = lax.broadcasted_iota(jnp.int32, (blk_len, blk_len), 1)
        diff = (ii - jj).astype(F32)
        row = lax.broadcasted_iota(jnp.int32, (blk_len, LANES), 0).astype(F32)
        for h in range(RET_HEADS):
            dec_ref[h] = jnp.where(diff >= 0, jnp.exp(log_gs[h] * jnp.maximum(diff, 0.0)), 0.0)
            rdec_ref[h] = jnp.exp(log_gs[h] * (row + 1.0))
            rdec_ref[RET_HEADS + h] = jnp.exp(log_gs[h] * (blk_len - 1.0 - row))

    @pl.when(blk == 0)
    def _():
        st_ref[...] = s0_ref[0]

    cos = cos_ref[...]
    sin = sin_ref[...]
    lane = lax.broadcasted_iota(jnp.int32, (1, LANES), 1)
    low_half = (lane % RET_DK) < (RET_DK // 2)

    def rope(x):
        x = x.astype(F32)
        swapped = jnp.where(low_half, pltpu.roll(x, LANES - RET_DK // 2, 1), pltpu.roll(x, RET_DK // 2, 1))
        return x * cos + swapped * sin

    srow = lax.broadcasted_iota(jnp.int32, (LANES, 1), 0)

    value = lambda h: v_ref[:, h * RET_DV:(h + 1) * RET_DV]
    pairs = []
    for p in range(RET_HEADS // 2):
        qr = rope(q_ref[:, p * LANES:(p + 1) * LANES])
        kr = rope(k_ref[:, p * LANES:(p + 1) * LANES]) * (RET_DK ** -0.5)
        pairs.append((qr, kr, kr.astype(BF16), st_ref[p]))
    heads = []
    for h in range(RET_HEADS):
        qr, kr, kb, state = pairs[h // 2]
        qj = jnp.where(_head_lanes(h % 2), qr, 0.0).astype(BF16)
        qk = lax.dot_general(qj, kb, NT, preferred_element_type=F32) * dec_ref[h]
        cross = jnp.dot(qj, state.astype(BF16), preferred_element_type=F32) * rdec_ref[h]
        kwj = jnp.where(_head_lanes(h % 2), kr * rdec_ref[RET_HEADS + h], 0.0).astype(BF16)
        heads.append((qk.astype(BF16), cross, kwj))
    updates = []
    for h, (qk, cross, kwj) in enumerate(heads):
        o = jnp.dot(qk, value(h), preferred_element_type=F32) + cross
        mu = jnp.mean(o, -1, keepdims=True)
        d = o - mu
        var = jnp.mean(d * d, -1, keepdims=True)
        gate = g_ref[:, h * RET_DV:(h + 1) * RET_DV].astype(F32)
        o_ref[:, h * RET_DV:(h + 1) * RET_DV] = (
            d * lax.rsqrt(var + RET_NORM_EPS) * (gate * jax.nn.sigmoid(gate))).astype(o_ref.dtype)
        updates.append(lax.dot_general(kwj, value(h), TN, preferred_element_type=F32))
    for p in range(RET_HEADS // 2):
        carry = jnp.where(srow < RET_DK, math.exp(log_gs[2 * p] * blk_len), math.exp(log_gs[2 * p + 1] * blk_len))
        st_ref[p] = carry * pairs[p][3] + updates[2 * p] + updates[2 * p + 1]

    @pl.when(blk == pl.num_programs(1) - 1)
    def _():
        sout_ref[0] = st_ref[...]


def retention(proj, cos, sin, s0, layer, o_all=None, *, row0, n_seq, seq_len, blk_len, pos_per_blk):
    nb = seq_len // blk_len
    rb0 = row0 // blk_len

    def rows(col, width):
        return pl.BlockSpec((blk_len, width), lambda b, t: (rb0 + b * nb + t, col // width))

    tab = pl.BlockSpec((blk_len, LANES), lambda b, t: (t if pos_per_blk else 0, 0))
    st = pl.BlockSpec((1, 4, LANES, LANES), lambda b, t: (b, 0, 0, 0))
    st_in = pl.BlockSpec((None, 1, 4, LANES, LANES), lambda b, t: (layer, b, 0, 0, 0))
    in_specs = [rows(COL_QB, 512), rows(COL_KB, 512), rows(COL_VB, 1024), rows(COL_GR, 1024), tab, tab, st_in]
    args = [proj, proj, proj, proj, cos, sin, s0]
    aliases = {}
    if o_all is not None:
        in_specs.append(pl.BlockSpec(memory_space=pl.ANY))
        args.append(o_all)
        aliases = {len(args) - 1: 0}
    return pl.pallas_call(
        functools.partial(_ret_kernel, blk_len=blk_len),
        grid=(n_seq, nb),
        in_specs=in_specs,
        out_specs=[rows(0, 1024), st],
        out_shape=[jax.ShapeDtypeStruct((ROWS, 1024), BF16),
                   jax.ShapeDtypeStruct((n_seq, 4, LANES, LANES), F32)],
        scratch_shapes=[pltpu.VMEM((4, LANES, LANES), F32), pltpu.VMEM((RET_HEADS, blk_len, blk_len), F32),
                        pltpu.VMEM((2 * RET_HEADS, blk_len, LANES), F32)],
        input_output_aliases=aliases,
        compiler_params=_params(("arbitrary", "arbitrary")),
        name="retention",
    )(*args)


def _mem_attn_kernel(q_ref, k_ref, v_ref, *refs, heads):
    o_ref = refs[-1]
    cols = [slice(h * MEM_HD, (h + 1) * MEM_HD) for h in range(heads)]
    scores = [lax.dot_general(q_ref[:, c], k_ref[0, :, c].astype(BF16), NT, preferred_element_type=F32)
              * (MEM_HD ** -0.5) for c in cols]
    probs = []
    for s in scores:
        e = jnp.exp(s - jnp.max(s, -1, keepdims=True))
        probs.append((e.astype(BF16), jnp.sum(e, -1, keepdims=True)))
    for c, (e, den) in zip(cols, probs):
        o_ref[:, c] = (jnp.dot(e, v_ref[0, :, c].astype(BF16), preferred_element_type=F32) / den
                       ).astype(o_ref.dtype)


def mem_attention(qm, k_src, v_src, layer, o_all=None, *, row0, n_rows, tm, heads, rows_per_kv):
    rb0 = row0 // tm
    w = heads * MEM_HD
    kv = pl.BlockSpec((None, 1, MEM_LEN, w), lambda i, h: (layer, i * tm // rows_per_kv, 0, h))
    in_specs = [pl.BlockSpec((tm, w), lambda i, h: (rb0 + i, h)), kv, kv]
    args = [qm, k_src, v_src]
    aliases = {}
    if o_all is not None:
        in_specs.append(pl.BlockSpec(memory_space=pl.ANY))
        args.append(o_all)
        aliases = {3: 0}
    return pl.pallas_call(
        functools.partial(_mem_attn_kernel, heads=heads),
        grid=(n_rows // tm, MEM_HEADS // heads),
        in_specs=in_specs,
        out_specs=pl.BlockSpec((tm, w), lambda i, h: (rb0 + i, h)),
        out_shape=jax.ShapeDtypeStruct((ROWS, D_MODEL), BF16),
        input_output_aliases=aliases,
        compiler_params=_params(("parallel", "parallel")),
        name="mem_attention",
    )(*args)


def _gelu(x):
    return 0.5 * x * (1.0 + lax.erf(x * (2.0 ** -0.5)))


def _ffn_in_kernel(*refs, sample, seq_tiles, tail):
    i = pl.program_id(1)
    if sample:
        a_ref, bg_ref, bv_ref, wg_ref, wv_ref, cbg_ref, cbv_ref, sg_ref, sv_ref = refs[:9]
        h_ref, tg_ref, tv_ref = refs[-3:]
    else:
        a_ref, ah_ref, wfg_ref, wfv_ref, wg_ref, wv_ref, cbg_ref, cbv_ref, side_ref = refs[:9]
        h_ref, tg_ref, tv_ref, side_out_ref, bg_ref, bv_ref = refs[-6:]
        side_out_ref[...] = side_ref[...].astype(BF16)

        @pl.when(i == 0)
        def _():
            bg_ref[...] = wfg_ref[...].astype(BF16)
            bv_ref[...] = wfv_ref[...].astype(BF16)

    tm = a_ref.shape[0]
    a = a_ref[...]
    if sample:
        pos = lax.broadcasted_iota(jnp.int32, (1, DEC_SEQ, 1), 1)
    else:
        top = lax.broadcasted_iota(jnp.int32, (8, 1), 0)
        keep = (i % seq_tiles != 0).astype(F32)

    def project(cols, b_ref):
        u = jnp.dot(a, b_ref[:, cols], preferred_element_type=F32)
        uh = None if sample else jnp.dot(ah_ref[...], b_ref[:, cols], preferred_element_type=F32)
        return u, uh

    def conv(cols, u, uh, w_ref, cb_ref, s_ref, t_ref):
        t_ref[0, :, cols] = u[tm - tail:, :]
        r1 = pltpu.roll(u, 1, 0)
        r2 = pltpu.roll(u, 2, 0)
        if sample:
            st = s_ref[:, :, cols]
            by_seq = lambda t: t.reshape(tm // DEC_SEQ, DEC_SEQ, t.shape[1])
            u1 = jnp.where(pos < 1, st[:, 1:2, :], by_seq(r1)).reshape(u.shape)
            u2 = jnp.where(pos < 1, st[:, 0:1, :], jnp.where(pos < 2, st[:, 1:2, :], by_seq(r2))).reshape(u.shape)
        else:
            n = uh.shape[0]
            prev1, prev2 = uh[n - 1:n, :] * keep, uh[n - 2:n - 1, :] * keep
            u1 = jnp.concatenate([jnp.where(top < 1, prev1, r1[:8]), r1[8:]], 0)
            top2 = jnp.where(top < 1, prev2, jnp.where(top < 2, prev1, r2[:8]))
            u2 = jnp.concatenate([top2, r2[8:]], 0)
        w = w_ref[:, cols]
        return w[0:1, :] * u2 + w[1:2, :] * u1 + w[2:3, :] * u + cb_ref[:, cols]

    sub = [slice(c0, c0 + MXU_COLS) for c0 in range(0, h_ref.shape[1], MXU_COLS)]
    prods = [(project(cols, bg_ref), project(cols, bv_ref)) for cols in sub]
    for cols, ((ug, uhg), (uv, uhv)) in zip(sub, prods):
        cg = conv(cols, ug, uhg, wg_ref, cbg_ref, sg_ref if sample else None, tg_ref)
        cv = conv(cols, uv, uhv, wv_ref, cbv_ref, sv_ref if sample else None, tv_ref)
        h_ref[:, cols] = (_gelu(cg) * cv).astype(h_ref.dtype)


def ffn_in(x, w, conv_w, conv_b, layer, *, row0, n_rows, tm, tn, tail, side_cast=None, sample_state=None):
    sample = sample_state is not None
    k = x.shape[1]
    nj = D_FF // tn
    ni = n_rows // tm
    rb0 = row0 // tm
    halo = BF16_ROWS
    a_spec = pl.BlockSpec((tm, k), lambda j, i: (rb0 + i, 0))
    wg = pl.BlockSpec((None, CONV_W, tn), lambda j, i: (layer, 0, j))
    wv = pl.BlockSpec((None, CONV_W, tn), lambda j, i: (layer, 0, nj + j))
    cg = pl.BlockSpec((None, 1, tn), lambda j, i: (layer, 0, j))
    cv = pl.BlockSpec((None, 1, tn), lambda j, i: (layer, 0, nj + j))
    conv_b3 = conv_b.reshape(DEPTH, 1, 2 * D_FF)
    w_half = pl.BlockSpec((k, tn), lambda j, i: (0, j))
    t_spec = pl.BlockSpec((1, tail, tn), lambda j, i: (i, 0, j))
    out_specs = [pl.BlockSpec((tm, tn), lambda j, i: (rb0 + i, j)), t_spec, t_spec]
    out_shape = [jax.ShapeDtypeStruct((ROWS, D_FF), BF16),
                 jax.ShapeDtypeStruct((ni, tail, D_FF), F32),
                 jax.ShapeDtypeStruct((ni, tail, D_FF), F32)]
    aliases = {}
    if sample:
        state, h_all = sample_state
        assert ni == 1 and tm == state.shape[1] * DEC_SEQ
        sg = pl.BlockSpec((None,) + state.shape[1:3] + (tn,), lambda j, i: (layer, 0, 0, j))
        sv = pl.BlockSpec((None,) + state.shape[1:3] + (tn,), lambda j, i: (layer, 0, 0, nj + j))
        in_specs = [a_spec, w_half, w_half, wg, wv, cg, cv, sg, sv, pl.BlockSpec(memory_space=pl.ANY)]
        args = [x, w[0], w[1], conv_w, conv_w, conv_b3, conv_b3, state, state, h_all]
        aliases = {len(args) - 1: 0}
    else:
        rows, cols = side_cast.shape[1:]
        slab = rows // (nj * ni)
        assert slab * nj * ni == rows and slab % BF16_ROWS == 0
        ah = pl.BlockSpec((halo, k), lambda j, i: (jnp.maximum((rb0 + i) * (tm // halo) - 1, 0), 0))
        in_specs = [a_spec, ah,
                    pl.BlockSpec((None, k, tn), lambda j, i: (layer, 0, j)),
                    pl.BlockSpec((None, k, tn), lambda j, i: (layer, 0, nj + j)),
                    wg, wv, cg, cv,
                    pl.BlockSpec((None, slab, cols), lambda j, i: (layer, j * ni + i, 0))]
        args = [x, x, w, w, conv_w, conv_w, conv_b3, conv_b3, side_cast]
        out_specs += [pl.BlockSpec((slab, cols), lambda j, i: (j * ni + i, 0)), w_half, w_half]
        out_shape += [jax.ShapeDtypeStruct((rows, cols), BF16),
                      jax.ShapeDtypeStruct((k, D_FF), BF16), jax.ShapeDtypeStruct((k, D_FF), BF16)]
    return pl.pallas_call(
        functools.partial(_ffn_in_kernel, sample=sample, seq_tiles=SEQ // tm if not sample else 1, tail=tail),
        grid=(nj, ni),
        in_specs=in_specs,
        out_specs=out_specs,
        out_shape=out_shape,
        input_output_aliases=aliases,
        compiler_params=_params(("parallel", "arbitrary")),
        name="ffn_in",
    )(*args)


def _t5_bucket(rel):
    nb = T5_BUCKETS // 2
    max_exact = nb // 2
    n = jnp.abs(rel)
    nf = jnp.maximum(n, 1).astype(F32)
    large = max_exact + (jnp.log(nf / max_exact) / math.log(T5_MAX_DIST / max_exact)
                         * (nb - max_exact)).astype(jnp.int32)
    large = jnp.minimum(large, nb - 1)
    return jnp.where(rel > 0, nb, 0) + jnp.where(n < max_exact, n, large)


def _t5_bias(table, rel):
    return jnp.transpose(table[_t5_bucket(rel)], (2, 0, 1)).astype(F32)


def _clipped_bias(table, rel):
    return table[:, jnp.clip(rel, -BAND_MAX_REL, BAND_MAX_REL) + BAND_MAX_REL].astype(F32)


def _toeplitz_diag(bias_fn, n_rows, n_cols, rel0, n):
    assert n_cols + n_rows - 1 <= n
    k = np.arange(n)
    rel = np.where(k < n_cols, k, k - n) + rel0
    return jnp.broadcast_to(bias_fn(jnp.asarray(rel)[None, :]), (N_HEADS, 8, n))


def _band_diag(bias_fn, n_back):
    n_prev = n_back * CHUNK
    return _toeplitz_diag(lambda d: bias_fn(-d), n_prev + QB, QB, n_prev, 1024)


def _cache_diag(bias_fn, cache_len):
    n = max(256, cache_len * 2)
    return _toeplitz_diag(bias_fn, DEC_SEQ, cache_len, -cache_len, n)


def _rope_tables(pos):
    half = RET_DK // 2
    inv = ROPE_BASE ** (-jnp.arange(half, dtype=F32) / half)
    ang = pos.astype(F32)[:, None] * inv[None, :]
    cos, sin = jnp.cos(ang), jnp.sin(ang)
    cos_t = jnp.concatenate([cos, cos, cos, cos], -1)
    sin_t = jnp.concatenate([-sin, sin, -sin, sin], -1)
    return cos_t, sin_t


def _dup_groups(t):
    g0, g1 = t[..., 0, :], t[..., 1, :]
    return jnp.concatenate([g0, g0, g1, g1], -1)


def _undup(t):
    return jnp.stack([t[:, 0:64], t[:, 128:192]], 1)


def kernel(x_prompt, x_sample, mem_prompt, cache_swa_k, cache_swa_v, state_ret, cache_band_k, cache_band_v, state_ffn_conv, cache_mem_k, cache_mem_v, w_in, t5_table, swa_sink, band_rel_table, w_br_a, w_br_b, w_br_c, w_mix_o, ln1_g, ln1_b, w_mq, w_mk, w_mv, w_mo, ln2_g, ln2_b, w_ffn_in, ffn_conv_w, ffn_conv_b, w_ffn_out, ln3_g, ln3_b):
    x, xb = pack_rows(x_prompt, x_sample)
    memb = mem_prompt.reshape(BATCH * MEM_LEN, D_MODEL).astype(BF16)

    t5 = functools.partial(_t5_bias, t5_table)
    diag_a = _band_diag(t5, SWA_BACK)
    qpos = PAST_LEN + jnp.arange(DEC_SEQ)
    la, lc = cache_swa_k.shape[2], cache_band_k.shape[2]
    rel_n = qpos[None, :] - qpos[:, None]
    diag_a_c, bias_a_n = _cache_diag(t5, la), t5(rel_n)
    cos_p, sin_p = _rope_tables(jnp.arange(SEQ))
    cos_s, sin_s = _rope_tables(qpos)
    zero_state = jnp.zeros((1, BATCH, 4, LANES, LANES), F32)
    no_sink = jnp.zeros((N_HEADS,), F32)

    w_in_b = permute_w_in(w_in)
    swa_k2, swa_v2 = _dup_groups(cache_swa_k), _dup_groups(cache_swa_v)
    band_k2 = cache_band_k.reshape(DEPTH, DEC_BATCH, lc, BAND_HEADS * HEAD_DIM)
    band_v2 = cache_band_v.reshape(DEPTH, DEC_BATCH, lc, BAND_HEADS * HEAD_DIM)
    mem_k2 = cache_mem_k.reshape(DEPTH, DEC_BATCH, MEM_LEN, D_MODEL)
    mem_v2 = cache_mem_v.reshape(DEPTH, DEC_BATCH, MEM_LEN, D_MODEL)
    ret_s0 = state_ret.reshape(DEPTH, DEC_BATCH, 4, LANES, LANES)

    outs = {k: [] for k in ("p_ak", "p_av", "p_rs", "p_bk", "p_bv", "p_fc", "p_mk", "p_mv",
                            "s_ak", "s_av", "s_rs", "s_bk", "s_bv", "s_fc")}
    for l in range(DEPTH):
        proj, w_br_a_b, w_br_b_b, w_br_c_b, w_mix_o_b, w_mo_b = matmul(
            xb, w_in_b, l, tm=1056, tn=1536, out_dtype=BF16, side_casts=(w_br_a, w_br_b, w_br_c, w_mix_o, w_mo))
        clipped = functools.partial(_clipped_bias, band_rel_table[l])
        oa = band_attention_prompt(proj, diag_a, swa_sink[l], q_col=COL_QA, k_col=COL_KA2, v_col=COL_VA2,
                                   shared_kv=True, n_back=SWA_BACK, has_sink=True, skew=None)
        oc = band_attention_prompt(proj, _band_diag(clipped, BAND_BACK), no_sink, q_col=COL_QC, k_col=COL_KC,
                                   v_col=COL_VC, shared_kv=False, n_back=BAND_BACK, has_sink=False, skew=1)
        ob, rs_p = retention(proj, cos_p, sin_p, zero_state, 0, row0=0, n_seq=BATCH, seq_len=SEQ,
                             blk_len=RET_L, pos_per_blk=True)
        oa = cached_attention_sample(
            proj, swa_k2, swa_v2, l, diag_a_c, bias_a_n, swa_sink[l], oa,
            q_col=COL_QA, k_col=COL_KA2, v_col=COL_VA2, shared_kv=True, has_sink=True)
        oc = cached_attention_sample(
            proj, band_k2, band_v2, l, _cache_diag(clipped, lc), clipped(rel_n), no_sink, oc,
            q_col=COL_QC, k_col=COL_KC, v_col=COL_VC, shared_kv=False, has_sink=False)
        ob, rs_s = retention(proj, cos_s, sin_s, ret_s0, l, ob, row0=P_ROWS, n_seq=DEC_BATCH, seq_len=DEC_SEQ,
                             blk_len=DEC_SEQ, pos_per_blk=False)
        mix = gated_branch_sum(oa, ob, oc, w_br_a_b, w_br_b_b, w_br_c_b, proj)
        x, xb = matmul_residual_ln(mix, w_mix_o_b, l, x, ln1_g, ln1_b, tm=528)

        mk = matmul(memb, w_mk, l, tm=512, tn=1024, out_dtype=F32)
        mv = matmul(memb, w_mv, l, tm=512, tn=1024, out_dtype=F32)
        qm = matmul(xb, w_mq, l, tm=1056, tn=2048, out_dtype=BF16)
        om = mem_attention(qm, mk.reshape(1, BATCH, MEM_LEN, D_MODEL), mv.reshape(1, BATCH, MEM_LEN, D_MODEL), 0,
                           row0=0, n_rows=P_ROWS, tm=1024, heads=MEM_HEADS, rows_per_kv=SEQ)
        om = mem_attention(qm, mem_k2, mem_v2, l, om, row0=P_ROWS, n_rows=S_ROWS, tm=DEC_SEQ,
                           heads=MEM_HEADS, rows_per_kv=DEC_SEQ)
        x, xb = matmul_residual_ln(om, w_mo_b, l, x, ln2_g, ln2_b, tm=528)

        h, tg_p, tv_p, w_ffn_out_b, wfg_b, wfv_b = ffn_in(
            xb, w_ffn_in, ffn_conv_w, ffn_conv_b, l, row0=0, n_rows=P_ROWS, tm=1024, tn=512, tail=8,
            side_cast=w_ffn_out)
        h, tg_s, tv_s = ffn_in(xb, (wfg_b, wfv_b), ffn_conv_w, ffn_conv_b, l, row0=P_ROWS, n_rows=S_ROWS,
                               tm=S_ROWS, tn=512, tail=S_ROWS, sample_state=(state_ffn_conv, h))
        if l < DEPTH - 1:
            x, xb = matmul_residual_ln(h, w_ffn_out_b, l, x, ln3_g, ln3_b, tm=S_ROWS)
        else:
            y_prompt, y_sample = matmul_residual_ln(h, w_ffn_out_b, l, x, ln3_g, ln3_b, tm=S_ROWS,
                                                    split_rows=P_ROWS)

        sf = proj[P_ROWS:]
        la_p, lc_p = min(SWA_BACK * CHUNK, SEQ), min(BAND_BACK * CHUNK, SEQ)
        seq_tail = lambda n, c0, w: jnp.stack(
            [proj[(b + 1) * SEQ - n:(b + 1) * SEQ, c0:c0 + w] for b in range(BATCH)], 0)
        outs["p_ak"].append(_undup(seq_tail(la_p, COL_KA2, 256).reshape(BATCH * la_p, 256))
                            .reshape(BATCH, la_p, 2, 64).astype(F32))
        outs["p_av"].append(_undup(seq_tail(la_p, COL_VA2, 256).reshape(BATCH * la_p, 256))
                            .reshape(BATCH, la_p, 2, 64).astype(F32))
        outs["p_rs"].append(rs_p.reshape(BATCH, RET_HEADS, RET_DK, RET_DV))
        outs["p_bk"].append(seq_tail(lc_p, COL_KC, 1024).reshape(BATCH, lc_p, 16, 64).astype(F32))
        outs["p_bv"].append(seq_tail(lc_p, COL_VC, 1024).reshape(BATCH, lc_p, 16, 64).astype(F32))
        last = [(b + 1) * (SEQ // 1024) - 1 for b in range(BATCH)]
        outs["p_fc"].append(jnp.stack(
            [jnp.concatenate([tg_p[t, 6:8], tv_p[t, 6:8]], -1) for t in last], 0))
        outs["p_mk"].append(mk.reshape(BATCH, MEM_LEN, MEM_HEADS, MEM_HD))
        outs["p_mv"].append(mv.reshape(BATCH, MEM_LEN, MEM_HEADS, MEM_HD))
        outs["s_ak"].append(_undup(sf[:, COL_KA2:COL_KA2 + 256]).reshape(DEC_BATCH, DEC_SEQ, 2, 64).astype(F32))
        outs["s_av"].append(_undup(sf[:, COL_VA2:COL_VA2 + 256]).reshape(DEC_BATCH, DEC_SEQ, 2, 64).astype(F32))
        outs["s_rs"].append(rs_s.reshape(DEC_BATCH, RET_HEADS, RET_DK, RET_DV))
        outs["s_bk"].append(sf[:, COL_KC:COL_KC + 1024].reshape(DEC_BATCH, DEC_SEQ, 16, 64).astype(F32))
        outs["s_bv"].append(sf[:, COL_VC:COL_VC + 1024].reshape(DEC_BATCH, DEC_SEQ, 16, 64).astype(F32))
        u_s = jnp.concatenate([tg_s[0], tv_s[0]], -1).reshape(DEC_BATCH, DEC_SEQ, 2 * D_FF)
        outs["s_fc"].append(u_s[:, DEC_SEQ - 2:])

    st = lambda name: jnp.stack(outs[name], 0)
    return (y_prompt.reshape(BATCH, SEQ, D_MODEL), y_sample[:S_ROWS].reshape(DEC_BATCH, DEC_SEQ, D_MODEL),
            st("p_ak"), st("p_av"), st("p_rs"), st("p_bk"), st("p_bv"), st("p_fc"), st("p_mk"), st("p_mv"),
            st("s_ak"), st("s_av"), st("s_rs"), st("s_bk"), st("s_bv"), st("s_fc"))
```

```python
import functools
import math

import numpy as np
import jax
import jax.numpy as jnp
from jax import lax
from jax.experimental import pallas as pl
from jax.experimental.pallas import tpu as pltpu

F32 = jnp.float32
BF16 = jnp.bfloat16

D_MODEL = 2048
BATCH = 2
SEQ = 4096
DEPTH = 2
DEC_BATCH = 16
DEC_SEQ = 16
PAST_LEN = 2048
CHUNK = 64
HEAD_DIM = 64
SWA_BACK = 2
SWA_HEADS = 16
SWA_KV_HEADS = 2
N_HEADS = SWA_HEADS
ATT_W = N_HEADS * HEAD_DIM
T5_BUCKETS = 32
T5_MAX_DIST = 128
RET_HEADS = 8
RET_DK = 64
RET_DV = 128
ROPE_BASE = 10000.0
RET_NORM_EPS = 1e-5
BAND_BACK = 8
BAND_HEADS = 16
BAND_MAX_REL = 256
MEM_LEN = 256
MEM_HEADS = 4
MEM_HD = D_MODEL // MEM_HEADS
D_FF = 5632
CONV_W = 3
DN_ALPHA = (2 * DEPTH) ** 0.25
LN_EPS = 1e-5

P_ROWS = BATCH * SEQ
S_ROWS = DEC_BATCH * DEC_SEQ
ROWS = P_ROWS + S_ROWS

COL_QA = 0
COL_QC = 1024
COL_KC = 2048
COL_VC = 3072
COL_VB = 4096
COL_GR = 5120
COL_GA = 6144
COL_GB = 8192
COL_GC = 10240
COL_QB = 12288
COL_KB = 12800
COL_KA2 = 13312
COL_VA2 = 13568
PROJ_COLS = 13824

LANES = 128
BF16_ROWS = 16
MXU_COLS = 256
QB = 256
RET_L = 256
NEG = -1e30
LOG2E = math.log2(math.e)
VMEM_LIMIT = 48 * 1024 * 1024

NT = (((1,), (1,)), ((), ()))
TN = (((0,), (0,)), ((), ()))


def _params(sem, vmem=VMEM_LIMIT):
    return pltpu.CompilerParams(dimension_semantics=sem, vmem_limit_bytes=vmem)


def _pack_rows_kernel(p_ref, s_ref, of_ref, ob_ref, *, n_prompt_tiles):
    src = jnp.where(pl.program_id(0) < n_prompt_tiles, p_ref[...], s_ref[...])
    of_ref[...] = src
    ob_ref[...] = src.astype(BF16)


def pack_rows(x_prompt, x_sample):
    tm = S_ROWS
    npt = P_ROWS // tm
    return pl.pallas_call(
        functools.partial(_pack_rows_kernel, n_prompt_tiles=npt),
        grid=(npt + 1,),
        in_specs=[pl.BlockSpec((tm, D_MODEL), lambda i: (jnp.minimum(i, npt - 1), 0)),
                  pl.BlockSpec((tm, D_MODEL), lambda i: (0, 0))],
        out_specs=[pl.BlockSpec((tm, D_MODEL), lambda i: (i, 0)), pl.BlockSpec((tm, D_MODEL), lambda i: (i, 0))],
        out_shape=[jax.ShapeDtypeStruct((ROWS, D_MODEL), F32), jax.ShapeDtypeStruct((ROWS, D_MODEL), BF16)],
        compiler_params=_params(("parallel",)),
        name="pack_rows",
    )(x_prompt.reshape(P_ROWS, D_MODEL), x_sample.reshape(S_ROWS, D_MODEL))


_W_IN_MOVES = ((0, COL_QA, 1024), (4352, COL_QC, 1024), (5376, COL_KC, 1024), (6400, COL_VC, 1024),
               (2304, COL_VB, 1024), (3328, COL_GR, 1024), (7424, COL_GA, 2048), (9472, COL_GB, 2048),
               (11520, COL_GC, 2048), (1280, COL_QB, 512), (1792, COL_KB, 512))
_W_IN_DUPS = ((1024, COL_KA2), (1152, COL_VA2))


def _w_in_permute_kernel(src_ref, o_ref):
    for src, dst, width in _W_IN_MOVES:
        o_ref[:, dst:dst + width] = src_ref[:, src:src + width].astype(BF16)
    for src, dst in _W_IN_DUPS:
        pair = src_ref[:, src:src + LANES]
        swapped = pltpu.roll(pair, HEAD_DIM, 1)
        o_ref[:, dst:dst + LANES] = jnp.where(_head_lanes(0), pair, swapped).astype(BF16)
        o_ref[:, dst + LANES:dst + 2 * LANES] = jnp.where(_head_lanes(0), swapped, pair).astype(BF16)


def permute_w_in(w_in, *, rows=128):
    depth, d, in_cols = w_in.shape
    assert sum(w for _, _, w in _W_IN_MOVES) + 2 * LANES == in_cols
    return pl.pallas_call(
        _w_in_permute_kernel,
        grid=(depth, d // rows),
        in_specs=[pl.BlockSpec((None, rows, in_cols), lambda l, r: (l, r, 0))],
        out_specs=pl.BlockSpec((None, rows, PROJ_COLS), lambda l, r: (l, r, 0)),
        out_shape=jax.ShapeDtypeStruct((depth, d, PROJ_COLS), BF16),
        compiler_params=_params(("parallel", "parallel")),
        name="permute_w_in",
    )(w_in)


def _mm_kernel(a_ref, b_ref, *refs, n_side, resident):
    side_in, o_ref, side_out = refs[:n_side], refs[n_side], refs[n_side + 1:2 * n_side + 1]
    for src, dst in zip(side_in, side_out):
        dst[...] = src[...].astype(BF16)
    if resident:
        bb_ref = refs[-1]

        @pl.when(pl.program_id(0) == 0)
        def _():
            bb_ref[...] = b_ref[...].astype(BF16)

        b = bb_ref[...]
    else:
        b = b_ref[...].astype(BF16)
    o_ref[...] = jnp.dot(a_ref[...], b, preferred_element_type=F32).astype(o_ref.dtype)


def matmul(a, b, layer, *, tm, tn, out_dtype, side_casts=()):
    m, k = a.shape
    n = b.shape[2]
    ni, nj = m // tm, n // tn
    resident = b.dtype == F32 and ni > 1
    assert b.dtype == BF16 or ni == 1 or nj == 1
    b_mode = dict(pipeline_mode=pl.Buffered(1)) if resident else {}
    in_specs = [pl.BlockSpec((tm, k), lambda i, j: (i, 0)),
                pl.BlockSpec((None, k, tn), lambda i, j: (layer, 0, j), **b_mode)]
    out_specs = [pl.BlockSpec((tm, tn), lambda i, j: (i, j))]
    out_shape = [jax.ShapeDtypeStruct((m, n), out_dtype)]
    n_slabs = ni * (nj - 1)
    slab_of = lambda i, j: i * (nj - 1) + jnp.minimum(j, nj - 2)
    for w in side_casts:
        rows, cols = w.shape[1:]
        slab = rows // n_slabs
        assert slab * n_slabs == rows and slab % BF16_ROWS == 0
        in_specs.append(pl.BlockSpec((None, slab, cols), lambda i, j: (layer, slab_of(i, j), 0)))
        out_specs.append(pl.BlockSpec((slab, cols), lambda i, j: (slab_of(i, j), 0)))
        out_shape.append(jax.ShapeDtypeStruct((rows, cols), BF16))
    sem = ("arbitrary", "arbitrary") if resident or side_casts else ("parallel", "parallel")
    outs = pl.pallas_call(
        functools.partial(_mm_kernel, n_side=len(side_casts), resident=resident),
        grid=(ni, nj),
        in_specs=in_specs,
        out_specs=out_specs,
        out_shape=out_shape,
        scratch_shapes=[pltpu.VMEM((k, tn), BF16)] if resident else [],
        compiler_params=_params(sem),
        name="matmul",
    )(a, b, *side_casts)
    return outs if side_casts else outs[0]


def _gate_mm_kernel(oa_ref, ob_ref, oc_ref, wa_ref, wb_ref, wc_ref, ga_ref, gb_ref, gc_ref, o_ref):
    acc = None
    for o, w, g in ((oa_ref, wa_ref, ga_ref), (ob_ref, wb_ref, gb_ref), (oc_ref, wc_ref, gc_ref)):
        t = jnp.dot(o[...], w[...], preferred_element_type=F32) * jax.nn.sigmoid(g[...].astype(F32))
        acc = t if acc is None else acc + t
    o_ref[...] = acc.astype(o_ref.dtype)


def gated_branch_sum(oa, ob, oc, wa, wb, wc, proj, *, tm=528):
    m, k = oa.shape
    n = wa.shape[1]
    o_spec = pl.BlockSpec((tm, k), lambda i: (i, 0))
    w_spec = pl.BlockSpec((k, n), lambda i: (0, 0), pipeline_mode=pl.Buffered(1))

    def g_spec(col):
        return pl.BlockSpec((tm, n), lambda i: (i, col // n))

    return pl.pallas_call(
        _gate_mm_kernel,
        grid=(m // tm,),
        in_specs=[o_spec, o_spec, o_spec, w_spec, w_spec, w_spec,
                  g_spec(COL_GA), g_spec(COL_GB), g_spec(COL_GC)],
        out_specs=pl.BlockSpec((tm, n), lambda i: (i, 0)),
        out_shape=jax.ShapeDtypeStruct((m, n), BF16),
        compiler_params=_params(("parallel",)),
        name="gated_branch_sum",
    )(oa, ob, oc, wa, wb, wc, proj, proj, proj)


def _mm_ln_kernel(a_ref, b_ref, r_ref, g_ref, bt_ref, o1_ref, o2_ref, acc_ref, *, split):
    @pl.when(pl.program_id(0) == 0)
    def _():
        acc_ref[...] = jnp.zeros_like(acc_ref)

    y = DN_ALPHA * r_ref[...] + acc_ref[...]
    mu = jnp.mean(y, -1, keepdims=True)
    d = y - mu
    var = jnp.mean(d * d, -1, keepdims=True)
    out = d * lax.rsqrt(var + LN_EPS) * g_ref[...] + bt_ref[...]
    o1_ref[...] = out
    o2_ref[...] = out if split else out.astype(BF16)
    acc_ref[...] = jnp.dot(a_ref[...], b_ref[...], preferred_element_type=F32)


def matmul_residual_ln(a, b, layer, resid, gain, bias, *, tm, split_rows=None):
    m, kk = a.shape
    n = b.shape[-1]
    nt = m // tm
    if b.ndim == 3:
        b_spec = pl.BlockSpec((None, kk, n), lambda i: (layer, 0, 0), pipeline_mode=pl.Buffered(1))
    else:
        b_spec = pl.BlockSpec((kk, n), lambda i: (0, 0), pipeline_mode=pl.Buffered(1))
    if split_rows is None:
        cur = lambda i: (jnp.minimum(i, nt - 1), 0)
        prev = lambda i: (jnp.maximum(i - 1, 0), 0)
        out_specs = [pl.BlockSpec((tm, n), prev), pl.BlockSpec((tm, n), prev)]
        out_shape = [jax.ShapeDtypeStruct((m, n), F32), jax.ShapeDtypeStruct((m, n), BF16)]
    else:
        assert split_rows == (nt - 1) * tm
        cur = lambda i: (jnp.where(i == 0, nt - 1, jnp.minimum(i - 1, nt - 2)), 0)
        prev = lambda i: (jnp.where(i <= 1, nt - 1, i - 2), 0)
        out_specs = [pl.BlockSpec((tm, n), lambda i: (jnp.clip(i - 2, 0, nt - 2), 0)),
                     pl.BlockSpec((tm, n), lambda i: (jnp.where(i <= 1, 0, 1), 0))]
        out_shape = [jax.ShapeDtypeStruct((split_rows, n), F32), jax.ShapeDtypeStruct((2 * tm, n), F32)]
    return pl.pallas_call(
        functools.partial(_mm_ln_kernel, split=split_rows is not None),
        grid=(nt + 1,),
        in_specs=[pl.BlockSpec((tm, kk), cur),
                  b_spec,
                  pl.BlockSpec((tm, n), prev),
                  pl.BlockSpec((None, 1, n), lambda i: (layer, 0, 0)),
                  pl.BlockSpec((None, 1, n), lambda i: (layer, 0, 0))],
        out_specs=out_specs,
        out_shape=out_shape,
        scratch_shapes=[pltpu.VMEM((tm, n), F32)],
        compiler_params=_params(("arbitrary",)),
        name="matmul_residual_ln",
    )(a, b, resid, gain.reshape(DEPTH, 1, n), bias.reshape(DEPTH, 1, n))


def _head_lanes(j):
    lane = lax.broadcasted_iota(jnp.int32, (1, LANES), 1)
    return (lane < HEAD_DIM) if j == 0 else (lane >= HEAD_DIM)


def _band_attn_kernel(sink_ref, diag_ref, q_ref, *refs, n_prev_blocks, n_back, shared_kv, has_sink, skew):
    nk = n_prev_blocks + 1
    k_refs = refs[:nk]
    v_refs = refs[nk:2 * nk]
    o_ref = refs[2 * nk]
    bias_ref = refs[2 * nk + 1]
    b = pl.program_id(0)
    i = pl.program_id(1)
    n_prev_rows = n_back * CHUNK
    kw = n_prev_rows + QB

    @pl.when((b == 0) & (i == 0))
    def _():
        n = diag_ref.shape[2]
        cb = lax.broadcasted_iota(jnp.int32, (kw, QB), 0) // CHUNK
        qc = lax.broadcasted_iota(jnp.int32, (kw, QB), 1) // CHUNK
        allowed = (cb >= qc) & (cb - n_back <= qc)
        for h in range(N_HEADS):
            t = pltpu.roll(jnp.broadcast_to(diag_ref[h, 0:1, :], (kw, n)), 0, 1, stride=1, stride_axis=0)
            bias_ref[h] = jnp.where(allowed, t[:, :QB] * LOG2E, NEG)

    krow = lax.broadcasted_iota(jnp.int32, (kw, 1), 0)
    kmask = jnp.where(krow >= jnp.maximum(n_prev_rows - i * QB, 0), 0.0, NEG).astype(BF16)
    lane = lax.broadcasted_iota(jnp.int32, (1, LANES), 1)

    def kv_cols(h):
        c0 = (h // 8 if shared_kv else h // 2) * LANES
        return slice(c0, c0 + LANES)

    def scores(h):
        p, j = divmod(h, 2)
        q = q_ref[:, p * LANES:(p + 1) * LANES] * (HEAD_DIM ** -0.5 * LOG2E)
        k_all = jnp.concatenate([r[:, kv_cols(h)] for r in k_refs], axis=0)
        mask_lane = lane == (HEAD_DIM if j == 0 else 0)
        qj = jnp.where(mask_lane, jnp.ones_like(q), jnp.where(_head_lanes(j), q, jnp.zeros_like(q)))
        kj = jnp.where(mask_lane, kmask, k_all)
        return lax.dot_general(kj, qj, NT, preferred_element_type=F32)

    def softmax(h, s_all):
        sink = sink_ref[h] * LOG2E if has_sink else None
        ps, dens = [], []
        for c in range(QB // LANES):
            lanes = slice(c * LANES, (c + 1) * LANES)
            lo, hi = c * LANES, c * LANES + n_prev_rows + LANES
            s = s_all[lo:hi, lanes] + bias_ref[h, lo:hi, lanes]
            m = jnp.max(s, 0, keepdims=True)
            if has_sink:
                m = jnp.maximum(m, sink)
            e = jnp.exp2(s - m)
            den = jnp.sum(e, 0, keepdims=True)
            if has_sink:
                den = den + jnp.exp2(sink - m)
            parts = [e.astype(BF16)]
            if lo:
                parts.insert(0, jnp.zeros((lo, LANES), BF16))
            if hi < kw:
                parts.append(jnp.zeros((kw - hi, LANES), BF16))
            ps.append(jnp.concatenate(parts, 0))
            dens.append(den)
        return jnp.concatenate(ps, 1), jnp.concatenate(dens, 1)

    def weighted_values(h, p_all, den):
        v_all = jnp.concatenate([r[:, kv_cols(h)] for r in v_refs], axis=0)
        return lax.dot_general(v_all, p_all, TN, preferred_element_type=F32) / den

    lag = N_HEADS if skew is None else skew
    s_q, p_q, outs = {}, {}, {}
    for t in range(N_HEADS + 2 * lag):
        if t < N_HEADS:
            s_q[t] = scores(t)
        if lag <= t < N_HEADS + lag:
            p_q[t - lag] = softmax(t - lag, s_q.pop(t - lag))
        if t >= 2 * lag:
            h = t - 2 * lag
            outs[h] = weighted_values(h, *p_q.pop(h))
            if h % 2:
                p = h // 2
                pair = jnp.where(lax.broadcasted_iota(jnp.int32, (LANES, 1), 0) < HEAD_DIM,
                                 outs.pop(h - 1), outs.pop(h))
                o_ref[:, p * LANES:(p + 1) * LANES] = pair.T.astype(o_ref.dtype)


def band_attention_prompt(proj, diag, sink, *, q_col, k_col, v_col, shared_kv, n_back, has_sink, skew):
    n_prev_rows = n_back * CHUNK
    nqb = SEQ // QB
    if n_prev_rows >= QB:
        n_prev_blocks, pb = n_prev_rows // QB, QB
    else:
        n_prev_blocks, pb = 1, n_prev_rows
    per = QB // pb
    kvw = 2 * SWA_KV_HEADS * HEAD_DIM if shared_kv else ATT_W

    def prev_spec(col, back):
        return pl.BlockSpec(
            (pb, kvw), lambda b, i: (b * (SEQ // pb) + jnp.maximum(i * per - back, 0), col // kvw))

    def own_spec(col):
        return pl.BlockSpec((QB, kvw), lambda b, i: (b * nqb + i, col // kvw))

    k_specs = [prev_spec(k_col, n_prev_blocks - t) for t in range(n_prev_blocks)] + [own_spec(k_col)]
    v_specs = [prev_spec(v_col, n_prev_blocks - t) for t in range(n_prev_blocks)] + [own_spec(v_col)]
    kern = functools.partial(_band_attn_kernel, n_prev_blocks=n_prev_blocks, n_back=n_back,
                             shared_kv=shared_kv, has_sink=has_sink, skew=skew)
    n_in = 2 * (n_prev_blocks + 1)
    return pl.pallas_call(
        kern,
        grid=(BATCH, nqb),
        in_specs=[pl.BlockSpec(memory_space=pltpu.SMEM),
                  pl.BlockSpec(diag.shape, lambda b, i: (0, 0, 0)),
                  pl.BlockSpec((QB, ATT_W), lambda b, i: (b * nqb + i, q_col // ATT_W))]
                 + k_specs + v_specs,
        out_specs=pl.BlockSpec((QB, ATT_W), lambda b, i: (b * nqb + i, 0)),
        out_shape=jax.ShapeDtypeStruct((ROWS, ATT_W), BF16),
        scratch_shapes=[pltpu.VMEM((N_HEADS, n_prev_rows + QB, QB), F32)],
        compiler_params=_params(("arbitrary", "arbitrary")),
        name="band_attention_prompt",
    )(sink, diag, proj, *([proj] * n_in))


def _cached_attn_kernel(sink_ref, q_ref, kn_ref, vn_ref, kc_ref, vc_ref, diag_ref, bn_ref, _, o_ref, bc_ref,
                        *, shared_kv, has_sink):
    lc = kc_ref.shape[1]

    @pl.when(pl.program_id(0) == 0)
    def _():
        n = diag_ref.shape[2]
        for h in range(N_HEADS):
            t = pltpu.roll(jnp.broadcast_to(diag_ref[h, 0:1, :], (DEC_SEQ, n)), 0, 1, stride=1, stride_axis=0)
            bc_ref[h] = t[:, :lc]

    kv_cols = [(p // 4 if shared_kv else p) * LANES for p in range(8)]
    scores = []
    for p in range(8):
        c0 = kv_cols[p]
        q = q_ref[:, p * LANES:(p + 1) * LANES] * (HEAD_DIM ** -0.5)
        kc = kc_ref[0, :, c0:c0 + LANES].astype(BF16)
        kn = kn_ref[:, c0:c0 + LANES]
        for j in range(2):
            h = 2 * p + j
            qj = jnp.where(_head_lanes(j), q, jnp.zeros_like(q))
            scores.append((lax.dot_general(qj, kc, NT, preferred_element_type=F32) + bc_ref[h],
                           lax.dot_general(qj, kn, NT, preferred_element_type=F32) + bn_ref[h]))
    probs = []
    for h, (s_c, s_n) in enumerate(scores):
        m = jnp.maximum(jnp.max(s_c, -1, keepdims=True), jnp.max(s_n, -1, keepdims=True))
        if has_sink:
            m = jnp.maximum(m, sink_ref[h])
        e_c, e_n = jnp.exp(s_c - m), jnp.exp(s_n - m)
        den = jnp.sum(e_c, -1, keepdims=True) + jnp.sum(e_n, -1, keepdims=True)
        if has_sink:
            den = den + jnp.exp(sink_ref[h] - m)
        probs.append((e_c.astype(BF16), e_n.astype(BF16), den))
    for p in range(8):
        c0 = kv_cols[p]
        vc = vc_ref[0, :, c0:c0 + LANES].astype(BF16)
        vn = vn_ref[:, c0:c0 + LANES]
        outs = []
        for j in range(2):
            e_c, e_n, den = probs[2 * p + j]
            outs.append((jnp.dot(e_c, vc, preferred_element_type=F32)
                         + jnp.dot(e_n, vn, preferred_element_type=F32)) / den)
        o_ref[:, p * LANES:(p + 1) * LANES] = jnp.where(_head_lanes(0), outs[0], outs[1]).astype(o_ref.dtype)


def cached_attention_sample(proj, k_cache, v_cache, layer, diag_c, bias_n, sink, o_all, *, q_col, k_col, v_col,
                            shared_kv, has_sink):
    lc, wc = k_cache.shape[2], k_cache.shape[3]
    r0 = P_ROWS // DEC_SEQ
    kern = functools.partial(_cached_attn_kernel, shared_kv=shared_kv, has_sink=has_sink)
    cache = pl.BlockSpec((None, 1, lc, wc), lambda s: (layer, s, 0, 0))
    return pl.pallas_call(
        kern,
        grid=(DEC_BATCH,),
        in_specs=[pl.BlockSpec(memory_space=pltpu.SMEM),
                  pl.BlockSpec((DEC_SEQ, ATT_W), lambda s: (r0 + s, q_col // ATT_W)),
                  pl.BlockSpec((DEC_SEQ, wc), lambda s: (r0 + s, k_col // wc)),
                  pl.BlockSpec((DEC_SEQ, wc), lambda s: (r0 + s, v_col // wc)),
                  cache, cache,
                  pl.BlockSpec(diag_c.shape, lambda s: (0, 0, 0)),
                  pl.BlockSpec((N_HEADS, DEC_SEQ, DEC_SEQ), lambda s: (0, 0, 0)),
                  pl.BlockSpec(memory_space=pl.ANY)],
        out_specs=pl.BlockSpec((DEC_SEQ, ATT_W), lambda s: (r0 + s, 0)),
        out_shape=jax.ShapeDtypeStruct(o_all.shape, o_all.dtype),
        scratch_shapes=[pltpu.VMEM((N_HEADS, DEC_SEQ, lc), F32)],
        input_output_aliases={8: 0},
        compiler_params=_params(("arbitrary",)),
        name="cached_attention_sample",
    )(sink, proj, proj, proj, k_cache, v_cache, diag_c, bias_n, o_all)


def _ret_kernel(q_ref, k_ref, v_ref, g_ref, cos_ref, sin_ref, s0_ref, *refs, blk_len):
    o_ref, sout_ref, st_ref, dec_ref, rdec_ref = refs[-5:]
    blk = pl.program_id(1)
    log_gs = [math.log(1.0 - 2.0 ** (-5.0 - h)) for h in range(RET_HEADS)]

    @pl.when((pl.program_id(0) == 0) & (blk == 0))
    def _():
        ii = lax.broadcasted_iota(jnp.int32, (blk_len, blk_len), 0)
        jj = lax.broadcasted_iota(jnp.int32, (blk_len, blk_len), 1)
        diff = (ii - jj).astype(F32)
        row = lax.broadcasted_iota(jnp.int32, (blk_len, LANES), 0).astype(F32)
        for h in range(RET_HEADS):
            dec_ref[h] = jnp.where(diff >= 0, jnp.exp(log_gs[h] * jnp.maximum(diff, 0.0)), 0.0)
            rdec_ref[h] = jnp.exp(log_gs[h] * (row + 1.0))
            rdec_ref[RET_HEADS + h] = jnp.exp(log_gs[h] * (blk_len - 1.0 - row))

    @pl.when(blk == 0)
    def _():
        st_ref[...] = s0_ref[0]

    cos = cos_ref[...]
    sin = sin_ref[...]
    lane = lax.broadcasted_iota(jnp.int32, (1, LANES), 1)
    low_half = (lane % RET_DK) < (RET_DK // 2)

    def rope(x):
        x = x.astype(F32)
        swapped = jnp.where(low_half, pltpu.roll(x, LANES - RET_DK // 2, 1), pltpu.roll(x, RET_DK // 2, 1))
        return x * cos + swapped * sin

    srow = lax.broadcasted_iota(jnp.int32, (LANES, 1), 0)

    value = lambda h: v_ref[:, h * RET_DV:(h + 1) * RET_DV]
    pairs = []
    for p in range(RET_HEADS // 2):
        qr = rope(q_ref[:, p * LANES:(p + 1) * LANES])
        kr = rope(k_ref[:, p * LANES:(p + 1) * LANES]) * (RET_DK ** -0.5)
        pairs.append((qr, kr, kr.astype(BF16), st_ref[p]))
    heads = []
    for h in range(RET_HEADS):
        qr, kr, kb, state = pairs[h // 2]
        qj = jnp.where(_head_lanes(h % 2), qr, 0.0).astype(BF16)
        qk = lax.dot_general(qj, kb, NT, preferred_element_type=F32) * dec_ref[h]
        cross = jnp.dot(qj, state.astype(BF16), preferred_element_type=F32) * rdec_ref[h]
        kwj = jnp.where(_head_lanes(h % 2), kr * rdec_ref[RET_HEADS + h], 0.0).astype(BF16)
        heads.append((qk.astype(BF16), cross, kwj))
    updates = []
    for h, (qk, cross, kwj) in enumerate(heads):
        o = jnp.dot(qk, value(h), preferred_element_type=F32) + cross
        mu = jnp.mean(o, -1, keepdims=True)
        d = o - mu
        var = jnp.mean(d * d, -1, keepdims=True)
        gate = g_ref[:, h * RET_DV:(h + 1) * RET_DV].astype(F32)
        o_ref[:, h * RET_DV:(h + 1) * RET_DV] = (
            d * lax.rsqrt(var + RET_NORM_EPS) * (gate * jax.nn.sigmoid(gate))).astype(o_ref.dtype)
        updates.append(lax.dot_general(kwj, value(h), TN, preferred_element_type=F32))
    for p in range(RET_HEADS // 2):
        carry = jnp.where(srow < RET_DK, math.exp(log_gs[2 * p] * blk_len), math.exp(log_gs[2 * p + 1] * blk_len))
        st_ref[p] = carry * pairs[p][3] + updates[2 * p] + updates[2 * p + 1]

    @pl.when(blk == pl.num_programs(1) - 1)
    def _():
        sout_ref[0] = st_ref[...]


def retention(proj, cos, sin, s0, layer, o_all=None, *, row0, n_seq, seq_len, blk_len, pos_per_blk):
    nb = seq_len // blk_len
    rb0 = row0 // blk_len

    def rows(col, width):
        return pl.BlockSpec((blk_len, width), lambda b, t: (rb0 + b * nb + t, col // width))

    tab = pl.BlockSpec((blk_len, LANES), lambda b, t: (t if pos_per_blk else 0, 0))
    st = pl.BlockSpec((1, 4, LANES, LANES), lambda b, t: (b, 0, 0, 0))
    st_in = pl.BlockSpec((None, 1, 4, LANES, LANES), lambda b, t: (layer, b, 0, 0, 0))
    in_specs = [rows(COL_QB, 512), rows(COL_KB, 512), rows(COL_VB, 1024), rows(COL_GR, 1024), tab, tab, st_in]
    args = [proj, proj, proj, proj, cos, sin, s0]
    aliases = {}
    if o_all is not None:
        in_specs.append(pl.BlockSpec(memory_space=pl.ANY))
        args.append(o_all)
        aliases = {len(args) - 1: 0}
    return pl.pallas_call(
        functools.partial(_ret_kernel, blk_len=blk_len),
        grid=(n_seq, nb),
        in_specs=in_specs,
        out_specs=[rows(0, 1024), st],
        out_shape=[jax.ShapeDtypeStruct((ROWS, 1024), BF16),
                   jax.ShapeDtypeStruct((n_seq, 4, LANES, LANES), F32)],
        scratch_shapes=[pltpu.VMEM((4, LANES, LANES), F32), pltpu.VMEM((RET_HEADS, blk_len, blk_len), F32),
                        pltpu.VMEM((2 * RET_HEADS, blk_len, LANES), F32)],
        input_output_aliases=aliases,
        compiler_params=_params(("arbitrary", "arbitrary")),
        name="retention",
    )(*args)


def _mem_attn_kernel(q_ref, k_ref, v_ref, *refs, heads):
    o_ref = refs[-1]
    n_kv = k_ref.shape[0]
    rows = q_ref.shape[0] // n_kv
    parts = [(s, slice(s * rows, (s + 1) * rows), slice(h * MEM_HD, (h + 1) * MEM_HD))
             for s in range(n_kv) for h in range(heads)]
    scores = [lax.dot_general(q_ref[r, c], k_ref[s, :, c].astype(BF16), NT, preferred_element_type=F32)
              * (MEM_HD ** -0.5) for s, r, c in parts]
    probs = []
    for sc in scores:
        e = jnp.exp(sc - jnp.max(sc, -1, keepdims=True))
        probs.append((e.astype(BF16), jnp.sum(e, -1, keepdims=True)))
    for (s, r, c), (e, den) in zip(parts, probs):
        o_ref[r, c] = (jnp.dot(e, v_ref[s, :, c].astype(BF16), preferred_element_type=F32) / den
                       ).astype(o_ref.dtype)


def mem_attention(qm, k_src, v_src, layer, o_all=None, *, row0, n_rows, tm, heads, rows_per_kv):
    rb0 = row0 // tm
    w = heads * MEM_HD
    sets = max(1, tm // rows_per_kv)
    kv = pl.BlockSpec((None, sets, MEM_LEN, w), lambda i, h: (layer, i * tm // rows_per_kv // sets, 0, h))
    in_specs = [pl.BlockSpec((tm, w), lambda i, h: (rb0 + i, h)), kv, kv]
    args = [qm, k_src, v_src]
    aliases = {}
    if o_all is not None:
        in_specs.append(pl.BlockSpec(memory_space=pl.ANY))
        args.append(o_all)
        aliases = {3: 0}
    return pl.pallas_call(
        functools.partial(_mem_attn_kernel, heads=heads),
        grid=(n_rows // tm, MEM_HEADS // heads),
        in_specs=in_specs,
        out_specs=pl.BlockSpec((tm, w), lambda i, h: (rb0 + i, h)),
        out_shape=jax.ShapeDtypeStruct((ROWS, D_MODEL), BF16),
        input_output_aliases=aliases,
        compiler_params=_params(("parallel", "parallel")),
        name="mem_attention",
    )(*args)


def _gelu(x):
    return 0.5 * x * (1.0 + lax.erf(x * (2.0 ** -0.5)))


def _ffn_in_kernel(*refs, sample, seq_tiles, tail):
    i = pl.program_id(1)
    if sample:
        a_ref, bg_ref, bv_ref, wg_ref, wv_ref, cbg_ref, cbv_ref, sg_ref, sv_ref = refs[:9]
        h_ref, tg_ref, tv_ref = refs[-3:]
    else:
        a_ref, ah_ref, wfg_ref, wfv_ref, wg_ref, wv_ref, cbg_ref, cbv_ref, side_ref = refs[:9]
        h_ref, tg_ref, tv_ref, side_out_ref, bg_ref, bv_ref = refs[-6:]
        side_out_ref[...] = side_ref[...].astype(BF16)

        @pl.when(i == 0)
        def _():
            bg_ref[...] = wfg_ref[...].astype(BF16)
            bv_ref[...] = wfv_ref[...].astype(BF16)

    tm = a_ref.shape[0]
    a = a_ref[...]
    if sample:
        pos = lax.broadcasted_iota(jnp.int32, (1, DEC_SEQ, 1), 1)
    else:
        top = lax.broadcasted_iota(jnp.int32, (8, 1), 0)
        keep = (i % seq_tiles != 0).astype(F32)

    def project(cols, b_ref):
        u = jnp.dot(a, b_ref[:, cols], preferred_element_type=F32)
        uh = None if sample else jnp.dot(ah_ref[...], b_ref[:, cols], preferred_element_type=F32)
        return u, uh

    def conv(cols, u, uh, w_ref, cb_ref, s_ref, t_ref):
        t_ref[0, :, cols] = u[tm - tail:, :]
        r1 = pltpu.roll(u, 1, 0)
        r2 = pltpu.roll(u, 2, 0)
        if sample:
            st = s_ref[:, :, cols]
            by_seq = lambda t: t.reshape(tm // DEC_SEQ, DEC_SEQ, t.shape[1])
            u1 = jnp.where(pos < 1, st[:, 1:2, :], by_seq(r1)).reshape(u.shape)
            u2 = jnp.where(pos < 1, st[:, 0:1, :], jnp.where(pos < 2, st[:, 1:2, :], by_seq(r2))).reshape(u.shape)
        else:
            n = uh.shape[0]
            prev1, prev2 = uh[n - 1:n, :] * keep, uh[n - 2:n - 1, :] * keep
            u1 = jnp.concatenate([jnp.where(top < 1, prev1, r1[:8]), r1[8:]], 0)
            top2 = jnp.where(top < 1, prev2, jnp.where(top < 2, prev1, r2[:8]))
            u2 = jnp.concatenate([top2, r2[8:]], 0)
        w = w_ref[:, cols]
        return w[0:1, :] * u2 + w[1:2, :] * u1 + w[2:3, :] * u + cb_ref[:, cols]

    sub = [slice(c0, c0 + MXU_COLS) for c0 in range(0, h_ref.shape[1], MXU_COLS)]
    prods = [(project(cols, bg_ref), project(cols, bv_ref)) for cols in sub]
    for cols, ((ug, uhg), (uv, uhv)) in zip(sub, prods):
        cg = conv(cols, ug, uhg, wg_ref, cbg_ref, sg_ref if sample else None, tg_ref)
        cv = conv(cols, uv, uhv, wv_ref, cbv_ref, sv_ref if sample else None, tv_ref)
        h_ref[:, cols] = (_gelu(cg) * cv).astype(h_ref.dtype)


def ffn_in(x, w, conv_w, conv_b, layer, *, row0, n_rows, tm, tn, tail, side_cast=None, sample_state=None):
    sample = sample_state is not None
    k = x.shape[1]
    nj = D_FF // tn
    ni = n_rows // tm
    rb0 = row0 // tm
    halo = BF16_ROWS
    a_spec = pl.BlockSpec((tm, k), lambda j, i: (rb0 + i, 0))
    wg = pl.BlockSpec((None, CONV_W, tn), lambda j, i: (layer, 0, j))
    wv = pl.BlockSpec((None, CONV_W, tn), lambda j, i: (layer, 0, nj + j))
    cg = pl.BlockSpec((None, 1, tn), lambda j, i: (layer, 0, j))
    cv = pl.BlockSpec((None, 1, tn), lambda j, i: (layer, 0, nj + j))
    conv_b3 = conv_b.reshape(DEPTH, 1, 2 * D_FF)
    w_half = pl.BlockSpec((k, tn), lambda j, i: (0, j))
    t_spec = pl.BlockSpec((1, tail, tn), lambda j, i: (i, 0, j))
    out_specs = [pl.BlockSpec((tm, tn), lambda j, i: (rb0 + i, j)), t_spec, t_spec]
    out_shape = [jax.ShapeDtypeStruct((ROWS, D_FF), BF16),
                 jax.ShapeDtypeStruct((ni, tail, D_FF), F32),
                 jax.ShapeDtypeStruct((ni, tail, D_FF), F32)]
    aliases = {}
    if sample:
        state, h_all = sample_state
        assert ni == 1 and tm == state.shape[1] * DEC_SEQ
        sg = pl.BlockSpec((None,) + state.shape[1:3] + (tn,), lambda j, i: (layer, 0, 0, j))
        sv = pl.BlockSpec((None,) + state.shape[1:3] + (tn,), lambda j, i: (layer, 0, 0, nj + j))
        in_specs = [a_spec, w_half, w_half, wg, wv, cg, cv, sg, sv, pl.BlockSpec(memory_space=pl.ANY)]
        args = [x, w[0], w[1], conv_w, conv_w, conv_b3, conv_b3, state, state, h_all]
        aliases = {len(args) - 1: 0}
    else:
        rows, cols = side_cast.shape[1:]
        slab = rows // (nj * ni)
        assert slab * nj * ni == rows and slab % BF16_ROWS == 0
        ah = pl.BlockSpec((halo, k), lambda j, i: (jnp.maximum((rb0 + i) * (tm // halo) - 1, 0), 0))
        in_specs = [a_spec, ah,
                    pl.BlockSpec((None, k, tn), lambda j, i: (layer, 0, j)),
                    pl.BlockSpec((None, k, tn), lambda j, i: (layer, 0, nj + j)),
                    wg, wv, cg, cv,
                    pl.BlockSpec((None, slab, cols), lambda j, i: (layer, j * ni + i, 0))]
        args = [x, x, w, w, conv_w, conv_w, conv_b3, conv_b3, side_cast]
        out_specs += [pl.BlockSpec((slab, cols), lambda j, i: (j * ni + i, 0)), w_half, w_half]
        out_shape += [jax.ShapeDtypeStruct((rows, cols), BF16),
                      jax.ShapeDtypeStruct((k, D_FF), BF16), jax.ShapeDtypeStruct((k, D_FF), BF16)]
    return pl.pallas_call(
        functools.partial(_ffn_in_kernel, sample=sample, seq_tiles=SEQ // tm if not sample else 1, tail=tail),
        grid=(nj, ni),
        in_specs=in_specs,
        out_specs=out_specs,
        out_shape=out_shape,
        input_output_aliases=aliases,
        compiler_params=_params(("parallel", "arbitrary")),
        name="ffn_in",
    )(*args)


def _t5_bucket(rel):
    nb = T5_BUCKETS // 2
    max_exact = nb // 2
    n = jnp.abs(rel)
    nf = jnp.maximum(n, 1).astype(F32)
    large = max_exact + (jnp.log(nf / max_exact) / math.log(T5_MAX_DIST / max_exact)
                         * (nb - max_exact)).astype(jnp.int32)
    large = jnp.minimum(large, nb - 1)
    return jnp.where(rel > 0, nb, 0) + jnp.where(n < max_exact, n, large)


def _t5_bias(table, rel):
    return jnp.transpose(table[_t5_bucket(rel)], (2, 0, 1)).astype(F32)


def _clipped_bias(table, rel):
    return table[:, jnp.clip(rel, -BAND_MAX_REL, BAND_MAX_REL) + BAND_MAX_REL].astype(F32)


def _toeplitz_diag(bias_fn, n_rows, n_cols, rel0, n):
    assert n_cols + n_rows - 1 <= n
    k = np.arange(n)
    rel = np.where(k < n_cols, k, k - n) + rel0
    return jnp.broadcast_to(bias_fn(jnp.asarray(rel)[None, :]), (N_HEADS, 8, n))


def _band_diag(bias_fn, n_back):
    n_prev = n_back * CHUNK
    return _toeplitz_diag(lambda d: bias_fn(-d), n_prev + QB, QB, n_prev, 1024)


def _cache_diag(bias_fn, cache_len):
    n = max(256, cache_len * 2)
    return _toeplitz_diag(bias_fn, DEC_SEQ, cache_len, -cache_len, n)


def _rope_tables(pos):
    half = RET_DK // 2
    inv = ROPE_BASE ** (-jnp.arange(half, dtype=F32) / half)
    ang = pos.astype(F32)[:, None] * inv[None, :]
    cos, sin = jnp.cos(ang), jnp.sin(ang)
    cos_t = jnp.concatenate([cos, cos, cos, cos], -1)
    sin_t = jnp.concatenate([-sin, sin, -sin, sin], -1)
    return cos_t, sin_t


def _dup_groups(t):
    g0, g1 = t[..., 0, :], t[..., 1, :]
    return jnp.concatenate([g0, g0, g1, g1], -1)


def _undup(t):
    return jnp.stack([t[:, 0:64], t[:, 128:192]], 1)


def kernel(x_prompt, x_sample, mem_prompt, cache_swa_k, cache_swa_v, state_ret, cache_band_k, cache_band_v, state_ffn_conv, cache_mem_k, cache_mem_v, w_in, t5_table, swa_sink, band_rel_table, w_br_a, w_br_b, w_br_c, w_mix_o, ln1_g, ln1_b, w_mq, w_mk, w_mv, w_mo, ln2_g, ln2_b, w_ffn_in, ffn_conv_w, ffn_conv_b, w_ffn_out, ln3_g, ln3_b):
    x, xb = pack_rows(x_prompt, x_sample)
    memb = mem_prompt.reshape(BATCH * MEM_LEN, D_MODEL).astype(BF16)

    t5 = functools.partial(_t5_bias, t5_table)
    diag_a = _band_diag(t5, SWA_BACK)
    qpos = PAST_LEN + jnp.arange(DEC_SEQ)
    la, lc = cache_swa_k.shape[2], cache_band_k.shape[2]
    rel_n = qpos[None, :] - qpos[:, None]
    diag_a_c, bias_a_n = _cache_diag(t5, la), t5(rel_n)
    cos_p, sin_p = _rope_tables(jnp.arange(SEQ))
    cos_s, sin_s = _rope_tables(qpos)
    zero_state = jnp.zeros((1, BATCH, 4, LANES, LANES), F32)
    no_sink = jnp.zeros((N_HEADS,), F32)

    w_in_b = permute_w_in(w_in)
    swa_k2, swa_v2 = _dup_groups(cache_swa_k), _dup_groups(cache_swa_v)
    band_k2 = cache_band_k.reshape(DEPTH, DEC_BATCH, lc, BAND_HEADS * HEAD_DIM)
    band_v2 = cache_band_v.reshape(DEPTH, DEC_BATCH, lc, BAND_HEADS * HEAD_DIM)
    mem_k2 = cache_mem_k.reshape(DEPTH, DEC_BATCH, MEM_LEN, D_MODEL)
    mem_v2 = cache_mem_v.reshape(DEPTH, DEC_BATCH, MEM_LEN, D_MODEL)
    ret_s0 = state_ret.reshape(DEPTH, DEC_BATCH, 4, LANES, LANES)

    outs = {k: [] for k in ("p_ak", "p_av", "p_rs", "p_bk", "p_bv", "p_fc", "p_mk", "p_mv",
                            "s_ak", "s_av", "s_rs", "s_bk", "s_bv", "s_fc")}
    for l in range(DEPTH):
        proj, w_br_a_b, w_br_b_b, w_br_c_b, w_mix_o_b, w_mo_b = matmul(
            xb, w_in_b, l, tm=1056, tn=1536, out_dtype=BF16, side_casts=(w_br_a, w_br_b, w_br_c, w_mix_o, w_mo))
        clipped = functools.partial(_clipped_bias, band_rel_table[l])
        oa = band_attention_prompt(proj, diag_a, swa_sink[l], q_col=COL_QA, k_col=COL_KA2, v_col=COL_VA2,
                                   shared_kv=True, n_back=SWA_BACK, has_sink=True, skew=None)
        oc = band_attention_prompt(proj, _band_diag(clipped, BAND_BACK), no_sink, q_col=COL_QC, k_col=COL_KC,
                                   v_col=COL_VC, shared_kv=False, n_back=BAND_BACK, has_sink=False, skew=1)
        ob, rs_p = retention(proj, cos_p, sin_p, zero_state, 0, row0=0, n_seq=BATCH, seq_len=SEQ,
                             blk_len=RET_L, pos_per_blk=True)
        oa = cached_attention_sample(
            proj, swa_k2, swa_v2, l, diag_a_c, bias_a_n, swa_sink[l], oa,
            q_col=COL_QA, k_col=COL_KA2, v_col=COL_VA2, shared_kv=True, has_sink=True)
        oc = cached_attention_sample(
            proj, band_k2, band_v2, l, _cache_diag(clipped, lc), clipped(rel_n), no_sink, oc,
            q_col=COL_QC, k_col=COL_KC, v_col=COL_VC, shared_kv=False, has_sink=False)
        ob, rs_s = retention(proj, cos_s, sin_s, ret_s0, l, ob, row0=P_ROWS, n_seq=DEC_BATCH, seq_len=DEC_SEQ,
                             blk_len=DEC_SEQ, pos_per_blk=False)
        mix = gated_branch_sum(oa, ob, oc, w_br_a_b, w_br_b_b, w_br_c_b, proj)
        x, xb = matmul_residual_ln(mix, w_mix_o_b, l, x, ln1_g, ln1_b, tm=528)

        mk = matmul(memb, w_mk, l, tm=512, tn=1024, out_dtype=F32)
        mv = matmul(memb, w_mv, l, tm=512, tn=1024, out_dtype=F32)
        qm = matmul(xb, w_mq, l, tm=1056, tn=2048, out_dtype=BF16)
        om = mem_attention(qm, mk.reshape(1, BATCH, MEM_LEN, D_MODEL), mv.reshape(1, BATCH, MEM_LEN, D_MODEL), 0,
                           row0=0, n_rows=P_ROWS, tm=1024, heads=MEM_HEADS, rows_per_kv=SEQ)
        om = mem_attention(qm, mem_k2, mem_v2, l, om, row0=P_ROWS, n_rows=S_ROWS, tm=4 * DEC_SEQ,
                           heads=MEM_HEADS, rows_per_kv=DEC_SEQ)
        x, xb = matmul_residual_ln(om, w_mo_b, l, x, ln2_g, ln2_b, tm=528)

        h, tg_p, tv_p, w_ffn_out_b, wfg_b, wfv_b = ffn_in(
            xb, w_ffn_in, ffn_conv_w, ffn_conv_b, l, row0=0, n_rows=P_ROWS, tm=1024, tn=512, tail=8,
            side_cast=w_ffn_out)
        h, tg_s, tv_s = ffn_in(xb, (wfg_b, wfv_b), ffn_conv_w, ffn_conv_b, l, row0=P_ROWS, n_rows=S_ROWS,
                               tm=S_ROWS, tn=512, tail=S_ROWS, sample_state=(state_ffn_conv, h))
        if l < DEPTH - 1:
            x, xb = matmul_residual_ln(h, w_ffn_out_b, l, x, ln3_g, ln3_b, tm=S_ROWS)
        else:
            y_prompt, y_sample = matmul_residual_ln(h, w_ffn_out_b, l, x, ln3_g, ln3_b, tm=S_ROWS,
                                                    split_rows=P_ROWS)

        sf = proj[P_ROWS:]
        la_p, lc_p = min(SWA_BACK * CHUNK, SEQ), min(BAND_BACK * CHUNK, SEQ)
        seq_tail = lambda n, c0, w: jnp.stack(
            [proj[(b + 1) * SEQ - n:(b + 1) * SEQ, c0:c0 + w] for b in range(BATCH)], 0)
        outs["p_ak"].append(_undup(seq_tail(la_p, COL_KA2, 256).reshape(BATCH * la_p, 256))
                            .reshape(BATCH, la_p, 2, 64).astype(F32))
        outs["p_av"].append(_undup(seq_tail(la_p, COL_VA2, 256).reshape(BATCH * la_p, 256))
                            .reshape(BATCH, la_p, 2, 64).astype(F32))
        outs["p_rs"].append(rs_p.reshape(BATCH, RET_HEADS, RET_DK, RET_DV))
        outs["p_bk"].append(seq_tail(lc_p, COL_KC, 1024).reshape(BATCH, lc_p, 16, 64).astype(F32))
        outs["p_bv"].append(seq_tail(lc_p, COL_VC, 1024).reshape(BATCH, lc_p, 16, 64).astype(F32))
        last = [(b + 1) * (SEQ // 1024) - 1 for b in range(BATCH)]
        outs["p_fc"].append(jnp.stack(
            [jnp.concatenate([tg_p[t, 6:8], tv_p[t, 6:8]], -1) for t in last], 0))
        outs["p_mk"].append(mk.reshape(BATCH, MEM_LEN, MEM_HEADS, MEM_HD))
        outs["p_mv"].append(mv.reshape(BATCH, MEM_LEN, MEM_HEADS, MEM_HD))
        outs["s_ak"].append(_undup(sf[:, COL_KA2:COL_KA2 + 256]).reshape(DEC_BATCH, DEC_SEQ, 2, 64).astype(F32))
        outs["s_av"].append(_undup(sf[:, COL_VA2:COL_VA2 + 256]).reshape(DEC_BATCH, DEC_SEQ, 2, 64).astype(F32))
        outs["s_rs"].append(rs_s.reshape(DEC_BATCH, RET_HEADS, RET_DK, RET_DV))
        outs["s_bk"].append(sf[:, COL_KC:COL_KC + 1024].reshape(DEC_BATCH, DEC_SEQ, 16, 64).astype(F32))
        outs["s_bv"].append(sf[:, COL_VC:COL_VC + 1024].reshape(DEC_BATCH, DEC_SEQ, 16, 64).astype(F32))
        u_s = jnp.concatenate([tg_s[0], tv_s[0]], -1).reshape(DEC_BATCH, DEC_SEQ, 2 * D_FF)
        outs["s_fc"].append(u_s[:, DEC_SEQ - 2:])

    st = lambda name: jnp.stack(outs[name], 0)
    return (y_prompt.reshape(BATCH, SEQ, D_MODEL), y_sample[:S_ROWS].reshape(DEC_BATCH, DEC_SEQ, D_MODEL),
            st("p_ak"), st("p_av"), st("p_rs"), st("p_bk"), st("p_bv"), st("p_fc"), st("p_mk"), st("p_mv"),
            st("s_ak"), st("s_av"), st("s_rs"), st("s_bk"), st("s_bv"), st("s_fc"))
```
